```python
import jax, jax.numpy as jnp
from jax import lax
import numpy as np

D_MODEL = 2048
BATCH = 2
SEQ = 4096
DEPTH = 2

N_MIXERS = 2
HGRN_EXPAND = 128
HGRN_HEADS = D_MODEL // HGRN_EXPAND
HGRN_F_DIM = HGRN_HEADS * HGRN_EXPAND
HGRN_I_DIM = D_MODEL
HGRN_HEAD_I = HGRN_I_DIM // HGRN_HEADS
CHUNK = 64
CONV_WIDTH = 3
D_FF = 5632
EPS = 1e-6
N_HGRN = (DEPTH + 1) // 2
N_SC = DEPTH // 2

kernel_name = "hgrn2_shortconv_interleaved_trunk"


def rms_norm(x, w):
    x32 = x.astype(jnp.float32)
    y = x32 * lax.rsqrt(jnp.mean(x32 * x32, axis=-1, keepdims=True) + EPS)
    return (y * w.astype(jnp.float32)).astype(x.dtype)


def causal_dwconv(u, w):
    K = w.shape[0]
    T = u.shape[1]
    up = jnp.pad(u, ((0, 0), (K - 1, 0), (0, 0)))
    return sum(up[:, k:k + T] * w[k] for k in range(K))


def chunked_gated_recurrence(q, k, v, logf):
    B, T, H, dk = q.shape
    dv = v.shape[-1]
    n = T // CHUNK

    def to_chunks(a):
        return a.reshape(B, n, CHUNK, H, a.shape[-1]).transpose(1, 0, 3, 2, 4)

    causal = jnp.tril(jnp.ones((CHUNK, CHUNK), dtype=bool))

    def step(S, inp):
        qc, kc, vc, gc = inp
        b = jnp.cumsum(gc, axis=2)
        rel = b[:, :, :, None, :] - b[:, :, None, :, :]
        rel = jnp.where(causal[:, :, None], rel, -jnp.inf)
        A = jnp.einsum('bhtd,bhsd,bhtsd->bhts', qc, kc, jnp.exp(rel))
        o = jnp.einsum('bhts,bhsv->bhtv', A, vc) + jnp.einsum('bhtd,bhdv->bhtv', qc * jnp.exp(b), S)
        b_last = b[:, :, -1:, :]
        S = jnp.exp(b_last[:, :, 0, :])[..., None] * S + jnp.einsum(
            'bhsd,bhsv->bhdv', kc * jnp.exp(b_last - b), vc)
        return S, o

    S0 = jnp.zeros((B, H, dk, dv), jnp.float32)
    _, o = lax.scan(step, S0, (to_chunks(q), to_chunks(k), to_chunks(v), to_chunks(logf)))
    return o.transpose(1, 0, 3, 2, 4).reshape(B, T, H * dv)


def hgrn2_mixer(xn, w_in, lb, out_gain, w_out):
    B, T, _ = xn.shape
    proj = xn @ w_in
    q, fr, v, g = jnp.split(proj, [HGRN_F_DIM, 2 * HGRN_F_DIM, 2 * HGRN_F_DIM + HGRN_I_DIM], axis=-1)
    q = jax.nn.silu(q.astype(jnp.float32)) * (HGRN_EXPAND ** -0.5)
    f = lb + (1.0 - lb) * jax.nn.sigmoid(fr.astype(jnp.float32))
    k = 1.0 - f
    logf = jnp.log(f)

    def heads(a, d):
        return a.reshape(B, T, HGRN_HEADS, d).astype(jnp.float32)

    o = chunked_gated_recurrence(heads(q, HGRN_EXPAND), heads(k, HGRN_EXPAND),
                                 heads(v, HGRN_HEAD_I), heads(logf, HGRN_EXPAND))
    o = rms_norm(o, out_gain) * jax.nn.silu(g.astype(jnp.float32))
    return (o @ w_out.astype(jnp.float32)).astype(xn.dtype)


def short_conv_mixer(xn, w_in, conv_w, w_out):
    proj = xn @ w_in
    gb, gc, h = jnp.split(proj, 3, axis=-1)
    y = gb * causal_dwconv(gc * h, conv_w)
    return y @ w_out


def conv_glu_ffn(xn, w_up, conv_w, w_down):
    u = causal_dwconv(xn @ w_up, conv_w)
    gate, val = jnp.split(u, 2, axis=-1)
    return (jax.nn.silu(gate) * val) @ w_down


def setup_inputs(seed: int = 0) -> dict:
    key = jax.random.key(seed)
    ks = jax.random.split(key, 16)
    f32 = jnp.float32

    def w(k, shape, fan_in):
        return jax.random.normal(k, shape, f32) * (fan_in ** -0.5)

    def gain(k, shape):
        return 1.0 + 0.02 * jax.random.normal(k, shape, f32)

    return {
        "x": jax.random.normal(ks[0], (BATCH, SEQ, D_MODEL), f32),
        "norm_mix": gain(ks[1], (DEPTH, D_MODEL)),
        "norm_ffn": gain(ks[2], (DEPTH, D_MODEL)),
        "hgrn_w_in": w(ks[3], (N_HGRN, D_MODEL, 2 * HGRN_F_DIM + 2 * HGRN_I_DIM), D_MODEL),
        "hgrn_lb_table": 0.1 * jax.random.normal(ks[4], (DEPTH + 1, HGRN_F_DIM), f32),
        "hgrn_out_norm": gain(ks[5], (N_HGRN, HGRN_I_DIM)),
        "hgrn_w_out": w(ks[6], (N_HGRN, HGRN_I_DIM, D_MODEL), HGRN_I_DIM),
        "sc_w_in": w(ks[7], (N_SC, D_MODEL, 3 * D_MODEL), D_MODEL),
        "sc_conv": w(ks[8], (N_SC, CONV_WIDTH, D_MODEL), CONV_WIDTH),
        "sc_w_out": w(ks[9], (N_SC, D_MODEL, D_MODEL), D_MODEL),
        "ffn_w_up": w(ks[10], (DEPTH, D_MODEL, 2 * D_FF), D_MODEL),
        "ffn_conv": w(ks[11], (DEPTH, CONV_WIDTH, 2 * D_FF), CONV_WIDTH),
        "ffn_w_down": w(ks[12], (DEPTH, D_FF, D_MODEL), D_FF),
        "final_norm": gain(ks[13], (D_MODEL,)),
    }


def reference(x, norm_mix, norm_ffn, hgrn_w_in, hgrn_lb_table, hgrn_out_norm, hgrn_w_out,
              sc_w_in, sc_conv, sc_w_out, ffn_w_up, ffn_conv, ffn_w_down, final_norm):
    lb_all = jnp.cumsum(jax.nn.softmax(hgrn_lb_table.astype(jnp.float32), axis=0), axis=0)
    h = x
    for i in range(DEPTH):
        j = i // N_MIXERS
        xn = rms_norm(h, norm_mix[i])
        if i % N_MIXERS == 0:
            mix = hgrn2_mixer(xn, hgrn_w_in[j], lb_all[i], hgrn_out_norm[j], hgrn_w_out[j])
        else:
            mix = short_conv_mixer(xn, sc_w_in[j], sc_conv[j], sc_w_out[j])
        h = h + mix.astype(h.dtype)
        h = h + conv_glu_ffn(rms_norm(h, norm_ffn[i]), ffn_w_up[i], ffn_conv[i], ffn_w_down[i]).astype(h.dtype)
    return rms_norm(h, final_norm)
```

```python
import functools

import jax
import jax.numpy as jnp
from jax import lax
from jax.experimental import pallas as pl
from jax.experimental.pallas import tpu as pltpu

F32 = jnp.float32
BF16 = jnp.bfloat16

EPS = 1e-6
HEAD_DIM = 128
CHUNK = 64
SUB = 16
N_SUB = CHUNK // SUB
EXP_CLAMP = 60.0
HALO = 16
VMEM_LIMIT = 56 * 1024 * 1024


def _dot(a, b):
    return jnp.dot(a, b, preferred_element_type=F32)


def _dot_nt(a, b):
    return lax.dot_general(a, b, (((1,), (1,)), ((), ())), preferred_element_type=F32)


def _dot_tn(a, b):
    return lax.dot_general(a, b, (((0,), (0,)), ((), ())), preferred_element_type=F32)


def _sigmoid(x):
    return 1.0 / (1.0 + jnp.exp(-x))


def _rms_rows(x, w):
    ms = jnp.mean(x * x, axis=-1, keepdims=True)
    return x * lax.rsqrt(ms + EPS) * w


def _params(semantics):
    return pltpu.CompilerParams(dimension_semantics=semantics, vmem_limit_bytes=VMEM_LIMIT)


def _hgrn_proj_kernel(x_ref, nw_ref, lbt_ref, wq_ref, wf_ref, wv_ref, wg_ref,
                      q_ref, lf_ref, v_ref, g_ref, xn_ref, *, layer, heads_per_step):
    @pl.when(pl.program_id(2) == 0)
    def _():
        xn_ref[...] = _rms_rows(x_ref[0], nw_ref[...]).astype(BF16)

    xn = xn_ref[...]

    rows = [lbt_ref[l:l + 1, :] for l in range(lbt_ref.shape[0])]
    m = functools.reduce(jnp.maximum, rows)
    es = [jnp.exp(r - m) for r in rows]
    lb = sum(es[:layer + 1]) / sum(es)

    q = _dot(xn, wq_ref[...])
    q = q * _sigmoid(q) * (HEAD_DIM ** -0.5)
    f = lb + (1.0 - lb) * _sigmoid(_dot(xn, wf_ref[...]))
    lf = jnp.log(f)
    v = _dot(xn, wv_ref[...])
    g = _dot(xn, wg_ref[...])
    g_ref[...] = (g * _sigmoid(g)).astype(BF16)
    for hh in range(heads_per_step):
        sl = slice(hh * HEAD_DIM, (hh + 1) * HEAD_DIM)
        q_ref[0, hh] = q[:, sl].astype(BF16)
        lf_ref[0, hh] = lf[:, sl]
        v_ref[0, hh] = v[:, sl].astype(BF16)


def _hgrn_proj(x, norm_w, lb_table, w_in, *, layer, tm=1024, heads_per_step=2):
    B, T, D = x.shape
    H = D // HEAD_DIM
    tn = heads_per_step * HEAD_DIM
    nq = D // tn
    grid = (B, T // tm, H // heads_per_step)
    head_shape = (B, H, T, HEAD_DIM)
    head_spec = pl.BlockSpec((1, heads_per_step, tm, HEAD_DIM), lambda b, i, j: (b, j, i, 0))
    w_spec = lambda part: pl.BlockSpec((D, tn), lambda b, i, j: (0, part * nq + j))
    return pl.pallas_call(
        functools.partial(_hgrn_proj_kernel, layer=layer, heads_per_step=heads_per_step),
        grid=grid,
        in_specs=[
            pl.BlockSpec((1, tm, D), lambda b, i, j: (b, i, 0)),
            pl.BlockSpec((1, D), lambda b, i, j: (0, 0)),
            pl.BlockSpec((lb_table.shape[0], tn), lambda b, i, j: (0, j)),
            w_spec(0), w_spec(1), w_spec(2), w_spec(3),
        ],
        out_specs=[
            head_spec, head_spec, head_spec,
            pl.BlockSpec((tm, tn), lambda b, i, j: (b * (T // tm) + i, j)),
        ],
        out_shape=[
            jax.ShapeDtypeStruct(head_shape, BF16),
            jax.ShapeDtypeStruct(head_shape, F32),
            jax.ShapeDtypeStruct(head_shape, BF16),
            jax.ShapeDtypeStruct((B * T, D), BF16),
        ],
        scratch_shapes=[pltpu.VMEM((tm, D), BF16)],
        compiler_params=_params(("parallel", "parallel", "arbitrary")),
        name="hgrn_proj",
    )(x, norm_w, lb_table, w_in, w_in, w_in, w_in)


def _exact_diag(slab, qb, kb, bb, lane0):
    lr = lax.broadcasted_iota(jnp.int32, slab.shape, 0)
    lc = lax.broadcasted_iota(jnp.int32, slab.shape, 1)
    for s in range(SUB):
        e = jnp.exp(jnp.minimum(bb - bb[s:s + 1, :], 0.0))
        col = jnp.sum(qb * kb[s:s + 1, :] * e, axis=-1, keepdims=True)
        slab = jnp.where((lc == lane0 + s) & (lr >= s), col, slab)
    return slab


def _hgrn_rec_kernel(q_ref, lf_ref, v_ref, o_ref, st_ref, a_ref, *, tm):
    @pl.when(pl.program_id(2) == 0)
    def _():
        st_ref[...] = jnp.zeros_like(st_ref)

    r64 = lax.broadcasted_iota(jnp.int32, (CHUNK, CHUNK), 0)
    c64 = lax.broadcasted_iota(jnp.int32, (CHUNK, CHUNK), 1)
    tril = (c64 <= r64).astype(BF16)
    wide = (CHUNK, N_SUB * CHUNK)
    rw = lax.broadcasted_iota(jnp.int32, wide, 0)
    cw = lax.broadcasted_iota(jnp.int32, wide, 1)
    blk = rw // SUB
    left_of_diag = (cw // CHUNK == blk) & (cw % CHUNK < blk * SUB)

    def chunk_body(c, carry):
        r0 = pl.multiple_of(c * CHUNK, CHUNK)
        rows = pl.ds(r0, CHUNK)
        lf = lf_ref[0, 0, rows, :]
        q = q_ref[0, 0, rows, :].astype(F32)
        v = v_ref[0, 0, rows, :]

        hi = lf.astype(BF16)
        r1 = lf - hi.astype(F32)
        mid = r1.astype(BF16)
        lo = (r1 - mid.astype(F32)).astype(BF16)
        b = _dot(tril, hi) + _dot(tril, mid) + _dot(tril, lo)
        k = 1.0 - jnp.exp(lf)
        b_last = b[CHUNK - 1:CHUNK, :]

        refs = [jnp.zeros((1, HEAD_DIM), F32)] + [b[SUB * i - 1:SUB * i, :] for i in range(1, N_SUB)]
        ref_rows = jnp.concatenate([jnp.broadcast_to(r, (SUB, HEAD_DIM)) for r in refs], axis=0)
        q_blk = (q * jnp.exp(b - ref_rows)).astype(BF16)
        k_all = jnp.concatenate(
            [(k * jnp.exp(jnp.minimum(r - b, EXP_CLAMP))).astype(BF16) for r in refs], axis=0)
        a = _dot_nt(q_blk, k_all)
        a_ref[...] = jnp.where(left_of_diag, a, 0.0)

        for i in range(N_SUB):
            rs = slice(SUB * i, SUB * (i + 1))
            cs = slice(128 * (i // 2), 128 * (i // 2 + 1))
            lane0 = (CHUNK * i) % 128 + SUB * i
            a_ref[rs, cs] = _exact_diag(a_ref[rs, cs], q[rs], k[rs], b[rs], lane0)

        v_rep = jnp.concatenate([v] * N_SUB, axis=0)
        st = st_ref[...]
        q_dec = (q * jnp.exp(b)).astype(BF16)
        o = _dot(a_ref[...].astype(BF16), v_rep) + _dot_nt(q_dec, st.astype(BF16))
        o_ref[0, rows, :] = o
        k_dec = (k * jnp.exp(b_last - b)).astype(BF16)
        st_ref[...] = st * jnp.exp(b_last) + _dot_tn(v, k_dec)
        return carry

    lax.fori_loop(0, tm // CHUNK, chunk_body, 0)


def _hgrn_rec(q, lf, v, *, tm=4096):
    B, H, T, _ = q.shape
    tm = min(tm, T)
    spec = pl.BlockSpec((1, 1, tm, HEAD_DIM), lambda b, h, t: (b, h, t, 0))
    return pl.pallas_call(
        functools.partial(_hgrn_rec_kernel, tm=tm),
        grid=(B, H, T // tm),
        in_specs=[spec, spec, spec],
        out_specs=pl.BlockSpec((1, tm, HEAD_DIM), lambda b, h, t: (b, t, h)),
        out_shape=jax.ShapeDtypeStruct((B, T, H * HEAD_DIM), F32),
        scratch_shapes=[pltpu.VMEM((HEAD_DIM, HEAD_DIM), F32),
                        pltpu.VMEM((CHUNK, N_SUB * CHUNK), F32)],
        compiler_params=_params(("parallel", "parallel", "arbitrary")),
        name="hgrn_rec",
    )(q, lf, v)


def _hgrn_out_kernel(o_ref, g_ref, h_ref, gain_ref, w_ref, out_ref):
    y = _rms_rows(o_ref[...], gain_ref[...]) * g_ref[...].astype(F32)
    out_ref[...] = h_ref[...] + _dot(y.astype(BF16), w_ref[...])


def _hgrn_out(o, gate, h, gain, w_out, *, tm=512):
    N, D = h.shape
    row_spec = pl.BlockSpec((tm, D), lambda i: (i, 0))
    return pl.pallas_call(
        _hgrn_out_kernel,
        grid=(N // tm,),
        in_specs=[row_spec, row_spec, row_spec,
                  pl.BlockSpec((1, D), lambda i: (0, 0)),
                  pl.BlockSpec((D, D), lambda i: (0, 0))],
        out_specs=row_spec,
        out_shape=jax.ShapeDtypeStruct((N, D), F32),
        compiler_params=_params(("parallel",)),
        name="hgrn_out",
    )(o, gate, h, gain, w_out)


def _prep_xn(h_ref, hprev_ref, nw_ref, xn_ref, first_in_batch):
    nw = nw_ref[...]
    xn_ref[HALO:, :] = _rms_rows(h_ref[...], nw).astype(BF16)
    prev = jnp.where(first_in_batch, 0.0, _rms_rows(hprev_ref[...], nw))
    xn_ref[0:HALO, :] = prev.astype(BF16)


def _conv3(u, cw, tm):
    return (cw[0:1, :] * u[HALO - 2:HALO - 2 + tm]
            + cw[1:2, :] * u[HALO - 1:HALO - 1 + tm]
            + cw[2:3, :] * u[HALO:HALO + tm])


def _ffn_kernel(h_ref, hprev_ref, nw_ref, wg_ref, wv_ref, cwg_ref, cwv_ref, wd_ref, fw_ref,
                out_ref, xn_ref, *, tm, tiles_per_batch, final_norm):
    i = pl.program_id(0)
    j = pl.program_id(1)

    @pl.when(j == 0)
    def _():
        _prep_xn(h_ref, hprev_ref, nw_ref, xn_ref, i % tiles_per_batch == 0)
        out_ref[...] = h_ref[...]

    xn = xn_ref[...]
    cg = _conv3(_dot(xn, wg_ref[...]), cwg_ref[...], tm)
    cv = _conv3(_dot(xn, wv_ref[...]), cwv_ref[...], tm)
    act = (cg * _sigmoid(cg) * cv).astype(BF16)
    out_ref[...] += _dot(act, wd_ref[...])

    if final_norm:
        @pl.when(j == pl.num_programs(1) - 1)
        def _():
            out_ref[...] = _rms_rows(out_ref[...], fw_ref[...])


def _ffn(h, seq_len, norm_w, w_up, conv_w, w_down, final_w, *, final_norm, tm=512, tn=512):
    N, D = h.shape
    F = w_down.shape[0]
    nf = F // tn
    tiles_per_batch = seq_len // tm
    halo_blocks = tm // HALO
    return pl.pallas_call(
        functools.partial(_ffn_kernel, tm=tm, tiles_per_batch=tiles_per_batch,
                          final_norm=final_norm),
        grid=(N // tm, nf),
        in_specs=[
            pl.BlockSpec((tm, D), lambda i, j: (i, 0)),
            pl.BlockSpec((HALO, D), lambda i, j: (jnp.maximum(i * halo_blocks - 1, 0), 0)),
            pl.BlockSpec((1, D), lambda i, j: (0, 0)),
            pl.BlockSpec((D, tn), lambda i, j: (0, j)),
            pl.BlockSpec((D, tn), lambda i, j: (0, nf + j)),
            pl.BlockSpec((3, tn), lambda i, j: (0, j)),
            pl.BlockSpec((3, tn), lambda i, j: (0, nf + j)),
            pl.BlockSpec((tn, D), lambda i, j: (j, 0)),
            pl.BlockSpec((1, D), lambda i, j: (0, 0)),
        ],
        out_specs=pl.BlockSpec((tm, D), lambda i, j: (i, 0)),
        out_shape=jax.ShapeDtypeStruct((N, D), F32),
        scratch_shapes=[pltpu.VMEM((tm + HALO, D), BF16)],
        compiler_params=_params(("parallel", "arbitrary")),
        name="ffn_final" if final_norm else "ffn",
    )(h, h, norm_w, w_up, w_up, conv_w, conv_w, w_down, final_w)


def _sc_kernel(h_ref, hprev_ref, nw_ref, wb_ref, wc_ref, wh_ref, cw_ref, wd_ref,
               out_ref, xn_ref, *, tm, tiles_per_batch):
    i = pl.program_id(0)
    j = pl.program_id(1)

    @pl.when(j == 0)
    def _():
        _prep_xn(h_ref, hprev_ref, nw_ref, xn_ref, i % tiles_per_batch == 0)
        out_ref[...] = h_ref[...]

    xn = xn_ref[...]
    z = _dot(xn, wc_ref[...]) * _dot(xn, wh_ref[...])
    gb = _dot(xn_ref[HALO:, :], wb_ref[...])
    y = (gb * _conv3(z, cw_ref[...], tm)).astype(BF16)
    out_ref[...] += _dot(y, wd_ref[...])


def _short_conv(h, seq_len, norm_w, w_in, conv_w, w_out, *, tm=512, tn=512):
    N, D = h.shape
    nd = D // tn
    tiles_per_batch = seq_len // tm
    halo_blocks = tm // HALO
    w_spec = lambda part: pl.BlockSpec((D, tn), lambda i, j: (0, part * nd + j))
    return pl.pallas_call(
        functools.partial(_sc_kernel, tm=tm, tiles_per_batch=tiles_per_batch),
        grid=(N // tm, nd),
        in_specs=[
            pl.BlockSpec((tm, D), lambda i, j: (i, 0)),
            pl.BlockSpec((HALO, D), lambda i, j: (jnp.maximum(i * halo_blocks - 1, 0), 0)),
            pl.BlockSpec((1, D), lambda i, j: (0, 0)),
            w_spec(0), w_spec(1), w_spec(2),
            pl.BlockSpec((3, tn), lambda i, j: (0, j)),
            pl.BlockSpec((tn, D), lambda i, j: (j, 0)),
        ],
        out_specs=pl.BlockSpec((tm, D), lambda i, j: (i, 0)),
        out_shape=jax.ShapeDtypeStruct((N, D), F32),
        scratch_shapes=[pltpu.VMEM((tm + HALO, D), BF16)],
        compiler_params=_params(("parallel", "arbitrary")),
        name="short_conv",
    )(h, h, norm_w, w_in, w_in, w_in, conv_w, w_out)


def kernel(x, norm_mix, norm_ffn, hgrn_w_in, hgrn_lb_table, hgrn_out_norm, hgrn_w_out,
           sc_w_in, sc_conv, sc_w_out, ffn_w_up, ffn_conv, ffn_w_down, final_norm):
    B, T, D = x.shape
    depth = norm_mix.shape[0]
    n_mixers = 2
    row = lambda w: w.reshape(1, -1).astype(F32)
    h = x.reshape(B * T, D)
    for i in range(depth):
        j = i // n_mixers
        if i % n_mixers == 0:
            q, lf, v, gate = _hgrn_proj(h.reshape(B, T, D), row(norm_mix[i]),
                                        hgrn_lb_table.astype(F32),
                                        hgrn_w_in[j].astype(BF16), layer=i)
            o = _hgrn_rec(q, lf, v)
            h = _hgrn_out(o.reshape(B * T, D), gate, h, row(hgrn_out_norm[j]),
                          hgrn_w_out[j].astype(BF16))
        else:
            h = _short_conv(h, T, row(norm_mix[i]), sc_w_in[j].astype(BF16),
                            sc_conv[j].astype(F32), sc_w_out[j].astype(BF16))
        h = _ffn(h, T, row(norm_ffn[i]), ffn_w_up[i].astype(BF16), ffn_conv[i].astype(F32),
                 ffn_w_down[i].astype(BF16), row(final_norm), final_norm=(i == depth - 1))
    return h.reshape(B, T, D)
```

```python
import functools

import jax
import jax.numpy as jnp
from jax import lax
from jax.experimental import pallas as pl
from jax.experimental.pallas import tpu as pltpu

F32 = jnp.float32
BF16 = jnp.bfloat16

EPS = 1e-6
HEAD_DIM = 128
CHUNK = 64
SUB = 16
N_SUB = CHUNK // SUB
N_PIECES = N_SUB * (N_SUB + 1) // 2
A_COLS = SUB * N_PIECES
EXP2_CLAMP = 112.0
INTRA_UNROLL = 8
STATE_UNROLL = 8
HALO = 16
ROW_PARTS = 2
VMEM_LIMIT = 56 * 1024 * 1024


def _dot(a, b):
    return jnp.dot(a, b, preferred_element_type=F32)


def _dot_nt(a, b):
    return lax.dot_general(a, b, (((1,), (1,)), ((), ())), preferred_element_type=F32)


def _dot_tn(a, b):
    return lax.dot_general(a, b, (((0,), (0,)), ((), ())), preferred_element_type=F32)


def _sigmoid(x):
    return 1.0 / (1.0 + jnp.exp(-x))


def _rms_rows(x, w):
    ms = jnp.mean(x * x, axis=-1, keepdims=True)
    return x * lax.rsqrt(ms + EPS) * w


def _params(semantics):
    return pltpu.CompilerParams(dimension_semantics=semantics, vmem_limit_bytes=VMEM_LIMIT)


def _chunk_cumsum(x):
    n, d = x.shape
    pos = lax.broadcasted_iota(jnp.int32, (n, d), 0) & (CHUNK - 1)
    s = 1
    while s < CHUNK:
        pad = max(8, s)
        xp = jnp.concatenate([jnp.zeros((pad, d), x.dtype), x], axis=0)
        x = x + jnp.where(pos >= s, xp[pad - s:pad - s + n], 0.0)
        s *= 2
    return x


def _hgrn_proj_kernel(x_ref, nw_ref, lbt_ref, wq_ref, wf_ref, wv_ref, wg_ref,
                      q_ref, k_ref, b_ref, bmin_ref, v_ref, g_ref, xn_ref, *, layer, heads_per_step):
    @pl.when(pl.program_id(2) == 0)
    def _():
        xn_ref[...] = _rms_rows(x_ref[0], nw_ref[...]).astype(BF16)

    xn = xn_ref[...]

    rows = [lbt_ref[l:l + 1, :] for l in range(lbt_ref.shape[0])]
    m = functools.reduce(jnp.maximum, rows)
    es = [jnp.exp(r - m) for r in rows]
    lb = sum(es[:layer + 1]) / sum(es)

    q = _dot(xn, wq_ref[...])
    q = q * _sigmoid(q) * (HEAD_DIM ** -0.5)
    f = lb + (1.0 - lb) * _sigmoid(_dot(xn, wf_ref[...]))
    b2 = _chunk_cumsum(jnp.log2(f))
    k = 1.0 - f
    v = _dot(xn, wv_ref[...])
    g = _dot(xn, wg_ref[...])
    g_ref[...] = (g * _sigmoid(g)).astype(BF16)
    for hh in range(heads_per_step):
        sl = slice(hh * HEAD_DIM, (hh + 1) * HEAD_DIM)
        q_ref[0, hh] = q[:, sl].astype(BF16)
        k_ref[0, hh] = k[:, sl].astype(BF16)
        b_ref[0, hh] = b2[:, sl]
        bmin_ref[0, hh, 0] = jnp.min(b2[:, sl].reshape(-1, 8, HEAD_DIM), axis=0)
        v_ref[0, hh] = v[:, sl].astype(BF16)


def _hgrn_proj(x, norm_w, lb_table, w_in, *, layer, tm=1024, heads_per_step=2):
    B, T, D = x.shape
    H = D // HEAD_DIM
    tn = heads_per_step * HEAD_DIM
    nq = D // tn
    grid = (B, T // tm, H // heads_per_step)
    head_shape = (B, H, T, HEAD_DIM)
    head_spec = pl.BlockSpec((1, heads_per_step, tm, HEAD_DIM), lambda b, i, j: (b, j, i, 0))
    w_spec = lambda part: pl.BlockSpec((D, tn), lambda b, i, j: (0, part * nq + j))
    return pl.pallas_call(
        functools.partial(_hgrn_proj_kernel, layer=layer, heads_per_step=heads_per_step),
        grid=grid,
        in_specs=[
            pl.BlockSpec((1, tm, D), lambda b, i, j: (b, i, 0)),
            pl.BlockSpec((1, D), lambda b, i, j: (0, 0)),
            pl.BlockSpec((lb_table.shape[0], tn), lambda b, i, j: (0, j)),
            w_spec(0), w_spec(1), w_spec(2), w_spec(3),
        ],
        out_specs=[
            head_spec, head_spec, head_spec,
            pl.BlockSpec((1, heads_per_step, 1, 8, HEAD_DIM), lambda b, i, j: (b, j, i, 0, 0)),
            head_spec,
            pl.BlockSpec((tm, tn), lambda b, i, j: (b * (T // tm) + i, j)),
        ],
        out_shape=[
            jax.ShapeDtypeStruct(head_shape, BF16),
            jax.ShapeDtypeStruct(head_shape, BF16),
            jax.ShapeDtypeStruct(head_shape, F32),
            jax.ShapeDtypeStruct((B, H, T // tm, 8, HEAD_DIM), F32),
            jax.ShapeDtypeStruct(head_shape, BF16),
            jax.ShapeDtypeStruct((B * T, D), BF16),
        ],
        scratch_shapes=[pltpu.VMEM((tm, D), BF16)],
        compiler_params=_params(("parallel", "parallel", "arbitrary")),
        name="hgrn_proj",
    )(x, norm_w, lb_table, w_in, w_in, w_in, w_in)


def _piece_offset(i):
    return SUB * i * (i + 1) // 2


def _score_mask():
    rw = lax.broadcasted_iota(jnp.int32, (CHUNK, A_COLS), 0)
    cw = lax.broadcasted_iota(jnp.int32, (CHUNK, A_COLS), 1)
    keep = None
    for i in range(N_SUB):
        diag0 = _piece_offset(i) + SUB * i
        in_rows = (rw >= SUB * i) & (rw < SUB * (i + 1))
        left = (cw >= _piece_offset(i)) & (cw < diag0)
        diag = (cw >= diag0) & (cw < diag0 + SUB) & (cw - diag0 <= rw - SUB * i)
        m = in_rows & (left | diag)
        keep = m if keep is None else keep | m
    return keep


def _exact_diag(slab, qb, kb, bb, lane0):
    lr = lax.broadcasted_iota(jnp.int32, slab.shape, 0)
    lc = lax.broadcasted_iota(jnp.int32, slab.shape, 1)
    for s in range(SUB):
        e = jnp.exp2(jnp.minimum(bb - bb[s:s + 1, :], 0.0))
        col = jnp.sum(qb * kb[s:s + 1, :] * e, axis=-1, keepdims=True)
        slab = jnp.where((lc == lane0 + s) & (lr >= s), col, slab)
    return slab


def _block_rows(rows):
    return jnp.concatenate([jnp.broadcast_to(r, (SUB, r.shape[-1])) for r in rows], axis=0)


def _chunk_factors(q_ref, k_ref, b_ref, rows):
    q = q_ref[0, 0, rows, :].astype(F32)
    k = k_ref[0, 0, rows, :].astype(F32)
    b = b_ref[0, 0, rows, :]

    ends = [b[SUB * (j + 1) - 1:SUB * (j + 1), :] for j in range(N_SUB)]
    refs = [jnp.zeros_like(ends[0])] + ends[:-1]
    b_last = ends[-1]
    d = b - _block_rows(refs)
    q_blk = (q * jnp.exp2(d)).astype(BF16)
    k_diag = k * jnp.exp2(jnp.minimum(-d, EXP2_CLAMP))
    k_end = k * jnp.exp2(_block_rows(ends) - b)
    pieces = []
    for i in range(N_SUB):
        for j in range(i):
            kj = k_end[SUB * j:SUB * (j + 1)]
            if j < i - 1:
                kj = kj * jnp.exp2(refs[i] - ends[j])
            pieces.append(kj.astype(BF16))
        pieces.append(k_diag[SUB * i:SUB * (i + 1)].astype(BF16))
    k_all = jnp.concatenate(pieces, axis=0)
    q_dec = (q * jnp.exp2(b)).astype(BF16)
    k_dec = (k_end * _block_rows([jnp.exp2(b_last - e) for e in ends])).astype(BF16)
    return (q, k, b), q_blk, k_all, q_dec, k_dec, jnp.exp2(b_last)


def _fix_diag(a, q, k, b):
    blocks = []
    for i in range(N_SUB):
        rs = slice(SUB * i, SUB * (i + 1))
        diag0 = _piece_offset(i) + SUB * i
        tile = diag0 // 128
        slabs = [a[rs, 128 * t:min(128 * (t + 1), A_COLS)] for t in range(pl.cdiv(A_COLS, 128))]
        slabs[tile] = _exact_diag(slabs[tile], q[rs], k[rs], b[rs], diag0 % 128)
        blocks.append(jnp.concatenate(slabs, axis=1))
    return jnp.concatenate(blocks, axis=0)


def _stack_values(v):
    return jnp.concatenate([v[0:SUB * (i + 1)] for i in range(N_SUB)], axis=0)


def _hgrn_rec_kernel(q_ref, k_ref, b_ref, bmin_ref, v_ref, o_ref, st_ref, qd_scr, kv_scr, dl_scr,
                     *, tm):
    @pl.when(pl.program_id(2) == 0)
    def _():
        st_ref[...] = jnp.zeros_like(st_ref)

    n_chunks = tm // CHUNK

    def chunk_rows(c):
        return pl.ds(pl.multiple_of(c * CHUNK, CHUNK), CHUNK)

    def state_rows(c):
        return pl.ds(pl.multiple_of(c * HEAD_DIM, HEAD_DIM), HEAD_DIM)

    def decay_rows(c, n):
        return pl.ds(pl.multiple_of(c * 8, 8), n)

    def fast():
        keep = _score_mask()

        def intra_body(c, carry):
            ccs = [c * INTRA_UNROLL + u for u in range(INTRA_UNROLL)]
            fac = [_chunk_factors(q_ref, k_ref, b_ref, chunk_rows(cc)) for cc in ccs]
            scores = [jnp.where(keep, _dot_nt(f[1], f[2]), 0.0).astype(BF16) for f in fac]
            for cc, f in zip(ccs, fac):
                kv_scr[state_rows(cc), :] = _dot_tn(v_ref[0, 0, chunk_rows(cc), :], f[4])
                qd_scr[chunk_rows(cc), :] = f[3]
                dl_scr[decay_rows(cc, 8), :] = jnp.broadcast_to(f[5], (8, HEAD_DIM))
            for cc, a in zip(ccs, scores):
                rows = chunk_rows(cc)
                o_ref[0, rows, :] = _dot(a, _stack_values(v_ref[0, 0, rows, :]))
            return carry

        lax.fori_loop(0, n_chunks // INTRA_UNROLL, intra_body, 0)

        def state_body(c, st):
            for u in range(STATE_UNROLL):
                cc = c * STATE_UNROLL + u
                rows = chunk_rows(cc)
                o_ref[0, rows, :] += _dot_nt(qd_scr[rows, :], st.astype(BF16))
                st = st * dl_scr[decay_rows(cc, 1), :] + kv_scr[state_rows(cc), :]
            return st

        st_ref[...] = lax.fori_loop(0, n_chunks // STATE_UNROLL, state_body, st_ref[...])

    def exact():
        keep = _score_mask()

        def body(c, st):
            rows = chunk_rows(c)
            (q, k, b), q_blk, k_all, q_dec, k_dec, dl = _chunk_factors(q_ref, k_ref, b_ref, rows)
            a = _fix_diag(jnp.where(keep, _dot_nt(q_blk, k_all), 0.0), q, k, b).astype(BF16)
            v = v_ref[0, 0, rows, :]
            o_ref[0, rows, :] = _dot(a, _stack_values(v)) + _dot_nt(q_dec, st.astype(BF16))
            return st * dl + _dot_tn(v, k_dec)

        st_ref[...] = lax.fori_loop(0, n_chunks, body, st_ref[...])

    clamp_may_bind = jnp.min(bmin_ref[0, 0]) < -(EXP2_CLAMP - 1.0)
    pl.when(jnp.logical_not(clamp_may_bind))(fast)
    pl.when(clamp_may_bind)(exact)


def _hgrn_rec(q, k, b2, bmin, v, *, tm=4096):
    B, H, T, _ = q.shape
    tm = min(tm, T)
    assert T % tm == 0 and (tm // CHUNK) % INTRA_UNROLL == 0 and (tm // CHUNK) % STATE_UNROLL == 0
    n_min = bmin.shape[2] // (T // tm)
    spec = pl.BlockSpec((1, 1, tm, HEAD_DIM), lambda b, h, t: (b, h, t, 0))
    return pl.pallas_call(
        functools.partial(_hgrn_rec_kernel, tm=tm),
        grid=(B, H, T // tm),
        in_specs=[spec, spec, spec,
                  pl.BlockSpec((1, 1, n_min, 8, HEAD_DIM), lambda b, h, t: (b, h, t, 0, 0)),
                  spec],
        out_specs=pl.BlockSpec((1, tm, HEAD_DIM), lambda b, h, t: (b, t, h)),
        out_shape=jax.ShapeDtypeStruct((B, T, H * HEAD_DIM), F32),
        scratch_shapes=[pltpu.VMEM((HEAD_DIM, HEAD_DIM), F32),
                        pltpu.VMEM((tm, HEAD_DIM), BF16),
                        pltpu.VMEM((tm // CHUNK * HEAD_DIM, HEAD_DIM), F32),
                        pltpu.VMEM((tm // CHUNK * 8, HEAD_DIM), F32)],
        compiler_params=_params(("parallel", "parallel", "arbitrary")),
        name="hgrn_rec",
    )(q, k, b2, bmin, v)


def _hgrn_out_kernel(o_ref, g_ref, h_ref, gain_ref, w_ref, out_ref):
    y = _rms_rows(o_ref[...], gain_ref[...]) * g_ref[...].astype(F32)
    out_ref[...] = h_ref[...] + _dot(y.astype(BF16), w_ref[...])


def _hgrn_out(o, gate, h, gain, w_out, *, tm=512):
    N, D = h.shape
    row_spec = pl.BlockSpec((tm, D), lambda i: (i, 0))
    return pl.pallas_call(
        _hgrn_out_kernel,
        grid=(N // tm,),
        in_specs=[row_spec, row_spec, row_spec,
                  pl.BlockSpec((1, D), lambda i: (0, 0)),
                  pl.BlockSpec((D, D), lambda i: (0, 0))],
        out_specs=row_spec,
        out_shape=jax.ShapeDtypeStruct((N, D), F32),
        compiler_params=_params(("parallel",)),
        name="hgrn_out",
    )(o, gate, h, gain, w_out)


def _prep_xn(h_ref, hprev_ref, nw_ref, xn_ref, first_in_batch):
    nw = nw_ref[...]
    xn_ref[HALO:, :] = _rms_rows(h_ref[...], nw).astype(BF16)
    prev = jnp.where(first_in_batch, 0.0, _rms_rows(hprev_ref[...], nw))
    xn_ref[0:HALO, :] = prev.astype(BF16)


def _conv3(u, cw, tm):
    return (cw[0:1, :] * u[HALO - 2:HALO - 2 + tm]
            + cw[1:2, :] * u[HALO - 1:HALO - 1 + tm]
            + cw[2:3, :] * u[HALO:HALO + tm])


def _ffn_kernel(h_ref, hprev_ref, nw_ref, wg_ref, wv_ref, cwg_ref, cwv_ref, wd_ref, fw_ref,
                out_ref, xn_ref, *, tm, tiles_per_batch, final_norm):
    i = pl.program_id(0)
    j = pl.program_id(1)

    @pl.when(j == 0)
    def _():
        _prep_xn(h_ref, hprev_ref, nw_ref, xn_ref, i % tiles_per_batch == 0)
        out_ref[...] = h_ref[...]

    rp = tm // ROW_PARTS
    xns = [xn_ref[p * rp:(p + 1) * rp + HALO, :] for p in range(ROW_PARTS)]
    ups = [(_dot(xn, wg_ref[...]), _dot(xn, wv_ref[...])) for xn in xns]
    acts = []
    for ug, uv in ups:
        cg = _conv3(ug, cwg_ref[...], rp)
        cv = _conv3(uv, cwv_ref[...], rp)
        acts.append((cg * _sigmoid(cg) * cv).astype(BF16))
    for p, act in enumerate(acts):
        out_ref[p * rp:(p + 1) * rp, :] += _dot(act, wd_ref[...])

    if final_norm:
        @pl.when(j == pl.num_programs(1) - 1)
        def _():
            out_ref[...] = _rms_rows(out_ref[...], fw_ref[...])


def _ffn(h, seq_len, norm_w, w_up, conv_w, w_down, final_w, *, final_norm, tm=1024, tn=512):
    N, D = h.shape
    F = w_down.shape[0]
    nf = F // tn
    tiles_per_batch = seq_len // tm
    halo_blocks = tm // HALO
    return pl.pallas_call(
        functools.partial(_ffn_kernel, tm=tm, tiles_per_batch=tiles_per_batch,
                          final_norm=final_norm),
        grid=(N // tm, nf),
        in_specs=[
            pl.BlockSpec((tm, D), lambda i, j: (i, 0), pipeline_mode=pl.Buffered(1)),
            pl.BlockSpec((HALO, D), lambda i, j: (jnp.maximum(i * halo_blocks - 1, 0), 0)),
            pl.BlockSpec((1, D), lambda i, j: (0, 0)),
            pl.BlockSpec((D, tn), lambda i, j: (0, j)),
            pl.BlockSpec((D, tn), lambda i, j: (0, nf + j)),
            pl.BlockSpec((3, tn), lambda i, j: (0, j)),
            pl.BlockSpec((3, tn), lambda i, j: (0, nf + j)),
            pl.BlockSpec((tn, D), lambda i, j: (j, 0)),
            pl.BlockSpec((1, D), lambda i, j: (0, 0)),
        ],
        out_specs=pl.BlockSpec((tm, D), lambda i, j: (i, 0)),
        out_shape=jax.ShapeDtypeStruct((N, D), F32),
        scratch_shapes=[pltpu.VMEM((tm + HALO, D), BF16)],
        compiler_params=_params(("parallel", "arbitrary")),
        name="ffn_final" if final_norm else "ffn",
    )(h, h, norm_w, w_up, w_up, conv_w, conv_w, w_down, final_w)


def _sc_kernel(h_ref, hprev_ref, nw_ref, wb_ref, wc_ref, wh_ref, cw_ref, wd_ref,
               out_ref, xn_ref, *, tm, tiles_per_batch):
    i = pl.program_id(0)
    j = pl.program_id(1)

    @pl.when(j == 0)
    def _():
        _prep_xn(h_ref, hprev_ref, nw_ref, xn_ref, i % tiles_per_batch == 0)
        out_ref[...] = h_ref[...]

    rp = tm // ROW_PARTS
    ups = []
    for p in range(ROW_PARTS):
        xn = xn_ref[p * rp:(p + 1) * rp + HALO, :]
        ups.append((_dot(xn, wc_ref[...]), _dot(xn, wh_ref[...]), _dot(xn[HALO:], wb_ref[...])))
    ys = [(gb * _conv3(uc * uh, cw_ref[...], rp)).astype(BF16) for uc, uh, gb in ups]
    for p, y in enumerate(ys):
        out_ref[p * rp:(p + 1) * rp, :] += _dot(y, wd_ref[...])


def _short_conv(h, seq_len, norm_w, w_in, conv_w, w_out, *, tm=1024, tn=256):
    N, D = h.shape
    nd = D // tn
    tiles_per_batch = seq_len // tm
    halo_blocks = tm // HALO
    w_spec = lambda part: pl.BlockSpec((D, tn), lambda i, j: (0, part * nd + j))
    return pl.pallas_call(
        functools.partial(_sc_kernel, tm=tm, tiles_per_batch=tiles_per_batch),
        grid=(N // tm, nd),
        in_specs=[
            pl.BlockSpec((tm, D), lambda i, j: (i, 0), pipeline_mode=pl.Buffered(1)),
            pl.BlockSpec((HALO, D), lambda i, j: (jnp.maximum(i * halo_blocks - 1, 0), 0)),
            pl.BlockSpec((1, D), lambda i, j: (0, 0)),
            w_spec(0), w_spec(1), w_spec(2),
            pl.BlockSpec((3, tn), lambda i, j: (0, j)),
            pl.BlockSpec((tn, D), lambda i, j: (j, 0)),
        ],
        out_specs=pl.BlockSpec((tm, D), lambda i, j: (i, 0)),
        out_shape=jax.ShapeDtypeStruct((N, D), F32),
        scratch_shapes=[pltpu.VMEM((tm + HALO, D), BF16)],
        compiler_params=_params(("parallel", "arbitrary")),
        name="short_conv",
    )(h, h, norm_w, w_in, w_in, w_in, conv_w, w_out)


def kernel(x, norm_mix, norm_ffn, hgrn_w_in, hgrn_lb_table, hgrn_out_norm, hgrn_w_out,
           sc_w_in, sc_conv, sc_w_out, ffn_w_up, ffn_conv, ffn_w_down, final_norm):
    B, T, D = x.shape
    depth = norm_mix.shape[0]
    n_mixers = 2
    row = lambda w: w.reshape(1, -1).astype(F32)
    h = x.reshape(B * T, D)
    for i in range(depth):
        j = i // n_mixers
        if i % n_mixers == 0:
            q, k, b2, bmin, v, gate = _hgrn_proj(h.reshape(B, T, D), row(norm_mix[i]),
                                                 hgrn_lb_table.astype(F32),
                                                 hgrn_w_in[j].astype(BF16), layer=i)
            o = _hgrn_rec(q, k, b2, bmin, v)
            h = _hgrn_out(o.reshape(B * T, D), gate, h, row(hgrn_out_norm[j]),
                          hgrn_w_out[j].astype(BF16))
        else:
            h = _short_conv(h, T, row(norm_mix[i]), sc_w_in[j].astype(BF16),
                            sc_conv[j].astype(F32), sc_w_out[j].astype(BF16))
        h = _ffn(h, T, row(norm_ffn[i]), ffn_w_up[i].astype(BF16), ffn_conv[i].astype(F32),
                 ffn_w_down[i].astype(BF16), row(final_norm), final_norm=(i == depth - 1))
    return h.reshape(B, T, D)
```

```python
import functools

import jax
import jax.numpy as jnp
from jax import lax
from jax.experimental import pallas as pl
from jax.experimental.pallas import tpu as pltpu

F32 = jnp.float32
BF16 = jnp.bfloat16

EPS = 1e-6
HEAD_DIM = 128
CHUNK = 64
SUB = 16
N_SUB = CHUNK // SUB
N_PIECES = N_SUB * (N_SUB + 1) // 2
A_COLS = SUB * N_PIECES
EXP2_CLAMP = 112.0
INTRA_UNROLL = 8
STATE_UNROLL = 8
HALO = 16
ROW_PARTS = 2
VMEM_LIMIT = 56 * 1024 * 1024


def _dot(a, b):
    return jnp.dot(a, b, preferred_element_type=F32)


def _dot_nt(a, b):
    return lax.dot_general(a, b, (((1,), (1,)), ((), ())), preferred_element_type=F32)


def _dot_tn(a, b):
    return lax.dot_general(a, b, (((0,), (0,)), ((), ())), preferred_element_type=F32)


def _sigmoid(x):
    return 1.0 / (1.0 + jnp.exp(-x))


def _rms_rows(x, w):
    ms = jnp.mean(x * x, axis=-1, keepdims=True)
    return x * lax.rsqrt(ms + EPS) * w


def _params(semantics):
    return pltpu.CompilerParams(dimension_semantics=semantics, vmem_limit_bytes=VMEM_LIMIT)


def _col_blocks(w, tn):
    k, n = w.shape
    return w.astype(BF16).reshape(k, n // tn, tn).transpose(1, 0, 2)


def _chunk_cumsum(x):
    n, d = x.shape
    pos = lax.broadcasted_iota(jnp.int32, (n, d), 0) & (CHUNK - 1)
    s = 1
    while s < CHUNK:
        pad = max(8, s)
        xp = jnp.concatenate([jnp.zeros((pad, d), x.dtype), x], axis=0)
        x = x + jnp.where(pos >= s, xp[pad - s:pad - s + n], 0.0)
        s *= 2
    return x


def _hgrn_proj_kernel(x_ref, nw_ref, lbt_ref, wq_ref, wf_ref, wv_ref, wg_ref,
                      q_ref, k_ref, b_ref, bmin_ref, v_ref, g_ref, xn_ref, *, layer, heads_per_step):
    @pl.when(pl.program_id(2) == 0)
    def _():
        xn_ref[...] = _rms_rows(x_ref[0], nw_ref[...]).astype(BF16)

    xn = xn_ref[...]

    rows = [lbt_ref[l:l + 1, :] for l in range(lbt_ref.shape[0])]
    m = functools.reduce(jnp.maximum, rows)
    es = [jnp.exp(r - m) for r in rows]
    lb = sum(es[:layer + 1]) / sum(es)

    q = _dot(xn, wq_ref[...])
    q = q * _sigmoid(q) * (HEAD_DIM ** -0.5)
    f = lb + (1.0 - lb) * _sigmoid(_dot(xn, wf_ref[...]))
    b2 = _chunk_cumsum(jnp.log2(f))
    k = 1.0 - f
    v = _dot(xn, wv_ref[...])
    g = _dot(xn, wg_ref[...])
    g_ref[...] = (g * _sigmoid(g)).astype(BF16)
    for hh in range(heads_per_step):
        sl = slice(hh * HEAD_DIM, (hh + 1) * HEAD_DIM)
        q_ref[0, hh] = q[:, sl].astype(BF16)
        k_ref[0, hh] = k[:, sl].astype(BF16)
        b_ref[0, hh] = b2[:, sl]
        bmin_ref[0, hh, 0] = jnp.min(b2[:, sl].reshape(-1, 8, HEAD_DIM), axis=0)
        v_ref[0, hh] = v[:, sl].astype(BF16)


def _hgrn_proj(x, norm_w, lb_table, w_in, *, layer, tm=1024, heads_per_step=2):
    B, T, D = x.shape
    H = D // HEAD_DIM
    tn = heads_per_step * HEAD_DIM
    nq = D // tn
    grid = (B, T // tm, H // heads_per_step)
    head_shape = (B, H, T, HEAD_DIM)
    head_spec = pl.BlockSpec((1, heads_per_step, tm, HEAD_DIM), lambda b, i, j: (b, j, i, 0))
    w_spec = lambda part: pl.BlockSpec((None, D, tn), lambda b, i, j: (part * nq + j, 0, 0))
    return pl.pallas_call(
        functools.partial(_hgrn_proj_kernel, layer=layer, heads_per_step=heads_per_step),
        grid=grid,
        in_specs=[
            pl.BlockSpec((1, tm, D), lambda b, i, j: (b, i, 0)),
            pl.BlockSpec((1, D), lambda b, i, j: (0, 0)),
            pl.BlockSpec((lb_table.shape[0], tn), lambda b, i, j: (0, j)),
            w_spec(0), w_spec(1), w_spec(2), w_spec(3),
        ],
        out_specs=[
            head_spec, head_spec, head_spec,
            pl.BlockSpec((1, heads_per_step, 1, 8, HEAD_DIM), lambda b, i, j: (b, j, i, 0, 0)),
            head_spec,
            pl.BlockSpec((tm, tn), lambda b, i, j: (b * (T // tm) + i, j)),
        ],
        out_shape=[
            jax.ShapeDtypeStruct(head_shape, BF16),
            jax.ShapeDtypeStruct(head_shape, BF16),
            jax.ShapeDtypeStruct(head_shape, F32),
            jax.ShapeDtypeStruct((B, H, T // tm, 8, HEAD_DIM), F32),
            jax.ShapeDtypeStruct(head_shape, BF16),
            jax.ShapeDtypeStruct((B * T, D), BF16),
        ],
        scratch_shapes=[pltpu.VMEM((tm, D), BF16)],
        compiler_params=_params(("parallel", "parallel", "arbitrary")),
        name="hgrn_proj",
    )(x, norm_w, lb_table, *([_col_blocks(w_in, tn)] * 4))


def _piece_offset(i):
    return SUB * i * (i + 1) // 2


def _score_mask():
    rw = lax.broadcasted_iota(jnp.int32, (CHUNK, A_COLS), 0)
    cw = lax.broadcasted_iota(jnp.int32, (CHUNK, A_COLS), 1)
    keep = None
    for i in range(N_SUB):
        diag0 = _piece_offset(i) + SUB * i
        in_rows = (rw >= SUB * i) & (rw < SUB * (i + 1))
        left = (cw >= _piece_offset(i)) & (cw < diag0)
        diag = (cw >= diag0) & (cw < diag0 + SUB) & (cw - diag0 <= rw - SUB * i)
        m = in_rows & (left | diag)
        keep = m if keep is None else keep | m
    return keep


def _exact_diag(slab, qb, kb, bb, lane0):
    lr = lax.broadcasted_iota(jnp.int32, slab.shape, 0)
    lc = lax.broadcasted_iota(jnp.int32, slab.shape, 1)
    for s in range(SUB):
        e = jnp.exp2(jnp.minimum(bb - bb[s:s + 1, :], 0.0))
        col = jnp.sum(qb * kb[s:s + 1, :] * e, axis=-1, keepdims=True)
        slab = jnp.where((lc == lane0 + s) & (lr >= s), col, slab)
    return slab


def _block_rows(rows):
    return jnp.concatenate([jnp.broadcast_to(r, (SUB, r.shape[-1])) for r in rows], axis=0)


def _chunk_factors(q_ref, k_ref, b_ref, rows):
    q = q_ref[0, 0, rows, :].astype(F32)
    k = k_ref[0, 0, rows, :].astype(F32)
    b = b_ref[0, 0, rows, :]

    ends = [b[SUB * (j + 1) - 1:SUB * (j + 1), :] for j in range(N_SUB)]
    refs = [jnp.zeros_like(ends[0])] + ends[:-1]
    b_last = ends[-1]
    d = b - _block_rows(refs)
    q_blk = (q * jnp.exp2(d)).astype(BF16)
    k_diag = k * jnp.exp2(jnp.minimum(-d, EXP2_CLAMP))
    k_end = k * jnp.exp2(_block_rows(ends) - b)
    pieces = []
    for i in range(N_SUB):
        for j in range(i):
            kj = k_end[SUB * j:SUB * (j + 1)]
            if j < i - 1:
                kj = kj * jnp.exp2(refs[i] - ends[j])
            pieces.append(kj.astype(BF16))
        pieces.append(k_diag[SUB * i:SUB * (i + 1)].astype(BF16))
    k_all = jnp.concatenate(pieces, axis=0)
    q_dec = (q * jnp.exp2(b)).astype(BF16)
    k_dec = (k_end * _block_rows([jnp.exp2(b_last - e) for e in ends])).astype(BF16)
    return (q, k, b), q_blk, k_all, q_dec, k_dec, jnp.exp2(b_last)


def _fix_diag(a, q, k, b):
    blocks = []
    for i in range(N_SUB):
        rs = slice(SUB * i, SUB * (i + 1))
        diag0 = _piece_offset(i) + SUB * i
        tile = diag0 // 128
        slabs = [a[rs, 128 * t:min(128 * (t + 1), A_COLS)] for t in range(pl.cdiv(A_COLS, 128))]
        slabs[tile] = _exact_diag(slabs[tile], q[rs], k[rs], b[rs], diag0 % 128)
        blocks.append(jnp.concatenate(slabs, axis=1))
    return jnp.concatenate(blocks, axis=0)


def _stack_values(v):
    return jnp.concatenate([v[0:SUB * (i + 1)] for i in range(N_SUB)], axis=0)


def _hgrn_rec_kernel(q_ref, k_ref, b_ref, bmin_ref, v_ref, o_ref, st_ref, qd_scr, kv_scr, dl_scr,
                     *, tm):
    @pl.when(pl.program_id(2) == 0)
    def _():
        st_ref[...] = jnp.zeros_like(st_ref)

    n_chunks = tm // CHUNK

    def chunk_rows(c):
        return pl.ds(pl.multiple_of(c * CHUNK, CHUNK), CHUNK)

    def state_rows(c):
        return pl.ds(pl.multiple_of(c * HEAD_DIM, HEAD_DIM), HEAD_DIM)

    def decay_rows(c, n):
        return pl.ds(pl.multiple_of(c * 8, 8), n)

    def fast():
        keep = _score_mask()

        def intra_body(c, carry):
            ccs = [c * INTRA_UNROLL + u for u in range(INTRA_UNROLL)]
            fac = [_chunk_factors(q_ref, k_ref, b_ref, chunk_rows(cc)) for cc in ccs]
            scores = [jnp.where(keep, _dot_nt(f[1], f[2]), 0.0).astype(BF16) for f in fac]
            for cc, f in zip(ccs, fac):
                kv_scr[state_rows(cc), :] = _dot_tn(v_ref[0, 0, chunk_rows(cc), :], f[4])
                qd_scr[chunk_rows(cc), :] = f[3]
                dl_scr[decay_rows(cc, 8), :] = jnp.broadcast_to(f[5], (8, HEAD_DIM))
            for cc, a in zip(ccs, scores):
                rows = chunk_rows(cc)
                o_ref[0, rows, :] = _dot(a, _stack_values(v_ref[0, 0, rows, :]))
            return carry

        lax.fori_loop(0, n_chunks // INTRA_UNROLL, intra_body, 0)

        def state_body(c, st):
            for u in range(STATE_UNROLL):
                cc = c * STATE_UNROLL + u
                rows = chunk_rows(cc)
                o_ref[0, rows, :] += _dot_nt(qd_scr[rows, :], st.astype(BF16))
                st = st * dl_scr[decay_rows(cc, 1), :] + kv_scr[state_rows(cc), :]
            return st

        st_ref[...] = lax.fori_loop(0, n_chunks // STATE_UNROLL, state_body, st_ref[...])

    def exact():
        keep = _score_mask()

        def body(c, st):
            rows = chunk_rows(c)
            (q, k, b), q_blk, k_all, q_dec, k_dec, dl = _chunk_factors(q_ref, k_ref, b_ref, rows)
            a = _fix_diag(jnp.where(keep, _dot_nt(q_blk, k_all), 0.0), q, k, b).astype(BF16)
            v = v_ref[0, 0, rows, :]
            o_ref[0, rows, :] = _dot(a, _stack_values(v)) + _dot_nt(q_dec, st.astype(BF16))
            return st * dl + _dot_tn(v, k_dec)

        st_ref[...] = lax.fori_loop(0, n_chunks, body, st_ref[...])

    clamp_may_bind = jnp.min(bmin_ref[0, 0]) < -(EXP2_CLAMP - 1.0)
    pl.when(jnp.logical_not(clamp_may_bind))(fast)
    pl.when(clamp_may_bind)(exact)


def _hgrn_rec(q, k, b2, bmin, v, *, tm=4096):
    B, H, T, _ = q.shape
    tm = min(tm, T)
    assert T % tm == 0 and (tm // CHUNK) % INTRA_UNROLL == 0 and (tm // CHUNK) % STATE_UNROLL == 0
    n_min = bmin.shape[2] // (T // tm)
    spec = pl.BlockSpec((1, 1, tm, HEAD_DIM), lambda b, h, t: (b, h, t, 0))
    return pl.pallas_call(
        functools.partial(_hgrn_rec_kernel, tm=tm),
        grid=(B, H, T // tm),
        in_specs=[spec, spec, spec,
                  pl.BlockSpec((1, 1, n_min, 8, HEAD_DIM), lambda b, h, t: (b, h, t, 0, 0)),
                  spec],
        out_specs=pl.BlockSpec((1, tm, HEAD_DIM), lambda b, h, t: (b, t, h)),
        out_shape=jax.ShapeDtypeStruct((B, T, H * HEAD_DIM), F32),
        scratch_shapes=[pltpu.VMEM((HEAD_DIM, HEAD_DIM), F32),
                        pltpu.VMEM((tm, HEAD_DIM), BF16),
                        pltpu.VMEM((tm // CHUNK * HEAD_DIM, HEAD_DIM), F32),
                        pltpu.VMEM((tm // CHUNK * 8, HEAD_DIM), F32)],
        compiler_params=_params(("parallel", "parallel", "arbitrary")),
        name="hgrn_rec",
    )(q, k, b2, bmin, v)


def _hgrn_out_kernel(o_ref, g_ref, h_ref, gain_ref, w_ref, out_ref):
    y = _rms_rows(o_ref[...], gain_ref[...]) * g_ref[...].astype(F32)
    out_ref[...] = h_ref[...] + _dot(y.astype(BF16), w_ref[...])


def _hgrn_out(o, gate, h, gain, w_out, *, tm=512):
    N, D = h.shape
    row_spec = pl.BlockSpec((tm, D), lambda i: (i, 0))
    return pl.pallas_call(
        _hgrn_out_kernel,
        grid=(N // tm,),
        in_specs=[row_spec, row_spec, row_spec,
                  pl.BlockSpec((1, D), lambda i: (0, 0)),
                  pl.BlockSpec((D, D), lambda i: (0, 0))],
        out_specs=row_spec,
        out_shape=jax.ShapeDtypeStruct((N, D), F32),
        compiler_params=_params(("parallel",)),
        name="hgrn_out",
    )(o, gate, h, gain, w_out)


def _prep_xn(h_ref, hprev_ref, nw_ref, xn_ref, first_in_batch):
    nw = nw_ref[...]
    xn_ref[HALO:, :] = _rms_rows(h_ref[...], nw).astype(BF16)
    prev = jnp.where(first_in_batch, 0.0, _rms_rows(hprev_ref[...], nw))
    xn_ref[0:HALO, :] = prev.astype(BF16)


def _conv3(u, cw, tm):
    return (cw[0:1, :] * u[HALO - 2:HALO - 2 + tm]
            + cw[1:2, :] * u[HALO - 1:HALO - 1 + tm]
            + cw[2:3, :] * u[HALO:HALO + tm])


def _ffn_kernel(h_ref, hprev_ref, nw_ref, wg_ref, wv_ref, cwg_ref, cwv_ref, wd_ref, fw_ref,
                out_ref, xn_ref, *, tm, tiles_per_batch, final_norm):
    i = pl.program_id(0)
    j = pl.program_id(1)

    @pl.when(j == 0)
    def _():
        _prep_xn(h_ref, hprev_ref, nw_ref, xn_ref, i % tiles_per_batch == 0)
        out_ref[...] = h_ref[...]

    rp = tm // ROW_PARTS
    xns = [xn_ref[p * rp:(p + 1) * rp + HALO, :] for p in range(ROW_PARTS)]
    ups = [(_dot(xn, wg_ref[...]), _dot(xn, wv_ref[...])) for xn in xns]
    acts = []
    for ug, uv in ups:
        cg = _conv3(ug, cwg_ref[...], rp)
        cv = _conv3(uv, cwv_ref[...], rp)
        acts.append((cg * _sigmoid(cg) * cv).astype(BF16))
    for p, act in enumerate(acts):
        out_ref[p * rp:(p + 1) * rp, :] += _dot(act, wd_ref[...])

    if final_norm:
        @pl.when(j == pl.num_programs(1) - 1)
        def _():
            out_ref[...] = _rms_rows(out_ref[...], fw_ref[...])


def _ffn(h, seq_len, norm_w, w_up, conv_w, w_down, final_w, *, final_norm, tm=1024, tn=512):
    N, D = h.shape
    F = w_down.shape[0]
    nf = F // tn
    tiles_per_batch = seq_len // tm
    halo_blocks = tm // HALO
    return pl.pallas_call(
        functools.partial(_ffn_kernel, tm=tm, tiles_per_batch=tiles_per_batch,
                          final_norm=final_norm),
        grid=(N // tm, nf),
        in_specs=[
            pl.BlockSpec((tm, D), lambda i, j: (i, 0), pipeline_mode=pl.Buffered(1)),
            pl.BlockSpec((HALO, D), lambda i, j: (jnp.maximum(i * halo_blocks - 1, 0), 0)),
            pl.BlockSpec((1, D), lambda i, j: (0, 0)),
            pl.BlockSpec((None, D, tn), lambda i, j: (j, 0, 0)),
            pl.BlockSpec((None, D, tn), lambda i, j: (nf + j, 0, 0)),
            pl.BlockSpec((3, tn), lambda i, j: (0, j)),
            pl.BlockSpec((3, tn), lambda i, j: (0, nf + j)),
            pl.BlockSpec((tn, D), lambda i, j: (j, 0)),
            pl.BlockSpec((1, D), lambda i, j: (0, 0)),
        ],
        out_specs=pl.BlockSpec((tm, D), lambda i, j: (i, 0)),
        out_shape=jax.ShapeDtypeStruct((N, D), F32),
        scratch_shapes=[pltpu.VMEM((tm + HALO, D), BF16)],
        compiler_params=_params(("parallel", "arbitrary")),
        name="ffn_final" if final_norm else "ffn",
    )(h, h, norm_w, *([_col_blocks(w_up, tn)] * 2), conv_w, conv_w, w_down.astype(BF16), final_w)


def _sc_kernel(h_ref, hprev_ref, nw_ref, wb_ref, wc_ref, wh_ref, cw_ref, wd_ref,
               out_ref, xn_ref, *, tm, tiles_per_batch):
    i = pl.program_id(0)
    j = pl.program_id(1)

    @pl.when(j == 0)
    def _():
        _prep_xn(h_ref, hprev_ref, nw_ref, xn_ref, i % tiles_per_batch == 0)
        out_ref[...] = h_ref[...]

    rp = tm // ROW_PARTS
    ups = []
    for p in range(ROW_PARTS):
        xn = xn_ref[p * rp:(p + 1) * rp + HALO, :]
        ups.append((_dot(xn, wc_ref[...]), _dot(xn, wh_ref[...]), _dot(xn[HALO:], wb_ref[...])))
    ys = [(gb * _conv3(uc * uh, cw_ref[...], rp)).astype(BF16) for uc, uh, gb in ups]
    for p, y in enumerate(ys):
        out_ref[p * rp:(p + 1) * rp, :] += _dot(y, wd_ref[...])


def _short_conv(h, seq_len, norm_w, w_in, conv_w, w_out, *, tm=1024, tn=256):
    N, D = h.shape
    nd = D // tn
    tiles_per_batch = seq_len // tm
    halo_blocks = tm // HALO
    w_spec = lambda part: pl.BlockSpec((None, D, tn), lambda i, j: (part * nd + j, 0, 0))
    return pl.pallas_call(
        functools.partial(_sc_kernel, tm=tm, tiles_per_batch=tiles_per_batch),
        grid=(N // tm, nd),
        in_specs=[
            pl.BlockSpec((tm, D), lambda i, j: (i, 0), pipeline_mode=pl.Buffered(1)),
            pl.BlockSpec((HALO, D), lambda i, j: (jnp.maximum(i * halo_blocks - 1, 0), 0)),
            pl.BlockSpec((1, D), lambda i, j: (0, 0)),
            w_spec(0), w_spec(1), w_spec(2),
            pl.BlockSpec((3, tn), lambda i, j: (0, j)),
            pl.BlockSpec((tn, D), lambda i, j: (j, 0)),
        ],
        out_specs=pl.BlockSpec((tm, D), lambda i, j: (i, 0)),
        out_shape=jax.ShapeDtypeStruct((N, D), F32),
        scratch_shapes=[pltpu.VMEM((tm + HALO, D), BF16)],
        compiler_params=_params(("parallel", "arbitrary")),
        name="short_conv",
    )(h, h, norm_w, *([_col_blocks(w_in, tn)] * 3), conv_w, w_out.astype(BF16))


def kernel(x, norm_mix, norm_ffn, hgrn_w_in, hgrn_lb_table, hgrn_out_norm, hgrn_w_out,
           sc_w_in, sc_conv, sc_w_out, ffn_w_up, ffn_conv, ffn_w_down, final_norm):
    B, T, D = x.shape
    depth = norm_mix.shape[0]
    n_mixers = 2
    row = lambda w: w.reshape(1, -1).astype(F32)
    h = x.reshape(B * T, D)
    for i in range(depth):
        j = i // n_mixers
        if i % n_mixers == 0:
            q, k, b2, bmin, v, gate = _hgrn_proj(h.reshape(B, T, D), row(norm_mix[i]),
                                                 hgrn_lb_table.astype(F32),
                                                 hgrn_w_in[j], layer=i)
            o = _hgrn_rec(q, k, b2, bmin, v)
            h = _hgrn_out(o.reshape(B * T, D), gate, h, row(hgrn_out_norm[j]),
                          hgrn_w_out[j].astype(BF16))
        else:
            h = _short_conv(h, T, row(norm_mix[i]), sc_w_in[j], sc_conv[j].astype(F32), sc_w_out[j])
        h = _ffn(h, T, row(norm_ffn[i]), ffn_w_up[i], ffn_conv[i].astype(F32), ffn_w_down[i],
                 row(final_norm), final_norm=(i == depth - 1))
    return h.reshape(B, T, D)
```

```python
import functools

import jax
import jax.numpy as jnp
from jax import lax
from jax.experimental import pallas as pl
from jax.experimental.pallas import tpu as pltpu

F32 = jnp.float32
BF16 = jnp.bfloat16

EPS = 1e-6
HEAD_DIM = 128
CHUNK = 64
SUB = 16
N_SUB = CHUNK // SUB
N_PIECES = N_SUB * (N_SUB + 1) // 2
A_COLS = SUB * N_PIECES
EXP2_CLAMP = 112.0
INTRA_UNROLL = 8
STATE_UNROLL = 8
HALO = 16
ROW_PARTS = 2
VMEM_LIMIT = 56 * 1024 * 1024


def _dot(a, b):
    return jnp.dot(a, b, preferred_element_type=F32)


def _dot_nt(a, b):
    return lax.dot_general(a, b, (((1,), (1,)), ((), ())), preferred_element_type=F32)


def _dot_tn(a, b):
    return lax.dot_general(a, b, (((0,), (0,)), ((), ())), preferred_element_type=F32)


def _sigmoid(x):
    return 1.0 / (1.0 + jnp.exp(-x))


def _rms_rows(x, w):
    ms = jnp.mean(x * x, axis=-1, keepdims=True)
    return x * lax.rsqrt(ms + EPS) * w


def _params(semantics):
    return pltpu.CompilerParams(dimension_semantics=semantics, vmem_limit_bytes=VMEM_LIMIT)


def _bf16(w):
    return w.astype(BF16)


def _chunk_cumsum(x):
    n, d = x.shape
    pos = lax.broadcasted_iota(jnp.int32, (n, d), 0) & (CHUNK - 1)
    s = 1
    while s < CHUNK:
        pad = max(8, s)
        xp = jnp.concatenate([jnp.zeros((pad, d), x.dtype), x], axis=0)
        x = x + jnp.where(pos >= s, xp[pad - s:pad - s + n], 0.0)
        s *= 2
    return x


def _hgrn_proj_kernel(x_ref, nw_ref, lbt_ref, wq_ref, wf_ref, wv_ref, wg_ref,
                      q_ref, k_ref, b_ref, bmin_ref, v_ref, g_ref, xn_ref, *, layer, heads_per_step):
    @pl.when(pl.program_id(2) == 0)
    def _():
        xn_ref[...] = _rms_rows(x_ref[0], nw_ref[...]).astype(BF16)

    xn = xn_ref[...]

    rows = [lbt_ref[l:l + 1, :] for l in range(lbt_ref.shape[0])]
    m = functools.reduce(jnp.maximum, rows)
    es = [jnp.exp(r - m) for r in rows]
    lb = sum(es[:layer + 1]) / sum(es)

    q = _dot(xn, wq_ref[...])
    q = q * _sigmoid(q) * (HEAD_DIM ** -0.5)
    f = lb + (1.0 - lb) * _sigmoid(_dot(xn, wf_ref[...]))
    b2 = _chunk_cumsum(jnp.log2(f))
    k = 1.0 - f
    v = _dot(xn, wv_ref[...])
    g = _dot(xn, wg_ref[...])
    g_ref[...] = (g * _sigmoid(g)).astype(BF16)
    for hh in range(heads_per_step):
        sl = slice(hh * HEAD_DIM, (hh + 1) * HEAD_DIM)
        q_ref[0, hh] = q[:, sl].astype(BF16)
        k_ref[0, hh] = k[:, sl].astype(BF16)
        b_ref[0, hh] = b2[:, sl]
        bmin_ref[0, hh, 0] = jnp.min(b2[:, sl].reshape(-1, 8, HEAD_DIM), axis=0)
        v_ref[0, hh] = v[:, sl].astype(BF16)


def _hgrn_proj(x, norm_w, lb_table, w_in, *, layer, tm=1024, heads_per_step=2):
    B, T, D = x.shape
    H = D // HEAD_DIM
    tn = heads_per_step * HEAD_DIM
    nq = D // tn
    grid = (B, T // tm, H // heads_per_step)
    head_shape = (B, H, T, HEAD_DIM)
    head_spec = pl.BlockSpec((1, heads_per_step, tm, HEAD_DIM), lambda b, i, j: (b, j, i, 0))
    w_spec = lambda part: pl.BlockSpec((D, tn), lambda b, i, j: (0, part * nq + j))
    return pl.pallas_call(
        functools.partial(_hgrn_proj_kernel, layer=layer, heads_per_step=heads_per_step),
        grid=grid,
        in_specs=[
            pl.BlockSpec((1, tm, D), lambda b, i, j: (b, i, 0)),
            pl.BlockSpec((1, D), lambda b, i, j: (0, 0)),
            pl.BlockSpec((lb_table.shape[0], tn), lambda b, i, j: (0, j)),
            w_spec(0), w_spec(1), w_spec(2), w_spec(3),
        ],
        out_specs=[
            head_spec, head_spec, head_spec,
            pl.BlockSpec((1, heads_per_step, 1, 8, HEAD_DIM), lambda b, i, j: (b, j, i, 0, 0)),
            head_spec,
            pl.BlockSpec((tm, tn), lambda b, i, j: (b * (T // tm) + i, j)),
        ],
        out_shape=[
            jax.ShapeDtypeStruct(head_shape, BF16),
            jax.ShapeDtypeStruct(head_shape, BF16),
            jax.ShapeDtypeStruct(head_shape, F32),
            jax.ShapeDtypeStruct((B, H, T // tm, 8, HEAD_DIM), F32),
            jax.ShapeDtypeStruct(head_shape, BF16),
            jax.ShapeDtypeStruct((B * T, D), BF16),
        ],
        scratch_shapes=[pltpu.VMEM((tm, D), BF16)],
        compiler_params=_params(("parallel", "parallel", "arbitrary")),
        name="hgrn_proj",
    )(x, norm_w, lb_table, *([_bf16(w_in)] * 4))


def _piece_offset(i):
    return SUB * i * (i + 1) // 2


def _score_mask():
    rw = lax.broadcasted_iota(jnp.int32, (CHUNK, A_COLS), 0)
    cw = lax.broadcasted_iota(jnp.int32, (CHUNK, A_COLS), 1)
    keep = None
    for i in range(N_SUB):
        diag0 = _piece_offset(i) + SUB * i
        in_rows = (rw >= SUB * i) & (rw < SUB * (i + 1))
        left = (cw >= _piece_offset(i)) & (cw < diag0)
        diag = (cw >= diag0) & (cw < diag0 + SUB) & (cw - diag0 <= rw - SUB * i)
        m = in_rows & (left | diag)
        keep = m if keep is None else keep | m
    return keep


def _exact_diag(slab, qb, kb, bb, lane0):
    lr = lax.broadcasted_iota(jnp.int32, slab.shape, 0)
    lc = lax.broadcasted_iota(jnp.int32, slab.shape, 1)
    for s in range(SUB):
        e = jnp.exp2(jnp.minimum(bb - bb[s:s + 1, :], 0.0))
        col = jnp.sum(qb * kb[s:s + 1, :] * e, axis=-1, keepdims=True)
        slab = jnp.where((lc == lane0 + s) & (lr >= s), col, slab)
    return slab


def _block_rows(rows):
    return jnp.concatenate([jnp.broadcast_to(r, (SUB, r.shape[-1])) for r in rows], axis=0)


def _chunk_factors(q_ref, k_ref, b_ref, rows):
    q = q_ref[0, 0, rows, :].astype(F32)
    k = k_ref[0, 0, rows, :].astype(F32)
    b = b_ref[0, 0, rows, :]

    ends = [b[SUB * (j + 1) - 1:SUB * (j + 1), :] for j in range(N_SUB)]
    refs = [jnp.zeros_like(ends[0])] + ends[:-1]
    b_last = ends[-1]
    d = b - _block_rows(refs)
    q_blk = (q * jnp.exp2(d)).astype(BF16)
    k_diag = k * jnp.exp2(jnp.minimum(-d, EXP2_CLAMP))
    k_end = k * jnp.exp2(_block_rows(ends) - b)
    pieces = []
    for i in range(N_SUB):
        for j in range(i):
            kj = k_end[SUB * j:SUB * (j + 1)]
            if j < i - 1:
                kj = kj * jnp.exp2(refs[i] - ends[j])
            pieces.append(kj.astype(BF16))
        pieces.append(k_diag[SUB * i:SUB * (i + 1)].astype(BF16))
    k_all = jnp.concatenate(pieces, axis=0)
    q_dec = (q * jnp.exp2(b)).astype(BF16)
    k_dec = (k_end * _block_rows([jnp.exp2(b_last - e) for e in ends])).astype(BF16)
    return (q, k, b), q_blk, k_all, q_dec, k_dec, jnp.exp2(b_last)


def _fix_diag(a, q, k, b):
    blocks = []
    for i in range(N_SUB):
        rs = slice(SUB * i, SUB * (i + 1))
        diag0 = _piece_offset(i) + SUB * i
        tile = diag0 // 128
        slabs = [a[rs, 128 * t:min(128 * (t + 1), A_COLS)] for t in range(pl.cdiv(A_COLS, 128))]
        slabs[tile] = _exact_diag(slabs[tile], q[rs], k[rs], b[rs], diag0 % 128)
        blocks.append(jnp.concatenate(slabs, axis=1))
    return jnp.concatenate(blocks, axis=0)


def _stack_values(v):
    return jnp.concatenate([v[0:SUB * (i + 1)] for i in range(N_SUB)], axis=0)


def _hgrn_rec_kernel(q_ref, k_ref, b_ref, bmin_ref, v_ref, o_ref, st_ref, qd_scr, kv_scr, dl_scr,
                     *, tm):
    @pl.when(pl.program_id(2) == 0)
    def _():
        st_ref[...] = jnp.zeros_like(st_ref)

    n_chunks = tm // CHUNK

    def chunk_rows(c):
        return pl.ds(pl.multiple_of(c * CHUNK, CHUNK), CHUNK)

    def state_rows(c):
        return pl.ds(pl.multiple_of(c * HEAD_DIM, HEAD_DIM), HEAD_DIM)

    def decay_rows(c, n):
        return pl.ds(pl.multiple_of(c * 8, 8), n)

    def fast():
        keep = _score_mask()

        def intra_body(c, carry):
            ccs = [c * INTRA_UNROLL + u for u in range(INTRA_UNROLL)]
            fac = [_chunk_factors(q_ref, k_ref, b_ref, chunk_rows(cc)) for cc in ccs]
            scores = [jnp.where(keep, _dot_nt(f[1], f[2]), 0.0).astype(BF16) for f in fac]
            for cc, f in zip(ccs, fac):
                kv_scr[state_rows(cc), :] = _dot_tn(v_ref[0, 0, chunk_rows(cc), :], f[4])
                qd_scr[chunk_rows(cc), :] = f[3]
                dl_scr[decay_rows(cc, 8), :] = jnp.broadcast_to(f[5], (8, HEAD_DIM))
            for cc, a in zip(ccs, scores):
                rows = chunk_rows(cc)
                o_ref[0, rows, :] = _dot(a, _stack_values(v_ref[0, 0, rows, :]))
            return carry

        lax.fori_loop(0, n_chunks // INTRA_UNROLL, intra_body, 0)

        def state_body(c, st):
            for u in range(STATE_UNROLL):
                cc = c * STATE_UNROLL + u
                rows = chunk_rows(cc)
                o_ref[0, rows, :] += _dot_nt(qd_scr[rows, :], st.astype(BF16))
                st = st * dl_scr[decay_rows(cc, 1), :] + kv_scr[state_rows(cc), :]
            return st

        st_ref[...] = lax.fori_loop(0, n_chunks // STATE_UNROLL, state_body, st_ref[...])

    def exact():
        keep = _score_mask()

        def body(c, st):
            rows = chunk_rows(c)
            (q, k, b), q_blk, k_all, q_dec, k_dec, dl = _chunk_factors(q_ref, k_ref, b_ref, rows)
            a = _fix_diag(jnp.where(keep, _dot_nt(q_blk, k_all), 0.0), q, k, b).astype(BF16)
            v = v_ref[0, 0, rows, :]
            o_ref[0, rows, :] = _dot(a, _stack_values(v)) + _dot_nt(q_dec, st.astype(BF16))
            return st * dl + _dot_tn(v, k_dec)

        st_ref[...] = lax.fori_loop(0, n_chunks, body, st_ref[...])

    clamp_may_bind = jnp.min(bmin_ref[0, 0]) < -(EXP2_CLAMP - 1.0)
    pl.when(jnp.logical_not(clamp_may_bind))(fast)
    pl.when(clamp_may_bind)(exact)


def _hgrn_rec(q, k, b2, bmin, v, *, tm=4096):
    B, H, T, _ = q.shape
    tm = min(tm, T)
    assert T % tm == 0 and (tm // CHUNK) % INTRA_UNROLL == 0 and (tm // CHUNK) % STATE_UNROLL == 0
    n_min = bmin.shape[2] // (T // tm)
    spec = pl.BlockSpec((1, 1, tm, HEAD_DIM), lambda b, h, t: (b, h, t, 0))
    return pl.pallas_call(
        functools.partial(_hgrn_rec_kernel, tm=tm),
        grid=(B, H, T // tm),
        in_specs=[spec, spec, spec,
                  pl.BlockSpec((1, 1, n_min, 8, HEAD_DIM), lambda b, h, t: (b, h, t, 0, 0)),
                  spec],
        out_specs=pl.BlockSpec((1, tm, HEAD_DIM), lambda b, h, t: (b, t, h)),
        out_shape=jax.ShapeDtypeStruct((B, T, H * HEAD_DIM), F32),
        scratch_shapes=[pltpu.VMEM((HEAD_DIM, HEAD_DIM), F32),
                        pltpu.VMEM((tm, HEAD_DIM), BF16),
                        pltpu.VMEM((tm // CHUNK * HEAD_DIM, HEAD_DIM), F32),
                        pltpu.VMEM((tm // CHUNK * 8, HEAD_DIM), F32)],
        compiler_params=_params(("parallel", "parallel", "arbitrary")),
        name="hgrn_rec",
    )(q, k, b2, bmin, v)


def _hgrn_out_kernel(o_ref, g_ref, h_ref, gain_ref, w_ref, out_ref):
    y = _rms_rows(o_ref[...], gain_ref[...]) * g_ref[...].astype(F32)
    out_ref[...] = h_ref[...] + _dot(y.astype(BF16), w_ref[...])


def _hgrn_out(o, gate, h, gain, w_out, *, tm=512):
    N, D = h.shape
    row_spec = pl.BlockSpec((tm, D), lambda i: (i, 0))
    return pl.pallas_call(
        _hgrn_out_kernel,
        grid=(N // tm,),
        in_specs=[row_spec, row_spec, row_spec,
                  pl.BlockSpec((1, D), lambda i: (0, 0)),
                  pl.BlockSpec((D, D), lambda i: (0, 0))],
        out_specs=row_spec,
        out_shape=jax.ShapeDtypeStruct((N, D), F32),
        compiler_params=_params(("parallel",)),
        name="hgrn_out",
    )(o, gate, h, gain, w_out)


def _prep_xn(h_ref, hprev_ref, nw_ref, xn_ref, first_in_batch):
    nw = nw_ref[...]
    xn_ref[HALO:, :] = _rms_rows(h_ref[...], nw).astype(BF16)
    prev = jnp.where(first_in_batch, 0.0, _rms_rows(hprev_ref[...], nw))
    xn_ref[0:HALO, :] = prev.astype(BF16)


def _conv3(u, cw, tm):
    return (cw[0:1, :] * u[HALO - 2:HALO - 2 + tm]
            + cw[1:2, :] * u[HALO - 1:HALO - 1 + tm]
            + cw[2:3, :] * u[HALO:HALO + tm])


def _ffn_kernel(h_ref, hprev_ref, nw_ref, wg_ref, wv_ref, cwg_ref, cwv_ref, wd_ref, fw_ref,
                out_ref, xn_ref, *, tm, tiles_per_batch, final_norm):
    i = pl.program_id(0)
    j = pl.program_id(1)

    @pl.when(j == 0)
    def _():
        _prep_xn(h_ref, hprev_ref, nw_ref, xn_ref, i % tiles_per_batch == 0)
        out_ref[...] = h_ref[...]

    rp = tm // ROW_PARTS
    xns = [xn_ref[p * rp:(p + 1) * rp + HALO, :] for p in range(ROW_PARTS)]
    ups = [(_dot(xn, wg_ref[...]), _dot(xn, wv_ref[...])) for xn in xns]
    acts = []
    for ug, uv in ups:
        cg = _conv3(ug, cwg_ref[...], rp)
        cv = _conv3(uv, cwv_ref[...], rp)
        acts.append((cg * _sigmoid(cg) * cv).astype(BF16))
    for p, act in enumerate(acts):
        out_ref[p * rp:(p + 1) * rp, :] += _dot(act, wd_ref[...])

    if final_norm:
        @pl.when(j == pl.num_programs(1) - 1)
        def _():
            out_ref[...] = _rms_rows(out_ref[...], fw_ref[...])


def _tile_spec(tm, d, single_buffer):
    if single_buffer:
        return pl.BlockSpec((tm, d), lambda i, j: (i, 0), pipeline_mode=pl.Buffered(1))
    return pl.BlockSpec((tm, d), lambda i, j: (i, 0))


def _ffn(h, seq_len, norm_w, w_up, conv_w, w_down, final_w, *, final_norm, tm=1024, tn=512,
         single_buffer=True):
    N, D = h.shape
    F = w_down.shape[0]
    nf = F // tn
    tiles_per_batch = seq_len // tm
    halo_blocks = tm // HALO
    return pl.pallas_call(
        functools.partial(_ffn_kernel, tm=tm, tiles_per_batch=tiles_per_batch,
                          final_norm=final_norm),
        grid=(N // tm, nf),
        in_specs=[
            _tile_spec(tm, D, single_buffer),
            pl.BlockSpec((HALO, D), lambda i, j: (jnp.maximum(i * halo_blocks - 1, 0), 0)),
            pl.BlockSpec((1, D), lambda i, j: (0, 0)),
            pl.BlockSpec((D, tn), lambda i, j: (0, j)),
            pl.BlockSpec((D, tn), lambda i, j: (0, nf + j)),
            pl.BlockSpec((3, tn), lambda i, j: (0, j)),
            pl.BlockSpec((3, tn), lambda i, j: (0, nf + j)),
            pl.BlockSpec((tn, D), lambda i, j: (j, 0)),
            pl.BlockSpec((1, D), lambda i, j: (0, 0)),
        ],
        out_specs=pl.BlockSpec((tm, D), lambda i, j: (i, 0)),
        out_shape=jax.ShapeDtypeStruct((N, D), F32),
        scratch_shapes=[pltpu.VMEM((tm + HALO, D), BF16)],
        compiler_params=_params(("parallel", "arbitrary")),
        name="ffn_final" if final_norm else "ffn",
    )(h, h, norm_w, *([_bf16(w_up)] * 2), conv_w, conv_w, _bf16(w_down), final_w)


def _sc_kernel(h_ref, hprev_ref, nw_ref, wb_ref, wc_ref, wh_ref, cw_ref, wd_ref,
               out_ref, xn_ref, *, tm, tiles_per_batch):
    i = pl.program_id(0)
    j = pl.program_id(1)

    @pl.when(j == 0)
    def _():
        _prep_xn(h_ref, hprev_ref, nw_ref, xn_ref, i % tiles_per_batch == 0)
        out_ref[...] = h_ref[...]

    rp = tm // ROW_PARTS
    ups = []
    for p in range(ROW_PARTS):
        xn = xn_ref[p * rp:(p + 1) * rp + HALO, :]
        ups.append((_dot(xn, wc_ref[...]), _dot(xn, wh_ref[...]), _dot(xn[HALO:], wb_ref[...])))
    ys = [(gb * _conv3(uc * uh, cw_ref[...], rp)).astype(BF16) for uc, uh, gb in ups]
    for p, y in enumerate(ys):
        out_ref[p * rp:(p + 1) * rp, :] += _dot(y, wd_ref[...])


def _short_conv(h, seq_len, norm_w, w_in, conv_w, w_out, *, tm=1024, tn=256, single_buffer=True):
    N, D = h.shape
    nd = D // tn
    tiles_per_batch = seq_len // tm
    halo_blocks = tm // HALO
    w_spec = lambda part: pl.BlockSpec((D, tn), lambda i, j: (0, part * nd + j))
    return pl.pallas_call(
        functools.partial(_sc_kernel, tm=tm, tiles_per_batch=tiles_per_batch),
        grid=(N // tm, nd),
        in_specs=[
            _tile_spec(tm, D, single_buffer),
            pl.BlockSpec((HALO, D), lambda i, j: (jnp.maximum(i * halo_blocks - 1, 0), 0)),
            pl.BlockSpec((1, D), lambda i, j: (0, 0)),
            w_spec(0), w_spec(1), w_spec(2),
            pl.BlockSpec((3, tn), lambda i, j: (0, j)),
            pl.BlockSpec((tn, D), lambda i, j: (j, 0)),
        ],
        out_specs=pl.BlockSpec((tm, D), lambda i, j: (i, 0)),
        out_shape=jax.ShapeDtypeStruct((N, D), F32),
        scratch_shapes=[pltpu.VMEM((tm + HALO, D), BF16)],
        compiler_params=_params(("parallel", "arbitrary")),
        name="short_conv",
    )(h, h, norm_w, *([_bf16(w_in)] * 3), conv_w, _bf16(w_out))


def kernel(x, norm_mix, norm_ffn, hgrn_w_in, hgrn_lb_table, hgrn_out_norm, hgrn_w_out,
           sc_w_in, sc_conv, sc_w_out, ffn_w_up, ffn_conv, ffn_w_down, final_norm):
    B, T, D = x.shape
    depth = norm_mix.shape[0]
    n_mixers = 2
    row = lambda w: w.reshape(1, -1).astype(F32)
    h = x.reshape(B * T, D)
    for i in range(depth):
        j = i // n_mixers
        if i % n_mixers == 0:
            q, k, b2, bmin, v, gate = _hgrn_proj(h.reshape(B, T, D), row(norm_mix[i]),
                                                 hgrn_lb_table.astype(F32),
                                                 hgrn_w_in[j], layer=i)
            o = _hgrn_rec(q, k, b2, bmin, v)
            h = _hgrn_out(o.reshape(B * T, D), gate, h, row(hgrn_out_norm[j]),
                          hgrn_w_out[j].astype(BF16))
        else:
            h = _short_conv(h, T, row(norm_mix[i]), sc_w_in[j], sc_conv[j].astype(F32), sc_w_out[j],
                            tm=512, tn=512, single_buffer=False)
        cfg = dict(tm=512, tn=512, single_buffer=False) if i == 0 else {}
        h = _ffn(h, T, row(norm_ffn[i]), ffn_w_up[i], ffn_conv[i].astype(F32), ffn_w_down[i],
                 row(final_norm), final_norm=(i == depth - 1), **cfg)
    return h.reshape(B, T, D)
```

```python
import functools

import jax
import jax.numpy as jnp
from jax import lax
from jax.experimental import pallas as pl
from jax.experimental.pallas import tpu as pltpu

F32 = jnp.float32
BF16 = jnp.bfloat16

EPS = 1e-6
HEAD_DIM = 128
CHUNK = 64
SUB = 16
N_SUB = CHUNK // SUB
N_PIECES = N_SUB * (N_SUB + 1) // 2
A_COLS = SUB * N_PIECES
EXP2_CLAMP = 112.0
INTRA_UNROLL = 8
STATE_UNROLL = 8
HALO = 16
ROW_PARTS = 2
VMEM_LIMIT = 56 * 1024 * 1024


def _dot(a, b):
    return jnp.dot(a, b, preferred_element_type=F32)


def _dot_nt(a, b):
    return lax.dot_general(a, b, (((1,), (1,)), ((), ())), preferred_element_type=F32)


def _dot_tn(a, b):
    return lax.dot_general(a, b, (((0,), (0,)), ((), ())), preferred_element_type=F32)


def _sigmoid(x):
    return 1.0 / (1.0 + jnp.exp(-x))


def _rms_rows(x, w):
    ms = jnp.mean(x * x, axis=-1, keepdims=True)
    return x * lax.rsqrt(ms + EPS) * w


def _params(semantics, vmem_limit=VMEM_LIMIT):
    return pltpu.CompilerParams(dimension_semantics=semantics, vmem_limit_bytes=vmem_limit)


def _bf16(w):
    return w.astype(BF16)


def _chunk_cumsum(x):
    n, d = x.shape
    pos = lax.broadcasted_iota(jnp.int32, (n, d), 0) & (CHUNK - 1)
    s = 1
    while s < CHUNK:
        pad = max(8, s)
        xp = jnp.concatenate([jnp.zeros((pad, d), x.dtype), x], axis=0)
        x = x + jnp.where(pos >= s, xp[pad - s:pad - s + n], 0.0)
        s *= 2
    return x


def _hgrn_proj_kernel(x_ref, nw_ref, lbt_ref, wq_ref, wf_ref, wv_ref, wg_ref,
                      q_ref, k_ref, b_ref, bmin_ref, v_ref, g_ref, xn_ref, *, layer, heads_per_step):
    @pl.when(pl.program_id(2) == 0)
    def _():
        xn_ref[...] = _rms_rows(x_ref[0], nw_ref[...]).astype(BF16)

    xn = xn_ref[...]

    rows = [lbt_ref[l:l + 1, :] for l in range(lbt_ref.shape[0])]
    m = functools.reduce(jnp.maximum, rows)
    es = [jnp.exp(r - m) for r in rows]
    lb = sum(es[:layer + 1]) / sum(es)

    q = _dot(xn, wq_ref[...])
    q = q * _sigmoid(q) * (HEAD_DIM ** -0.5)
    f = lb + (1.0 - lb) * _sigmoid(_dot(xn, wf_ref[...]))
    b2 = _chunk_cumsum(jnp.log2(f))
    k = 1.0 - f
    v = _dot(xn, wv_ref[...])
    g = _dot(xn, wg_ref[...])
    g_ref[...] = (g * _sigmoid(g)).astype(BF16)
    for hh in range(heads_per_step):
        sl = slice(hh * HEAD_DIM, (hh + 1) * HEAD_DIM)
        q_ref[0, hh] = q[:, sl].astype(BF16)
        k_ref[0, hh] = k[:, sl].astype(BF16)
        b_ref[0, hh] = b2[:, sl]
        bmin_ref[0, hh, 0] = jnp.min(b2[:, sl].reshape(-1, 8, HEAD_DIM), axis=0)
        v_ref[0, hh] = v[:, sl].astype(BF16)


def _hgrn_proj(x, norm_w, lb_table, w_in, *, layer, tm=1024, heads_per_step=4):
    B, T, D = x.shape
    H = D // HEAD_DIM
    tn = heads_per_step * HEAD_DIM
    nq = D // tn
    grid = (B, T // tm, H // heads_per_step)
    head_shape = (B, H, T, HEAD_DIM)
    head_spec = pl.BlockSpec((1, heads_per_step, tm, HEAD_DIM), lambda b, i, j: (b, j, i, 0))
    w_spec = lambda part: pl.BlockSpec((D, tn), lambda b, i, j: (0, part * nq + j))
    return pl.pallas_call(
        functools.partial(_hgrn_proj_kernel, layer=layer, heads_per_step=heads_per_step),
        grid=grid,
        in_specs=[
            pl.BlockSpec((1, tm, D), lambda b, i, j: (b, i, 0)),
            pl.BlockSpec((1, D), lambda b, i, j: (0, 0)),
            pl.BlockSpec((lb_table.shape[0], tn), lambda b, i, j: (0, j)),
            w_spec(0), w_spec(1), w_spec(2), w_spec(3),
        ],
        out_specs=[
            head_spec, head_spec, head_spec,
            pl.BlockSpec((1, heads_per_step, 1, 8, HEAD_DIM), lambda b, i, j: (b, j, i, 0, 0)),
            head_spec,
            pl.BlockSpec((tm, tn), lambda b, i, j: (b * (T // tm) + i, j)),
        ],
        out_shape=[
            jax.ShapeDtypeStruct(head_shape, BF16),
            jax.ShapeDtypeStruct(head_shape, BF16),
            jax.ShapeDtypeStruct(head_shape, F32),
            jax.ShapeDtypeStruct((B, H, T // tm, 8, HEAD_DIM), F32),
            jax.ShapeDtypeStruct(head_shape, BF16),
            jax.ShapeDtypeStruct((B * T, D), BF16),
        ],
        scratch_shapes=[pltpu.VMEM((tm, D), BF16)],
        compiler_params=_params(("parallel", "parallel", "arbitrary")),
        name="hgrn_proj",
    )(x, norm_w, lb_table, *([_bf16(w_in)] * 4))


def _piece_offset(i):
    return SUB * i * (i + 1) // 2


def _score_mask():
    rw = lax.broadcasted_iota(jnp.int32, (CHUNK, A_COLS), 0)
    cw = lax.broadcasted_iota(jnp.int32, (CHUNK, A_COLS), 1)
    keep = None
    for i in range(N_SUB):
        diag0 = _piece_offset(i) + SUB * i
        in_rows = (rw >= SUB * i) & (rw < SUB * (i + 1))
        left = (cw >= _piece_offset(i)) & (cw < diag0)
        diag = (cw >= diag0) & (cw < diag0 + SUB) & (cw - diag0 <= rw - SUB * i)
        m = in_rows & (left | diag)
        keep = m if keep is None else keep | m
    return keep


def _exact_diag(slab, qb, kb, bb, lane0):
    lr = lax.broadcasted_iota(jnp.int32, slab.shape, 0)
    lc = lax.broadcasted_iota(jnp.int32, slab.shape, 1)
    for s in range(SUB):
        e = jnp.exp2(jnp.minimum(bb - bb[s:s + 1, :], 0.0))
        col = jnp.sum(qb * kb[s:s + 1, :] * e, axis=-1, keepdims=True)
        slab = jnp.where((lc == lane0 + s) & (lr >= s), col, slab)
    return slab


def _block_rows(rows):
    return jnp.concatenate([jnp.broadcast_to(r, (SUB, r.shape[-1])) for r in rows], axis=0)


def _chunk_factors(q_ref, k_ref, b_ref, rows):
    q = q_ref[0, 0, rows, :].astype(F32)
    k = k_ref[0, 0, rows, :].astype(F32)
    b = b_ref[0, 0, rows, :]

    ends = [b[SUB * (j + 1) - 1:SUB * (j + 1), :] for j in range(N_SUB)]
    refs = [jnp.zeros_like(ends[0])] + ends[:-1]
    b_last = ends[-1]
    d = b - _block_rows(refs)
    q_blk = (q * jnp.exp2(d)).astype(BF16)
    k_diag = k * jnp.exp2(jnp.minimum(-d, EXP2_CLAMP))
    k_end = k * jnp.exp2(_block_rows(ends) - b)
    pieces = []
    for i in range(N_SUB):
        for j in range(i):
            kj = k_end[SUB * j:SUB * (j + 1)]
            if j < i - 1:
                kj = kj * jnp.exp2(refs[i] - ends[j])
            pieces.append(kj.astype(BF16))
        pieces.append(k_diag[SUB * i:SUB * (i + 1)].astype(BF16))
    k_all = jnp.concatenate(pieces, axis=0)
    q_dec = (q * jnp.exp2(b)).astype(BF16)
    k_dec = (k_end * _block_rows([jnp.exp2(b_last - e) for e in ends])).astype(BF16)
    return (q, k, b), q_blk, k_all, q_dec, k_dec, jnp.exp2(b_last)


def _fix_diag(a, q, k, b):
    blocks = []
    for i in range(N_SUB):
        rs = slice(SUB * i, SUB * (i + 1))
        diag0 = _piece_offset(i) + SUB * i
        tile = diag0 // 128
        slabs = [a[rs, 128 * t:min(128 * (t + 1), A_COLS)] for t in range(pl.cdiv(A_COLS, 128))]
        slabs[tile] = _exact_diag(slabs[tile], q[rs], k[rs], b[rs], diag0 % 128)
        blocks.append(jnp.concatenate(slabs, axis=1))
    return jnp.concatenate(blocks, axis=0)


def _stack_values(v):
    return jnp.concatenate([v[0:SUB * (i + 1)] for i in range(N_SUB)], axis=0)


def _hgrn_rec_kernel(q_ref, k_ref, b_ref, bmin_ref, v_ref, o_ref, st_ref, qd_scr, kv_scr, dl_scr,
                     *, tm):
    @pl.when(pl.program_id(2) == 0)
    def _():
        st_ref[...] = jnp.zeros_like(st_ref)

    n_chunks = tm // CHUNK

    def chunk_rows(c):
        return pl.ds(pl.multiple_of(c * CHUNK, CHUNK), CHUNK)

    def state_rows(c):
        return pl.ds(pl.multiple_of(c * HEAD_DIM, HEAD_DIM), HEAD_DIM)

    def decay_rows(c, n):
        return pl.ds(pl.multiple_of(c * 8, 8), n)

    def fast():
        keep = _score_mask()

        def intra_body(c, carry):
            ccs = [c * INTRA_UNROLL + u for u in range(INTRA_UNROLL)]
            fac = [_chunk_factors(q_ref, k_ref, b_ref, chunk_rows(cc)) for cc in ccs]
            scores = [jnp.where(keep, _dot_nt(f[1], f[2]), 0.0).astype(BF16) for f in fac]
            for cc, f in zip(ccs, fac):
                kv_scr[state_rows(cc), :] = _dot_tn(v_ref[0, 0, chunk_rows(cc), :], f[4])
                qd_scr[chunk_rows(cc), :] = f[3]
                dl_scr[decay_rows(cc, 8), :] = jnp.broadcast_to(f[5], (8, HEAD_DIM))
            for cc, a in zip(ccs, scores):
                rows = chunk_rows(cc)
                o_ref[0, rows, :] = _dot(a, _stack_values(v_ref[0, 0, rows, :]))
            return carry

        lax.fori_loop(0, n_chunks // INTRA_UNROLL, intra_body, 0)

        def state_body(c, st):
            for u in range(STATE_UNROLL):
                cc = c * STATE_UNROLL + u
                rows = chunk_rows(cc)
                o_ref[0, rows, :] += _dot_nt(qd_scr[rows, :], st.astype(BF16))
                st = st * dl_scr[decay_rows(cc, 1), :] + kv_scr[state_rows(cc), :]
            return st

        st_ref[...] = lax.fori_loop(0, n_chunks // STATE_UNROLL, state_body, st_ref[...])

    def exact():
        keep = _score_mask()

        def body(c, st):
            rows = chunk_rows(c)
            (q, k, b), q_blk, k_all, q_dec, k_dec, dl = _chunk_factors(q_ref, k_ref, b_ref, rows)
            a = _fix_diag(jnp.where(keep, _dot_nt(q_blk, k_all), 0.0), q, k, b).astype(BF16)
            v = v_ref[0, 0, rows, :]
            o_ref[0, rows, :] = _dot(a, _stack_values(v)) + _dot_nt(q_dec, st.astype(BF16))
            return st * dl + _dot_tn(v, k_dec)

        st_ref[...] = lax.fori_loop(0, n_chunks, body, st_ref[...])

    clamp_may_bind = jnp.min(bmin_ref[0, 0]) < -(EXP2_CLAMP - 1.0)
    pl.when(jnp.logical_not(clamp_may_bind))(fast)
    pl.when(clamp_may_bind)(exact)


def _hgrn_rec(q, k, b2, bmin, v, *, tm=4096):
    B, H, T, _ = q.shape
    tm = min(tm, T)
    assert T % tm == 0 and (tm // CHUNK) % INTRA_UNROLL == 0 and (tm // CHUNK) % STATE_UNROLL == 0
    n_min = bmin.shape[2] // (T // tm)
    spec = pl.BlockSpec((1, 1, tm, HEAD_DIM), lambda b, h, t: (b, h, t, 0))
    return pl.pallas_call(
        functools.partial(_hgrn_rec_kernel, tm=tm),
        grid=(B, H, T // tm),
        in_specs=[spec, spec, spec,
                  pl.BlockSpec((1, 1, n_min, 8, HEAD_DIM), lambda b, h, t: (b, h, t, 0, 0)),
                  spec],
        out_specs=pl.BlockSpec((1, tm, HEAD_DIM), lambda b, h, t: (b, t, h)),
        out_shape=jax.ShapeDtypeStruct((B, T, H * HEAD_DIM), F32),
        scratch_shapes=[pltpu.VMEM((HEAD_DIM, HEAD_DIM), F32),
                        pltpu.VMEM((tm, HEAD_DIM), BF16),
                        pltpu.VMEM((tm // CHUNK * HEAD_DIM, HEAD_DIM), F32),
                        pltpu.VMEM((tm // CHUNK * 8, HEAD_DIM), F32)],
        compiler_params=_params(("parallel", "parallel", "arbitrary")),
        name="hgrn_rec",
    )(q, k, b2, bmin, v)


def _hgrn_out_kernel(o_ref, g_ref, h_ref, gain_ref, w_ref, out_ref):
    y = _rms_rows(o_ref[...], gain_ref[...]) * g_ref[...].astype(F32)
    out_ref[...] = h_ref[...] + _dot(y.astype(BF16), w_ref[...])


def _hgrn_out(o, gate, h, gain, w_out, *, tm=512):
    N, D = h.shape
    row_spec = pl.BlockSpec((tm, D), lambda i: (i, 0))
    return pl.pallas_call(
        _hgrn_out_kernel,
        grid=(N // tm,),
        in_specs=[row_spec, row_spec, row_spec,
                  pl.BlockSpec((1, D), lambda i: (0, 0)),
                  pl.BlockSpec((D, D), lambda i: (0, 0))],
        out_specs=row_spec,
        out_shape=jax.ShapeDtypeStruct((N, D), F32),
        compiler_params=_params(("parallel",)),
        name="hgrn_out",
    )(o, gate, h, gain, w_out)


def _prep_xn(h_ref, hprev_ref, nw_ref, xn_ref, first_in_batch):
    nw = nw_ref[...]
    xn_ref[HALO:, :] = _rms_rows(h_ref[...], nw).astype(BF16)
    prev = jnp.where(first_in_batch, 0.0, _rms_rows(hprev_ref[...], nw))
    xn_ref[0:HALO, :] = prev.astype(BF16)


def _conv3(u, cw, tm):
    return (cw[0:1, :] * u[HALO - 2:HALO - 2 + tm]
            + cw[1:2, :] * u[HALO - 1:HALO - 1 + tm]
            + cw[2:3, :] * u[HALO:HALO + tm])


def _ffn_kernel(h_ref, hprev_ref, nw_ref, wg_ref, wv_ref, cwg_ref, cwv_ref, wd_ref, fw_ref,
                out_ref, xn_ref, *, tm, tiles_per_batch, final_norm, row_parts):
    i = pl.program_id(0)
    j = pl.program_id(1)

    @pl.when(j == 0)
    def _():
        _prep_xn(h_ref, hprev_ref, nw_ref, xn_ref, i % tiles_per_batch == 0)
        out_ref[...] = h_ref[...]

    rp = tm // row_parts
    xns = [xn_ref[p * rp:(p + 1) * rp + HALO, :] for p in range(row_parts)]
    ups = [(_dot(xn, wg_ref[...]), _dot(xn, wv_ref[...])) for xn in xns]
    acts = []
    for ug, uv in ups:
        cg = _conv3(ug, cwg_ref[...], rp)
        cv = _conv3(uv, cwv_ref[...], rp)
        acts.append((cg * _sigmoid(cg) * cv).astype(BF16))
    for p, act in enumerate(acts):
        out_ref[p * rp:(p + 1) * rp, :] += _dot(act, wd_ref[...])

    if final_norm:
        @pl.when(j == pl.num_programs(1) - 1)
        def _():
            out_ref[...] = _rms_rows(out_ref[...], fw_ref[...])


def _tile_spec(tm, d, single_buffer):
    if single_buffer:
        return pl.BlockSpec((tm, d), lambda i, j: (i, 0), pipeline_mode=pl.Buffered(1))
    return pl.BlockSpec((tm, d), lambda i, j: (i, 0))


def _ffn(h, seq_len, norm_w, w_up, conv_w, w_down, final_w, *, final_norm, tm=1024, tn=512,
         single_buffer=True, row_parts=ROW_PARTS, vmem_limit=VMEM_LIMIT):
    N, D = h.shape
    F = w_down.shape[0]
    nf = F // tn
    tiles_per_batch = seq_len // tm
    halo_blocks = tm // HALO
    return pl.pallas_call(
        functools.partial(_ffn_kernel, tm=tm, tiles_per_batch=tiles_per_batch,
                          final_norm=final_norm, row_parts=row_parts),
        grid=(N // tm, nf),
        in_specs=[
            _tile_spec(tm, D, single_buffer),
            pl.BlockSpec((HALO, D), lambda i, j: (jnp.maximum(i * halo_blocks - 1, 0), 0)),
            pl.BlockSpec((1, D), lambda i, j: (0, 0)),
            pl.BlockSpec((D, tn), lambda i, j: (0, j)),
            pl.BlockSpec((D, tn), lambda i, j: (0, nf + j)),
            pl.BlockSpec((3, tn), lambda i, j: (0, j)),
            pl.BlockSpec((3, tn), lambda i, j: (0, nf + j)),
            pl.BlockSpec((tn, D), lambda i, j: (j, 0)),
            pl.BlockSpec((1, D), lambda i, j: (0, 0)),
        ],
        out_specs=pl.BlockSpec((tm, D), lambda i, j: (i, 0)),
        out_shape=jax.ShapeDtypeStruct((N, D), F32),
        scratch_shapes=[pltpu.VMEM((tm + HALO, D), BF16)],
        compiler_params=_params(("parallel", "arbitrary"), vmem_limit),
        name="ffn_final" if final_norm else "ffn",
    )(h, h, norm_w, *([_bf16(w_up)] * 2), conv_w, conv_w, _bf16(w_down), final_w)


def _sc_kernel(h_ref, hprev_ref, nw_ref, wb_ref, wc_ref, wh_ref, cw_ref, wd_ref,
               out_ref, xn_ref, *, tm, tiles_per_batch, row_parts):
    i = pl.program_id(0)
    j = pl.program_id(1)

    @pl.when(j == 0)
    def _():
        _prep_xn(h_ref, hprev_ref, nw_ref, xn_ref, i % tiles_per_batch == 0)
        out_ref[...] = h_ref[...]

    rp = tm // row_parts
    ups = []
    for p in range(row_parts):
        xn = xn_ref[p * rp:(p + 1) * rp + HALO, :]
        ups.append((_dot(xn, wc_ref[...]), _dot(xn, wh_ref[...]), _dot(xn[HALO:], wb_ref[...])))
    ys = [(gb * _conv3(uc * uh, cw_ref[...], rp)).astype(BF16) for uc, uh, gb in ups]
    for p, y in enumerate(ys):
        out_ref[p * rp:(p + 1) * rp, :] += _dot(y, wd_ref[...])


def _short_conv(h, seq_len, norm_w, w_in, conv_w, w_out, *, tm=1024, tn=256, single_buffer=True,
                row_parts=ROW_PARTS, vmem_limit=VMEM_LIMIT):
    N, D = h.shape
    nd = D // tn
    tiles_per_batch = seq_len // tm
    halo_blocks = tm // HALO
    w_spec = lambda part: pl.BlockSpec((D, tn), lambda i, j: (0, part * nd + j))
    return pl.pallas_call(
        functools.partial(_sc_kernel, tm=tm, tiles_per_batch=tiles_per_batch, row_parts=row_parts),
        grid=(N // tm, nd),
        in_specs=[
            _tile_spec(tm, D, single_buffer),
            pl.BlockSpec((HALO, D), lambda i, j: (jnp.maximum(i * halo_blocks - 1, 0), 0)),
            pl.BlockSpec((1, D), lambda i, j: (0, 0)),
            w_spec(0), w_spec(1), w_spec(2),
            pl.BlockSpec((3, tn), lambda i, j: (0, j)),
            pl.BlockSpec((tn, D), lambda i, j: (j, 0)),
        ],
        out_specs=pl.BlockSpec((tm, D), lambda i, j: (i, 0)),
        out_shape=jax.ShapeDtypeStruct((N, D), F32),
        scratch_shapes=[pltpu.VMEM((tm + HALO, D), BF16)],
        compiler_params=_params(("parallel", "arbitrary"), vmem_limit),
        name="short_conv",
    )(h, h, norm_w, *([_bf16(w_in)] * 3), conv_w, _bf16(w_out))


def kernel(x, norm_mix, norm_ffn, hgrn_w_in, hgrn_lb_table, hgrn_out_norm, hgrn_w_out,
           sc_w_in, sc_conv, sc_w_out, ffn_w_up, ffn_conv, ffn_w_down, final_norm):
    B, T, D = x.shape
    depth = norm_mix.shape[0]
    n_mixers = 2
    row = lambda w: w.reshape(1, -1).astype(F32)
    h = x.reshape(B * T, D)
    for i in range(depth):
        j = i // n_mixers
        if i % n_mixers == 0:
            q, k, b2, bmin, v, gate = _hgrn_proj(h.reshape(B, T, D), row(norm_mix[i]),
                                                 hgrn_lb_table.astype(F32),
                                                 hgrn_w_in[j], layer=i)
            o = _hgrn_rec(q, k, b2, bmin, v)
            h = _hgrn_out(o.reshape(B * T, D), gate, h, row(hgrn_out_norm[j]),
                          hgrn_w_out[j].astype(BF16))
        else:
            h = _short_conv(h, T, row(norm_mix[i]), sc_w_in[j], sc_conv[j].astype(F32), sc_w_out[j],
                            tm=1024, tn=512, vmem_limit=60 * 1024 * 1024)
        cfg = dict(row_parts=1) if i == 0 else {}
        h = _ffn(h, T, row(norm_ffn[i]), ffn_w_up[i], ffn_conv[i].astype(F32), ffn_w_down[i],
                 row(final_norm), final_norm=(i == depth - 1), **cfg)
    return h.reshape(B, T, D)
```

```python
import functools
import math

import jax
import jax.numpy as jnp
from jax import lax
from jax.experimental import pallas as pl
from jax.experimental.pallas import tpu as pltpu

F32 = jnp.float32
BF16 = jnp.bfloat16

EPS = 1e-6
HEAD_DIM = 128
CHUNK = 64
SUB = 16
N_SUB = CHUNK // SUB
N_PIECES = N_SUB * (N_SUB + 1) // 2
A_COLS = SUB * N_PIECES
EXP2_CLAMP = 112.0
INTRA_UNROLL = 8
STATE_UNROLL = 8
HALO = 16
ROW_PARTS = 2
VMEM_LIMIT = 56 * 1024 * 1024


def _dot(a, b):
    return jnp.dot(a, b, preferred_element_type=F32)


def _dot_nt(a, b):
    return lax.dot_general(a, b, (((1,), (1,)), ((), ())), preferred_element_type=F32)


def _dot_tn(a, b):
    return lax.dot_general(a, b, (((0,), (0,)), ((), ())), preferred_element_type=F32)


def _sigmoid(x):
    return 1.0 / (1.0 + jnp.exp(-x))


def _rms_rows(x, w):
    ms = jnp.mean(x * x, axis=-1, keepdims=True)
    return x * lax.rsqrt(ms + EPS) * w


def _params(semantics, vmem_limit=VMEM_LIMIT):
    return pltpu.CompilerParams(dimension_semantics=semantics, vmem_limit_bytes=vmem_limit)


def _bf16(w):
    return w.astype(BF16)


def _side_cast_plan(weights, grid):
    n_steps = math.prod(grid)
    strides = [math.prod(grid[d + 1:]) for d in range(len(grid))]

    def index_map(*ids):
        return (sum(i * s for i, s in zip(ids, strides)), 0, 0)

    ins, specs, shapes = [], [], []
    for w in weights:
        rows, cols = w.shape
        assert rows % n_steps == 0, (w.shape, grid)
        chunk = (n_steps, rows // n_steps, cols)
        ins.append(w.reshape(chunk))
        specs.append(pl.BlockSpec((None,) + chunk[1:], index_map))
        shapes.append(jax.ShapeDtypeStruct(chunk, BF16))
    return ins, specs, shapes


def _side_cast(src_refs, dst_refs):
    for src, dst in zip(src_refs, dst_refs):
        dst[...] = src[...].astype(BF16)


def _chunk_cumsum(x):
    n, d = x.shape
    pos = lax.broadcasted_iota(jnp.int32, (n, d), 0) & (CHUNK - 1)
    s = 1
    while s < CHUNK:
        pad = max(8, s)
        xp = jnp.concatenate([jnp.zeros((pad, d), x.dtype), x], axis=0)
        x = x + jnp.where(pos >= s, xp[pad - s:pad - s + n], 0.0)
        s *= 2
    return x


def _hgrn_proj_kernel(*refs, layer, heads_per_step, n_side):
    x_ref, nw_ref, lbt_ref, wq_ref, wf_ref, wv_ref, wg_ref = refs[:7]
    q_ref, k_ref, b_ref, bmin_ref, v_ref, g_ref = refs[7 + n_side:13 + n_side]
    xn_ref = refs[-1]
    _side_cast(refs[7:7 + n_side], refs[13 + n_side:13 + 2 * n_side])

    @pl.when(pl.program_id(2) == 0)
    def _():
        xn_ref[...] = _rms_rows(x_ref[0], nw_ref[...]).astype(BF16)

    xn = xn_ref[...]

    rows = [lbt_ref[l:l + 1, :] for l in range(lbt_ref.shape[0])]
    m = functools.reduce(jnp.maximum, rows)
    es = [jnp.exp(r - m) for r in rows]
    lb = sum(es[:layer + 1]) / sum(es)

    q = _dot(xn, wq_ref[...])
    q = q * _sigmoid(q) * (HEAD_DIM ** -0.5)
    f = lb + (1.0 - lb) * _sigmoid(_dot(xn, wf_ref[...]))
    b2 = _chunk_cumsum(jnp.log2(f))
    k = 1.0 - f
    v = _dot(xn, wv_ref[...])
    g = _dot(xn, wg_ref[...])
    g_ref[...] = (g * _sigmoid(g)).astype(BF16)
    for hh in range(heads_per_step):
        sl = slice(hh * HEAD_DIM, (hh + 1) * HEAD_DIM)
        q_ref[0, hh] = q[:, sl].astype(BF16)
        k_ref[0, hh] = k[:, sl].astype(BF16)
        b_ref[0, hh] = b2[:, sl]
        bmin_ref[0, hh, 0] = jnp.min(b2[:, sl].reshape(-1, 8, HEAD_DIM), axis=0)
        v_ref[0, hh] = v[:, sl].astype(BF16)


def _hgrn_proj(x, norm_w, lb_table, w_in, side_weights, *, layer, tm=1024, heads_per_step=2):
    B, T, D = x.shape
    H = D // HEAD_DIM
    tn = heads_per_step * HEAD_DIM
    nq = D // tn
    grid = (B, T // tm, H // heads_per_step)
    head_shape = (B, H, T, HEAD_DIM)
    head_spec = pl.BlockSpec((1, heads_per_step, tm, HEAD_DIM), lambda b, i, j: (b, j, i, 0))
    w_spec = lambda part: pl.BlockSpec((D, tn), lambda b, i, j: (0, part * nq + j))
    side_in, side_specs, side_shapes = _side_cast_plan(side_weights, grid)
    outs = pl.pallas_call(
        functools.partial(_hgrn_proj_kernel, layer=layer, heads_per_step=heads_per_step,
                          n_side=len(side_in)),
        grid=grid,
        in_specs=[
            pl.BlockSpec((1, tm, D), lambda b, i, j: (b, i, 0)),
            pl.BlockSpec((1, D), lambda b, i, j: (0, 0)),
            pl.BlockSpec((lb_table.shape[0], tn), lambda b, i, j: (0, j)),
            w_spec(0), w_spec(1), w_spec(2), w_spec(3),
        ] + side_specs,
        out_specs=[
            head_spec, head_spec, head_spec,
            pl.BlockSpec((1, heads_per_step, 1, 8, HEAD_DIM), lambda b, i, j: (b, j, i, 0, 0)),
            head_spec,
            pl.BlockSpec((tm, tn), lambda b, i, j: (b * (T // tm) + i, j)),
        ] + side_specs,
        out_shape=[
            jax.ShapeDtypeStruct(head_shape, BF16),
            jax.ShapeDtypeStruct(head_shape, BF16),
            jax.ShapeDtypeStruct(head_shape, F32),
            jax.ShapeDtypeStruct((B, H, T // tm, 8, HEAD_DIM), F32),
            jax.ShapeDtypeStruct(head_shape, BF16),
            jax.ShapeDtypeStruct((B * T, D), BF16),
        ] + side_shapes,
        scratch_shapes=[pltpu.VMEM((tm, D), BF16)],
        compiler_params=_params(("parallel", "parallel", "arbitrary")),
        name="hgrn_proj",
    )(x, norm_w, lb_table, *([_bf16(w_in)] * 4), *side_in)
    return outs[:6], [o.reshape(w.shape) for o, w in zip(outs[6:], side_weights)]


def _piece_offset(i):
    return SUB * i * (i + 1) // 2


def _score_mask():
    rw = lax.broadcasted_iota(jnp.int32, (CHUNK, A_COLS), 0)
    cw = lax.broadcasted_iota(jnp.int32, (CHUNK, A_COLS), 1)
    keep = None
    for i in range(N_SUB):
        diag0 = _piece_offset(i) + SUB * i
        in_rows = (rw >= SUB * i) & (rw < SUB * (i + 1))
        left = (cw >= _piece_offset(i)) & (cw < diag0)
        diag = (cw >= diag0) & (cw < diag0 + SUB) & (cw - diag0 <= rw - SUB * i)
        m = in_rows & (left | diag)
        keep = m if keep is None else keep | m
    return keep


def _exact_diag(slab, qb, kb, bb, lane0):
    lr = lax.broadcasted_iota(jnp.int32, slab.shape, 0)
    lc = lax.broadcasted_iota(jnp.int32, slab.shape, 1)
    for s in range(SUB):
        e = jnp.exp2(jnp.minimum(bb - bb[s:s + 1, :], 0.0))
        col = jnp.sum(qb * kb[s:s + 1, :] * e, axis=-1, keepdims=True)
        slab = jnp.where((lc == lane0 + s) & (lr >= s), col, slab)
    return slab


def _block_rows(rows):
    return jnp.concatenate([jnp.broadcast_to(r, (SUB, r.shape[-1])) for r in rows], axis=0)


def _chunk_factors(q_ref, k_ref, b_ref, rows):
    q = q_ref[0, 0, rows, :].astype(F32)
    k = k_ref[0, 0, rows, :].astype(F32)
    b = b_ref[0, 0, rows, :]

    ends = [b[SUB * (j + 1) - 1:SUB * (j + 1), :] for j in range(N_SUB)]
    refs = [jnp.zeros_like(ends[0])] + ends[:-1]
    b_last = ends[-1]
    d = b - _block_rows(refs)
    q_blk = (q * jnp.exp2(d)).astype(BF16)
    k_diag = k * jnp.exp2(jnp.minimum(-d, EXP2_CLAMP))
    k_end = k * jnp.exp2(_block_rows(ends) - b)
    pieces = []
    for i in range(N_SUB):
        for j in range(i):
            kj = k_end[SUB * j:SUB * (j + 1)]
            if j < i - 1:
                kj = kj * jnp.exp2(refs[i] - ends[j])
            pieces.append(kj.astype(BF16))
        pieces.append(k_diag[SUB * i:SUB * (i + 1)].astype(BF16))
    k_all = jnp.concatenate(pieces, axis=0)
    q_dec = (q * jnp.exp2(b)).astype(BF16)
    k_dec = (k_end * _block_rows([jnp.exp2(b_last - e) for e in ends])).astype(BF16)
    return (q, k, b), q_blk, k_all, q_dec, k_dec, jnp.exp2(b_last)


def _fix_diag(a, q, k, b):
    blocks = []
    for i in range(N_SUB):
        rs = slice(SUB * i, SUB * (i + 1))
        diag0 = _piece_offset(i) + SUB * i
        tile = diag0 // 128
        slabs = [a[rs, 128 * t:min(128 * (t + 1), A_COLS)] for t in range(pl.cdiv(A_COLS, 128))]
        slabs[tile] = _exact_diag(slabs[tile], q[rs], k[rs], b[rs], diag0 % 128)
        blocks.append(jnp.concatenate(slabs, axis=1))
    return jnp.concatenate(blocks, axis=0)


def _stack_values(v):
    return jnp.concatenate([v[0:SUB * (i + 1)] for i in range(N_SUB)], axis=0)


def _hgrn_rec_kernel(*refs, tm, n_side):
    q_ref, k_ref, b_ref, bmin_ref, v_ref = refs[:5]
    o_ref = refs[5 + n_side]
    st_ref, qd_scr, kv_scr, dl_scr = refs[6 + 2 * n_side:]
    _side_cast(refs[5:5 + n_side], refs[6 + n_side:6 + 2 * n_side])

    @pl.when(pl.program_id(2) == 0)
    def _():
        st_ref[...] = jnp.zeros_like(st_ref)

    n_chunks = tm // CHUNK

    def chunk_rows(c):
        return pl.ds(pl.multiple_of(c * CHUNK, CHUNK), CHUNK)

    def state_rows(c):
        return pl.ds(pl.multiple_of(c * HEAD_DIM, HEAD_DIM), HEAD_DIM)

    def decay_rows(c, n):
        return pl.ds(pl.multiple_of(c * 8, 8), n)

    def fast():
        keep = _score_mask()

        def intra_body(c, carry):
            ccs = [c * INTRA_UNROLL + u for u in range(INTRA_UNROLL)]
            fac = [_chunk_factors(q_ref, k_ref, b_ref, chunk_rows(cc)) for cc in ccs]
            scores = [jnp.where(keep, _dot_nt(f[1], f[2]), 0.0).astype(BF16) for f in fac]
            for cc, f in zip(ccs, fac):
                kv_scr[state_rows(cc), :] = _dot_tn(v_ref[0, 0, chunk_rows(cc), :], f[4])
                qd_scr[chunk_rows(cc), :] = f[3]
                dl_scr[decay_rows(cc, 8), :] = jnp.broadcast_to(f[5], (8, HEAD_DIM))
            for cc, a in zip(ccs, scores):
                rows = chunk_rows(cc)
                o_ref[0, rows, :] = _dot(a, _stack_values(v_ref[0, 0, rows, :]))
            return carry

        lax.fori_loop(0, n_chunks // INTRA_UNROLL, intra_body, 0)

        def state_body(c, st):
            for u in range(STATE_UNROLL):
                cc = c * STATE_UNROLL + u
                rows = chunk_rows(cc)
                o_ref[0, rows, :] += _dot_nt(qd_scr[rows, :], st.astype(BF16))
                st = st * dl_scr[decay_rows(cc, 1), :] + kv_scr[state_rows(cc), :]
            return st

        st_ref[...] = lax.fori_loop(0, n_chunks // STATE_UNROLL, state_body, st_ref[...])

    def exact():
        keep = _score_mask()

        def body(c, st):
            rows = chunk_rows(c)
            (q, k, b), q_blk, k_all, q_dec, k_dec, dl = _chunk_factors(q_ref, k_ref, b_ref, rows)
            a = _fix_diag(jnp.where(keep, _dot_nt(q_blk, k_all), 0.0), q, k, b).astype(BF16)
            v = v_ref[0, 0, rows, :]
            o_ref[0, rows, :] = _dot(a, _stack_values(v)) + _dot_nt(q_dec, st.astype(BF16))
            return st * dl + _dot_tn(v, k_dec)

        st_ref[...] = lax.fori_loop(0, n_chunks, body, st_ref[...])

    clamp_may_bind = jnp.min(bmin_ref[0, 0]) < -(EXP2_CLAMP - 1.0)
    pl.when(jnp.logical_not(clamp_may_bind))(fast)
    pl.when(clamp_may_bind)(exact)


def _hgrn_rec(q, k, b2, bmin, v, side_weights, *, tm=4096):
    B, H, T, _ = q.shape
    tm = min(tm, T)
    assert T % tm == 0 and (tm // CHUNK) % INTRA_UNROLL == 0 and (tm // CHUNK) % STATE_UNROLL == 0
    n_min = bmin.shape[2] // (T // tm)
    spec = pl.BlockSpec((1, 1, tm, HEAD_DIM), lambda b, h, t: (b, h, t, 0))
    grid = (B, H, T // tm)
    side_in, side_specs, side_shapes = _side_cast_plan(side_weights, grid)
    outs = pl.pallas_call(
        functools.partial(_hgrn_rec_kernel, tm=tm, n_side=len(side_in)),
        grid=grid,
        in_specs=[spec, spec, spec,
                  pl.BlockSpec((1, 1, n_min, 8, HEAD_DIM), lambda b, h, t: (b, h, t, 0, 0)),
                  spec] + side_specs,
        out_specs=[pl.BlockSpec((1, tm, HEAD_DIM), lambda b, h, t: (b, t, h))] + side_specs,
        out_shape=[jax.ShapeDtypeStruct((B, T, H * HEAD_DIM), F32)] + side_shapes,
        scratch_shapes=[pltpu.VMEM((HEAD_DIM, HEAD_DIM), F32),
                        pltpu.VMEM((tm, HEAD_DIM), BF16),
                        pltpu.VMEM((tm // CHUNK * HEAD_DIM, HEAD_DIM), F32),
                        pltpu.VMEM((tm // CHUNK * 8, HEAD_DIM), F32)],
        compiler_params=_params(("parallel", "parallel", "arbitrary")),
        name="hgrn_rec",
    )(q, k, b2, bmin, v, *side_in)
    return outs[0], [o.reshape(w.shape) for o, w in zip(outs[1:], side_weights)]


def _hgrn_out_kernel(o_ref, g_ref, h_ref, gain_ref, w_ref, out_ref):
    y = _rms_rows(o_ref[...], gain_ref[...]) * g_ref[...].astype(F32)
    out_ref[...] = h_ref[...] + _dot(y.astype(BF16), w_ref[...])


def _hgrn_out(o, gate, h, gain, w_out, *, tm=512):
    N, D = h.shape
    row_spec = pl.BlockSpec((tm, D), lambda i: (i, 0))
    return pl.pallas_call(
        _hgrn_out_kernel,
        grid=(N // tm,),
        in_specs=[row_spec, row_spec, row_spec,
                  pl.BlockSpec((1, D), lambda i: (0, 0)),
                  pl.BlockSpec((D, D), lambda i: (0, 0))],
        out_specs=row_spec,
        out_shape=jax.ShapeDtypeStruct((N, D), F32),
        compiler_params=_params(("parallel",)),
        name="hgrn_out",
    )(o, gate, h, gain, w_out)


def _prep_xn(h_ref, hprev_ref, nw_ref, xn_ref, first_in_batch):
    nw = nw_ref[...]
    xn_ref[HALO:, :] = _rms_rows(h_ref[...], nw).astype(BF16)
    prev = jnp.where(first_in_batch, 0.0, _rms_rows(hprev_ref[...], nw))
    xn_ref[0:HALO, :] = prev.astype(BF16)


def _conv3(u, cw, tm):
    return (cw[0:1, :] * u[HALO - 2:HALO - 2 + tm]
            + cw[1:2, :] * u[HALO - 1:HALO - 1 + tm]
            + cw[2:3, :] * u[HALO:HALO + tm])


def _ffn_kernel(h_ref, hprev_ref, nw_ref, wg_ref, wv_ref, cwg_ref, cwv_ref, wd_ref, fw_ref,
                out_ref, xn_ref, *, tm, tiles_per_batch, final_norm, row_parts):
    i = pl.program_id(0)
    j = pl.program_id(1)

    @pl.when(j == 0)
    def _():
        _prep_xn(h_ref, hprev_ref, nw_ref, xn_ref, i % tiles_per_batch == 0)
        out_ref[...] = h_ref[...]

    rp = tm // row_parts
    xns = [xn_ref[p * rp:(p + 1) * rp + HALO, :] for p in range(row_parts)]
    ups = [(_dot(xn, wg_ref[...]), _dot(xn, wv_ref[...])) for xn in xns]
    acts = []
    for ug, uv in ups:
        cg = _conv3(ug, cwg_ref[...], rp)
        cv = _conv3(uv, cwv_ref[...], rp)
        acts.append((cg * _sigmoid(cg) * cv).astype(BF16))
    for p, act in enumerate(acts):
        out_ref[p * rp:(p + 1) * rp, :] += _dot(act, wd_ref[...])

    if final_norm:
        @pl.when(j == pl.num_programs(1) - 1)
        def _():
            out_ref[...] = _rms_rows(out_ref[...], fw_ref[...])


def _tile_spec(tm, d, single_buffer):
    if single_buffer:
        return pl.BlockSpec((tm, d), lambda i, j: (i, 0), pipeline_mode=pl.Buffered(1))
    return pl.BlockSpec((tm, d), lambda i, j: (i, 0))


def _ffn(h, seq_len, norm_w, w_up, conv_w, w_down, final_w, *, final_norm, tm=1024, tn=512,
         single_buffer=True, row_parts=ROW_PARTS, vmem_limit=VMEM_LIMIT):
    N, D = h.shape
    F = w_down.shape[0]
    nf = F // tn
    tiles_per_batch = seq_len // tm
    halo_blocks = tm // HALO
    return pl.pallas_call(
        functools.partial(_ffn_kernel, tm=tm, tiles_per_batch=tiles_per_batch,
                          final_norm=final_norm, row_parts=row_parts),
        grid=(N // tm, nf),
        in_specs=[
            _tile_spec(tm, D, single_buffer),
            pl.BlockSpec((HALO, D), lambda i, j: (jnp.maximum(i * halo_blocks - 1, 0), 0)),
            pl.BlockSpec((1, D), lambda i, j: (0, 0)),
            pl.BlockSpec((D, tn), lambda i, j: (0, j)),
            pl.BlockSpec((D, tn), lambda i, j: (0, nf + j)),
            pl.BlockSpec((3, tn), lambda i, j: (0, j)),
            pl.BlockSpec((3, tn), lambda i, j: (0, nf + j)),
            pl.BlockSpec((tn, D), lambda i, j: (j, 0)),
            pl.BlockSpec((1, D), lambda i, j: (0, 0)),
        ],
        out_specs=pl.BlockSpec((tm, D), lambda i, j: (i, 0)),
        out_shape=jax.ShapeDtypeStruct((N, D), F32),
        scratch_shapes=[pltpu.VMEM((tm + HALO, D), BF16)],
        compiler_params=_params(("parallel", "arbitrary"), vmem_limit),
        name="ffn_final" if final_norm else "ffn",
    )(h, h, norm_w, *([_bf16(w_up)] * 2), conv_w, conv_w, _bf16(w_down), final_w)


def _sc_kernel(h_ref, hprev_ref, nw_ref, wb_ref, wc_ref, wh_ref, cw_ref, wd_ref,
               out_ref, xn_ref, *, tm, tiles_per_batch, row_parts):
    i = pl.program_id(0)
    j = pl.program_id(1)

    @pl.when(j == 0)
    def _():
        _prep_xn(h_ref, hprev_ref, nw_ref, xn_ref, i % tiles_per_batch == 0)
        out_ref[...] = h_ref[...]

    rp = tm // row_parts
    ups = []
    for p in range(row_parts):
        xn = xn_ref[p * rp:(p + 1) * rp + HALO, :]
        ups.append((_dot(xn, wc_ref[...]), _dot(xn, wh_ref[...]), _dot(xn[HALO:], wb_ref[...])))
    ys = [(gb * _conv3(uc * uh, cw_ref[...], rp)).astype(BF16) for uc, uh, gb in ups]
    for p, y in enumerate(ys):
        out_ref[p * rp:(p + 1) * rp, :] += _dot(y, wd_ref[...])


def _short_conv(h, seq_len, norm_w, w_in, conv_w, w_out, *, tm=512, tn=512, single_buffer=False,
                row_parts=1, vmem_limit=VMEM_LIMIT):
    N, D = h.shape
    nd = D // tn
    tiles_per_batch = seq_len // tm
    halo_blocks = tm // HALO
    w_spec = lambda part: pl.BlockSpec((D, tn), lambda i, j: (0, part * nd + j))
    return pl.pallas_call(
        functools.partial(_sc_kernel, tm=tm, tiles_per_batch=tiles_per_batch, row_parts=row_parts),
        grid=(N // tm, nd),
        in_specs=[
            _tile_spec(tm, D, single_buffer),
            pl.BlockSpec((HALO, D), lambda i, j: (jnp.maximum(i * halo_blocks - 1, 0), 0)),
            pl.BlockSpec((1, D), lambda i, j: (0, 0)),
            w_spec(0), w_spec(1), w_spec(2),
            pl.BlockSpec((3, tn), lambda i, j: (0, j)),
            pl.BlockSpec((tn, D), lambda i, j: (j, 0)),
        ],
        out_specs=pl.BlockSpec((tm, D), lambda i, j: (i, 0)),
        out_shape=jax.ShapeDtypeStruct((N, D), F32),
        scratch_shapes=[pltpu.VMEM((tm + HALO, D), BF16)],
        compiler_params=_params(("parallel", "arbitrary"), vmem_limit),
        name="short_conv",
    )(h, h, norm_w, *([_bf16(w_in)] * 3), conv_w, _bf16(w_out))


def kernel(x, norm_mix, norm_ffn, hgrn_w_in, hgrn_lb_table, hgrn_out_norm, hgrn_w_out,
           sc_w_in, sc_conv, sc_w_out, ffn_w_up, ffn_conv, ffn_w_down, final_norm):
    B, T, D = x.shape
    depth = norm_mix.shape[0]
    n_mixers = 2
    row = lambda w: w.reshape(1, -1).astype(F32)
    stacks = dict(hgrn_w_in=hgrn_w_in, hgrn_w_out=hgrn_w_out, sc_w_in=sc_w_in, sc_w_out=sc_w_out,
                  ffn_w_up=ffn_w_up, ffn_w_down=ffn_w_down)

    def layer_weights(i):
        mixer = ("hgrn_w_in", "hgrn_w_out") if i % n_mixers == 0 else ("sc_w_in", "sc_w_out")
        return [(name, i // n_mixers) for name in mixer] + [("ffn_w_up", i), ("ffn_w_down", i)]

    cast = {}

    def weight(name, idx):
        return cast.get((name, idx), stacks[name][idx])

    h = x.reshape(B * T, D)
    for i in range(depth):
        j = i // n_mixers
        if i % n_mixers == 0:
            proj_side = layer_weights(i)[1:] if i == 0 else []
            rec_side = [kw for l in range(1, depth) for kw in layer_weights(l)] if i == 0 else []
            outs, done = _hgrn_proj(h.reshape(B, T, D), row(norm_mix[i]), hgrn_lb_table.astype(F32),
                                    weight("hgrn_w_in", j), [weight(*kw) for kw in proj_side], layer=i)
            cast.update(zip(proj_side, done))
            q, k, b2, bmin, v, gate = outs
            o, done = _hgrn_rec(q, k, b2, bmin, v, [weight(*kw) for kw in rec_side])
            cast.update(zip(rec_side, done))
            h = _hgrn_out(o.reshape(B * T, D), gate, h, row(hgrn_out_norm[j]),
                          weight("hgrn_w_out", j).astype(BF16))
        else:
            h = _short_conv(h, T, row(norm_mix[i]), weight("sc_w_in", j), sc_conv[j].astype(F32),
                            weight("sc_w_out", j))
        h = _ffn(h, T, row(norm_ffn[i]), weight("ffn_w_up", i), ffn_conv[i].astype(F32),
                 weight("ffn_w_down", i), row(final_norm), final_norm=(i == depth - 1))
    return h.reshape(B, T, D)
```

```python
import functools
import math

import jax
import jax.numpy as jnp
from jax import lax
from jax.experimental import pallas as pl
from jax.experimental.pallas import tpu as pltpu

F32 = jnp.float32
BF16 = jnp.bfloat16

EPS = 1e-6
HEAD_DIM = 128
CHUNK = 64
SUB = 16
N_SUB = CHUNK // SUB
N_PIECES = N_SUB * (N_SUB + 1) // 2
A_COLS = SUB * N_PIECES
EXP2_CLAMP = 112.0
INTRA_UNROLL = 8
STATE_UNROLL = 8
HALO = 16
ROW_PARTS = 2
VMEM_LIMIT = 56 * 1024 * 1024


def _dot(a, b):
    return jnp.dot(a, b, preferred_element_type=F32)


def _dot_nt(a, b):
    return lax.dot_general(a, b, (((1,), (1,)), ((), ())), preferred_element_type=F32)


def _dot_tn(a, b):
    return lax.dot_general(a, b, (((0,), (0,)), ((), ())), preferred_element_type=F32)


def _sigmoid(x):
    return 1.0 / (1.0 + jnp.exp(-x))


def _rms_rows(x, w):
    ms = jnp.mean(x * x, axis=-1, keepdims=True)
    return x * lax.rsqrt(ms + EPS) * w


def _params(semantics, vmem_limit=VMEM_LIMIT):
    return pltpu.CompilerParams(dimension_semantics=semantics, vmem_limit_bytes=vmem_limit)


def _bf16(w):
    return w.astype(BF16)


def _side_cast_plan(weights, grid):
    n_steps = math.prod(grid)
    strides = [math.prod(grid[d + 1:]) for d in range(len(grid))]

    def step(ids):
        return sum(i * s for i, s in zip(ids, strides))

    ins, in_specs, out_specs, shapes = [], [], [], []
    for stack, idx in weights:
        n, rows, cols = stack.shape
        assert rows % n_steps == 0, (stack.shape, grid)
        block = (None, rows // n_steps, cols)
        ins.append(stack.reshape(n * n_steps, rows // n_steps, cols))
        in_specs.append(pl.BlockSpec(block, lambda *ids, base=idx * n_steps: (base + step(ids), 0, 0)))
        out_specs.append(pl.BlockSpec(block, lambda *ids: (step(ids), 0, 0)))
        shapes.append(jax.ShapeDtypeStruct((n_steps, rows // n_steps, cols), BF16))
    return ins, in_specs, out_specs, shapes


def _side_cast(src_refs, dst_refs):
    for src, dst in zip(src_refs, dst_refs):
        dst[...] = src[...].astype(BF16)


def _chunk_cumsum(x):
    n, d = x.shape
    pos = lax.broadcasted_iota(jnp.int32, (n, d), 0) & (CHUNK - 1)
    s = 1
    while s < CHUNK:
        pad = max(8, s)
        xp = jnp.concatenate([jnp.zeros((pad, d), x.dtype), x], axis=0)
        x = x + jnp.where(pos >= s, xp[pad - s:pad - s + n], 0.0)
        s *= 2
    return x


def _hgrn_proj_kernel(*refs, layer, heads_per_step, n_side):
    x_ref, nw_ref, lbt_ref, wq_ref, wf_ref, wv_ref, wg_ref = refs[:7]
    q_ref, k_ref, b_ref, bmin_ref, v_ref, g_ref = refs[7 + n_side:13 + n_side]
    xn_ref = refs[-1]
    _side_cast(refs[7:7 + n_side], refs[13 + n_side:13 + 2 * n_side])

    @pl.when(pl.program_id(2) == 0)
    def _():
        xn_ref[...] = _rms_rows(x_ref[0], nw_ref[...]).astype(BF16)

    xn = xn_ref[...]

    rows = [lbt_ref[l:l + 1, :] for l in range(lbt_ref.shape[0])]
    m = functools.reduce(jnp.maximum, rows)
    es = [jnp.exp(r - m) for r in rows]
    lb = sum(es[:layer + 1]) / sum(es)

    q = _dot(xn, wq_ref[...])
    q = q * _sigmoid(q) * (HEAD_DIM ** -0.5)
    f = lb + (1.0 - lb) * _sigmoid(_dot(xn, wf_ref[...]))
    b2 = _chunk_cumsum(jnp.log2(f))
    k = 1.0 - f
    v = _dot(xn, wv_ref[...])
    g = _dot(xn, wg_ref[...])
    g_ref[...] = (g * _sigmoid(g)).astype(BF16)
    for hh in range(heads_per_step):
        sl = slice(hh * HEAD_DIM, (hh + 1) * HEAD_DIM)
        q_ref[0, hh] = q[:, sl].astype(BF16)
        k_ref[0, hh] = k[:, sl].astype(BF16)
        b_ref[0, hh] = b2[:, sl]
        bmin_ref[0, hh, 0] = jnp.min(b2[:, sl].reshape(-1, 8, HEAD_DIM), axis=0)
        v_ref[0, hh] = v[:, sl].astype(BF16)


def _hgrn_proj(x, norm_w, lb_table, w_in, side_weights, *, layer, tm=1024, heads_per_step=2):
    B, T, D = x.shape
    H = D // HEAD_DIM
    tn = heads_per_step * HEAD_DIM
    nq = D // tn
    grid = (B, T // tm, H // heads_per_step)
    head_shape = (B, H, T, HEAD_DIM)
    head_spec = pl.BlockSpec((1, heads_per_step, tm, HEAD_DIM), lambda b, i, j: (b, j, i, 0))
    w_spec = lambda part: pl.BlockSpec((D, tn), lambda b, i, j: (0, part * nq + j))
    side_in, side_in_specs, side_out_specs, side_shapes = _side_cast_plan(side_weights, grid)
    outs = pl.pallas_call(
        functools.partial(_hgrn_proj_kernel, layer=layer, heads_per_step=heads_per_step,
                          n_side=len(side_in)),
        grid=grid,
        in_specs=[
            pl.BlockSpec((1, tm, D), lambda b, i, j: (b, i, 0)),
            pl.BlockSpec((1, D), lambda b, i, j: (0, 0)),
            pl.BlockSpec((lb_table.shape[0], tn), lambda b, i, j: (0, j)),
            w_spec(0), w_spec(1), w_spec(2), w_spec(3),
        ] + side_in_specs,
        out_specs=[
            head_spec, head_spec, head_spec,
            pl.BlockSpec((1, heads_per_step, 1, 8, HEAD_DIM), lambda b, i, j: (b, j, i, 0, 0)),
            head_spec,
            pl.BlockSpec((tm, tn), lambda b, i, j: (b * (T // tm) + i, j)),
        ] + side_out_specs,
        out_shape=[
            jax.ShapeDtypeStruct(head_shape, BF16),
            jax.ShapeDtypeStruct(head_shape, BF16),
            jax.ShapeDtypeStruct(head_shape, F32),
            jax.ShapeDtypeStruct((B, H, T // tm, 8, HEAD_DIM), F32),
            jax.ShapeDtypeStruct(head_shape, BF16),
            jax.ShapeDtypeStruct((B * T, D), BF16),
        ] + side_shapes,
        scratch_shapes=[pltpu.VMEM((tm, D), BF16)],
        compiler_params=_params(("parallel", "parallel", "arbitrary")),
        name="hgrn_proj",
    )(x, norm_w, lb_table, *([_bf16(w_in)] * 4), *side_in)
    return outs[:6], [o.reshape(w.shape[1:]) for o, (w, _) in zip(outs[6:], side_weights)]


def _piece_offset(i):
    return SUB * i * (i + 1) // 2


def _score_mask():
    rw = lax.broadcasted_iota(jnp.int32, (CHUNK, A_COLS), 0)
    cw = lax.broadcasted_iota(jnp.int32, (CHUNK, A_COLS), 1)
    keep = None
    for i in range(N_SUB):
        diag0 = _piece_offset(i) + SUB * i
        in_rows = (rw >= SUB * i) & (rw < SUB * (i + 1))
        left = (cw >= _piece_offset(i)) & (cw < diag0)
        diag = (cw >= diag0) & (cw < diag0 + SUB) & (cw - diag0 <= rw - SUB * i)
        m = in_rows & (left | diag)
        keep = m if keep is None else keep | m
    return keep


def _exact_diag(slab, qb, kb, bb, lane0):
    lr = lax.broadcasted_iota(jnp.int32, slab.shape, 0)
    lc = lax.broadcasted_iota(jnp.int32, slab.shape, 1)
    for s in range(SUB):
        e = jnp.exp2(jnp.minimum(bb - bb[s:s + 1, :], 0.0))
        col = jnp.sum(qb * kb[s:s + 1, :] * e, axis=-1, keepdims=True)
        slab = jnp.where((lc == lane0 + s) & (lr >= s), col, slab)
    return slab


def _block_rows(rows):
    return jnp.concatenate([jnp.broadcast_to(r, (SUB, r.shape[-1])) for r in rows], axis=0)


def _chunk_factors(q_ref, k_ref, b_ref, rows):
    q = q_ref[0, 0, rows, :].astype(F32)
    k = k_ref[0, 0, rows, :].astype(F32)
    b = b_ref[0, 0, rows, :]

    ends = [b[SUB * (j + 1) - 1:SUB * (j + 1), :] for j in range(N_SUB)]
    refs = [jnp.zeros_like(ends[0])] + ends[:-1]
    b_last = ends[-1]
    d = b - _block_rows(refs)
    q_blk = (q * jnp.exp2(d)).astype(BF16)
    k_diag = k * jnp.exp2(jnp.minimum(-d, EXP2_CLAMP))
    k_end = k * jnp.exp2(_block_rows(ends) - b)
    pieces = []
    for i in range(N_SUB):
        for j in range(i):
            kj = k_end[SUB * j:SUB * (j + 1)]
            if j < i - 1:
                kj = kj * jnp.exp2(refs[i] - ends[j])
            pieces.append(kj.astype(BF16))
        pieces.append(k_diag[SUB * i:SUB * (i + 1)].astype(BF16))
    k_all = jnp.concatenate(pieces, axis=0)
    q_dec = (q * jnp.exp2(b)).astype(BF16)
    k_dec = (k_end * _block_rows([jnp.exp2(b_last - e) for e in ends])).astype(BF16)
    return (q, k, b), q_blk, k_all, q_dec, k_dec, jnp.exp2(b_last)


def _fix_diag(a, q, k, b):
    blocks = []
    for i in range(N_SUB):
        rs = slice(SUB * i, SUB * (i + 1))
        diag0 = _piece_offset(i) + SUB * i
        tile = diag0 // 128
        slabs = [a[rs, 128 * t:min(128 * (t + 1), A_COLS)] for t in range(pl.cdiv(A_COLS, 128))]
        slabs[tile] = _exact_diag(slabs[tile], q[rs], k[rs], b[rs], diag0 % 128)
        blocks.append(jnp.concatenate(slabs, axis=1))
    return jnp.concatenate(blocks, axis=0)


def _stack_values(v):
    return jnp.concatenate([v[0:SUB * (i + 1)] for i in range(N_SUB)], axis=0)


def _hgrn_rec_kernel(*refs, tm, n_side):
    q_ref, k_ref, b_ref, bmin_ref, v_ref = refs[:5]
    o_ref = refs[5 + n_side]
    st_ref, qd_scr, kv_scr, dl_scr = refs[6 + 2 * n_side:]
    _side_cast(refs[5:5 + n_side], refs[6 + n_side:6 + 2 * n_side])

    @pl.when(pl.program_id(2) == 0)
    def _():
        st_ref[...] = jnp.zeros_like(st_ref)

    n_chunks = tm // CHUNK

    def chunk_rows(c):
        return pl.ds(pl.multiple_of(c * CHUNK, CHUNK), CHUNK)

    def state_rows(c):
        return pl.ds(pl.multiple_of(c * HEAD_DIM, HEAD_DIM), HEAD_DIM)

    def decay_rows(c, n):
        return pl.ds(pl.multiple_of(c * 8, 8), n)

    def fast():
        keep = _score_mask()

        def intra_body(c, carry):
            ccs = [c * INTRA_UNROLL + u for u in range(INTRA_UNROLL)]
            fac = [_chunk_factors(q_ref, k_ref, b_ref, chunk_rows(cc)) for cc in ccs]
            scores = [jnp.where(keep, _dot_nt(f[1], f[2]), 0.0).astype(BF16) for f in fac]
            for cc, f in zip(ccs, fac):
                kv_scr[state_rows(cc), :] = _dot_tn(v_ref[0, 0, chunk_rows(cc), :], f[4])
                qd_scr[chunk_rows(cc), :] = f[3]
                dl_scr[decay_rows(cc, 8), :] = jnp.broadcast_to(f[5], (8, HEAD_DIM))
            for cc, a in zip(ccs, scores):
                rows = chunk_rows(cc)
                o_ref[0, rows, :] = _dot(a, _stack_values(v_ref[0, 0, rows, :]))
            return carry

        lax.fori_loop(0, n_chunks // INTRA_UNROLL, intra_body, 0)

        def state_body(c, st):
            for u in range(STATE_UNROLL):
                cc = c * STATE_UNROLL + u
                rows = chunk_rows(cc)
                o_ref[0, rows, :] += _dot_nt(qd_scr[rows, :], st.astype(BF16))
                st = st * dl_scr[decay_rows(cc, 1), :] + kv_scr[state_rows(cc), :]
            return st

        st_ref[...] = lax.fori_loop(0, n_chunks // STATE_UNROLL, state_body, st_ref[...])

    def exact():
        keep = _score_mask()

        def body(c, st):
            rows = chunk_rows(c)
            (q, k, b), q_blk, k_all, q_dec, k_dec, dl = _chunk_factors(q_ref, k_ref, b_ref, rows)
            a = _fix_diag(jnp.where(keep, _dot_nt(q_blk, k_all), 0.0), q, k, b).astype(BF16)
            v = v_ref[0, 0, rows, :]
            o_ref[0, rows, :] = _dot(a, _stack_values(v)) + _dot_nt(q_dec, st.astype(BF16))
            return st * dl + _dot_tn(v, k_dec)

        st_ref[...] = lax.fori_loop(0, n_chunks, body, st_ref[...])

    clamp_may_bind = jnp.min(bmin_ref[0, 0]) < -(EXP2_CLAMP - 1.0)
    pl.when(jnp.logical_not(clamp_may_bind))(fast)
    pl.when(clamp_may_bind)(exact)


def _hgrn_rec(q, k, b2, bmin, v, side_weights, *, tm=4096):
    B, H, T, _ = q.shape
    tm = min(tm, T)
    assert T % tm == 0 and (tm // CHUNK) % INTRA_UNROLL == 0 and (tm // CHUNK) % STATE_UNROLL == 0
    n_min = bmin.shape[2] // (T // tm)
    spec = pl.BlockSpec((1, 1, tm, HEAD_DIM), lambda b, h, t: (b, h, t, 0))
    grid = (B, H, T // tm)
    side_in, side_in_specs, side_out_specs, side_shapes = _side_cast_plan(side_weights, grid)
    outs = pl.pallas_call(
        functools.partial(_hgrn_rec_kernel, tm=tm, n_side=len(side_in)),
        grid=grid,
        in_specs=[spec, spec, spec,
                  pl.BlockSpec((1, 1, n_min, 8, HEAD_DIM), lambda b, h, t: (b, h, t, 0, 0)),
                  spec] + side_in_specs,
        out_specs=[pl.BlockSpec((1, tm, HEAD_DIM), lambda b, h, t: (b, t, h))] + side_out_specs,
        out_shape=[jax.ShapeDtypeStruct((B, T, H * HEAD_DIM), F32)] + side_shapes,
        scratch_shapes=[pltpu.VMEM((HEAD_DIM, HEAD_DIM), F32),
                        pltpu.VMEM((tm, HEAD_DIM), BF16),
                        pltpu.VMEM((tm // CHUNK * HEAD_DIM, HEAD_DIM), F32),
                        pltpu.VMEM((tm // CHUNK * 8, HEAD_DIM), F32)],
        compiler_params=_params(("parallel", "parallel", "arbitrary")),
        name="hgrn_rec",
    )(q, k, b2, bmin, v, *side_in)
    return outs[0], [o.reshape(w.shape[1:]) for o, (w, _) in zip(outs[1:], side_weights)]


def _hgrn_out_kernel(o_ref, g_ref, h_ref, gain_ref, w_ref, out_ref):
    y = _rms_rows(o_ref[...], gain_ref[...]) * g_ref[...].astype(F32)
    out_ref[...] = h_ref[...] + _dot(y.astype(BF16), w_ref[...])


def _hgrn_out(o, gate, h, gain, w_out, *, tm=512):
    N, D = h.shape
    row_spec = pl.BlockSpec((tm, D), lambda i: (i, 0))
    return pl.pallas_call(
        _hgrn_out_kernel,
        grid=(N // tm,),
        in_specs=[row_spec, row_spec, row_spec,
                  pl.BlockSpec((1, D), lambda i: (0, 0)),
                  pl.BlockSpec((D, D), lambda i: (0, 0))],
        out_specs=row_spec,
        out_shape=jax.ShapeDtypeStruct((N, D), F32),
        compiler_params=_params(("parallel",)),
        name="hgrn_out",
    )(o, gate, h, gain, w_out)


def _prep_xn(h_ref, hprev_ref, nw_ref, xn_ref, first_in_batch):
    nw = nw_ref[...]
    xn_ref[HALO:, :] = _rms_rows(h_ref[...], nw).astype(BF16)
    prev = jnp.where(first_in_batch, 0.0, _rms_rows(hprev_ref[...], nw))
    xn_ref[0:HALO, :] = prev.astype(BF16)


def _conv3(u, cw, tm):
    return (cw[0:1, :] * u[HALO - 2:HALO - 2 + tm]
            + cw[1:2, :] * u[HALO - 1:HALO - 1 + tm]
            + cw[2:3, :] * u[HALO:HALO + tm])


def _ffn_kernel(h_ref, hprev_ref, nw_ref, wg_ref, wv_ref, cwg_ref, cwv_ref, wd_ref, fw_ref,
                out_ref, xn_ref, *, tm, tiles_per_batch, final_norm, row_parts):
    i = pl.program_id(0)
    j = pl.program_id(1)

    @pl.when(j == 0)
    def _():
        _prep_xn(h_ref, hprev_ref, nw_ref, xn_ref, i % tiles_per_batch == 0)
        out_ref[...] = h_ref[...]

    rp = tm // row_parts
    xns = [xn_ref[p * rp:(p + 1) * rp + HALO, :] for p in range(row_parts)]
    ups = [(_dot(xn, wg_ref[...]), _dot(xn, wv_ref[...])) for xn in xns]
    acts = []
    for ug, uv in ups:
        cg = _conv3(ug, cwg_ref[...], rp)
        cv = _conv3(uv, cwv_ref[...], rp)
        acts.append((cg * _sigmoid(cg) * cv).astype(BF16))
    for p, act in enumerate(acts):
        out_ref[p * rp:(p + 1) * rp, :] += _dot(act, wd_ref[...])

    if final_norm:
        @pl.when(j == pl.num_programs(1) - 1)
        def _():
            out_ref[...] = _rms_rows(out_ref[...], fw_ref[...])


def _tile_spec(tm, d, single_buffer):
    if single_buffer:
        return pl.BlockSpec((tm, d), lambda i, j: (i, 0), pipeline_mode=pl.Buffered(1))
    return pl.BlockSpec((tm, d), lambda i, j: (i, 0))


def _ffn(h, seq_len, norm_w, w_up, conv_w, w_down, final_w, *, final_norm, tm=1024, tn=512,
         single_buffer=True, row_parts=ROW_PARTS, vmem_limit=VMEM_LIMIT):
    N, D = h.shape
    F = w_down.shape[0]
    nf = F // tn
    tiles_per_batch = seq_len // tm
    halo_blocks = tm // HALO
    return pl.pallas_call(
        functools.partial(_ffn_kernel, tm=tm, tiles_per_batch=tiles_per_batch,
                          final_norm=final_norm, row_parts=row_parts),
        grid=(N // tm, nf),
        in_specs=[
            _tile_spec(tm, D, single_buffer),
            pl.BlockSpec((HALO, D), lambda i, j: (jnp.maximum(i * halo_blocks - 1, 0), 0)),
            pl.BlockSpec((1, D), lambda i, j: (0, 0)),
            pl.BlockSpec((D, tn), lambda i, j: (0, j)),
            pl.BlockSpec((D, tn), lambda i, j: (0, nf + j)),
            pl.BlockSpec((3, tn), lambda i, j: (0, j)),
            pl.BlockSpec((3, tn), lambda i, j: (0, nf + j)),
            pl.BlockSpec((tn, D), lambda i, j: (j, 0)),
            pl.BlockSpec((1, D), lambda i, j: (0, 0)),
        ],
        out_specs=pl.BlockSpec((tm, D), lambda i, j: (i, 0)),
        out_shape=jax.ShapeDtypeStruct((N, D), F32),
        scratch_shapes=[pltpu.VMEM((tm + HALO, D), BF16)],
        compiler_params=_params(("parallel", "arbitrary"), vmem_limit),
        name="ffn_final" if final_norm else "ffn",
    )(h, h, norm_w, *([_bf16(w_up)] * 2), conv_w, conv_w, _bf16(w_down), final_w)


def _sc_kernel(h_ref, hprev_ref, nw_ref, wb_ref, wc_ref, wh_ref, cw_ref, wd_ref,
               out_ref, xn_ref, *, tm, tiles_per_batch, row_parts):
    i = pl.program_id(0)
    j = pl.program_id(1)

    @pl.when(j == 0)
    def _():
        _prep_xn(h_ref, hprev_ref, nw_ref, xn_ref, i % tiles_per_batch == 0)
        out_ref[...] = h_ref[...]

    rp = tm // row_parts
    ups = []
    for p in range(row_parts):
        xn = xn_ref[p * rp:(p + 1) * rp + HALO, :]
        ups.append((_dot(xn, wc_ref[...]), _dot(xn, wh_ref[...]), _dot(xn[HALO:], wb_ref[...])))
    ys = [(gb * _conv3(uc * uh, cw_ref[...], rp)).astype(BF16) for uc, uh, gb in ups]
    for p, y in enumerate(ys):
        out_ref[p * rp:(p + 1) * rp, :] += _dot(y, wd_ref[...])


def _short_conv(h, seq_len, norm_w, w_in, conv_w, w_out, *, tm=512, tn=512, single_buffer=False,
                row_parts=1, vmem_limit=VMEM_LIMIT):
    N, D = h.shape
    nd = D // tn
    tiles_per_batch = seq_len // tm
    halo_blocks = tm // HALO
    w_spec = lambda part: pl.BlockSpec((D, tn), lambda i, j: (0, part * nd + j))
    return pl.pallas_call(
        functools.partial(_sc_kernel, tm=tm, tiles_per_batch=tiles_per_batch, row_parts=row_parts),
        grid=(N // tm, nd),
        in_specs=[
            _tile_spec(tm, D, single_buffer),
            pl.BlockSpec((HALO, D), lambda i, j: (jnp.maximum(i * halo_blocks - 1, 0), 0)),
            pl.BlockSpec((1, D), lambda i, j: (0, 0)),
            w_spec(0), w_spec(1), w_spec(2),
            pl.BlockSpec((3, tn), lambda i, j: (0, j)),
            pl.BlockSpec((tn, D), lambda i, j: (j, 0)),
        ],
        out_specs=pl.BlockSpec((tm, D), lambda i, j: (i, 0)),
        out_shape=jax.ShapeDtypeStruct((N, D), F32),
        scratch_shapes=[pltpu.VMEM((tm + HALO, D), BF16)],
        compiler_params=_params(("parallel", "arbitrary"), vmem_limit),
        name="short_conv",
    )(h, h, norm_w, *([_bf16(w_in)] * 3), conv_w, _bf16(w_out))


def kernel(x, norm_mix, norm_ffn, hgrn_w_in, hgrn_lb_table, hgrn_out_norm, hgrn_w_out,
           sc_w_in, sc_conv, sc_w_out, ffn_w_up, ffn_conv, ffn_w_down, final_norm):
    B, T, D = x.shape
    depth = norm_mix.shape[0]
    n_mixers = 2
    row = lambda w: w.reshape(1, -1).astype(F32)
    stacks = dict(hgrn_w_in=hgrn_w_in, hgrn_w_out=hgrn_w_out, sc_w_in=sc_w_in, sc_w_out=sc_w_out,
                  ffn_w_up=ffn_w_up, ffn_w_down=ffn_w_down)

    def layer_weights(i):
        mixer = ("hgrn_w_in", "hgrn_w_out") if i % n_mixers == 0 else ("sc_w_in", "sc_w_out")
        return [(name, i // n_mixers) for name in mixer] + [("ffn_w_up", i), ("ffn_w_down", i)]

    cast = {}

    def weight(name, idx):
        return cast.get((name, idx), stacks[name][idx])

    h = x.reshape(B * T, D)
    for i in range(depth):
        j = i // n_mixers
        if i % n_mixers == 0:
            proj_side = layer_weights(i)[1:] if i == 0 else []
            rec_side = [kw for l in range(1, depth) for kw in layer_weights(l)] if i == 0 else []
            side = lambda keys: [(stacks[name], idx) for name, idx in keys]
            outs, done = _hgrn_proj(h.reshape(B, T, D), row(norm_mix[i]), hgrn_lb_table.astype(F32),
                                    weight("hgrn_w_in", j), side(proj_side), layer=i)
            cast.update(zip(proj_side, done))
            q, k, b2, bmin, v, gate = outs
            o, done = _hgrn_rec(q, k, b2, bmin, v, side(rec_side))
            cast.update(zip(rec_side, done))
            h = _hgrn_out(o.reshape(B * T, D), gate, h, row(hgrn_out_norm[j]),
                          weight("hgrn_w_out", j).astype(BF16))
        else:
            h = _short_conv(h, T, row(norm_mix[i]), weight("sc_w_in", j), sc_conv[j].astype(F32),
                            weight("sc_w_out", j))
        h = _ffn(h, T, row(norm_ffn[i]), weight("ffn_w_up", i), ffn_conv[i].astype(F32),
                 weight("ffn_w_down", i), row(final_norm), final_norm=(i == depth - 1))
    return h.reshape(B, T, D)
```

```python
import functools
import math

import jax
import jax.numpy as jnp
from jax import lax
from jax.experimental import pallas as pl
from jax.experimental.pallas import tpu as pltpu

F32 = jnp.float32
BF16 = jnp.bfloat16

EPS = 1e-6
HEAD_DIM = 128
CHUNK = 64
SUB = 16
N_SUB = CHUNK // SUB
N_PIECES = N_SUB * (N_SUB + 1) // 2
A_COLS = SUB * N_PIECES
EXP2_CLAMP = 112.0
INTRA_UNROLL = 16
STATE_UNROLL = 16
HALO = 16
ROW_PARTS = 2
VMEM_LIMIT = 56 * 1024 * 1024


def _dot(a, b):
    return jnp.dot(a, b, preferred_element_type=F32)


def _dot_nt(a, b):
    return lax.dot_general(a, b, (((1,), (1,)), ((), ())), preferred_element_type=F32)


def _dot_tn(a, b):
    return lax.dot_general(a, b, (((0,), (0,)), ((), ())), preferred_element_type=F32)


def _sigmoid(x):
    return 1.0 / (1.0 + jnp.exp(-x))


def _rms_rows(x, w):
    ms = jnp.mean(x * x, axis=-1, keepdims=True)
    return x * lax.rsqrt(ms + EPS) * w


def _params(semantics, vmem_limit=VMEM_LIMIT):
    return pltpu.CompilerParams(dimension_semantics=semantics, vmem_limit_bytes=vmem_limit)


def _bf16(w):
    return w.astype(BF16)


def _side_cast_plan(weights, grid):
    n_steps = math.prod(grid)
    strides = [math.prod(grid[d + 1:]) for d in range(len(grid))]

    def step(ids):
        return sum(i * s for i, s in zip(ids, strides))

    ins, in_specs, out_specs, shapes = [], [], [], []
    for stack, idx in weights:
        n, rows, cols = stack.shape
        assert rows % n_steps == 0, (stack.shape, grid)
        block = (None, rows // n_steps, cols)
        ins.append(stack.reshape(n * n_steps, rows // n_steps, cols))
        in_specs.append(pl.BlockSpec(block, lambda *ids, base=idx * n_steps: (base + step(ids), 0, 0)))
        out_specs.append(pl.BlockSpec(block, lambda *ids: (step(ids), 0, 0)))
        shapes.append(jax.ShapeDtypeStruct((n_steps, rows // n_steps, cols), BF16))
    return ins, in_specs, out_specs, shapes


def _side_cast(src_refs, dst_refs):
    for src, dst in zip(src_refs, dst_refs):
        dst[...] = src[...].astype(BF16)


def _chunk_cumsum(x):
    n, d = x.shape
    pos = lax.broadcasted_iota(jnp.int32, (n, d), 0) & (CHUNK - 1)
    s = 1
    while s < CHUNK:
        pad = max(8, s)
        xp = jnp.concatenate([jnp.zeros((pad, d), x.dtype), x], axis=0)
        x = x + jnp.where(pos >= s, xp[pad - s:pad - s + n], 0.0)
        s *= 2
    return x


def _hgrn_proj_kernel(*refs, layer, heads_per_step, n_side):
    x_ref, nw_ref, lbt_ref, wq_ref, wf_ref, wv_ref, wg_ref = refs[:7]
    q_ref, k_ref, b_ref, bmin_ref, v_ref, g_ref = refs[7 + n_side:13 + n_side]
    xn_ref = refs[-1]
    _side_cast(refs[7:7 + n_side], refs[13 + n_side:13 + 2 * n_side])

    @pl.when(pl.program_id(2) == 0)
    def _():
        xn_ref[...] = _rms_rows(x_ref[0], nw_ref[...]).astype(BF16)

    xn = xn_ref[...]

    rows = [lbt_ref[l:l + 1, :] for l in range(lbt_ref.shape[0])]
    m = functools.reduce(jnp.maximum, rows)
    es = [jnp.exp(r - m) for r in rows]
    lb = sum(es[:layer + 1]) / sum(es)

    q = _dot(xn, wq_ref[...].astype(BF16))
    q = q * _sigmoid(q) * (HEAD_DIM ** -0.5)
    f = lb + (1.0 - lb) * _sigmoid(_dot(xn, wf_ref[...].astype(BF16)))
    b2 = _chunk_cumsum(jnp.log2(f))
    k = 1.0 - f
    v = _dot(xn, wv_ref[...].astype(BF16))
    g = _dot(xn, wg_ref[...].astype(BF16))
    g_ref[...] = (g * _sigmoid(g)).astype(BF16)
    for hh in range(heads_per_step):
        sl = slice(hh * HEAD_DIM, (hh + 1) * HEAD_DIM)
        q_ref[0, hh] = q[:, sl].astype(BF16)
        k_ref[0, hh] = k[:, sl].astype(BF16)
        b_ref[0, hh] = b2[:, sl]
        bmin_ref[0, hh, 0] = jnp.min(b2[:, sl].reshape(-1, 8, HEAD_DIM), axis=0)
        v_ref[0, hh] = v[:, sl].astype(BF16)


def _hgrn_proj(x, norm_w, lb_table, w_in, side_weights, *, layer, tm=1024, heads_per_step=2):
    B, T, D = x.shape
    H = D // HEAD_DIM
    tn = heads_per_step * HEAD_DIM
    nq = D // tn
    grid = (B, T // tm, H // heads_per_step)
    head_shape = (B, H, T, HEAD_DIM)
    head_spec = pl.BlockSpec((1, heads_per_step, tm, HEAD_DIM), lambda b, i, j: (b, j, i, 0))
    w_spec = lambda part: pl.BlockSpec((D, tn), lambda b, i, j: (0, part * nq + j))
    side_in, side_in_specs, side_out_specs, side_shapes = _side_cast_plan(side_weights, grid)
    outs = pl.pallas_call(
        functools.partial(_hgrn_proj_kernel, layer=layer, heads_per_step=heads_per_step,
                          n_side=len(side_in)),
        grid=grid,
        in_specs=[
            pl.BlockSpec((1, tm, D), lambda b, i, j: (b, i, 0)),
            pl.BlockSpec((1, D), lambda b, i, j: (0, 0)),
            pl.BlockSpec((lb_table.shape[0], tn), lambda b, i, j: (0, j)),
            w_spec(0), w_spec(1), w_spec(2), w_spec(3),
        ] + side_in_specs,
        out_specs=[
            head_spec, head_spec, head_spec,
            pl.BlockSpec((1, heads_per_step, 1, 8, HEAD_DIM), lambda b, i, j: (b, j, i, 0, 0)),
            head_spec,
            pl.BlockSpec((tm, tn), lambda b, i, j: (b * (T // tm) + i, j)),
        ] + side_out_specs,
        out_shape=[
            jax.ShapeDtypeStruct(head_shape, BF16),
            jax.ShapeDtypeStruct(head_shape, BF16),
            jax.ShapeDtypeStruct(head_shape, F32),
            jax.ShapeDtypeStruct((B, H, T // tm, 8, HEAD_DIM), F32),
            jax.ShapeDtypeStruct(head_shape, BF16),
            jax.ShapeDtypeStruct((B * T, D), BF16),
        ] + side_shapes,
        scratch_shapes=[pltpu.VMEM((tm, D), BF16)],
        compiler_params=_params(("parallel", "parallel", "arbitrary")),
        name="hgrn_proj",
    )(x, norm_w, lb_table, *([w_in] * 4), *side_in)
    return outs[:6], [o.reshape(w.shape[1:]) for o, (w, _) in zip(outs[6:], side_weights)]


def _piece_offset(i):
    return SUB * i * (i + 1) // 2


def _score_mask():
    rw = lax.broadcasted_iota(jnp.int32, (CHUNK, A_COLS), 0)
    cw = lax.broadcasted_iota(jnp.int32, (CHUNK, A_COLS), 1)
    keep = None
    for i in range(N_SUB):
        diag0 = _piece_offset(i) + SUB * i
        in_rows = (rw >= SUB * i) & (rw < SUB * (i + 1))
        left = (cw >= _piece_offset(i)) & (cw < diag0)
        diag = (cw >= diag0) & (cw < diag0 + SUB) & (cw - diag0 <= rw - SUB * i)
        m = in_rows & (left | diag)
        keep = m if keep is None else keep | m
    return keep


def _exact_diag(slab, qb, kb, bb, lane0):
    lr = lax.broadcasted_iota(jnp.int32, slab.shape, 0)
    lc = lax.broadcasted_iota(jnp.int32, slab.shape, 1)
    for s in range(SUB):
        e = jnp.exp2(jnp.minimum(bb - bb[s:s + 1, :], 0.0))
        col = jnp.sum(qb * kb[s:s + 1, :] * e, axis=-1, keepdims=True)
        slab = jnp.where((lc == lane0 + s) & (lr >= s), col, slab)
    return slab


def _block_rows(rows):
    return jnp.concatenate([jnp.broadcast_to(r, (SUB, r.shape[-1])) for r in rows], axis=0)


def _chunk_factors(q_ref, k_ref, b_ref, rows):
    q = q_ref[0, 0, rows, :].astype(F32)
    k = k_ref[0, 0, rows, :].astype(F32)
    b = b_ref[0, 0, rows, :]

    ends = [b[SUB * (j + 1) - 1:SUB * (j + 1), :] for j in range(N_SUB)]
    refs = [jnp.zeros_like(ends[0])] + ends[:-1]
    b_last = ends[-1]
    d = b - _block_rows(refs)
    q_blk = (q * jnp.exp2(d)).astype(BF16)
    k_diag = k * jnp.exp2(jnp.minimum(-d, EXP2_CLAMP))
    k_end = k * jnp.exp2(_block_rows(ends) - b)
    pieces = []
    for i in range(N_SUB):
        for j in range(i):
            kj = k_end[SUB * j:SUB * (j + 1)]
            if j < i - 1:
                kj = kj * jnp.exp2(refs[i] - ends[j])
            pieces.append(kj.astype(BF16))
        pieces.append(k_diag[SUB * i:SUB * (i + 1)].astype(BF16))
    k_all = jnp.concatenate(pieces, axis=0)
    q_dec = (q * jnp.exp2(b)).astype(BF16)
    k_dec = (k_end * _block_rows([jnp.exp2(b_last - e) for e in ends])).astype(BF16)
    return (q, k, b), q_blk, k_all, q_dec, k_dec, jnp.exp2(b_last)


def _fix_diag(a, q, k, b):
    blocks = []
    for i in range(N_SUB):
        rs = slice(SUB * i, SUB * (i + 1))
        diag0 = _piece_offset(i) + SUB * i
        tile = diag0 // 128
        slabs = [a[rs, 128 * t:min(128 * (t + 1), A_COLS)] for t in range(pl.cdiv(A_COLS, 128))]
        slabs[tile] = _exact_diag(slabs[tile], q[rs], k[rs], b[rs], diag0 % 128)
        blocks.append(jnp.concatenate(slabs, axis=1))
    return jnp.concatenate(blocks, axis=0)


def _stack_values(v):
    return jnp.concatenate([v[0:SUB * (i + 1)] for i in range(N_SUB)], axis=0)


def _hgrn_rec_kernel(*refs, tm, n_side):
    q_ref, k_ref, b_ref, bmin_ref, v_ref = refs[:5]
    o_ref = refs[5 + n_side]
    st_ref, qd_scr, oi_scr, kv_scr, dl_scr = refs[6 + 2 * n_side:]
    _side_cast(refs[5:5 + n_side], refs[6 + n_side:6 + 2 * n_side])

    @pl.when(pl.program_id(2) == 0)
    def _():
        st_ref[...] = jnp.zeros_like(st_ref)

    n_chunks = tm // CHUNK

    def chunk_rows(c):
        return pl.ds(pl.multiple_of(c * CHUNK, CHUNK), CHUNK)

    def state_rows(c):
        return pl.ds(pl.multiple_of(c * HEAD_DIM, HEAD_DIM), HEAD_DIM)

    def decay_rows(c, n):
        return pl.ds(pl.multiple_of(c * 8, 8), n)

    def fast():
        keep = _score_mask()

        def intra_body(c, carry):
            ccs = [c * INTRA_UNROLL + u for u in range(INTRA_UNROLL)]
            fac = [_chunk_factors(q_ref, k_ref, b_ref, chunk_rows(cc)) for cc in ccs]
            scores = [jnp.where(keep, _dot_nt(f[1], f[2]), 0.0).astype(BF16) for f in fac]
            for cc, f in zip(ccs, fac):
                kv_scr[state_rows(cc), :] = _dot_tn(v_ref[0, 0, chunk_rows(cc), :], f[4])
                qd_scr[chunk_rows(cc), :] = f[3]
                dl_scr[decay_rows(cc, 8), :] = jnp.broadcast_to(f[5], (8, HEAD_DIM))
            for cc, a in zip(ccs, scores):
                rows = chunk_rows(cc)
                oi_scr[rows, :] = _dot(a, _stack_values(v_ref[0, 0, rows, :]))
            return carry

        lax.fori_loop(0, n_chunks // INTRA_UNROLL, intra_body, 0)

        def state_body(c, st):
            for u in range(STATE_UNROLL):
                cc = c * STATE_UNROLL + u
                rows = chunk_rows(cc)
                o = oi_scr[rows, :] + _dot_nt(qd_scr[rows, :], st.astype(BF16))
                o_ref[0, rows, :] = o.astype(o_ref.dtype)
                st = st * dl_scr[decay_rows(cc, 1), :] + kv_scr[state_rows(cc), :]
            return st

        st_ref[...] = lax.fori_loop(0, n_chunks // STATE_UNROLL, state_body, st_ref[...])

    def exact():
        keep = _score_mask()

        def body(c, st):
            rows = chunk_rows(c)
            (q, k, b), q_blk, k_all, q_dec, k_dec, dl = _chunk_factors(q_ref, k_ref, b_ref, rows)
            a = _fix_diag(jnp.where(keep, _dot_nt(q_blk, k_all), 0.0), q, k, b).astype(BF16)
            v = v_ref[0, 0, rows, :]
            o = _dot(a, _stack_values(v)) + _dot_nt(q_dec, st.astype(BF16))
            o_ref[0, rows, :] = o.astype(o_ref.dtype)
            return st * dl + _dot_tn(v, k_dec)

        st_ref[...] = lax.fori_loop(0, n_chunks, body, st_ref[...])

    clamp_may_bind = jnp.min(bmin_ref[0, 0]) < -(EXP2_CLAMP - 1.0)
    pl.when(jnp.logical_not(clamp_may_bind))(fast)
    pl.when(clamp_may_bind)(exact)


def _hgrn_rec(q, k, b2, bmin, v, side_weights, *, tm=4096):
    B, H, T, _ = q.shape
    tm = min(tm, T)
    assert T % tm == 0 and (tm // CHUNK) % INTRA_UNROLL == 0 and (tm // CHUNK) % STATE_UNROLL == 0
    n_min = bmin.shape[2] // (T // tm)
    spec = pl.BlockSpec((1, 1, tm, HEAD_DIM), lambda b, h, t: (b, h, t, 0))
    grid = (B, H, T // tm)
    side_in, side_in_specs, side_out_specs, side_shapes = _side_cast_plan(side_weights, grid)
    outs = pl.pallas_call(
        functools.partial(_hgrn_rec_kernel, tm=tm, n_side=len(side_in)),
        grid=grid,
        in_specs=[spec, spec, spec,
                  pl.BlockSpec((1, 1, n_min, 8, HEAD_DIM), lambda b, h, t: (b, h, t, 0, 0)),
                  spec] + side_in_specs,
        out_specs=[pl.BlockSpec((1, tm, HEAD_DIM), lambda b, h, t: (b, t, h))] + side_out_specs,
        out_shape=[jax.ShapeDtypeStruct((B, T, H * HEAD_DIM), BF16)] + side_shapes,
        scratch_shapes=[pltpu.VMEM((HEAD_DIM, HEAD_DIM), F32),
                        pltpu.VMEM((tm, HEAD_DIM), BF16),
                        pltpu.VMEM((tm, HEAD_DIM), F32),
                        pltpu.VMEM((tm // CHUNK * HEAD_DIM, HEAD_DIM), F32),
                        pltpu.VMEM((tm // CHUNK * 8, HEAD_DIM), F32)],
        compiler_params=_params(("parallel", "parallel", "arbitrary")),
        name="hgrn_rec",
    )(q, k, b2, bmin, v, *side_in)
    return outs[0], [o.reshape(w.shape[1:]) for o, (w, _) in zip(outs[1:], side_weights)]


def _hgrn_out_kernel(o_ref, g_ref, h_ref, gain_ref, w_ref, out_ref):
    y = _rms_rows(o_ref[...].astype(F32), gain_ref[...]) * g_ref[...].astype(F32)
    out_ref[...] = h_ref[...] + _dot(y.astype(BF16), w_ref[...])


def _hgrn_out(o, gate, h, gain, w_out, *, tm=512):
    N, D = h.shape
    row_spec = pl.BlockSpec((tm, D), lambda i: (i, 0))
    return pl.pallas_call(
        _hgrn_out_kernel,
        grid=(N // tm,),
        in_specs=[row_spec, row_spec, row_spec,
                  pl.BlockSpec((1, D), lambda i: (0, 0)),
                  pl.BlockSpec((D, D), lambda i: (0, 0))],
        out_specs=row_spec,
        out_shape=jax.ShapeDtypeStruct((N, D), F32),
        compiler_params=_params(("parallel",)),
        name="hgrn_out",
    )(o, gate, h, gain, w_out)


def _prep_xn(h_ref, hprev_ref, nw_ref, xn_ref, first_in_batch):
    nw = nw_ref[...]
    xn_ref[HALO:, :] = _rms_rows(h_ref[...], nw).astype(BF16)
    prev = jnp.where(first_in_batch, 0.0, _rms_rows(hprev_ref[...], nw))
    xn_ref[0:HALO, :] = prev.astype(BF16)


def _conv3(u, cw, tm):
    return (cw[0:1, :] * u[HALO - 2:HALO - 2 + tm]
            + cw[1:2, :] * u[HALO - 1:HALO - 1 + tm]
            + cw[2:3, :] * u[HALO:HALO + tm])


def _ffn_kernel(h_ref, hprev_ref, nw_ref, wg_ref, wv_ref, cwg_ref, cwv_ref, wd_ref, fw_ref,
                out_ref, xn_ref, *, tm, tiles_per_batch, final_norm, row_parts):
    i = pl.program_id(0)
    j = pl.program_id(1)

    @pl.when(j == 0)
    def _():
        _prep_xn(h_ref, hprev_ref, nw_ref, xn_ref, i % tiles_per_batch == 0)
        out_ref[...] = h_ref[...]

    rp = tm // row_parts
    xns = [xn_ref[p * rp:(p + 1) * rp + HALO, :] for p in range(row_parts)]
    ups = [(_dot(xn, wg_ref[...]), _dot(xn, wv_ref[...])) for xn in xns]
    acts = []
    for ug, uv in ups:
        cg = _conv3(ug, cwg_ref[...], rp)
        cv = _conv3(uv, cwv_ref[...], rp)
        acts.append((cg * _sigmoid(cg) * cv).astype(BF16))
    for p, act in enumerate(acts):
        out_ref[p * rp:(p + 1) * rp, :] += _dot(act, wd_ref[...])

    if final_norm:
        @pl.when(j == pl.num_programs(1) - 1)
        def _():
            out_ref[...] = _rms_rows(out_ref[...], fw_ref[...])


def _tile_spec(tm, d, single_buffer):
    if single_buffer:
        return pl.BlockSpec((tm, d), lambda i, j: (i, 0), pipeline_mode=pl.Buffered(1))
    return pl.BlockSpec((tm, d), lambda i, j: (i, 0))


def _ffn(h, seq_len, norm_w, w_up, conv_w, w_down, final_w, *, final_norm, tm=1024, tn=512,
         single_buffer=True, row_parts=ROW_PARTS, vmem_limit=VMEM_LIMIT):
    N, D = h.shape
    F = w_down.shape[0]
    nf = F // tn
    tiles_per_batch = seq_len // tm
    halo_blocks = tm // HALO
    return pl.pallas_call(
        functools.partial(_ffn_kernel, tm=tm, tiles_per_batch=tiles_per_batch,
                          final_norm=final_norm, row_parts=row_parts),
        grid=(N // tm, nf),
        in_specs=[
            _tile_spec(tm, D, single_buffer),
            pl.BlockSpec((HALO, D), lambda i, j: (jnp.maximum(i * halo_blocks - 1, 0), 0)),
            pl.BlockSpec((1, D), lambda i, j: (0, 0)),
            pl.BlockSpec((D, tn), lambda i, j: (0, j)),
            pl.BlockSpec((D, tn), lambda i, j: (0, nf + j)),
            pl.BlockSpec((3, tn), lambda i, j: (0, j)),
            pl.BlockSpec((3, tn), lambda i, j: (0, nf + j)),
            pl.BlockSpec((tn, D), lambda i, j: (j, 0)),
            pl.BlockSpec((1, D), lambda i, j: (0, 0)),
        ],
        out_specs=pl.BlockSpec((tm, D), lambda i, j: (i, 0)),
        out_shape=jax.ShapeDtypeStruct((N, D), F32),
        scratch_shapes=[pltpu.VMEM((tm + HALO, D), BF16)],
        compiler_params=_params(("parallel", "arbitrary"), vmem_limit),
        name="ffn_final" if final_norm else "ffn",
    )(h, h, norm_w, *([_bf16(w_up)] * 2), conv_w, conv_w, _bf16(w_down), final_w)


def _sc_kernel(h_ref, hprev_ref, nw_ref, wb_ref, wc_ref, wh_ref, cw_ref, wd_ref,
               out_ref, xn_ref, *, tm, tiles_per_batch, row_parts):
    i = pl.program_id(0)
    j = pl.program_id(1)

    @pl.when(j == 0)
    def _():
        _prep_xn(h_ref, hprev_ref, nw_ref, xn_ref, i % tiles_per_batch == 0)
        out_ref[...] = h_ref[...]

    rp = tm // row_parts
    ups = []
    for p in range(row_parts):
        xn = xn_ref[p * rp:(p + 1) * rp + HALO, :]
        ups.append((_dot(xn, wc_ref[...]), _dot(xn, wh_ref[...]), _dot(xn[HALO:], wb_ref[...])))
    ys = [(gb * _conv3(uc * uh, cw_ref[...], rp)).astype(BF16) for uc, uh, gb in ups]
    for p, y in enumerate(ys):
        out_ref[p * rp:(p + 1) * rp, :] += _dot(y, wd_ref[...])


def _short_conv(h, seq_len, norm_w, w_in, conv_w, w_out, *, tm=512, tn=512, single_buffer=False,
                row_parts=1, vmem_limit=VMEM_LIMIT):
    N, D = h.shape
    nd = D // tn
    tiles_per_batch = seq_len // tm
    halo_blocks = tm // HALO
    w_spec = lambda part: pl.BlockSpec((D, tn), lambda i, j: (0, part * nd + j))
    return pl.pallas_call(
        functools.partial(_sc_kernel, tm=tm, tiles_per_batch=tiles_per_batch, row_parts=row_parts),
        grid=(N // tm, nd),
        in_specs=[
            _tile_spec(tm, D, single_buffer),
            pl.BlockSpec((HALO, D), lambda i, j: (jnp.maximum(i * halo_blocks - 1, 0), 0)),
            pl.BlockSpec((1, D), lambda i, j: (0, 0)),
            w_spec(0), w_spec(1), w_spec(2),
            pl.BlockSpec((3, tn), lambda i, j: (0, j)),
            pl.BlockSpec((tn, D), lambda i, j: (j, 0)),
        ],
        out_specs=pl.BlockSpec((tm, D), lambda i, j: (i, 0)),
        out_shape=jax.ShapeDtypeStruct((N, D), F32),
        scratch_shapes=[pltpu.VMEM((tm + HALO, D), BF16)],
        compiler_params=_params(("parallel", "arbitrary"), vmem_limit),
        name="short_conv",
    )(h, h, norm_w, *([_bf16(w_in)] * 3), conv_w, _bf16(w_out))


def kernel(x, norm_mix, norm_ffn, hgrn_w_in, hgrn_lb_table, hgrn_out_norm, hgrn_w_out,
           sc_w_in, sc_conv, sc_w_out, ffn_w_up, ffn_conv, ffn_w_down, final_norm):
    B, T, D = x.shape
    depth = norm_mix.shape[0]
    n_mixers = 2
    row = lambda w: w.reshape(1, -1).astype(F32)
    stacks = dict(hgrn_w_in=hgrn_w_in, hgrn_w_out=hgrn_w_out, sc_w_in=sc_w_in, sc_w_out=sc_w_out,
                  ffn_w_up=ffn_w_up, ffn_w_down=ffn_w_down)

    def layer_weights(i):
        mixer = ("hgrn_w_in", "hgrn_w_out") if i % n_mixers == 0 else ("sc_w_in", "sc_w_out")
        return [(name, i // n_mixers) for name in mixer] + [("ffn_w_up", i), ("ffn_w_down", i)]

    cast = {}

    def weight(name, idx):
        return cast.get((name, idx), stacks[name][idx])

    h = x.reshape(B * T, D)
    for i in range(depth):
        j = i // n_mixers
        if i % n_mixers == 0:
            proj_side = layer_weights(i)[1:] if i == 0 else []
            rec_side = [kw for l in range(1, depth) for kw in layer_weights(l)] if i == 0 else []
            side = lambda keys: [(stacks[name], idx) for name, idx in keys]
            outs, done = _hgrn_proj(h.reshape(B, T, D), row(norm_mix[i]), hgrn_lb_table.astype(F32),
                                    weight("hgrn_w_in", j), side(proj_side), layer=i)
            cast.update(zip(proj_side, done))
            q, k, b2, bmin, v, gate = outs
            o, done = _hgrn_rec(q, k, b2, bmin, v, side(rec_side))
            cast.update(zip(rec_side, done))
            h = _hgrn_out(o.reshape(B * T, D), gate, h, row(hgrn_out_norm[j]),
                          weight("hgrn_w_out", j).astype(BF16))
        else:
            h = _short_conv(h, T, row(norm_mix[i]), weight("sc_w_in", j), sc_conv[j].astype(F32),
                            weight("sc_w_out", j))
        h = _ffn(h, T, row(norm_ffn[i]), weight("ffn_w_up", i), ffn_conv[i].astype(F32),
                 weight("ffn_w_down", i), row(final_norm), final_norm=(i == depth - 1))
    return h.reshape(B, T, D)
```

```python
import functools
import math

import jax
import jax.numpy as jnp
from jax import lax
from jax.experimental import pallas as pl
from jax.experimental.pallas import tpu as pltpu

F32 = jnp.float32
BF16 = jnp.bfloat16

EPS = 1e-6
HEAD_DIM = 128
CHUNK = 64
SUB = 16
N_SUB = CHUNK // SUB
N_PIECES = N_SUB * (N_SUB + 1) // 2
A_COLS = SUB * N_PIECES
EXP2_CLAMP = 112.0
INTRA_UNROLL = 16
STATE_UNROLL = 16
ROW_PARTS = 2
VMEM_LIMIT = 56 * 1024 * 1024


def _dot(a, b):
    return jnp.dot(a, b, preferred_element_type=F32)


def _dot_nt(a, b):
    return lax.dot_general(a, b, (((1,), (1,)), ((), ())), preferred_element_type=F32)


def _dot_tn(a, b):
    return lax.dot_general(a, b, (((0,), (0,)), ((), ())), preferred_element_type=F32)


def _sigmoid(x):
    return 1.0 / (1.0 + jnp.exp(-x))


def _rms_rows(x, w):
    ms = jnp.mean(x * x, axis=-1, keepdims=True)
    return x * lax.rsqrt(ms + EPS) * w


def _params(semantics, vmem_limit=VMEM_LIMIT):
    return pltpu.CompilerParams(dimension_semantics=semantics, vmem_limit_bytes=vmem_limit)


def _bf16(w):
    return w.astype(BF16)


def _side_cast_plan(weights, grid):
    n_steps = math.prod(grid)
    strides = [math.prod(grid[d + 1:]) for d in range(len(grid))]

    def step(ids):
        return sum(i * s for i, s in zip(ids, strides))

    ins, in_specs, out_specs, shapes = [], [], [], []
    for stack, idx in weights:
        n, rows, cols = stack.shape
        assert rows % n_steps == 0, (stack.shape, grid)
        block = (None, rows // n_steps, cols)
        ins.append(stack.reshape(n * n_steps, rows // n_steps, cols))
        in_specs.append(pl.BlockSpec(block, lambda *ids, base=idx * n_steps: (base + step(ids), 0, 0)))
        out_specs.append(pl.BlockSpec(block, lambda *ids: (step(ids), 0, 0)))
        shapes.append(jax.ShapeDtypeStruct((n_steps, rows // n_steps, cols), BF16))
    return ins, in_specs, out_specs, shapes


def _side_cast(src_refs, dst_refs):
    for src, dst in zip(src_refs, dst_refs):
        dst[...] = src[...].astype(BF16)


def _chunk_cumsum(x):
    n, d = x.shape
    pos = lax.broadcasted_iota(jnp.int32, (n, d), 0) & (CHUNK - 1)
    s = 1
    while s < CHUNK:
        pad = max(8, s)
        xp = jnp.concatenate([jnp.zeros((pad, d), x.dtype), x], axis=0)
        x = x + jnp.where(pos >= s, xp[pad - s:pad - s + n], 0.0)
        s *= 2
    return x


def _norm_cast_kernel(x_ref, w_ref, o_ref):
    o_ref[...] = _rms_rows(x_ref[...], w_ref[...]).astype(BF16)


def _norm_cast(x, norm_w, *, tm=1024):
    N, D = x.shape
    return pl.pallas_call(
        _norm_cast_kernel,
        grid=(N // tm,),
        in_specs=[pl.BlockSpec((tm, D), lambda i: (i, 0)), pl.BlockSpec((1, D), lambda i: (0, 0))],
        out_specs=pl.BlockSpec((tm, D), lambda i: (i, 0)),
        out_shape=jax.ShapeDtypeStruct((N, D), BF16),
        compiler_params=_params(("parallel",)),
        name="norm_cast",
    )(x, norm_w)


def _hgrn_proj_kernel(*refs, layer, heads_per_step, n_side):
    xn_ref, lbt_ref = refs[:2]
    w_refs = refs[2:6]
    q_ref, k_ref, b_ref, bmin_ref, v_ref, g_ref = refs[6 + n_side:12 + n_side]
    wb_ref = refs[-1]
    _side_cast(refs[6:6 + n_side], refs[12 + n_side:12 + 2 * n_side])

    @pl.when((pl.program_id(1) == 0) & (pl.program_id(2) == 0))
    def _():
        for p, w_ref in enumerate(w_refs):
            wb_ref[p] = w_ref[...].astype(BF16)

    xn = xn_ref[0]

    rows = [lbt_ref[l:l + 1, :] for l in range(lbt_ref.shape[0])]
    m = functools.reduce(jnp.maximum, rows)
    es = [jnp.exp(r - m) for r in rows]
    lb = sum(es[:layer + 1]) / sum(es)

    q = _dot(xn, wb_ref[0])
    q = q * _sigmoid(q) * (HEAD_DIM ** -0.5)
    f = lb + (1.0 - lb) * _sigmoid(_dot(xn, wb_ref[1]))
    b2 = _chunk_cumsum(jnp.log2(f))
    k = 1.0 - f
    v = _dot(xn, wb_ref[2])
    g = _dot(xn, wb_ref[3])
    g_ref[...] = (g * _sigmoid(g)).astype(BF16)
    for hh in range(heads_per_step):
        sl = slice(hh * HEAD_DIM, (hh + 1) * HEAD_DIM)
        q_ref[0, hh] = q[:, sl].astype(BF16)
        k_ref[0, hh] = k[:, sl].astype(BF16)
        b_ref[0, hh] = b2[:, sl]
        bmin_ref[0, hh, 0] = jnp.min(b2[:, sl].reshape(-1, 8, HEAD_DIM), axis=0)
        v_ref[0, hh] = v[:, sl].astype(BF16)


def _hgrn_proj(xn, lb_table, w_in, side_weights, *, layer, tm=1024, heads_per_step=2):
    B, T, D = xn.shape
    H = D // HEAD_DIM
    tn = heads_per_step * HEAD_DIM
    nq = D // tn
    grid = (H // heads_per_step, B, T // tm)
    head_shape = (B, H, T, HEAD_DIM)
    head_spec = pl.BlockSpec((1, heads_per_step, tm, HEAD_DIM), lambda j, b, i: (b, j, i, 0))
    w_spec = lambda part: pl.BlockSpec((D, tn), lambda j, b, i: (0, part * nq + j))
    side_in, side_in_specs, side_out_specs, side_shapes = _side_cast_plan(side_weights, grid)
    outs = pl.pallas_call(
        functools.partial(_hgrn_proj_kernel, layer=layer, heads_per_step=heads_per_step,
                          n_side=len(side_in)),
        grid=grid,
        in_specs=[
            pl.BlockSpec((1, tm, D), lambda j, b, i: (b, i, 0)),
            pl.BlockSpec((lb_table.shape[0], tn), lambda j, b, i: (0, j)),
            w_spec(0), w_spec(1), w_spec(2), w_spec(3),
        ] + side_in_specs,
        out_specs=[
            head_spec, head_spec, head_spec,
            pl.BlockSpec((1, heads_per_step, 1, 8, HEAD_DIM), lambda j, b, i: (b, j, i, 0, 0)),
            head_spec,
            pl.BlockSpec((tm, tn), lambda j, b, i: (b * (T // tm) + i, j)),
        ] + side_out_specs,
        out_shape=[
            jax.ShapeDtypeStruct(head_shape, BF16),
            jax.ShapeDtypeStruct(head_shape, BF16),
            jax.ShapeDtypeStruct(head_shape, F32),
            jax.ShapeDtypeStruct((B, H, T // tm, 8, HEAD_DIM), F32),
            jax.ShapeDtypeStruct(head_shape, BF16),
            jax.ShapeDtypeStruct((B * T, D), BF16),
        ] + side_shapes,
        scratch_shapes=[pltpu.VMEM((4, D, tn), BF16)],
        compiler_params=_params(("arbitrary", "arbitrary", "arbitrary")),
        name="hgrn_proj",
    )(xn, lb_table, *([w_in] * 4), *side_in)
    return outs[:6], [o.reshape(w.shape[1:]) for o, (w, _) in zip(outs[6:], side_weights)]


def _piece_offset(i):
    return SUB * i * (i + 1) // 2


def _score_mask():
    rw = lax.broadcasted_iota(jnp.int32, (CHUNK, A_COLS), 0)
    cw = lax.broadcasted_iota(jnp.int32, (CHUNK, A_COLS), 1)
    keep = None
    for i in range(N_SUB):
        diag0 = _piece_offset(i) + SUB * i
        in_rows = (rw >= SUB * i) & (rw < SUB * (i + 1))
        left = (cw >= _piece_offset(i)) & (cw < diag0)
        diag = (cw >= diag0) & (cw < diag0 + SUB) & (cw - diag0 <= rw - SUB * i)
        m = in_rows & (left | diag)
        keep = m if keep is None else keep | m
    return keep


def _exact_diag(slab, qb, kb, bb, lane0):
    lr = lax.broadcasted_iota(jnp.int32, slab.shape, 0)
    lc = lax.broadcasted_iota(jnp.int32, slab.shape, 1)
    for s in range(SUB):
        e = jnp.exp2(jnp.minimum(bb - bb[s:s + 1, :], 0.0))
        col = jnp.sum(qb * kb[s:s + 1, :] * e, axis=-1, keepdims=True)
        slab = jnp.where((lc == lane0 + s) & (lr >= s), col, slab)
    return slab


def _block_rows(rows):
    return jnp.concatenate([jnp.broadcast_to(r, (SUB, r.shape[-1])) for r in rows], axis=0)


def _chunk_factors(q_ref, k_ref, b_ref, rows):
    q = q_ref[0, 0, rows, :].astype(F32)
    k = k_ref[0, 0, rows, :].astype(F32)
    b = b_ref[0, 0, rows, :]

    ends = [b[SUB * (j + 1) - 1:SUB * (j + 1), :] for j in range(N_SUB)]
    refs = [jnp.zeros_like(ends[0])] + ends[:-1]
    b_last = ends[-1]
    d = b - _block_rows(refs)
    q_blk = (q * jnp.exp2(d)).astype(BF16)
    k_diag = k * jnp.exp2(jnp.minimum(-d, EXP2_CLAMP))
    k_end = k * jnp.exp2(_block_rows(ends) - b)
    pieces = []
    for i in range(N_SUB):
        for j in range(i):
            kj = k_end[SUB * j:SUB * (j + 1)]
            if j < i - 1:
                kj = kj * jnp.exp2(refs[i] - ends[j])
            pieces.append(kj.astype(BF16))
        pieces.append(k_diag[SUB * i:SUB * (i + 1)].astype(BF16))
    k_all = jnp.concatenate(pieces, axis=0)
    q_dec = (q * jnp.exp2(b)).astype(BF16)
    k_dec = (k_end * _block_rows([jnp.exp2(b_last - e) for e in ends])).astype(BF16)
    return (q, k, b), q_blk, k_all, q_dec, k_dec, jnp.exp2(b_last)


def _fix_diag(a, q, k, b):
    blocks = []
    for i in range(N_SUB):
        rs = slice(SUB * i, SUB * (i + 1))
        diag0 = _piece_offset(i) + SUB * i
        tile = diag0 // 128
        slabs = [a[rs, 128 * t:min(128 * (t + 1), A_COLS)] for t in range(pl.cdiv(A_COLS, 128))]
        slabs[tile] = _exact_diag(slabs[tile], q[rs], k[rs], b[rs], diag0 % 128)
        blocks.append(jnp.concatenate(slabs, axis=1))
    return jnp.concatenate(blocks, axis=0)


def _stack_values(v):
    return jnp.concatenate([v[0:SUB * (i + 1)] for i in range(N_SUB)], axis=0)


def _hgrn_rec_kernel(*refs, tm, n_side):
    q_ref, k_ref, b_ref, bmin_ref, v_ref = refs[:5]
    o_ref = refs[5 + n_side]
    st_ref, qd_scr, oi_scr, kv_scr, dl_scr = refs[6 + 2 * n_side:]
    _side_cast(refs[5:5 + n_side], refs[6 + n_side:6 + 2 * n_side])

    @pl.when(pl.program_id(2) == 0)
    def _():
        st_ref[...] = jnp.zeros_like(st_ref)

    n_chunks = tm // CHUNK

    def chunk_rows(c):
        return pl.ds(pl.multiple_of(c * CHUNK, CHUNK), CHUNK)

    def state_rows(c):
        return pl.ds(pl.multiple_of(c * HEAD_DIM, HEAD_DIM), HEAD_DIM)

    def decay_rows(c, n):
        return pl.ds(pl.multiple_of(c * 8, 8), n)

    def fast():
        keep = _score_mask()

        def intra_body(c, carry):
            ccs = [c * INTRA_UNROLL + u for u in range(INTRA_UNROLL)]
            fac = [_chunk_factors(q_ref, k_ref, b_ref, chunk_rows(cc)) for cc in ccs]
            scores = [jnp.where(keep, _dot_nt(f[1], f[2]), 0.0).astype(BF16) for f in fac]
            for cc, f in zip(ccs, fac):
                kv_scr[state_rows(cc), :] = _dot_tn(v_ref[0, 0, chunk_rows(cc), :], f[4])
                qd_scr[chunk_rows(cc), :] = f[3]
                dl_scr[decay_rows(cc, 8), :] = jnp.broadcast_to(f[5], (8, HEAD_DIM))
            for cc, a in zip(ccs, scores):
                rows = chunk_rows(cc)
                oi_scr[rows, :] = _dot(a, _stack_values(v_ref[0, 0, rows, :]))
            return carry

        lax.fori_loop(0, n_chunks // INTRA_UNROLL, intra_body, 0)

        def state_body(c, st):
            for u in range(STATE_UNROLL):
                cc = c * STATE_UNROLL + u
                rows = chunk_rows(cc)
                o = oi_scr[rows, :] + _dot_nt(qd_scr[rows, :], st.astype(BF16))
                o_ref[0, rows, :] = o.astype(o_ref.dtype)
                st = st * dl_scr[decay_rows(cc, 1), :] + kv_scr[state_rows(cc), :]
            return st

        st_ref[...] = lax.fori_loop(0, n_chunks // STATE_UNROLL, state_body, st_ref[...])

    def exact():
        keep = _score_mask()

        def body(c, st):
            rows = chunk_rows(c)
            (q, k, b), q_blk, k_all, q_dec, k_dec, dl = _chunk_factors(q_ref, k_ref, b_ref, rows)
            a = _fix_diag(jnp.where(keep, _dot_nt(q_blk, k_all), 0.0), q, k, b).astype(BF16)
            v = v_ref[0, 0, rows, :]
            o = _dot(a, _stack_values(v)) + _dot_nt(q_dec, st.astype(BF16))
            o_ref[0, rows, :] = o.astype(o_ref.dtype)
            return st * dl + _dot_tn(v, k_dec)

        st_ref[...] = lax.fori_loop(0, n_chunks, body, st_ref[...])

    clamp_may_bind = jnp.min(bmin_ref[0, 0]) < -(EXP2_CLAMP - 1.0)
    pl.when(jnp.logical_not(clamp_may_bind))(fast)
    pl.when(clamp_may_bind)(exact)


def _hgrn_rec(q, k, b2, bmin, v, side_weights, *, tm=4096):
    B, H, T, _ = q.shape
    tm = min(tm, T)
    assert T % tm == 0 and (tm // CHUNK) % INTRA_UNROLL == 0 and (tm // CHUNK) % STATE_UNROLL == 0
    n_min = bmin.shape[2] // (T // tm)
    spec = pl.BlockSpec((1, 1, tm, HEAD_DIM), lambda b, h, t: (b, h, t, 0))
    grid = (B, H, T // tm)
    side_in, side_in_specs, side_out_specs, side_shapes = _side_cast_plan(side_weights, grid)
    outs = pl.pallas_call(
        functools.partial(_hgrn_rec_kernel, tm=tm, n_side=len(side_in)),
        grid=grid,
        in_specs=[spec, spec, spec,
                  pl.BlockSpec((1, 1, n_min, 8, HEAD_DIM), lambda b, h, t: (b, h, t, 0, 0)),
                  spec] + side_in_specs,
        out_specs=[pl.BlockSpec((1, tm, HEAD_DIM), lambda b, h, t: (b, t, h))] + side_out_specs,
        out_shape=[jax.ShapeDtypeStruct((B, T, H * HEAD_DIM), BF16)] + side_shapes,
        scratch_shapes=[pltpu.VMEM((HEAD_DIM, HEAD_DIM), F32),
                        pltpu.VMEM((tm, HEAD_DIM), BF16),
                        pltpu.VMEM((tm, HEAD_DIM), F32),
                        pltpu.VMEM((tm // CHUNK * HEAD_DIM, HEAD_DIM), F32),
                        pltpu.VMEM((tm // CHUNK * 8, HEAD_DIM), F32)],
        compiler_params=_params(("parallel", "parallel", "arbitrary")),
        name="hgrn_rec",
    )(q, k, b2, bmin, v, *side_in)
    return outs[0], [o.reshape(w.shape[1:]) for o, (w, _) in zip(outs[1:], side_weights)]


def _hgrn_out_kernel(o_ref, g_ref, h_ref, gain_ref, w_ref, out_ref):
    y = _rms_rows(o_ref[...].astype(F32), gain_ref[...]) * g_ref[...].astype(F32)
    out_ref[...] = h_ref[...] + _dot(y.astype(BF16), w_ref[...])


def _hgrn_out(o, gate, h, gain, w_out, *, tm=512):
    N, D = h.shape
    row_spec = pl.BlockSpec((tm, D), lambda i: (i, 0))
    return pl.pallas_call(
        _hgrn_out_kernel,
        grid=(N // tm,),
        in_specs=[row_spec, row_spec, row_spec,
                  pl.BlockSpec((1, D), lambda i: (0, 0)),
                  pl.BlockSpec((D, D), lambda i: (0, 0))],
        out_specs=row_spec,
        out_shape=jax.ShapeDtypeStruct((N, D), F32),
        compiler_params=_params(("parallel",)),
        name="hgrn_out",
    )(o, gate, h, gain, w_out)


TAIL = 8


def _conv3(prev, u, cw):
    rows = u.shape[0]
    ue = jnp.concatenate([prev, u], axis=0)
    return (cw[0:1, :] * ue[TAIL - 2:TAIL - 2 + rows]
            + cw[1:2, :] * ue[TAIL - 1:TAIL - 1 + rows]
            + cw[2:3, :] * u)


def _start_tile(h_ref, nw_ref, xn_ref, out_ref, tails, first_in_sequence):
    j = pl.program_id(1)

    @pl.when(j == 0)
    def _():
        xn_ref[...] = _rms_rows(h_ref[...], nw_ref[...]).astype(BF16)
        out_ref[...] = h_ref[...]

    @pl.when(first_in_sequence)
    def _():
        for t in tails:
            t[j] = jnp.zeros(t.shape[1:], t.dtype)


def _ffn_kernel(h_ref, nw_ref, wg_ref, wv_ref, cwg_ref, cwv_ref, wd_ref, fw_ref,
                out_ref, xn_ref, tg_ref, tv_ref, *, tm, tiles_per_batch, final_norm, row_parts):
    j = pl.program_id(1)
    _start_tile(h_ref, nw_ref, xn_ref, out_ref, (tg_ref, tv_ref),
                pl.program_id(0) % tiles_per_batch == 0)

    rp = tm // row_parts
    xns = [xn_ref[p * rp:(p + 1) * rp, :] for p in range(row_parts)]
    ups = [(_dot(xn, wg_ref[...]), _dot(xn, wv_ref[...])) for xn in xns]
    prev_g, prev_v = tg_ref[j], tv_ref[j]
    acts = []
    for ug, uv in ups:
        cg = _conv3(prev_g, ug, cwg_ref[...])
        cv = _conv3(prev_v, uv, cwv_ref[...])
        prev_g, prev_v = ug[rp - TAIL:], uv[rp - TAIL:]
        acts.append((cg * _sigmoid(cg) * cv).astype(BF16))
    tg_ref[j] = prev_g
    tv_ref[j] = prev_v
    for p, act in enumerate(acts):
        out_ref[p * rp:(p + 1) * rp, :] += _dot(act, wd_ref[...])

    if final_norm:
        @pl.when(j == pl.num_programs(1) - 1)
        def _():
            out_ref[...] = _rms_rows(out_ref[...], fw_ref[...])


def _tile_spec(tm, d, single_buffer):
    if single_buffer:
        return pl.BlockSpec((tm, d), lambda i, j: (i, 0), pipeline_mode=pl.Buffered(1))
    return pl.BlockSpec((tm, d), lambda i, j: (i, 0))


def _ffn(h, seq_len, norm_w, w_up, conv_w, w_down, final_w, *, final_norm, tm=1024, tn=512,
         single_buffer=True, row_parts=ROW_PARTS, vmem_limit=VMEM_LIMIT):
    N, D = h.shape
    F = w_down.shape[0]
    nf = F // tn
    return pl.pallas_call(
        functools.partial(_ffn_kernel, tm=tm, tiles_per_batch=seq_len // tm,
                          final_norm=final_norm, row_parts=row_parts),
        grid=(N // tm, nf),
        in_specs=[
            _tile_spec(tm, D, single_buffer),
            pl.BlockSpec((1, D), lambda i, j: (0, 0)),
            pl.BlockSpec((D, tn), lambda i, j: (0, j)),
            pl.BlockSpec((D, tn), lambda i, j: (0, nf + j)),
            pl.BlockSpec((3, tn), lambda i, j: (0, j)),
            pl.BlockSpec((3, tn), lambda i, j: (0, nf + j)),
            pl.BlockSpec((tn, D), lambda i, j: (j, 0)),
            pl.BlockSpec((1, D), lambda i, j: (0, 0)),
        ],
        out_specs=pl.BlockSpec((tm, D), lambda i, j: (i, 0)),
        out_shape=jax.ShapeDtypeStruct((N, D), F32),
        scratch_shapes=[pltpu.VMEM((tm, D), BF16),
                        pltpu.VMEM((nf, TAIL, tn), F32), pltpu.VMEM((nf, TAIL, tn), F32)],
        compiler_params=_params(("arbitrary", "arbitrary"), vmem_limit),
        name="ffn_final" if final_norm else "ffn",
    )(h, norm_w, *([_bf16(w_up)] * 2), conv_w, conv_w, _bf16(w_down), final_w)


def _sc_kernel(h_ref, nw_ref, wb_ref, wc_ref, wh_ref, cw_ref, wd_ref,
               out_ref, xn_ref, tz_ref, *, tm, tiles_per_batch, row_parts):
    j = pl.program_id(1)
    _start_tile(h_ref, nw_ref, xn_ref, out_ref, (tz_ref,), pl.program_id(0) % tiles_per_batch == 0)

    rp = tm // row_parts
    xns = [xn_ref[p * rp:(p + 1) * rp, :] for p in range(row_parts)]
    ups = [(_dot(xn, wc_ref[...]), _dot(xn, wh_ref[...]), _dot(xn, wb_ref[...])) for xn in xns]
    prev_z = tz_ref[j]
    ys = []
    for uc, uh, gb in ups:
        z = uc * uh
        ys.append((gb * _conv3(prev_z, z, cw_ref[...])).astype(BF16))
        prev_z = z[rp - TAIL:]
    tz_ref[j] = prev_z
    for p, y in enumerate(ys):
        out_ref[p * rp:(p + 1) * rp, :] += _dot(y, wd_ref[...])


def _short_conv(h, seq_len, norm_w, w_in, conv_w, w_out, *, tm=512, tn=512, single_buffer=False,
                row_parts=1, vmem_limit=VMEM_LIMIT):
    N, D = h.shape
    nd = D // tn
    w_spec = lambda part: pl.BlockSpec((D, tn), lambda i, j: (0, part * nd + j))
    return pl.pallas_call(
        functools.partial(_sc_kernel, tm=tm, tiles_per_batch=seq_len // tm, row_parts=row_parts),
        grid=(N // tm, nd),
        in_specs=[
            _tile_spec(tm, D, single_buffer),
            pl.BlockSpec((1, D), lambda i, j: (0, 0)),
            w_spec(0), w_spec(1), w_spec(2),
            pl.BlockSpec((3, tn), lambda i, j: (0, j)),
            pl.BlockSpec((tn, D), lambda i, j: (j, 0)),
        ],
        out_specs=pl.BlockSpec((tm, D), lambda i, j: (i, 0)),
        out_shape=jax.ShapeDtypeStruct((N, D), F32),
        scratch_shapes=[pltpu.VMEM((tm, D), BF16), pltpu.VMEM((nd, TAIL, tn), F32)],
        compiler_params=_params(("arbitrary", "arbitrary"), vmem_limit),
        name="short_conv",
    )(h, norm_w, *([_bf16(w_in)] * 3), conv_w, _bf16(w_out))


def kernel(x, norm_mix, norm_ffn, hgrn_w_in, hgrn_lb_table, hgrn_out_norm, hgrn_w_out,
           sc_w_in, sc_conv, sc_w_out, ffn_w_up, ffn_conv, ffn_w_down, final_norm):
    B, T, D = x.shape
    depth = norm_mix.shape[0]
    n_mixers = 2
    row = lambda w: w.reshape(1, -1).astype(F32)
    stacks = dict(hgrn_w_in=hgrn_w_in, hgrn_w_out=hgrn_w_out, sc_w_in=sc_w_in, sc_w_out=sc_w_out,
                  ffn_w_up=ffn_w_up, ffn_w_down=ffn_w_down)

    def layer_weights(i):
        mixer = ("hgrn_w_in", "hgrn_w_out") if i % n_mixers == 0 else ("sc_w_in", "sc_w_out")
        return [(name, i // n_mixers) for name in mixer] + [("ffn_w_up", i), ("ffn_w_down", i)]

    cast = {}

    def weight(name, idx):
        return cast.get((name, idx), stacks[name][idx])

    h = x.reshape(B * T, D)
    for i in range(depth):
        j = i // n_mixers
        if i % n_mixers == 0:
            proj_side = layer_weights(i)[1:] if i == 0 else []
            rec_side = [kw for l in range(1, depth) for kw in layer_weights(l)] if i == 0 else []
            side = lambda keys: [(stacks[name], idx) for name, idx in keys]
            xn = _norm_cast(h, row(norm_mix[i])).reshape(B, T, D)
            outs, done = _hgrn_proj(xn, hgrn_lb_table.astype(F32), weight("hgrn_w_in", j),
                                    side(proj_side), layer=i)
            cast.update(zip(proj_side, done))
            q, k, b2, bmin, v, gate = outs
            o, done = _hgrn_rec(q, k, b2, bmin, v, side(rec_side))
            cast.update(zip(rec_side, done))
            h = _hgrn_out(o.reshape(B * T, D), gate, h, row(hgrn_out_norm[j]),
                          weight("hgrn_w_out", j).astype(BF16))
        else:
            h = _short_conv(h, T, row(norm_mix[i]), weight("sc_w_in", j), sc_conv[j].astype(F32),
                            weight("sc_w_out", j))
        h = _ffn(h, T, row(norm_ffn[i]), weight("ffn_w_up", i), ffn_conv[i].astype(F32),
                 weight("ffn_w_down", i), row(final_norm), final_norm=(i == depth - 1))
    return h.reshape(B, T, D)
```

```python
import functools
import math

import jax
import jax.numpy as jnp
from jax import lax
from jax.experimental import pallas as pl
from jax.experimental.pallas import tpu as pltpu

F32 = jnp.float32
BF16 = jnp.bfloat16

EPS = 1e-6
HEAD_DIM = 128
CHUNK = 64
SUB = 16
N_SUB = CHUNK // SUB
N_PIECES = N_SUB * (N_SUB + 1) // 2
A_COLS = SUB * N_PIECES
EXP2_CLAMP = 112.0
INTRA_UNROLL = 16
STATE_UNROLL = 16
ROW_PARTS = 2
VMEM_LIMIT = 56 * 1024 * 1024


def _dot(a, b):
    return jnp.dot(a, b, preferred_element_type=F32)


def _dot_nt(a, b):
    return lax.dot_general(a, b, (((1,), (1,)), ((), ())), preferred_element_type=F32)


def _dot_tn(a, b):
    return lax.dot_general(a, b, (((0,), (0,)), ((), ())), preferred_element_type=F32)


def _sigmoid(x):
    return 1.0 / (1.0 + jnp.exp(-x))


def _rms_rows(x, w):
    ms = jnp.mean(x * x, axis=-1, keepdims=True)
    return x * lax.rsqrt(ms + EPS) * w


def _params(semantics, vmem_limit=VMEM_LIMIT):
    return pltpu.CompilerParams(dimension_semantics=semantics, vmem_limit_bytes=vmem_limit)


def _bf16(w):
    return w.astype(BF16)


def _side_cast_plan(weights, grid):
    n_steps = math.prod(grid)
    strides = [math.prod(grid[d + 1:]) for d in range(len(grid))]

    def step(ids):
        return sum(i * s for i, s in zip(ids, strides))

    ins, in_specs, out_specs, shapes = [], [], [], []
    for stack, idx in weights:
        n, rows, cols = stack.shape
        assert rows % n_steps == 0, (stack.shape, grid)
        block = (None, rows // n_steps, cols)
        ins.append(stack.reshape(n * n_steps, rows // n_steps, cols))
        in_specs.append(pl.BlockSpec(block, lambda *ids, base=idx * n_steps: (base + step(ids), 0, 0)))
        out_specs.append(pl.BlockSpec(block, lambda *ids: (step(ids), 0, 0)))
        shapes.append(jax.ShapeDtypeStruct((n_steps, rows // n_steps, cols), BF16))
    return ins, in_specs, out_specs, shapes


def _side_cast(src_refs, dst_refs):
    for src, dst in zip(src_refs, dst_refs):
        dst[...] = src[...].astype(BF16)


def _chunk_cumsum(x):
    n, d = x.shape
    pos = lax.broadcasted_iota(jnp.int32, (n, d), 0) & (CHUNK - 1)
    s = 1
    while s < CHUNK:
        pad = max(8, s)
        xp = jnp.concatenate([jnp.zeros((pad, d), x.dtype), x], axis=0)
        x = x + jnp.where(pos >= s, xp[pad - s:pad - s + n], 0.0)
        s *= 2
    return x


def _norm_cast_kernel(x_ref, w_ref, o_ref):
    o_ref[...] = _rms_rows(x_ref[...], w_ref[...]).astype(BF16)


def _norm_cast(x, norm_w, *, tm=1024):
    N, D = x.shape
    return pl.pallas_call(
        _norm_cast_kernel,
        grid=(N // tm,),
        in_specs=[pl.BlockSpec((tm, D), lambda i: (i, 0)), pl.BlockSpec((1, D), lambda i: (0, 0))],
        out_specs=pl.BlockSpec((tm, D), lambda i: (i, 0)),
        out_shape=jax.ShapeDtypeStruct((N, D), BF16),
        compiler_params=_params(("parallel",)),
        name="norm_cast",
    )(x, norm_w)


def _hgrn_proj_kernel(*refs, layer, heads_per_step, n_side):
    xn_ref, lbt_ref = refs[:2]
    w_refs = refs[2:6]
    q_ref, k_ref, b_ref, bmin_ref, v_ref, g_ref = refs[6 + n_side:12 + n_side]
    wb_ref = refs[-1]
    _side_cast(refs[6:6 + n_side], refs[12 + n_side:12 + 2 * n_side])

    @pl.when((pl.program_id(1) == 0) & (pl.program_id(2) == 0))
    def _():
        for p, w_ref in enumerate(w_refs):
            wb_ref[p] = w_ref[...].astype(BF16)

    xn = xn_ref[...]

    rows = [lbt_ref[l:l + 1, :] for l in range(lbt_ref.shape[0])]
    m = functools.reduce(jnp.maximum, rows)
    es = [jnp.exp(r - m) for r in rows]
    lb = sum(es[:layer + 1]) / sum(es)

    q = _dot(xn, wb_ref[0])
    q = q * _sigmoid(q) * (HEAD_DIM ** -0.5)
    f = lb + (1.0 - lb) * _sigmoid(_dot(xn, wb_ref[1]))
    b2 = _chunk_cumsum(jnp.log2(f))
    k = 1.0 - f
    v = _dot(xn, wb_ref[2])
    g = _dot(xn, wb_ref[3])
    g_ref[...] = (g * _sigmoid(g)).astype(BF16)
    for hh in range(heads_per_step):
        sl = slice(hh * HEAD_DIM, (hh + 1) * HEAD_DIM)
        q_ref[0, hh] = q[:, sl].astype(BF16)
        k_ref[0, hh] = k[:, sl].astype(BF16)
        b_ref[0, hh] = b2[:, sl]
        bmin_ref[0, hh, 0] = jnp.min(b2[:, sl].reshape(-1, 8, HEAD_DIM), axis=0)
        v_ref[0, hh] = v[:, sl].astype(BF16)


def _hgrn_proj(xn, lb_table, w_in, side_weights, *, layer, tm=1024, heads_per_step=2):
    B, T, D = xn.shape
    xn = xn.reshape(B * T, D)
    H = D // HEAD_DIM
    tn = heads_per_step * HEAD_DIM
    nq = D // tn
    grid = (H // heads_per_step, B, T // tm)
    head_shape = (B, H, T, HEAD_DIM)
    head_spec = pl.BlockSpec((1, heads_per_step, tm, HEAD_DIM), lambda j, b, i: (b, j, i, 0))
    w_spec = lambda part: pl.BlockSpec((D, tn), lambda j, b, i: (0, part * nq + j))
    side_in, side_in_specs, side_out_specs, side_shapes = _side_cast_plan(side_weights, grid)
    outs = pl.pallas_call(
        functools.partial(_hgrn_proj_kernel, layer=layer, heads_per_step=heads_per_step,
                          n_side=len(side_in)),
        grid=grid,
        in_specs=[
            pl.BlockSpec((tm, D), lambda j, b, i: (b * (T // tm) + i, 0)),
            pl.BlockSpec((lb_table.shape[0], tn), lambda j, b, i: (0, j)),
            w_spec(0), w_spec(1), w_spec(2), w_spec(3),
        ] + side_in_specs,
        out_specs=[
            head_spec, head_spec, head_spec,
            pl.BlockSpec((1, heads_per_step, 1, 8, HEAD_DIM), lambda j, b, i: (b, j, i, 0, 0)),
            head_spec,
            pl.BlockSpec((tm, tn), lambda j, b, i: (b * (T // tm) + i, j)),
        ] + side_out_specs,
        out_shape=[
            jax.ShapeDtypeStruct(head_shape, BF16),
            jax.ShapeDtypeStruct(head_shape, BF16),
            jax.ShapeDtypeStruct(head_shape, F32),
            jax.ShapeDtypeStruct((B, H, T // tm, 8, HEAD_DIM), F32),
            jax.ShapeDtypeStruct(head_shape, BF16),
            jax.ShapeDtypeStruct((B * T, D), BF16),
        ] + side_shapes,
        scratch_shapes=[pltpu.VMEM((4, D, tn), BF16)],
        compiler_params=_params(("arbitrary", "arbitrary", "arbitrary")),
        name="hgrn_proj",
    )(xn, lb_table, *([w_in] * 4), *side_in)
    return outs[:6], [o.reshape(w.shape[1:]) for o, (w, _) in zip(outs[6:], side_weights)]


def _piece_offset(i):
    return SUB * i * (i + 1) // 2


def _score_mask():
    rw = lax.broadcasted_iota(jnp.int32, (CHUNK, A_COLS), 0)
    cw = lax.broadcasted_iota(jnp.int32, (CHUNK, A_COLS), 1)
    keep = None
    for i in range(N_SUB):
        diag0 = _piece_offset(i) + SUB * i
        in_rows = (rw >= SUB * i) & (rw < SUB * (i + 1))
        left = (cw >= _piece_offset(i)) & (cw < diag0)
        diag = (cw >= diag0) & (cw < diag0 + SUB) & (cw - diag0 <= rw - SUB * i)
        m = in_rows & (left | diag)
        keep = m if keep is None else keep | m
    return keep


def _exact_diag(slab, qb, kb, bb, lane0):
    lr = lax.broadcasted_iota(jnp.int32, slab.shape, 0)
    lc = lax.broadcasted_iota(jnp.int32, slab.shape, 1)
    for s in range(SUB):
        e = jnp.exp2(jnp.minimum(bb - bb[s:s + 1, :], 0.0))
        col = jnp.sum(qb * kb[s:s + 1, :] * e, axis=-1, keepdims=True)
        slab = jnp.where((lc == lane0 + s) & (lr >= s), col, slab)
    return slab


def _block_rows(rows):
    return jnp.concatenate([jnp.broadcast_to(r, (SUB, r.shape[-1])) for r in rows], axis=0)


def _chunk_factors(q_ref, k_ref, b_ref, rows):
    q = q_ref[0, 0, rows, :].astype(F32)
    k = k_ref[0, 0, rows, :].astype(F32)
    b = b_ref[0, 0, rows, :]

    ends = [b[SUB * (j + 1) - 1:SUB * (j + 1), :] for j in range(N_SUB)]
    refs = [jnp.zeros_like(ends[0])] + ends[:-1]
    b_last = ends[-1]
    d = b - _block_rows(refs)
    q_blk = (q * jnp.exp2(d)).astype(BF16)
    k_diag = k * jnp.exp2(jnp.minimum(-d, EXP2_CLAMP))
    k_end = k * jnp.exp2(_block_rows(ends) - b)
    pieces = []
    for i in range(N_SUB):
        for j in range(i):
            kj = k_end[SUB * j:SUB * (j + 1)]
            if j < i - 1:
                kj = kj * jnp.exp2(refs[i] - ends[j])
            pieces.append(kj.astype(BF16))
        pieces.append(k_diag[SUB * i:SUB * (i + 1)].astype(BF16))
    k_all = jnp.concatenate(pieces, axis=0)
    q_dec = (q * jnp.exp2(b)).astype(BF16)
    k_dec = (k_end * _block_rows([jnp.exp2(b_last - e) for e in ends])).astype(BF16)
    return (q, k, b), q_blk, k_all, q_dec, k_dec, jnp.exp2(b_last)


def _fix_diag(a, q, k, b):
    blocks = []
    for i in range(N_SUB):
        rs = slice(SUB * i, SUB * (i + 1))
        diag0 = _piece_offset(i) + SUB * i
        tile = diag0 // 128
        slabs = [a[rs, 128 * t:min(128 * (t + 1), A_COLS)] for t in range(pl.cdiv(A_COLS, 128))]
        slabs[tile] = _exact_diag(slabs[tile], q[rs], k[rs], b[rs], diag0 % 128)
        blocks.append(jnp.concatenate(slabs, axis=1))
    return jnp.concatenate(blocks, axis=0)


def _stack_values(v):
    return jnp.concatenate([v[0:SUB * (i + 1)] for i in range(N_SUB)], axis=0)


def _hgrn_rec_kernel(*refs, tm, n_side):
    q_ref, k_ref, b_ref, bmin_ref, v_ref = refs[:5]
    o_ref = refs[5 + n_side]
    st_ref, qd_scr, oi_scr, kv_scr, dl_scr = refs[6 + 2 * n_side:]
    _side_cast(refs[5:5 + n_side], refs[6 + n_side:6 + 2 * n_side])

    @pl.when(pl.program_id(2) == 0)
    def _():
        st_ref[...] = jnp.zeros_like(st_ref)

    n_chunks = tm // CHUNK

    def chunk_rows(c):
        return pl.ds(pl.multiple_of(c * CHUNK, CHUNK), CHUNK)

    def state_rows(c):
        return pl.ds(pl.multiple_of(c * HEAD_DIM, HEAD_DIM), HEAD_DIM)

    def decay_rows(c, n):
        return pl.ds(pl.multiple_of(c * 8, 8), n)

    def fast():
        keep = _score_mask()

        def intra_body(c, carry):
            ccs = [c * INTRA_UNROLL + u for u in range(INTRA_UNROLL)]
            fac = [_chunk_factors(q_ref, k_ref, b_ref, chunk_rows(cc)) for cc in ccs]
            scores = [jnp.where(keep, _dot_nt(f[1], f[2]), 0.0).astype(BF16) for f in fac]
            for cc, f in zip(ccs, fac):
                kv_scr[state_rows(cc), :] = _dot_tn(v_ref[0, 0, chunk_rows(cc), :], f[4])
                qd_scr[chunk_rows(cc), :] = f[3]
                dl_scr[decay_rows(cc, 8), :] = jnp.broadcast_to(f[5], (8, HEAD_DIM))
            for cc, a in zip(ccs, scores):
                rows = chunk_rows(cc)
                oi_scr[rows, :] = _dot(a, _stack_values(v_ref[0, 0, rows, :]))
            return carry

        lax.fori_loop(0, n_chunks // INTRA_UNROLL, intra_body, 0)

        def state_body(c, st):
            for u in range(STATE_UNROLL):
                cc = c * STATE_UNROLL + u
                rows = chunk_rows(cc)
                o = oi_scr[rows, :] + _dot_nt(qd_scr[rows, :], st.astype(BF16))
                o_ref[0, rows, :] = o.astype(o_ref.dtype)
                st = st * dl_scr[decay_rows(cc, 1), :] + kv_scr[state_rows(cc), :]
            return st

        st_ref[...] = lax.fori_loop(0, n_chunks // STATE_UNROLL, state_body, st_ref[...])

    def exact():
        keep = _score_mask()

        def body(c, st):
            rows = chunk_rows(c)
            (q, k, b), q_blk, k_all, q_dec, k_dec, dl = _chunk_factors(q_ref, k_ref, b_ref, rows)
            a = _fix_diag(jnp.where(keep, _dot_nt(q_blk, k_all), 0.0), q, k, b).astype(BF16)
            v = v_ref[0, 0, rows, :]
            o = _dot(a, _stack_values(v)) + _dot_nt(q_dec, st.astype(BF16))
            o_ref[0, rows, :] = o.astype(o_ref.dtype)
            return st * dl + _dot_tn(v, k_dec)

        st_ref[...] = lax.fori_loop(0, n_chunks, body, st_ref[...])

    clamp_may_bind = jnp.min(bmin_ref[0, 0]) < -(EXP2_CLAMP - 1.0)
    pl.when(jnp.logical_not(clamp_may_bind))(fast)
    pl.when(clamp_may_bind)(exact)


def _hgrn_rec(q, k, b2, bmin, v, side_weights, *, tm=4096):
    B, H, T, _ = q.shape
    tm = min(tm, T)
    assert T % tm == 0 and (tm // CHUNK) % INTRA_UNROLL == 0 and (tm // CHUNK) % STATE_UNROLL == 0
    n_min = bmin.shape[2] // (T // tm)
    spec = pl.BlockSpec((1, 1, tm, HEAD_DIM), lambda b, h, t: (b, h, t, 0))
    grid = (B, H, T // tm)
    side_in, side_in_specs, side_out_specs, side_shapes = _side_cast_plan(side_weights, grid)
    outs = pl.pallas_call(
        functools.partial(_hgrn_rec_kernel, tm=tm, n_side=len(side_in)),
        grid=grid,
        in_specs=[spec, spec, spec,
                  pl.BlockSpec((1, 1, n_min, 8, HEAD_DIM), lambda b, h, t: (b, h, t, 0, 0)),
                  spec] + side_in_specs,
        out_specs=[pl.BlockSpec((1, tm, HEAD_DIM), lambda b, h, t: (b, t, h))] + side_out_specs,
        out_shape=[jax.ShapeDtypeStruct((B, T, H * HEAD_DIM), BF16)] + side_shapes,
        scratch_shapes=[pltpu.VMEM((HEAD_DIM, HEAD_DIM), F32),
                        pltpu.VMEM((tm, HEAD_DIM), BF16),
                        pltpu.VMEM((tm, HEAD_DIM), F32),
                        pltpu.VMEM((tm // CHUNK * HEAD_DIM, HEAD_DIM), F32),
                        pltpu.VMEM((tm // CHUNK * 8, HEAD_DIM), F32)],
        compiler_params=_params(("parallel", "parallel", "arbitrary")),
        name="hgrn_rec",
    )(q, k, b2, bmin, v, *side_in)
    return outs[0], [o.reshape(w.shape[1:]) for o, (w, _) in zip(outs[1:], side_weights)]


def _hgrn_out_kernel(o_ref, g_ref, h_ref, gain_ref, w_ref, out_ref):
    y = _rms_rows(o_ref[...].astype(F32), gain_ref[...]) * g_ref[...].astype(F32)
    out_ref[...] = h_ref[...] + _dot(y.astype(BF16), w_ref[...])


def _hgrn_out(o, gate, h, gain, w_out, *, tm=512):
    N, D = h.shape
    row_spec = pl.BlockSpec((tm, D), lambda i: (i, 0))
    return pl.pallas_call(
        _hgrn_out_kernel,
        grid=(N // tm,),
        in_specs=[row_spec, row_spec, row_spec,
                  pl.BlockSpec((1, D), lambda i: (0, 0)),
                  pl.BlockSpec((D, D), lambda i: (0, 0))],
        out_specs=row_spec,
        out_shape=jax.ShapeDtypeStruct((N, D), F32),
        compiler_params=_params(("parallel",)),
        name="hgrn_out",
    )(o, gate, h, gain, w_out)


TAIL = 8


def _conv3(prev, u, cw):
    rows = u.shape[0]
    ue = jnp.concatenate([prev, u], axis=0)
    return (cw[0:1, :] * ue[TAIL - 2:TAIL - 2 + rows]
            + cw[1:2, :] * ue[TAIL - 1:TAIL - 1 + rows]
            + cw[2:3, :] * u)


def _start_tile(h_ref, nw_ref, xn_ref, out_ref, tails, first_in_sequence):
    j = pl.program_id(1)

    @pl.when(j == 0)
    def _():
        xn_ref[...] = _rms_rows(h_ref[...], nw_ref[...]).astype(BF16)
        out_ref[...] = h_ref[...]

    @pl.when(first_in_sequence)
    def _():
        for t in tails:
            t[j] = jnp.zeros(t.shape[1:], t.dtype)


def _ffn_kernel(h_ref, nw_ref, wg_ref, wv_ref, cwg_ref, cwv_ref, wd_ref, fw_ref,
                out_ref, xn_ref, tg_ref, tv_ref, *, tm, tiles_per_batch, final_norm, row_parts):
    j = pl.program_id(1)
    _start_tile(h_ref, nw_ref, xn_ref, out_ref, (tg_ref, tv_ref),
                pl.program_id(0) % tiles_per_batch == 0)

    rp = tm // row_parts
    xns = [xn_ref[p * rp:(p + 1) * rp, :] for p in range(row_parts)]
    ups = [(_dot(xn, wg_ref[...]), _dot(xn, wv_ref[...])) for xn in xns]
    prev_g, prev_v = tg_ref[j], tv_ref[j]
    acts = []
    for ug, uv in ups:
        cg = _conv3(prev_g, ug, cwg_ref[...])
        cv = _conv3(prev_v, uv, cwv_ref[...])
        prev_g, prev_v = ug[rp - TAIL:], uv[rp - TAIL:]
        acts.append((cg * _sigmoid(cg) * cv).astype(BF16))
    tg_ref[j] = prev_g
    tv_ref[j] = prev_v
    for p, act in enumerate(acts):
        out_ref[p * rp:(p + 1) * rp, :] += _dot(act, wd_ref[...])

    if final_norm:
        @pl.when(j == pl.num_programs(1) - 1)
        def _():
            out_ref[...] = _rms_rows(out_ref[...], fw_ref[...])


def _tile_spec(tm, d, single_buffer):
    if single_buffer:
        return pl.BlockSpec((tm, d), lambda i, j: (i, 0), pipeline_mode=pl.Buffered(1))
    return pl.BlockSpec((tm, d), lambda i, j: (i, 0))


def _ffn(h, seq_len, norm_w, w_up, conv_w, w_down, final_w, *, final_norm, tm=1024, tn=512,
         single_buffer=True, row_parts=ROW_PARTS, vmem_limit=VMEM_LIMIT):
    N, D = h.shape
    F = w_down.shape[0]
    nf = F // tn
    return pl.pallas_call(
        functools.partial(_ffn_kernel, tm=tm, tiles_per_batch=seq_len // tm,
                          final_norm=final_norm, row_parts=row_parts),
        grid=(N // tm, nf),
        in_specs=[
            _tile_spec(tm, D, single_buffer),
            pl.BlockSpec((1, D), lambda i, j: (0, 0)),
            pl.BlockSpec((D, tn), lambda i, j: (0, j)),
            pl.BlockSpec((D, tn), lambda i, j: (0, nf + j)),
            pl.BlockSpec((3, tn), lambda i, j: (0, j)),
            pl.BlockSpec((3, tn), lambda i, j: (0, nf + j)),
            pl.BlockSpec((tn, D), lambda i, j: (j, 0)),
            pl.BlockSpec((1, D), lambda i, j: (0, 0)),
        ],
        out_specs=pl.BlockSpec((tm, D), lambda i, j: (i, 0)),
        out_shape=jax.ShapeDtypeStruct((N, D), F32),
        scratch_shapes=[pltpu.VMEM((tm, D), BF16),
                        pltpu.VMEM((nf, TAIL, tn), F32), pltpu.VMEM((nf, TAIL, tn), F32)],
        compiler_params=_params(("arbitrary", "arbitrary"), vmem_limit),
        name="ffn_final" if final_norm else "ffn",
    )(h, norm_w, *([_bf16(w_up)] * 2), conv_w, conv_w, _bf16(w_down), final_w)


def _sc_kernel(h_ref, nw_ref, wb_ref, wc_ref, wh_ref, cw_ref, wd_ref,
               out_ref, xn_ref, tz_ref, *, tm, tiles_per_batch, row_parts):
    j = pl.program_id(1)
    _start_tile(h_ref, nw_ref, xn_ref, out_ref, (tz_ref,), pl.program_id(0) % tiles_per_batch == 0)

    rp = tm // row_parts
    xns = [xn_ref[p * rp:(p + 1) * rp, :] for p in range(row_parts)]
    ups = [(_dot(xn, wc_ref[...]), _dot(xn, wh_ref[...]), _dot(xn, wb_ref[...])) for xn in xns]
    prev_z = tz_ref[j]
    ys = []
    for uc, uh, gb in ups:
        z = uc * uh
        ys.append((gb * _conv3(prev_z, z, cw_ref[...])).astype(BF16))
        prev_z = z[rp - TAIL:]
    tz_ref[j] = prev_z
    for p, y in enumerate(ys):
        out_ref[p * rp:(p + 1) * rp, :] += _dot(y, wd_ref[...])


def _short_conv(h, seq_len, norm_w, w_in, conv_w, w_out, *, tm=512, tn=512, single_buffer=False,
                row_parts=1, vmem_limit=VMEM_LIMIT):
    N, D = h.shape
    nd = D // tn
    w_spec = lambda part: pl.BlockSpec((D, tn), lambda i, j: (0, part * nd + j))
    return pl.pallas_call(
        functools.partial(_sc_kernel, tm=tm, tiles_per_batch=seq_len // tm, row_parts=row_parts),
        grid=(N // tm, nd),
        in_specs=[
            _tile_spec(tm, D, single_buffer),
            pl.BlockSpec((1, D), lambda i, j: (0, 0)),
            w_spec(0), w_spec(1), w_spec(2),
            pl.BlockSpec((3, tn), lambda i, j: (0, j)),
            pl.BlockSpec((tn, D), lambda i, j: (j, 0)),
        ],
        out_specs=pl.BlockSpec((tm, D), lambda i, j: (i, 0)),
        out_shape=jax.ShapeDtypeStruct((N, D), F32),
        scratch_shapes=[pltpu.VMEM((tm, D), BF16), pltpu.VMEM((nd, TAIL, tn), F32)],
        compiler_params=_params(("arbitrary", "arbitrary"), vmem_limit),
        name="short_conv",
    )(h, norm_w, *([_bf16(w_in)] * 3), conv_w, _bf16(w_out))


def kernel(x, norm_mix, norm_ffn, hgrn_w_in, hgrn_lb_table, hgrn_out_norm, hgrn_w_out,
           sc_w_in, sc_conv, sc_w_out, ffn_w_up, ffn_conv, ffn_w_down, final_norm):
    B, T, D = x.shape
    depth = norm_mix.shape[0]
    n_mixers = 2
    row = lambda w: w.reshape(1, -1).astype(F32)
    stacks = dict(hgrn_w_in=hgrn_w_in, hgrn_w_out=hgrn_w_out, sc_w_in=sc_w_in, sc_w_out=sc_w_out,
                  ffn_w_up=ffn_w_up, ffn_w_down=ffn_w_down)

    def layer_weights(i):
        mixer = ("hgrn_w_in", "hgrn_w_out") if i % n_mixers == 0 else ("sc_w_in", "sc_w_out")
        return [(name, i // n_mixers) for name in mixer] + [("ffn_w_up", i), ("ffn_w_down", i)]

    cast = {}

    def weight(name, idx):
        return cast.get((name, idx), stacks[name][idx])

    h = x.reshape(B * T, D)
    for i in range(depth):
        j = i // n_mixers
        if i % n_mixers == 0:
            proj_side = layer_weights(i)[1:] if i == 0 else []
            rec_side = [kw for l in range(1, depth) for kw in layer_weights(l)] if i == 0 else []
            side = lambda keys: [(stacks[name], idx) for name, idx in keys]
            xn = _norm_cast(h, row(norm_mix[i])).reshape(B, T, D)
            outs, done = _hgrn_proj(xn, hgrn_lb_table.astype(F32), weight("hgrn_w_in", j),
                                    side(proj_side), layer=i)
            cast.update(zip(proj_side, done))
            q, k, b2, bmin, v, gate = outs
            o, done = _hgrn_rec(q, k, b2, bmin, v, side(rec_side))
            cast.update(zip(rec_side, done))
            h = _hgrn_out(o.reshape(B * T, D), gate, h, row(hgrn_out_norm[j]),
                          weight("hgrn_w_out", j).astype(BF16))
        else:
            h = _short_conv(h, T, row(norm_mix[i]), weight("sc_w_in", j), sc_conv[j].astype(F32),
                            weight("sc_w_out", j))
        h = _ffn(h, T, row(norm_ffn[i]), weight("ffn_w_up", i), ffn_conv[i].astype(F32),
                 weight("ffn_w_down", i), row(final_norm), final_norm=(i == depth - 1))
    return h.reshape(B, T, D)
```

```python
import functools
import math

import jax
import jax.numpy as jnp
from jax import lax
from jax.experimental import pallas as pl
from jax.experimental.pallas import tpu as pltpu

F32 = jnp.float32
BF16 = jnp.bfloat16

EPS = 1e-6
HEAD_DIM = 128
CHUNK = 64
SUB = 16
N_SUB = CHUNK // SUB
N_PIECES = N_SUB * (N_SUB + 1) // 2
A_COLS = SUB * N_PIECES
EXP2_CLAMP = 112.0
INTRA_UNROLL = 16
STATE_UNROLL = 16
ROW_PARTS = 2
VMEM_LIMIT = 56 * 1024 * 1024


def _dot(a, b):
    return jnp.dot(a, b, preferred_element_type=F32)


def _dot_nt(a, b):
    return lax.dot_general(a, b, (((1,), (1,)), ((), ())), preferred_element_type=F32)


def _dot_tn(a, b):
    return lax.dot_general(a, b, (((0,), (0,)), ((), ())), preferred_element_type=F32)


def _sigmoid(x):
    return 1.0 / (1.0 + jnp.exp(-x))


def _rms_rows(x, w):
    ms = jnp.mean(x * x, axis=-1, keepdims=True)
    return x * lax.rsqrt(ms + EPS) * w


def _params(semantics, vmem_limit=VMEM_LIMIT):
    return pltpu.CompilerParams(dimension_semantics=semantics, vmem_limit_bytes=vmem_limit)


def _bf16(w):
    return w.astype(BF16)


def _side_cast_plan(weights, grid):
    n_steps = math.prod(grid)
    strides = [math.prod(grid[d + 1:]) for d in range(len(grid))]

    def step(ids):
        return sum(i * s for i, s in zip(ids, strides))

    ins, in_specs, out_specs, shapes = [], [], [], []
    for stack, idx in weights:
        n, rows, cols = stack.shape
        assert rows % n_steps == 0, (stack.shape, grid)
        block = (None, rows // n_steps, cols)
        ins.append(stack.reshape(n * n_steps, rows // n_steps, cols))
        in_specs.append(pl.BlockSpec(block, lambda *ids, base=idx * n_steps: (base + step(ids), 0, 0)))
        out_specs.append(pl.BlockSpec(block, lambda *ids: (step(ids), 0, 0)))
        shapes.append(jax.ShapeDtypeStruct((n_steps, rows // n_steps, cols), BF16))
    return ins, in_specs, out_specs, shapes


def _side_cast(src_refs, dst_refs):
    for src, dst in zip(src_refs, dst_refs):
        dst[...] = src[...].astype(BF16)


def _chunk_cumsum(x):
    n, d = x.shape
    pos = lax.broadcasted_iota(jnp.int32, (n, d), 0) & (CHUNK - 1)
    s = 1
    while s < CHUNK:
        pad = max(8, s)
        xp = jnp.concatenate([jnp.zeros((pad, d), x.dtype), x], axis=0)
        x = x + jnp.where(pos >= s, xp[pad - s:pad - s + n], 0.0)
        s *= 2
    return x


def _norm_cast_kernel(x_ref, w_ref, o_ref):
    o_ref[...] = _rms_rows(x_ref[...], w_ref[...]).astype(BF16)


def _norm_cast(x, norm_w, *, tm=1024):
    N, D = x.shape
    return pl.pallas_call(
        _norm_cast_kernel,
        grid=(N // tm,),
        in_specs=[pl.BlockSpec((tm, D), lambda i: (i, 0)), pl.BlockSpec((1, D), lambda i: (0, 0))],
        out_specs=pl.BlockSpec((tm, D), lambda i: (i, 0)),
        out_shape=jax.ShapeDtypeStruct((N, D), BF16),
        compiler_params=_params(("parallel",)),
        name="norm_cast",
    )(x, norm_w)


def _hgrn_proj_kernel(*refs, layer, heads_per_step, n_side):
    xn_ref, lbt_ref = refs[:2]
    w_refs = refs[2:6]
    q_ref, k_ref, b_ref, bmin_ref, v_ref, g_ref = refs[6 + n_side:12 + n_side]
    wb_ref = refs[-1]
    _side_cast(refs[6:6 + n_side], refs[12 + n_side:12 + 2 * n_side])

    @pl.when((pl.program_id(1) == 0) & (pl.program_id(2) == 0))
    def _():
        for p, w_ref in enumerate(w_refs):
            wb_ref[p] = w_ref[...].astype(BF16)

    xn = xn_ref[...]

    rows = [lbt_ref[l:l + 1, :] for l in range(lbt_ref.shape[0])]
    m = functools.reduce(jnp.maximum, rows)
    es = [jnp.exp(r - m) for r in rows]
    lb = sum(es[:layer + 1]) / sum(es)

    q = _dot(xn, wb_ref[0])
    q = q * _sigmoid(q) * (HEAD_DIM ** -0.5)
    f = lb + (1.0 - lb) * _sigmoid(_dot(xn, wb_ref[1]))
    b2 = _chunk_cumsum(jnp.log2(f))
    k = 1.0 - f
    v = _dot(xn, wb_ref[2])
    g = _dot(xn, wb_ref[3])
    g_ref[...] = (g * _sigmoid(g)).astype(BF16)
    for hh in range(heads_per_step):
        sl = slice(hh * HEAD_DIM, (hh + 1) * HEAD_DIM)
        q_ref[0, hh] = q[:, sl].astype(BF16)
        k_ref[0, hh] = k[:, sl].astype(BF16)
        b_ref[0, hh] = b2[:, sl]
        bmin_ref[0, hh, 0] = jnp.min(b2[:, sl].reshape(-1, 8, HEAD_DIM), axis=0)
        v_ref[0, hh] = v[:, sl].astype(BF16)


def _hgrn_proj(xn, lb_table, w_in, side_weights, *, layer, tm=1024, heads_per_step=2):
    B, T, D = xn.shape
    xn = xn.reshape(B * T, D)
    H = D // HEAD_DIM
    tn = heads_per_step * HEAD_DIM
    nq = D // tn
    grid = (H // heads_per_step, B, T // tm)
    head_shape = (B, H, T, HEAD_DIM)
    head_spec = pl.BlockSpec((1, heads_per_step, tm, HEAD_DIM), lambda j, b, i: (b, j, i, 0))
    w_spec = lambda part: pl.BlockSpec((D, tn), lambda j, b, i: (0, part * nq + j))
    side_in, side_in_specs, side_out_specs, side_shapes = _side_cast_plan(side_weights, grid)
    outs = pl.pallas_call(
        functools.partial(_hgrn_proj_kernel, layer=layer, heads_per_step=heads_per_step,
                          n_side=len(side_in)),
        grid=grid,
        in_specs=[
            pl.BlockSpec((tm, D), lambda j, b, i: (b * (T // tm) + i, 0)),
            pl.BlockSpec((lb_table.shape[0], tn), lambda j, b, i: (0, j)),
            w_spec(0), w_spec(1), w_spec(2), w_spec(3),
        ] + side_in_specs,
        out_specs=[
            head_spec, head_spec, head_spec,
            pl.BlockSpec((1, heads_per_step, 1, 8, HEAD_DIM), lambda j, b, i: (b, j, i, 0, 0)),
            head_spec,
            pl.BlockSpec((tm, tn), lambda j, b, i: (b * (T // tm) + i, j)),
        ] + side_out_specs,
        out_shape=[
            jax.ShapeDtypeStruct(head_shape, BF16),
            jax.ShapeDtypeStruct(head_shape, BF16),
            jax.ShapeDtypeStruct(head_shape, F32),
            jax.ShapeDtypeStruct((B, H, T // tm, 8, HEAD_DIM), F32),
            jax.ShapeDtypeStruct(head_shape, BF16),
            jax.ShapeDtypeStruct((B * T, D), BF16),
        ] + side_shapes,
        scratch_shapes=[pltpu.VMEM((4, D, tn), BF16)],
        compiler_params=_params(("arbitrary", "arbitrary", "arbitrary")),
        name="hgrn_proj",
    )(xn, lb_table, *([w_in] * 4), *side_in)
    return outs[:6], [o.reshape(w.shape[1:]) for o, (w, _) in zip(outs[6:], side_weights)]


def _piece_offset(i):
    return SUB * i * (i + 1) // 2


def _score_mask():
    rw = lax.broadcasted_iota(jnp.int32, (CHUNK, A_COLS), 0)
    cw = lax.broadcasted_iota(jnp.int32, (CHUNK, A_COLS), 1)
    keep = None
    for i in range(N_SUB):
        diag0 = _piece_offset(i) + SUB * i
        in_rows = (rw >= SUB * i) & (rw < SUB * (i + 1))
        left = (cw >= _piece_offset(i)) & (cw < diag0)
        diag = (cw >= diag0) & (cw < diag0 + SUB) & (cw - diag0 <= rw - SUB * i)
        m = in_rows & (left | diag)
        keep = m if keep is None else keep | m
    return keep


def _exact_diag(slab, qb, kb, bb, lane0):
    lr = lax.broadcasted_iota(jnp.int32, slab.shape, 0)
    lc = lax.broadcasted_iota(jnp.int32, slab.shape, 1)
    for s in range(SUB):
        e = jnp.exp2(jnp.minimum(bb - bb[s:s + 1, :], 0.0))
        col = jnp.sum(qb * kb[s:s + 1, :] * e, axis=-1, keepdims=True)
        slab = jnp.where((lc == lane0 + s) & (lr >= s), col, slab)
    return slab


def _block_rows(rows):
    return jnp.concatenate([jnp.broadcast_to(r, (SUB, r.shape[-1])) for r in rows], axis=0)


def _chunk_factors(q_ref, k_ref, b_ref, rows):
    q = q_ref[0, 0, rows, :].astype(F32)
    k = k_ref[0, 0, rows, :].astype(F32)
    b = b_ref[0, 0, rows, :]

    ends = [b[SUB * (j + 1) - 1:SUB * (j + 1), :] for j in range(N_SUB)]
    refs = [jnp.zeros_like(ends[0])] + ends[:-1]
    b_last = ends[-1]
    d = b - _block_rows(refs)
    q_blk = (q * jnp.exp2(d)).astype(BF16)
    k_diag = k * jnp.exp2(jnp.minimum(-d, EXP2_CLAMP))
    k_end = k * jnp.exp2(_block_rows(ends) - b)
    pieces = []
    for i in range(N_SUB):
        for j in range(i):
            kj = k_end[SUB * j:SUB * (j + 1)]
            if j < i - 1:
                kj = kj * jnp.exp2(refs[i] - ends[j])
            pieces.append(kj.astype(BF16))
        pieces.append(k_diag[SUB * i:SUB * (i + 1)].astype(BF16))
    k_all = jnp.concatenate(pieces, axis=0)
    q_dec = (q * jnp.exp2(b)).astype(BF16)
    k_dec = (k_end * _block_rows([jnp.exp2(b_last - e) for e in ends])).astype(BF16)
    return (q, k, b), q_blk, k_all, q_dec, k_dec, jnp.exp2(b_last)


def _fix_diag(a, q, k, b):
    blocks = []
    for i in range(N_SUB):
        rs = slice(SUB * i, SUB * (i + 1))
        diag0 = _piece_offset(i) + SUB * i
        tile = diag0 // 128
        slabs = [a[rs, 128 * t:min(128 * (t + 1), A_COLS)] for t in range(pl.cdiv(A_COLS, 128))]
        slabs[tile] = _exact_diag(slabs[tile], q[rs], k[rs], b[rs], diag0 % 128)
        blocks.append(jnp.concatenate(slabs, axis=1))
    return jnp.concatenate(blocks, axis=0)


def _stack_values(v):
    return jnp.concatenate([v[0:SUB * (i + 1)] for i in range(N_SUB)], axis=0)


def _hgrn_rec_kernel(*refs, tm, n_side):
    q_ref, k_ref, b_ref, bmin_ref, v_ref = refs[:5]
    o_ref = refs[5 + n_side]
    st_ref, qd_scr, oi_scr, kv_scr, dl_scr = refs[6 + 2 * n_side:]
    _side_cast(refs[5:5 + n_side], refs[6 + n_side:6 + 2 * n_side])

    @pl.when(pl.program_id(2) == 0)
    def _():
        st_ref[...] = jnp.zeros_like(st_ref)

    n_chunks = tm // CHUNK

    def chunk_rows(c):
        return pl.ds(pl.multiple_of(c * CHUNK, CHUNK), CHUNK)

    def state_rows(c):
        return pl.ds(pl.multiple_of(c * HEAD_DIM, HEAD_DIM), HEAD_DIM)

    def decay_rows(c, n):
        return pl.ds(pl.multiple_of(c * 8, 8), n)

    keep = _score_mask()

    def fast():
        def intra_body(c, carry):
            ccs = [c * INTRA_UNROLL + u for u in range(INTRA_UNROLL)]
            fac = [_chunk_factors(q_ref, k_ref, b_ref, chunk_rows(cc)) for cc in ccs]
            scores = [jnp.where(keep, _dot_nt(f[1], f[2]), 0.0).astype(BF16) for f in fac]
            for cc, f in zip(ccs, fac):
                kv_scr[state_rows(cc), :] = _dot_tn(v_ref[0, 0, chunk_rows(cc), :], f[4])
                qd_scr[chunk_rows(cc), :] = f[3]
                dl_scr[decay_rows(cc, 8), :] = jnp.broadcast_to(f[5], (8, HEAD_DIM))
            for cc, a in zip(ccs, scores):
                rows = chunk_rows(cc)
                oi_scr[rows, :] = _dot(a, _stack_values(v_ref[0, 0, rows, :]))
            return carry

        lax.fori_loop(0, n_chunks // INTRA_UNROLL, intra_body, 0)

        def state_body(c, st):
            for u in range(STATE_UNROLL):
                cc = c * STATE_UNROLL + u
                rows = chunk_rows(cc)
                o = oi_scr[rows, :] + _dot_nt(qd_scr[rows, :], st.astype(BF16))
                o_ref[0, rows, :] = o.astype(o_ref.dtype)
                st = st * dl_scr[decay_rows(cc, 1), :] + kv_scr[state_rows(cc), :]
            return st

        st_ref[...] = lax.fori_loop(0, n_chunks // STATE_UNROLL, state_body, st_ref[...])

    def exact():
        def body(c, st):
            rows = chunk_rows(c)
            (q, k, b), q_blk, k_all, q_dec, k_dec, dl = _chunk_factors(q_ref, k_ref, b_ref, rows)
            a = _fix_diag(jnp.where(keep, _dot_nt(q_blk, k_all), 0.0), q, k, b).astype(BF16)
            v = v_ref[0, 0, rows, :]
            o = _dot(a, _stack_values(v)) + _dot_nt(q_dec, st.astype(BF16))
            o_ref[0, rows, :] = o.astype(o_ref.dtype)
            return st * dl + _dot_tn(v, k_dec)

        st_ref[...] = lax.fori_loop(0, n_chunks, body, st_ref[...])

    clamp_may_bind = jnp.min(bmin_ref[0, 0]) < -(EXP2_CLAMP - 1.0)
    pl.when(jnp.logical_not(clamp_may_bind))(fast)
    pl.when(clamp_may_bind)(exact)


def _hgrn_rec(q, k, b2, bmin, v, side_weights, *, tm=4096):
    B, H, T, _ = q.shape
    tm = min(tm, T)
    assert T % tm == 0 and (tm // CHUNK) % INTRA_UNROLL == 0 and (tm // CHUNK) % STATE_UNROLL == 0
    n_min = bmin.shape[2] // (T // tm)
    spec = pl.BlockSpec((1, 1, tm, HEAD_DIM), lambda b, h, t: (b, h, t, 0))
    grid = (B, H, T // tm)
    side_in, side_in_specs, side_out_specs, side_shapes = _side_cast_plan(side_weights, grid)
    outs = pl.pallas_call(
        functools.partial(_hgrn_rec_kernel, tm=tm, n_side=len(side_in)),
        grid=grid,
        in_specs=[spec, spec, spec,
                  pl.BlockSpec((1, 1, n_min, 8, HEAD_DIM), lambda b, h, t: (b, h, t, 0, 0)),
                  spec] + side_in_specs,
        out_specs=[pl.BlockSpec((1, tm, HEAD_DIM), lambda b, h, t: (b, t, h))] + side_out_specs,
        out_shape=[jax.ShapeDtypeStruct((B, T, H * HEAD_DIM), BF16)] + side_shapes,
        scratch_shapes=[pltpu.VMEM((HEAD_DIM, HEAD_DIM), F32),
                        pltpu.VMEM((tm, HEAD_DIM), BF16),
                        pltpu.VMEM((tm, HEAD_DIM), F32),
                        pltpu.VMEM((tm // CHUNK * HEAD_DIM, HEAD_DIM), F32),
                        pltpu.VMEM((tm // CHUNK * 8, HEAD_DIM), F32)],
        compiler_params=_params(("parallel", "parallel", "arbitrary")),
        name="hgrn_rec",
    )(q, k, b2, bmin, v, *side_in)
    return outs[0], [o.reshape(w.shape[1:]) for o, (w, _) in zip(outs[1:], side_weights)]


def _hgrn_out_kernel(o_ref, g_ref, h_ref, gain_ref, w_ref, out_ref):
    y = _rms_rows(o_ref[...].astype(F32), gain_ref[...]) * g_ref[...].astype(F32)
    out_ref[...] = h_ref[...] + _dot(y.astype(BF16), w_ref[...])


def _hgrn_out(o, gate, h, gain, w_out, *, tm=512):
    N, D = h.shape
    row_spec = pl.BlockSpec((tm, D), lambda i: (i, 0))
    return pl.pallas_call(
        _hgrn_out_kernel,
        grid=(N // tm,),
        in_specs=[row_spec, row_spec, row_spec,
                  pl.BlockSpec((1, D), lambda i: (0, 0)),
                  pl.BlockSpec((D, D), lambda i: (0, 0))],
        out_specs=row_spec,
        out_shape=jax.ShapeDtypeStruct((N, D), F32),
        compiler_params=_params(("parallel",)),
        name="hgrn_out",
    )(o, gate, h, gain, w_out)


TAIL = 8


def _conv3(prev, u, cw):
    rows = u.shape[0]
    ue = jnp.concatenate([prev, u], axis=0)
    return (cw[0:1, :] * ue[TAIL - 2:TAIL - 2 + rows]
            + cw[1:2, :] * ue[TAIL - 1:TAIL - 1 + rows]
            + cw[2:3, :] * u)


def _start_tile(h_ref, nw_ref, xn_ref, out_ref):
    xn_ref[...] = _rms_rows(h_ref[...], nw_ref[...]).astype(BF16)
    out_ref[...] = h_ref[...]


def _reset_tails(tails, first_in_sequence):
    @pl.when(first_in_sequence)
    def _():
        for t in tails:
            t[pl.program_id(1)] = jnp.zeros(t.shape[1:], t.dtype)


def _ffn_kernel(h_hbm, nw_ref, wg_ref, wv_ref, cwg_ref, cwv_ref, wd_ref, fw_ref,
                out_ref, xn_ref, tg_ref, tv_ref, h_buf, h_sem,
                *, tm, tiles_per_batch, final_norm, row_parts):
    i = pl.program_id(0)
    j = pl.program_id(1)

    def tile_copy(t):
        rows = pl.ds(pl.multiple_of(t * tm, tm), tm)
        return pltpu.make_async_copy(h_hbm.at[rows, :], h_buf, h_sem)

    @pl.when((i == 0) & (j == 0))
    def _():
        tile_copy(i).start()

    @pl.when(j == 0)
    def _():
        tile_copy(i).wait()
        _start_tile(h_buf, nw_ref, xn_ref, out_ref)

    @pl.when((j == 1) & (i + 1 < pl.num_programs(0)))
    def _():
        tile_copy(i + 1).start()

    _reset_tails((tg_ref, tv_ref), i % tiles_per_batch == 0)

    rp = tm // row_parts
    xns = [xn_ref[p * rp:(p + 1) * rp, :] for p in range(row_parts)]
    ups = [(_dot(xn, wg_ref[...]), _dot(xn, wv_ref[...])) for xn in xns]
    prev_g, prev_v = tg_ref[j], tv_ref[j]
    acts = []
    for ug, uv in ups:
        cg = _conv3(prev_g, ug, cwg_ref[...])
        cv = _conv3(prev_v, uv, cwv_ref[...])
        prev_g, prev_v = ug[rp - TAIL:], uv[rp - TAIL:]
        acts.append((cg * _sigmoid(cg) * cv).astype(BF16))
    tg_ref[j] = prev_g
    tv_ref[j] = prev_v
    for p, act in enumerate(acts):
        out_ref[p * rp:(p + 1) * rp, :] += _dot(act, wd_ref[...])

    if final_norm:
        @pl.when(j == pl.num_programs(1) - 1)
        def _():
            out_ref[...] = _rms_rows(out_ref[...], fw_ref[...])


def _ffn(h, seq_len, norm_w, w_up, conv_w, w_down, final_w, *, final_norm, tm=1024, tn=512,
         row_parts=ROW_PARTS, vmem_limit=VMEM_LIMIT):
    N, D = h.shape
    F = w_down.shape[0]
    nf = F // tn
    assert nf >= 2
    return pl.pallas_call(
        functools.partial(_ffn_kernel, tm=tm, tiles_per_batch=seq_len // tm,
                          final_norm=final_norm, row_parts=row_parts),
        grid=(N // tm, nf),
        in_specs=[
            pl.BlockSpec(memory_space=pl.ANY),
            pl.BlockSpec((1, D), lambda i, j: (0, 0)),
            pl.BlockSpec((D, tn), lambda i, j: (0, j)),
            pl.BlockSpec((D, tn), lambda i, j: (0, nf + j)),
            pl.BlockSpec((3, tn), lambda i, j: (0, j)),
            pl.BlockSpec((3, tn), lambda i, j: (0, nf + j)),
            pl.BlockSpec((tn, D), lambda i, j: (j, 0)),
            pl.BlockSpec((1, D), lambda i, j: (0, 0)),
        ],
        out_specs=pl.BlockSpec((tm, D), lambda i, j: (i, 0)),
        out_shape=jax.ShapeDtypeStruct((N, D), F32),
        scratch_shapes=[pltpu.VMEM((tm, D), BF16),
                        pltpu.VMEM((nf, TAIL, tn), F32), pltpu.VMEM((nf, TAIL, tn), F32),
                        pltpu.VMEM((tm, D), F32), pltpu.SemaphoreType.DMA(())],
        compiler_params=_params(("arbitrary", "arbitrary"), vmem_limit),
        name="ffn_final" if final_norm else "ffn",
    )(h, norm_w, *([_bf16(w_up)] * 2), conv_w, conv_w, _bf16(w_down), final_w)


def _sc_kernel(h_ref, nw_ref, wb_ref, wc_ref, wh_ref, cw_ref, wd_ref,
               out_ref, xn_ref, tz_ref, *, tm, tiles_per_batch, row_parts):
    j = pl.program_id(1)

    @pl.when(j == 0)
    def _():
        _start_tile(h_ref, nw_ref, xn_ref, out_ref)

    _reset_tails((tz_ref,), pl.program_id(0) % tiles_per_batch == 0)

    rp = tm // row_parts
    xns = [xn_ref[p * rp:(p + 1) * rp, :] for p in range(row_parts)]
    ups = [(_dot(xn, wc_ref[...]), _dot(xn, wh_ref[...]), _dot(xn, wb_ref[...])) for xn in xns]
    prev_z = tz_ref[j]
    ys = []
    for uc, uh, gb in ups:
        z = uc * uh
        ys.append((gb * _conv3(prev_z, z, cw_ref[...])).astype(BF16))
        prev_z = z[rp - TAIL:]
    tz_ref[j] = prev_z
    for p, y in enumerate(ys):
        out_ref[p * rp:(p + 1) * rp, :] += _dot(y, wd_ref[...])


def _short_conv(h, seq_len, norm_w, w_in, conv_w, w_out, *, tm=512, tn=512, row_parts=1,
                vmem_limit=VMEM_LIMIT):
    N, D = h.shape
    nd = D // tn
    w_spec = lambda part: pl.BlockSpec((D, tn), lambda i, j: (0, part * nd + j))
    return pl.pallas_call(
        functools.partial(_sc_kernel, tm=tm, tiles_per_batch=seq_len // tm, row_parts=row_parts),
        grid=(N // tm, nd),
        in_specs=[
            pl.BlockSpec((tm, D), lambda i, j: (i, 0)),
            pl.BlockSpec((1, D), lambda i, j: (0, 0)),
            w_spec(0), w_spec(1), w_spec(2),
            pl.BlockSpec((3, tn), lambda i, j: (0, j)),
            pl.BlockSpec((tn, D), lambda i, j: (j, 0)),
        ],
        out_specs=pl.BlockSpec((tm, D), lambda i, j: (i, 0)),
        out_shape=jax.ShapeDtypeStruct((N, D), F32),
        scratch_shapes=[pltpu.VMEM((tm, D), BF16), pltpu.VMEM((nd, TAIL, tn), F32)],
        compiler_params=_params(("arbitrary", "arbitrary"), vmem_limit),
        name="short_conv",
    )(h, norm_w, *([_bf16(w_in)] * 3), conv_w, _bf16(w_out))


def kernel(x, norm_mix, norm_ffn, hgrn_w_in, hgrn_lb_table, hgrn_out_norm, hgrn_w_out,
           sc_w_in, sc_conv, sc_w_out, ffn_w_up, ffn_conv, ffn_w_down, final_norm):
    B, T, D = x.shape
    depth = norm_mix.shape[0]
    n_mixers = 2
    row = lambda w: w.reshape(1, -1).astype(F32)
    stacks = dict(hgrn_w_in=hgrn_w_in, hgrn_w_out=hgrn_w_out, sc_w_in=sc_w_in, sc_w_out=sc_w_out,
                  ffn_w_up=ffn_w_up, ffn_w_down=ffn_w_down)

    def layer_weights(i):
        mixer = ("hgrn_w_in", "hgrn_w_out") if i % n_mixers == 0 else ("sc_w_in", "sc_w_out")
        return [(name, i // n_mixers) for name in mixer] + [("ffn_w_up", i), ("ffn_w_down", i)]

    cast = {}

    def weight(name, idx):
        return cast.get((name, idx), stacks[name][idx])

    h = x.reshape(B * T, D)
    for i in range(depth):
        j = i // n_mixers
        if i % n_mixers == 0:
            proj_side = layer_weights(i)[1:] if i == 0 else []
            rec_side = [kw for l in range(1, depth) for kw in layer_weights(l)] if i == 0 else []
            side = lambda keys: [(stacks[name], idx) for name, idx in keys]
            xn = _norm_cast(h, row(norm_mix[i])).reshape(B, T, D)
            outs, done = _hgrn_proj(xn, hgrn_lb_table.astype(F32), weight("hgrn_w_in", j),
                                    side(proj_side), layer=i)
            cast.update(zip(proj_side, done))
            q, k, b2, bmin, v, gate = outs
            o, done = _hgrn_rec(q, k, b2, bmin, v, side(rec_side))
            cast.update(zip(rec_side, done))
            h = _hgrn_out(o.reshape(B * T, D), gate, h, row(hgrn_out_norm[j]),
                          weight("hgrn_w_out", j).astype(BF16))
        else:
            h = _short_conv(h, T, row(norm_mix[i]), weight("sc_w_in", j), sc_conv[j].astype(F32),
                            weight("sc_w_out", j))
        h = _ffn(h, T, row(norm_ffn[i]), weight("ffn_w_up", i), ffn_conv[i].astype(F32),
                 weight("ffn_w_down", i), row(final_norm), final_norm=(i == depth - 1))
    return h.reshape(B, T, D)
```

```python
import functools
import math

import jax
import jax.numpy as jnp
from jax import lax
from jax.experimental import pallas as pl
from jax.experimental.pallas import tpu as pltpu

F32 = jnp.float32
BF16 = jnp.bfloat16

EPS = 1e-6
HEAD_DIM = 128
CHUNK = 64
SUB = 16
N_SUB = CHUNK // SUB
N_PIECES = N_SUB * (N_SUB + 1) // 2
A_COLS = SUB * N_PIECES
EXP2_CLAMP = 112.0
INTRA_UNROLL = 16
STATE_UNROLL = 16
ROW_PARTS = 2
VMEM_LIMIT = 56 * 1024 * 1024


def _dot(a, b):
    return jnp.dot(a, b, preferred_element_type=F32)


def _dot_nt(a, b):
    return lax.dot_general(a, b, (((1,), (1,)), ((), ())), preferred_element_type=F32)


def _dot_tn(a, b):
    return lax.dot_general(a, b, (((0,), (0,)), ((), ())), preferred_element_type=F32)


def _sigmoid(x):
    return 1.0 / (1.0 + jnp.exp(-x))


def _rms_rows(x, w):
    ms = jnp.mean(x * x, axis=-1, keepdims=True)
    return x * lax.rsqrt(ms + EPS) * w


def _params(semantics, vmem_limit=VMEM_LIMIT):
    return pltpu.CompilerParams(dimension_semantics=semantics, vmem_limit_bytes=vmem_limit)


def _bf16(w):
    return w.astype(BF16)


def _side_cast_plan(weights, grid):
    n_steps = math.prod(grid)
    strides = [math.prod(grid[d + 1:]) for d in range(len(grid))]

    def step(ids):
        return sum(i * s for i, s in zip(ids, strides))

    ins, in_specs, out_specs, shapes = [], [], [], []
    for stack, idx in weights:
        n, rows, cols = stack.shape
        assert rows % n_steps == 0, (stack.shape, grid)
        block = (None, rows // n_steps, cols)
        ins.append(stack.reshape(n * n_steps, rows // n_steps, cols))
        in_specs.append(pl.BlockSpec(block, lambda *ids, base=idx * n_steps: (base + step(ids), 0, 0)))
        out_specs.append(pl.BlockSpec(block, lambda *ids: (step(ids), 0, 0)))
        shapes.append(jax.ShapeDtypeStruct((n_steps, rows // n_steps, cols), BF16))
    return ins, in_specs, out_specs, shapes


def _side_cast(src_refs, dst_refs):
    for src, dst in zip(src_refs, dst_refs):
        dst[...] = src[...].astype(BF16)


def _chunk_cumsum(x):
    n, d = x.shape
    pos = lax.broadcasted_iota(jnp.int32, (n, d), 0) & (CHUNK - 1)
    s = 1
    while s < CHUNK:
        pad = max(8, s)
        xp = jnp.concatenate([jnp.zeros((pad, d), x.dtype), x], axis=0)
        x = x + jnp.where(pos >= s, xp[pad - s:pad - s + n], 0.0)
        s *= 2
    return x


def _norm_cast_kernel(x_ref, w_ref, o_ref):
    o_ref[...] = _rms_rows(x_ref[...], w_ref[...]).astype(BF16)


def _norm_cast(x, norm_w, *, tm=1024):
    N, D = x.shape
    return pl.pallas_call(
        _norm_cast_kernel,
        grid=(N // tm,),
        in_specs=[pl.BlockSpec((tm, D), lambda i: (i, 0)), pl.BlockSpec((1, D), lambda i: (0, 0))],
        out_specs=pl.BlockSpec((tm, D), lambda i: (i, 0)),
        out_shape=jax.ShapeDtypeStruct((N, D), BF16),
        compiler_params=_params(("parallel",)),
        name="norm_cast",
    )(x, norm_w)


def _hgrn_proj_kernel(*refs, layer, heads_per_step, n_side):
    xn_ref, lbt_ref = refs[:2]
    w_refs = refs[2:6]
    q_ref, k_ref, b_ref, bmin_ref, v_ref, g_ref = refs[6 + n_side:12 + n_side]
    wb_ref = refs[-1]
    _side_cast(refs[6:6 + n_side], refs[12 + n_side:12 + 2 * n_side])

    @pl.when((pl.program_id(1) == 0) & (pl.program_id(2) == 0))
    def _():
        for p, w_ref in enumerate(w_refs):
            wb_ref[p] = w_ref[...].astype(BF16)

    xn = xn_ref[...]

    rows = [lbt_ref[l:l + 1, :] for l in range(lbt_ref.shape[0])]
    m = functools.reduce(jnp.maximum, rows)
    es = [jnp.exp(r - m) for r in rows]
    lb = sum(es[:layer + 1]) / sum(es)

    q = _dot(xn, wb_ref[0])
    q = q * _sigmoid(q) * (HEAD_DIM ** -0.5)
    f = lb + (1.0 - lb) * _sigmoid(_dot(xn, wb_ref[1]))
    b2 = _chunk_cumsum(jnp.log2(f))
    k = 1.0 - f
    v = _dot(xn, wb_ref[2])
    g = _dot(xn, wb_ref[3])
    g_ref[...] = (g * _sigmoid(g)).astype(BF16)
    for hh in range(heads_per_step):
        sl = slice(hh * HEAD_DIM, (hh + 1) * HEAD_DIM)
        q_ref[0, hh] = q[:, sl].astype(BF16)
        k_ref[0, hh] = k[:, sl].astype(BF16)
        b_ref[0, hh] = b2[:, sl]
        bmin_ref[0, hh, 0] = jnp.min(b2[:, sl].reshape(-1, 8, HEAD_DIM), axis=0)
        v_ref[0, hh] = v[:, sl].astype(BF16)


def _hgrn_proj(xn, lb_table, w_in, side_weights, *, layer, tm=1024, heads_per_step=2):
    B, T, D = xn.shape
    xn = xn.reshape(B * T, D)
    H = D // HEAD_DIM
    tn = heads_per_step * HEAD_DIM
    nq = D // tn
    grid = (H // heads_per_step, B, T // tm)
    head_shape = (B, H, T, HEAD_DIM)
    head_spec = pl.BlockSpec((1, heads_per_step, tm, HEAD_DIM), lambda j, b, i: (b, j, i, 0))
    w_spec = lambda part: pl.BlockSpec((D, tn), lambda j, b, i: (0, part * nq + j))
    side_in, side_in_specs, side_out_specs, side_shapes = _side_cast_plan(side_weights, grid)
    outs = pl.pallas_call(
        functools.partial(_hgrn_proj_kernel, layer=layer, heads_per_step=heads_per_step,
                          n_side=len(side_in)),
        grid=grid,
        in_specs=[
            pl.BlockSpec((tm, D), lambda j, b, i: (b * (T // tm) + i, 0)),
            pl.BlockSpec((lb_table.shape[0], tn), lambda j, b, i: (0, j)),
            w_spec(0), w_spec(1), w_spec(2), w_spec(3),
        ] + side_in_specs,
        out_specs=[
            head_spec, head_spec, head_spec,
            pl.BlockSpec((1, heads_per_step, 1, 8, HEAD_DIM), lambda j, b, i: (b, j, i, 0, 0)),
            head_spec,
            pl.BlockSpec((tm, tn), lambda j, b, i: (b * (T // tm) + i, j)),
        ] + side_out_specs,
        out_shape=[
            jax.ShapeDtypeStruct(head_shape, BF16),
            jax.ShapeDtypeStruct(head_shape, BF16),
            jax.ShapeDtypeStruct(head_shape, F32),
            jax.ShapeDtypeStruct((B, H, T // tm, 8, HEAD_DIM), F32),
            jax.ShapeDtypeStruct(head_shape, BF16),
            jax.ShapeDtypeStruct((B * T, D), BF16),
        ] + side_shapes,
        scratch_shapes=[pltpu.VMEM((4, D, tn), BF16)],
        compiler_params=_params(("arbitrary", "arbitrary", "arbitrary")),
        name="hgrn_proj",
    )(xn, lb_table, *([w_in] * 4), *side_in)
    return outs[:6], [o.reshape(w.shape[1:]) for o, (w, _) in zip(outs[6:], side_weights)]


def _piece_offset(i):
    return SUB * i * (i + 1) // 2


def _score_mask():
    rw = lax.broadcasted_iota(jnp.int32, (CHUNK, A_COLS), 0)
    cw = lax.broadcasted_iota(jnp.int32, (CHUNK, A_COLS), 1)
    keep = None
    for i in range(N_SUB):
        diag0 = _piece_offset(i) + SUB * i
        in_rows = (rw >= SUB * i) & (rw < SUB * (i + 1))
        left = (cw >= _piece_offset(i)) & (cw < diag0)
        diag = (cw >= diag0) & (cw < diag0 + SUB) & (cw - diag0 <= rw - SUB * i)
        m = in_rows & (left | diag)
        keep = m if keep is None else keep | m
    return keep


def _exact_diag(slab, qb, kb, bb, lane0):
    lr = lax.broadcasted_iota(jnp.int32, slab.shape, 0)
    lc = lax.broadcasted_iota(jnp.int32, slab.shape, 1)
    for s in range(SUB):
        e = jnp.exp2(jnp.minimum(bb - bb[s:s + 1, :], 0.0))
        col = jnp.sum(qb * kb[s:s + 1, :] * e, axis=-1, keepdims=True)
        slab = jnp.where((lc == lane0 + s) & (lr >= s), col, slab)
    return slab


def _block_rows(rows):
    return jnp.concatenate([jnp.broadcast_to(r, (SUB, r.shape[-1])) for r in rows], axis=0)


def _chunk_factors(q_ref, k_ref, b_ref, rows):
    q = q_ref[0, 0, rows, :].astype(F32)
    k = k_ref[0, 0, rows, :].astype(F32)
    b = b_ref[0, 0, rows, :]

    ends = [b[SUB * (j + 1) - 1:SUB * (j + 1), :] for j in range(N_SUB)]
    refs = [jnp.zeros_like(ends[0])] + ends[:-1]
    b_last = ends[-1]
    d = b - _block_rows(refs)
    q_blk = (q * jnp.exp2(d)).astype(BF16)
    k_diag = k * jnp.exp2(jnp.minimum(-d, EXP2_CLAMP))
    k_end = k * jnp.exp2(_block_rows(ends) - b)
    pieces = []
    for i in range(N_SUB):
        for j in range(i):
            kj = k_end[SUB * j:SUB * (j + 1)]
            if j < i - 1:
                kj = kj * jnp.exp2(refs[i] - ends[j])
            pieces.append(kj.astype(BF16))
        pieces.append(k_diag[SUB * i:SUB * (i + 1)].astype(BF16))
    k_all = jnp.concatenate(pieces, axis=0)
    q_dec = (q * jnp.exp2(b)).astype(BF16)
    k_dec = (k_end * _block_rows([jnp.exp2(b_last - e) for e in ends])).astype(BF16)
    return (q, k, b), q_blk, k_all, q_dec, k_dec, jnp.exp2(b_last)


def _fix_diag(a, q, k, b):
    blocks = []
    for i in range(N_SUB):
        rs = slice(SUB * i, SUB * (i + 1))
        diag0 = _piece_offset(i) + SUB * i
        tile = diag0 // 128
        slabs = [a[rs, 128 * t:min(128 * (t + 1), A_COLS)] for t in range(pl.cdiv(A_COLS, 128))]
        slabs[tile] = _exact_diag(slabs[tile], q[rs], k[rs], b[rs], diag0 % 128)
        blocks.append(jnp.concatenate(slabs, axis=1))
    return jnp.concatenate(blocks, axis=0)


def _stack_values(v):
    return jnp.concatenate([v[0:SUB * (i + 1)] for i in range(N_SUB)], axis=0)


def _hgrn_rec_kernel(*refs, tm, n_side):
    q_ref, k_ref, b_ref, bmin_ref, v_ref = refs[:5]
    o_ref = refs[5 + n_side]
    st_ref, qd_scr, oi_scr, kv_scr, dl_scr = refs[6 + 2 * n_side:]
    _side_cast(refs[5:5 + n_side], refs[6 + n_side:6 + 2 * n_side])

    @pl.when(pl.program_id(2) == 0)
    def _():
        st_ref[...] = jnp.zeros_like(st_ref)

    n_chunks = tm // CHUNK

    def chunk_rows(c):
        return pl.ds(pl.multiple_of(c * CHUNK, CHUNK), CHUNK)

    def state_rows(c):
        return pl.ds(pl.multiple_of(c * HEAD_DIM, HEAD_DIM), HEAD_DIM)

    def decay_rows(c, n):
        return pl.ds(pl.multiple_of(c * 8, 8), n)

    keep = _score_mask()

    def fast():
        def intra_body(c, carry):
            ccs = [c * INTRA_UNROLL + u for u in range(INTRA_UNROLL)]
            fac = [_chunk_factors(q_ref, k_ref, b_ref, chunk_rows(cc)) for cc in ccs]
            scores = [jnp.where(keep, _dot_nt(f[1], f[2]), 0.0).astype(BF16) for f in fac]
            for cc, f in zip(ccs, fac):
                kv_scr[state_rows(cc), :] = _dot_tn(v_ref[0, 0, chunk_rows(cc), :], f[4])
                qd_scr[chunk_rows(cc), :] = f[3]
                dl_scr[decay_rows(cc, 8), :] = jnp.broadcast_to(f[5], (8, HEAD_DIM))
            for cc, a in zip(ccs, scores):
                rows = chunk_rows(cc)
                oi_scr[rows, :] = _dot(a, _stack_values(v_ref[0, 0, rows, :]))
            return carry

        lax.fori_loop(0, n_chunks // INTRA_UNROLL, intra_body, 0)

        def state_body(c, st):
            for u in range(STATE_UNROLL):
                cc = c * STATE_UNROLL + u
                rows = chunk_rows(cc)
                o = oi_scr[rows, :] + _dot_nt(qd_scr[rows, :], st.astype(BF16))
                o_ref[0, rows, :] = o.astype(o_ref.dtype)
                st = st * dl_scr[decay_rows(cc, 1), :] + kv_scr[state_rows(cc), :]
            return st

        st_ref[...] = lax.fori_loop(0, n_chunks // STATE_UNROLL, state_body, st_ref[...])

    def exact():
        def body(c, st):
            rows = chunk_rows(c)
            (q, k, b), q_blk, k_all, q_dec, k_dec, dl = _chunk_factors(q_ref, k_ref, b_ref, rows)
            a = _fix_diag(jnp.where(keep, _dot_nt(q_blk, k_all), 0.0), q, k, b).astype(BF16)
            v = v_ref[0, 0, rows, :]
            o = _dot(a, _stack_values(v)) + _dot_nt(q_dec, st.astype(BF16))
            o_ref[0, rows, :] = o.astype(o_ref.dtype)
            return st * dl + _dot_tn(v, k_dec)

        st_ref[...] = lax.fori_loop(0, n_chunks, body, st_ref[...])

    clamp_may_bind = jnp.min(bmin_ref[0, 0]) < -(EXP2_CLAMP - 1.0)
    pl.when(jnp.logical_not(clamp_may_bind))(fast)
    pl.when(clamp_may_bind)(exact)


def _hgrn_rec(q, k, b2, bmin, v, side_weights, *, tm=4096):
    B, H, T, _ = q.shape
    tm = min(tm, T)
    assert T % tm == 0 and (tm // CHUNK) % INTRA_UNROLL == 0 and (tm // CHUNK) % STATE_UNROLL == 0
    n_min = bmin.shape[2] // (T // tm)
    spec = pl.BlockSpec((1, 1, tm, HEAD_DIM), lambda b, h, t: (b, h, t, 0))
    grid = (B, H, T // tm)
    side_in, side_in_specs, side_out_specs, side_shapes = _side_cast_plan(side_weights, grid)
    outs = pl.pallas_call(
        functools.partial(_hgrn_rec_kernel, tm=tm, n_side=len(side_in)),
        grid=grid,
        in_specs=[spec, spec, spec,
                  pl.BlockSpec((1, 1, n_min, 8, HEAD_DIM), lambda b, h, t: (b, h, t, 0, 0)),
                  spec] + side_in_specs,
        out_specs=[pl.BlockSpec((1, tm, HEAD_DIM), lambda b, h, t: (b, t, h))] + side_out_specs,
        out_shape=[jax.ShapeDtypeStruct((B, T, H * HEAD_DIM), BF16)] + side_shapes,
        scratch_shapes=[pltpu.VMEM((HEAD_DIM, HEAD_DIM), F32),
                        pltpu.VMEM((tm, HEAD_DIM), BF16),
                        pltpu.VMEM((tm, HEAD_DIM), F32),
                        pltpu.VMEM((tm // CHUNK * HEAD_DIM, HEAD_DIM), F32),
                        pltpu.VMEM((tm // CHUNK * 8, HEAD_DIM), F32)],
        compiler_params=_params(("parallel", "parallel", "arbitrary")),
        name="hgrn_rec",
    )(q, k, b2, bmin, v, *side_in)
    return outs[0], [o.reshape(w.shape[1:]) for o, (w, _) in zip(outs[1:], side_weights)]


def _hgrn_out_kernel(o_ref, g_ref, h_ref, gain_ref, w_ref, out_ref):
    y = _rms_rows(o_ref[...].astype(F32), gain_ref[...]) * g_ref[...].astype(F32)
    out_ref[...] = h_ref[...] + _dot(y.astype(BF16), w_ref[...])


def _hgrn_out(o, gate, h, gain, w_out, *, tm=512):
    N, D = h.shape
    row_spec = pl.BlockSpec((tm, D), lambda i: (i, 0))
    return pl.pallas_call(
        _hgrn_out_kernel,
        grid=(N // tm,),
        in_specs=[row_spec, row_spec, row_spec,
                  pl.BlockSpec((1, D), lambda i: (0, 0)),
                  pl.BlockSpec((D, D), lambda i: (0, 0))],
        out_specs=row_spec,
        out_shape=jax.ShapeDtypeStruct((N, D), F32),
        compiler_params=_params(("parallel",)),
        name="hgrn_out",
    )(o, gate, h, gain, w_out)


TAIL = 8


def _conv3(prev, u, cw):
    rows = u.shape[0]
    ue = jnp.concatenate([prev, u], axis=0)
    return (cw[0:1, :] * ue[TAIL - 2:TAIL - 2 + rows]
            + cw[1:2, :] * ue[TAIL - 1:TAIL - 1 + rows]
            + cw[2:3, :] * u)


def _start_tile(h_hbm, h_buf, h_sem, nw_ref, xn_ref, out_ref, tails, tm, tiles_per_batch):
    i = pl.program_id(0)
    j = pl.program_id(1)

    def tile_copy(t):
        rows = pl.ds(pl.multiple_of(t * tm, tm), tm)
        return pltpu.make_async_copy(h_hbm.at[rows, :], h_buf, h_sem)

    @pl.when((i == 0) & (j == 0))
    def _():
        tile_copy(i).start()

    @pl.when(j == 0)
    def _():
        tile_copy(i).wait()
        xn_ref[...] = _rms_rows(h_buf[...], nw_ref[...]).astype(BF16)
        out_ref[...] = h_buf[...]

    @pl.when((j == 1) & (i + 1 < pl.num_programs(0)))
    def _():
        tile_copy(i + 1).start()

    @pl.when(i % tiles_per_batch == 0)
    def _():
        for t in tails:
            t[j] = jnp.zeros(t.shape[1:], t.dtype)


def _tile_scratch(tm, d):
    return [pltpu.VMEM((tm, d), BF16), pltpu.VMEM((tm, d), F32), pltpu.SemaphoreType.DMA(())]


def _ffn_kernel(h_hbm, nw_ref, wg_ref, wv_ref, cwg_ref, cwv_ref, wd_ref, fw_ref,
                out_ref, xn_ref, h_buf, h_sem, tg_ref, tv_ref,
                *, tm, tiles_per_batch, final_norm, row_parts):
    j = pl.program_id(1)
    _start_tile(h_hbm, h_buf, h_sem, nw_ref, xn_ref, out_ref, (tg_ref, tv_ref), tm, tiles_per_batch)

    rp = tm // row_parts
    xns = [xn_ref[p * rp:(p + 1) * rp, :] for p in range(row_parts)]
    ups = [(_dot(xn, wg_ref[...]), _dot(xn, wv_ref[...])) for xn in xns]
    prev_g, prev_v = tg_ref[j], tv_ref[j]
    acts = []
    for ug, uv in ups:
        cg = _conv3(prev_g, ug, cwg_ref[...])
        cv = _conv3(prev_v, uv, cwv_ref[...])
        prev_g, prev_v = ug[rp - TAIL:], uv[rp - TAIL:]
        acts.append((cg * _sigmoid(cg) * cv).astype(BF16))
    tg_ref[j] = prev_g
    tv_ref[j] = prev_v
    for p, act in enumerate(acts):
        out_ref[p * rp:(p + 1) * rp, :] += _dot(act, wd_ref[...])

    if final_norm:
        @pl.when(j == pl.num_programs(1) - 1)
        def _():
            out_ref[...] = _rms_rows(out_ref[...], fw_ref[...])


def _ffn(h, seq_len, norm_w, w_up, conv_w, w_down, final_w, *, final_norm, tm=1024, tn=512,
         row_parts=ROW_PARTS, vmem_limit=VMEM_LIMIT):
    N, D = h.shape
    F = w_down.shape[0]
    nf = F // tn
    assert nf >= 2
    return pl.pallas_call(
        functools.partial(_ffn_kernel, tm=tm, tiles_per_batch=seq_len // tm,
                          final_norm=final_norm, row_parts=row_parts),
        grid=(N // tm, nf),
        in_specs=[
            pl.BlockSpec(memory_space=pl.ANY),
            pl.BlockSpec((1, D), lambda i, j: (0, 0)),
            pl.BlockSpec((D, tn), lambda i, j: (0, j)),
            pl.BlockSpec((D, tn), lambda i, j: (0, nf + j)),
            pl.BlockSpec((3, tn), lambda i, j: (0, j)),
            pl.BlockSpec((3, tn), lambda i, j: (0, nf + j)),
            pl.BlockSpec((tn, D), lambda i, j: (j, 0)),
            pl.BlockSpec((1, D), lambda i, j: (0, 0)),
        ],
        out_specs=pl.BlockSpec((tm, D), lambda i, j: (i, 0)),
        out_shape=jax.ShapeDtypeStruct((N, D), F32),
        scratch_shapes=_tile_scratch(tm, D) + [pltpu.VMEM((nf, TAIL, tn), F32)] * 2,
        compiler_params=_params(("arbitrary", "arbitrary"), vmem_limit),
        name="ffn_final" if final_norm else "ffn",
    )(h, norm_w, *([_bf16(w_up)] * 2), conv_w, conv_w, _bf16(w_down), final_w)


def _sc_kernel(h_hbm, nw_ref, wb_ref, wc_ref, wh_ref, cw_ref, wd_ref,
               out_ref, xn_ref, h_buf, h_sem, tz_ref, *, tm, tiles_per_batch, row_parts):
    j = pl.program_id(1)
    _start_tile(h_hbm, h_buf, h_sem, nw_ref, xn_ref, out_ref, (tz_ref,), tm, tiles_per_batch)

    rp = tm // row_parts
    xns = [xn_ref[p * rp:(p + 1) * rp, :] for p in range(row_parts)]
    ups = [(_dot(xn, wc_ref[...]), _dot(xn, wh_ref[...]), _dot(xn, wb_ref[...])) for xn in xns]
    prev_z = tz_ref[j]
    ys = []
    for uc, uh, gb in ups:
        z = uc * uh
        ys.append((gb * _conv3(prev_z, z, cw_ref[...])).astype(BF16))
        prev_z = z[rp - TAIL:]
    tz_ref[j] = prev_z
    for p, y in enumerate(ys):
        out_ref[p * rp:(p + 1) * rp, :] += _dot(y, wd_ref[...])


def _short_conv(h, seq_len, norm_w, w_in, conv_w, w_out, *, tm=1024, tn=512, row_parts=ROW_PARTS,
                vmem_limit=60 * 1024 * 1024):
    N, D = h.shape
    nd = D // tn
    assert nd >= 2
    w_spec = lambda part: pl.BlockSpec((D, tn), lambda i, j: (0, part * nd + j))
    return pl.pallas_call(
        functools.partial(_sc_kernel, tm=tm, tiles_per_batch=seq_len // tm, row_parts=row_parts),
        grid=(N // tm, nd),
        in_specs=[
            pl.BlockSpec(memory_space=pl.ANY),
            pl.BlockSpec((1, D), lambda i, j: (0, 0)),
            w_spec(0), w_spec(1), w_spec(2),
            pl.BlockSpec((3, tn), lambda i, j: (0, j)),
            pl.BlockSpec((tn, D), lambda i, j: (j, 0)),
        ],
        out_specs=pl.BlockSpec((tm, D), lambda i, j: (i, 0)),
        out_shape=jax.ShapeDtypeStruct((N, D), F32),
        scratch_shapes=_tile_scratch(tm, D) + [pltpu.VMEM((nd, TAIL, tn), F32)],
        compiler_params=_params(("arbitrary", "arbitrary"), vmem_limit),
        name="short_conv",
    )(h, norm_w, *([_bf16(w_in)] * 3), conv_w, _bf16(w_out))


def kernel(x, norm_mix, norm_ffn, hgrn_w_in, hgrn_lb_table, hgrn_out_norm, hgrn_w_out,
           sc_w_in, sc_conv, sc_w_out, ffn_w_up, ffn_conv, ffn_w_down, final_norm):
    B, T, D = x.shape
    depth = norm_mix.shape[0]
    n_mixers = 2
    row = lambda w: w.reshape(1, -1).astype(F32)
    stacks = dict(hgrn_w_in=hgrn_w_in, hgrn_w_out=hgrn_w_out, sc_w_in=sc_w_in, sc_w_out=sc_w_out,
                  ffn_w_up=ffn_w_up, ffn_w_down=ffn_w_down)

    def layer_weights(i):
        mixer = ("hgrn_w_in", "hgrn_w_out") if i % n_mixers == 0 else ("sc_w_in", "sc_w_out")
        return [(name, i // n_mixers) for name in mixer] + [("ffn_w_up", i), ("ffn_w_down", i)]

    cast = {}

    def weight(name, idx):
        return cast.get((name, idx), stacks[name][idx])

    h = x.reshape(B * T, D)
    for i in range(depth):
        j = i // n_mixers
        if i % n_mixers == 0:
            proj_side = layer_weights(i)[1:] if i == 0 else []
            rec_side = [kw for l in range(1, depth) for kw in layer_weights(l)] if i == 0 else []
            side = lambda keys: [(stacks[name], idx) for name, idx in keys]
            xn = _norm_cast(h, row(norm_mix[i])).reshape(B, T, D)
            outs, done = _hgrn_proj(xn, hgrn_lb_table.astype(F32), weight("hgrn_w_in", j),
                                    side(proj_side), layer=i)
            cast.update(zip(proj_side, done))
            q, k, b2, bmin, v, gate = outs
            o, done = _hgrn_rec(q, k, b2, bmin, v, side(rec_side))
            cast.update(zip(rec_side, done))
            h = _hgrn_out(o.reshape(B * T, D), gate, h, row(hgrn_out_norm[j]),
                          weight("hgrn_w_out", j).astype(BF16))
        else:
            h = _short_conv(h, T, row(norm_mix[i]), weight("sc_w_in", j), sc_conv[j].astype(F32),
                            weight("sc_w_out", j))
        h = _ffn(h, T, row(norm_ffn[i]), weight("ffn_w_up", i), ffn_conv[i].astype(F32),
                 weight("ffn_w_down", i), row(final_norm), final_norm=(i == depth - 1))
    return h.reshape(B, T, D)
```

```python
import functools
import math

import jax
import jax.numpy as jnp
from jax import lax
from jax.experimental import pallas as pl
from jax.experimental.pallas import tpu as pltpu

F32 = jnp.float32
BF16 = jnp.bfloat16

EPS = 1e-6
HEAD_DIM = 128
CHUNK = 64
SUB = 16
N_SUB = CHUNK // SUB
N_PIECES = N_SUB * (N_SUB + 1) // 2
A_COLS = SUB * N_PIECES
EXP2_CLAMP = 112.0
INTRA_UNROLL = 16
STATE_UNROLL = 16
ROW_PARTS = 2
VMEM_LIMIT = 56 * 1024 * 1024


def _dot(a, b):
    return jnp.dot(a, b, preferred_element_type=F32)


def _dot_nt(a, b):
    return lax.dot_general(a, b, (((1,), (1,)), ((), ())), preferred_element_type=F32)


def _dot_tn(a, b):
    return lax.dot_general(a, b, (((0,), (0,)), ((), ())), preferred_element_type=F32)


def _sigmoid(x):
    return 1.0 / (1.0 + jnp.exp(-x))


def _rms_rows(x, w):
    ms = jnp.mean(x * x, axis=-1, keepdims=True)
    return x * lax.rsqrt(ms + EPS) * w


def _params(semantics, vmem_limit=VMEM_LIMIT):
    return pltpu.CompilerParams(dimension_semantics=semantics, vmem_limit_bytes=vmem_limit)


def _bf16(w):
    return w.astype(BF16)


def _side_cast_plan(weights, grid):
    n_steps = math.prod(grid)
    strides = [math.prod(grid[d + 1:]) for d in range(len(grid))]

    def step(ids):
        return sum(i * s for i, s in zip(ids, strides))

    ins, in_specs, out_specs, shapes = [], [], [], []
    for stack, idx in weights:
        n, rows, cols = stack.shape
        assert rows % n_steps == 0, (stack.shape, grid)
        block = (None, rows // n_steps, cols)
        ins.append(stack.reshape(n * n_steps, rows // n_steps, cols))
        in_specs.append(pl.BlockSpec(block, lambda *ids, base=idx * n_steps: (base + step(ids), 0, 0)))
        out_specs.append(pl.BlockSpec(block, lambda *ids: (step(ids), 0, 0)))
        shapes.append(jax.ShapeDtypeStruct((n_steps, rows // n_steps, cols), BF16))
    return ins, in_specs, out_specs, shapes


def _side_cast(src_refs, dst_refs):
    for src, dst in zip(src_refs, dst_refs):
        dst[...] = src[...].astype(BF16)


def _chunk_cumsum(x):
    n, d = x.shape
    pos = lax.broadcasted_iota(jnp.int32, (n, d), 0) & (CHUNK - 1)
    s = 1
    while s < CHUNK:
        pad = max(8, s)
        xp = jnp.concatenate([jnp.zeros((pad, d), x.dtype), x], axis=0)
        x = x + jnp.where(pos >= s, xp[pad - s:pad - s + n], 0.0)
        s *= 2
    return x


def _norm_cast_kernel(x_ref, w_ref, o_ref):
    o_ref[...] = _rms_rows(x_ref[...], w_ref[...]).astype(BF16)


def _norm_cast(x, norm_w, *, tm=1024):
    N, D = x.shape
    return pl.pallas_call(
        _norm_cast_kernel,
        grid=(N // tm,),
        in_specs=[pl.BlockSpec((tm, D), lambda i: (i, 0)), pl.BlockSpec((1, D), lambda i: (0, 0))],
        out_specs=pl.BlockSpec((tm, D), lambda i: (i, 0)),
        out_shape=jax.ShapeDtypeStruct((N, D), BF16),
        compiler_params=_params(("parallel",)),
        name="norm_cast",
    )(x, norm_w)


def _hgrn_proj_kernel(*refs, layer, heads_per_step, n_side):
    xn_ref, lbt_ref = refs[:2]
    w_refs = refs[2:6]
    q_ref, k_ref, b_ref, bmin_ref, v_ref, g_ref = refs[6 + n_side:12 + n_side]
    wb_ref = refs[-1]
    _side_cast(refs[6:6 + n_side], refs[12 + n_side:12 + 2 * n_side])

    @pl.when((pl.program_id(1) == 0) & (pl.program_id(2) == 0))
    def _():
        for p, w_ref in enumerate(w_refs):
            wb_ref[p] = w_ref[...].astype(BF16)

    rows = [lbt_ref[l:l + 1, :] for l in range(lbt_ref.shape[0])]
    m = functools.reduce(jnp.maximum, rows)
    es = [jnp.exp(r - m) for r in rows]
    lb = sum(es[:layer + 1]) / sum(es)

    rp = xn_ref.shape[0] // ROW_PARTS
    bmins = [None] * heads_per_step
    for p in range(ROW_PARTS):
        rs = slice(p * rp, (p + 1) * rp)
        xn = xn_ref[rs, :]
        q = _dot(xn, wb_ref[0])
        q = q * _sigmoid(q) * (HEAD_DIM ** -0.5)
        f = lb + (1.0 - lb) * _sigmoid(_dot(xn, wb_ref[1]))
        b2 = _chunk_cumsum(jnp.log2(f))
        k = 1.0 - f
        v = _dot(xn, wb_ref[2])
        g = _dot(xn, wb_ref[3])
        g_ref[rs, :] = (g * _sigmoid(g)).astype(BF16)
        for hh in range(heads_per_step):
            sl = slice(hh * HEAD_DIM, (hh + 1) * HEAD_DIM)
            q_ref[0, hh, rs, :] = q[:, sl].astype(BF16)
            k_ref[0, hh, rs, :] = k[:, sl].astype(BF16)
            b_ref[0, hh, rs, :] = b2[:, sl]
            v_ref[0, hh, rs, :] = v[:, sl].astype(BF16)
            bm = jnp.min(b2[:, sl].reshape(-1, 8, HEAD_DIM), axis=0)
            bmins[hh] = bm if bmins[hh] is None else jnp.minimum(bmins[hh], bm)
    for hh in range(heads_per_step):
        bmin_ref[0, hh, 0] = bmins[hh]


def _hgrn_proj(xn, lb_table, w_in, side_weights, *, layer, tm=1024, heads_per_step=2):
    B, T, D = xn.shape
    xn = xn.reshape(B * T, D)
    H = D // HEAD_DIM
    tn = heads_per_step * HEAD_DIM
    nq = D // tn
    grid = (H // heads_per_step, B, T // tm)
    head_shape = (B, H, T, HEAD_DIM)
    head_spec = pl.BlockSpec((1, heads_per_step, tm, HEAD_DIM), lambda j, b, i: (b, j, i, 0))
    w_spec = lambda part: pl.BlockSpec((D, tn), lambda j, b, i: (0, part * nq + j))
    side_in, side_in_specs, side_out_specs, side_shapes = _side_cast_plan(side_weights, grid)
    outs = pl.pallas_call(
        functools.partial(_hgrn_proj_kernel, layer=layer, heads_per_step=heads_per_step,
                          n_side=len(side_in)),
        grid=grid,
        in_specs=[
            pl.BlockSpec((tm, D), lambda j, b, i: (b * (T // tm) + i, 0)),
            pl.BlockSpec((lb_table.shape[0], tn), lambda j, b, i: (0, j)),
            w_spec(0), w_spec(1), w_spec(2), w_spec(3),
        ] + side_in_specs,
        out_specs=[
            head_spec, head_spec, head_spec,
            pl.BlockSpec((1, heads_per_step, 1, 8, HEAD_DIM), lambda j, b, i: (b, j, i, 0, 0)),
            head_spec,
            pl.BlockSpec((tm, tn), lambda j, b, i: (b * (T // tm) + i, j)),
        ] + side_out_specs,
        out_shape=[
            jax.ShapeDtypeStruct(head_shape, BF16),
            jax.ShapeDtypeStruct(head_shape, BF16),
            jax.ShapeDtypeStruct(head_shape, F32),
            jax.ShapeDtypeStruct((B, H, T // tm, 8, HEAD_DIM), F32),
            jax.ShapeDtypeStruct(head_shape, BF16),
            jax.ShapeDtypeStruct((B * T, D), BF16),
        ] + side_shapes,
        scratch_shapes=[pltpu.VMEM((4, D, tn), BF16)],
        compiler_params=_params(("arbitrary", "arbitrary", "arbitrary")),
        name="hgrn_proj",
    )(xn, lb_table, *([w_in] * 4), *side_in)
    return outs[:6], [o.reshape(w.shape[1:]) for o, (w, _) in zip(outs[6:], side_weights)]


def _piece_offset(i):
    return SUB * i * (i + 1) // 2


def _score_mask():
    rw = lax.broadcasted_iota(jnp.int32, (CHUNK, A_COLS), 0)
    cw = lax.broadcasted_iota(jnp.int32, (CHUNK, A_COLS), 1)
    keep = None
    for i in range(N_SUB):
        diag0 = _piece_offset(i) + SUB * i
        in_rows = (rw >= SUB * i) & (rw < SUB * (i + 1))
        left = (cw >= _piece_offset(i)) & (cw < diag0)
        diag = (cw >= diag0) & (cw < diag0 + SUB) & (cw - diag0 <= rw - SUB * i)
        m = in_rows & (left | diag)
        keep = m if keep is None else keep | m
    return keep


def _exact_diag(slab, qb, kb, bb, lane0):
    lr = lax.broadcasted_iota(jnp.int32, slab.shape, 0)
    lc = lax.broadcasted_iota(jnp.int32, slab.shape, 1)
    for s in range(SUB):
        e = jnp.exp2(jnp.minimum(bb - bb[s:s + 1, :], 0.0))
        col = jnp.sum(qb * kb[s:s + 1, :] * e, axis=-1, keepdims=True)
        slab = jnp.where((lc == lane0 + s) & (lr >= s), col, slab)
    return slab


def _block_rows(rows):
    return jnp.concatenate([jnp.broadcast_to(r, (SUB, r.shape[-1])) for r in rows], axis=0)


def _chunk_factors(q_ref, k_ref, b_ref, rows):
    q = q_ref[0, 0, rows, :].astype(F32)
    k = k_ref[0, 0, rows, :].astype(F32)
    b = b_ref[0, 0, rows, :]

    ends = [b[SUB * (j + 1) - 1:SUB * (j + 1), :] for j in range(N_SUB)]
    refs = [jnp.zeros_like(ends[0])] + ends[:-1]
    b_last = ends[-1]
    d = b - _block_rows(refs)
    q_blk = (q * jnp.exp2(d)).astype(BF16)
    k_diag = k * jnp.exp2(jnp.minimum(-d, EXP2_CLAMP))
    k_end = k * jnp.exp2(_block_rows(ends) - b)
    pieces = []
    for i in range(N_SUB):
        for j in range(i):
            kj = k_end[SUB * j:SUB * (j + 1)]
            if j < i - 1:
                kj = kj * jnp.exp2(refs[i] - ends[j])
            pieces.append(kj.astype(BF16))
        pieces.append(k_diag[SUB * i:SUB * (i + 1)].astype(BF16))
    k_all = jnp.concatenate(pieces, axis=0)
    q_dec = (q * jnp.exp2(b)).astype(BF16)
    k_dec = (k_end * _block_rows([jnp.exp2(b_last - e) for e in ends])).astype(BF16)
    return (q, k, b), q_blk, k_all, q_dec, k_dec, jnp.exp2(b_last)


def _fix_diag(a, q, k, b):
    blocks = []
    for i in range(N_SUB):
        rs = slice(SUB * i, SUB * (i + 1))
        diag0 = _piece_offset(i) + SUB * i
        tile = diag0 // 128
        slabs = [a[rs, 128 * t:min(128 * (t + 1), A_COLS)] for t in range(pl.cdiv(A_COLS, 128))]
        slabs[tile] = _exact_diag(slabs[tile], q[rs], k[rs], b[rs], diag0 % 128)
        blocks.append(jnp.concatenate(slabs, axis=1))
    return jnp.concatenate(blocks, axis=0)


def _stack_values(v):
    return jnp.concatenate([v[0:SUB * (i + 1)] for i in range(N_SUB)], axis=0)


def _hgrn_rec_kernel(*refs, tm, n_side):
    q_ref, k_ref, b_ref, bmin_ref, v_ref = refs[:5]
    o_ref = refs[5 + n_side]
    st_ref, qd_scr, oi_scr, kv_scr, dl_scr = refs[6 + 2 * n_side:]
    _side_cast(refs[5:5 + n_side], refs[6 + n_side:6 + 2 * n_side])

    @pl.when(pl.program_id(2) == 0)
    def _():
        st_ref[...] = jnp.zeros_like(st_ref)

    n_chunks = tm // CHUNK

    def chunk_rows(c):
        return pl.ds(pl.multiple_of(c * CHUNK, CHUNK), CHUNK)

    def state_rows(c):
        return pl.ds(pl.multiple_of(c * HEAD_DIM, HEAD_DIM), HEAD_DIM)

    def decay_rows(c, n):
        return pl.ds(pl.multiple_of(c * 8, 8), n)

    keep = _score_mask()

    def fast():
        def intra_body(c, carry):
            ccs = [c * INTRA_UNROLL + u for u in range(INTRA_UNROLL)]
            fac = [_chunk_factors(q_ref, k_ref, b_ref, chunk_rows(cc)) for cc in ccs]
            scores = [jnp.where(keep, _dot_nt(f[1], f[2]), 0.0).astype(BF16) for f in fac]
            for cc, f in zip(ccs, fac):
                kv_scr[state_rows(cc), :] = _dot_tn(v_ref[0, 0, chunk_rows(cc), :], f[4])
                qd_scr[chunk_rows(cc), :] = f[3]
                dl_scr[decay_rows(cc, 8), :] = jnp.broadcast_to(f[5], (8, HEAD_DIM))
            for cc, a in zip(ccs, scores):
                rows = chunk_rows(cc)
                oi_scr[rows, :] = _dot(a, _stack_values(v_ref[0, 0, rows, :]))
            return carry

        lax.fori_loop(0, n_chunks // INTRA_UNROLL, intra_body, 0)

        def state_body(c, st):
            for u in range(STATE_UNROLL):
                cc = c * STATE_UNROLL + u
                rows = chunk_rows(cc)
                o = oi_scr[rows, :] + _dot_nt(qd_scr[rows, :], st.astype(BF16))
                o_ref[0, rows, :] = o.astype(o_ref.dtype)
                st = st * dl_scr[decay_rows(cc, 1), :] + kv_scr[state_rows(cc), :]
            return st

        st_ref[...] = lax.fori_loop(0, n_chunks // STATE_UNROLL, state_body, st_ref[...])

    def exact():
        def body(c, st):
            rows = chunk_rows(c)
            (q, k, b), q_blk, k_all, q_dec, k_dec, dl = _chunk_factors(q_ref, k_ref, b_ref, rows)
            a = _fix_diag(jnp.where(keep, _dot_nt(q_blk, k_all), 0.0), q, k, b).astype(BF16)
            v = v_ref[0, 0, rows, :]
            o = _dot(a, _stack_values(v)) + _dot_nt(q_dec, st.astype(BF16))
            o_ref[0, rows, :] = o.astype(o_ref.dtype)
            return st * dl + _dot_tn(v, k_dec)

        st_ref[...] = lax.fori_loop(0, n_chunks, body, st_ref[...])

    clamp_may_bind = jnp.min(bmin_ref[0, 0]) < -(EXP2_CLAMP - 1.0)
    pl.when(jnp.logical_not(clamp_may_bind))(fast)
    pl.when(clamp_may_bind)(exact)


def _hgrn_rec(q, k, b2, bmin, v, side_weights, *, tm=4096):
    B, H, T, _ = q.shape
    tm = min(tm, T)
    assert T % tm == 0 and (tm // CHUNK) % INTRA_UNROLL == 0 and (tm // CHUNK) % STATE_UNROLL == 0
    n_min = bmin.shape[2] // (T // tm)
    spec = pl.BlockSpec((1, 1, tm, HEAD_DIM), lambda b, h, t: (b, h, t, 0))
    grid = (B, H, T // tm)
    side_in, side_in_specs, side_out_specs, side_shapes = _side_cast_plan(side_weights, grid)
    outs = pl.pallas_call(
        functools.partial(_hgrn_rec_kernel, tm=tm, n_side=len(side_in)),
        grid=grid,
        in_specs=[spec, spec, spec,
                  pl.BlockSpec((1, 1, n_min, 8, HEAD_DIM), lambda b, h, t: (b, h, t, 0, 0)),
                  spec] + side_in_specs,
        out_specs=[pl.BlockSpec((1, tm, HEAD_DIM), lambda b, h, t: (b, t, h))] + side_out_specs,
        out_shape=[jax.ShapeDtypeStruct((B, T, H * HEAD_DIM), BF16)] + side_shapes,
        scratch_shapes=[pltpu.VMEM((HEAD_DIM, HEAD_DIM), F32),
                        pltpu.VMEM((tm, HEAD_DIM), BF16),
                        pltpu.VMEM((tm, HEAD_DIM), F32),
                        pltpu.VMEM((tm // CHUNK * HEAD_DIM, HEAD_DIM), F32),
                        pltpu.VMEM((tm // CHUNK * 8, HEAD_DIM), F32)],
        compiler_params=_params(("parallel", "parallel", "arbitrary")),
        name="hgrn_rec",
    )(q, k, b2, bmin, v, *side_in)
    return outs[0], [o.reshape(w.shape[1:]) for o, (w, _) in zip(outs[1:], side_weights)]


def _hgrn_out_kernel(o_ref, g_ref, h_ref, gain_ref, w_ref, out_ref):
    y = _rms_rows(o_ref[...].astype(F32), gain_ref[...]) * g_ref[...].astype(F32)
    out_ref[...] = h_ref[...] + _dot(y.astype(BF16), w_ref[...])


def _hgrn_out(o, gate, h, gain, w_out, *, tm=512):
    N, D = h.shape
    row_spec = pl.BlockSpec((tm, D), lambda i: (i, 0))
    return pl.pallas_call(
        _hgrn_out_kernel,
        grid=(N // tm,),
        in_specs=[row_spec, row_spec, row_spec,
                  pl.BlockSpec((1, D), lambda i: (0, 0)),
                  pl.BlockSpec((D, D), lambda i: (0, 0))],
        out_specs=row_spec,
        out_shape=jax.ShapeDtypeStruct((N, D), F32),
        compiler_params=_params(("parallel",)),
        name="hgrn_out",
    )(o, gate, h, gain, w_out)


TAIL = 8


def _conv3(prev, u, cw):
    rows = u.shape[0]
    ue = jnp.concatenate([prev, u], axis=0)
    return (cw[0:1, :] * ue[TAIL - 2:TAIL - 2 + rows]
            + cw[1:2, :] * ue[TAIL - 1:TAIL - 1 + rows]
            + cw[2:3, :] * u)


def _start_tile(h_hbm, h_buf, h_sem, nw_ref, xn_ref, out_ref, tails, tm, tiles_per_batch):
    i = pl.program_id(0)
    j = pl.program_id(1)

    def tile_copy(t):
        rows = pl.ds(pl.multiple_of(t * tm, tm), tm)
        return pltpu.make_async_copy(h_hbm.at[rows, :], h_buf, h_sem)

    @pl.when((i == 0) & (j == 0))
    def _():
        tile_copy(i).start()

    @pl.when(j == 0)
    def _():
        tile_copy(i).wait()
        xn_ref[...] = _rms_rows(h_buf[...], nw_ref[...]).astype(BF16)
        out_ref[...] = h_buf[...]

    @pl.when((j == 1) & (i + 1 < pl.num_programs(0)))
    def _():
        tile_copy(i + 1).start()

    @pl.when(i % tiles_per_batch == 0)
    def _():
        for t in tails:
            t[j] = jnp.zeros(t.shape[1:], t.dtype)


def _tile_scratch(tm, d):
    return [pltpu.VMEM((tm, d), BF16), pltpu.VMEM((tm, d), F32), pltpu.SemaphoreType.DMA(())]


def _ffn_kernel(h_hbm, nw_ref, wg_ref, wv_ref, cwg_ref, cwv_ref, wd_ref, fw_ref,
                out_ref, xn_ref, h_buf, h_sem, tg_ref, tv_ref,
                *, tm, tiles_per_batch, final_norm, row_parts):
    j = pl.program_id(1)
    _start_tile(h_hbm, h_buf, h_sem, nw_ref, xn_ref, out_ref, (tg_ref, tv_ref), tm, tiles_per_batch)

    rp = tm // row_parts
    xns = [xn_ref[p * rp:(p + 1) * rp, :] for p in range(row_parts)]
    ups = [(_dot(xn, wg_ref[...]), _dot(xn, wv_ref[...])) for xn in xns]
    prev_g, prev_v = tg_ref[j], tv_ref[j]
    acts = []
    for ug, uv in ups:
        cg = _conv3(prev_g, ug, cwg_ref[...])
        cv = _conv3(prev_v, uv, cwv_ref[...])
        prev_g, prev_v = ug[rp - TAIL:], uv[rp - TAIL:]
        acts.append((cg * _sigmoid(cg) * cv).astype(BF16))
    tg_ref[j] = prev_g
    tv_ref[j] = prev_v
    for p, act in enumerate(acts):
        out_ref[p * rp:(p + 1) * rp, :] += _dot(act, wd_ref[...])

    if final_norm:
        @pl.when(j == pl.num_programs(1) - 1)
        def _():
            out_ref[...] = _rms_rows(out_ref[...], fw_ref[...])


def _ffn(h, seq_len, norm_w, w_up, conv_w, w_down, final_w, *, final_norm, tm=1024, tn=512,
         row_parts=ROW_PARTS, vmem_limit=VMEM_LIMIT):
    N, D = h.shape
    F = w_down.shape[0]
    nf = F // tn
    assert nf >= 2
    return pl.pallas_call(
        functools.partial(_ffn_kernel, tm=tm, tiles_per_batch=seq_len // tm,
                          final_norm=final_norm, row_parts=row_parts),
        grid=(N // tm, nf),
        in_specs=[
            pl.BlockSpec(memory_space=pl.ANY),
            pl.BlockSpec((1, D), lambda i, j: (0, 0)),
            pl.BlockSpec((D, tn), lambda i, j: (0, j)),
            pl.BlockSpec((D, tn), lambda i, j: (0, nf + j)),
            pl.BlockSpec((3, tn), lambda i, j: (0, j)),
            pl.BlockSpec((3, tn), lambda i, j: (0, nf + j)),
            pl.BlockSpec((tn, D), lambda i, j: (j, 0)),
            pl.BlockSpec((1, D), lambda i, j: (0, 0)),
        ],
        out_specs=pl.BlockSpec((tm, D), lambda i, j: (i, 0)),
        out_shape=jax.ShapeDtypeStruct((N, D), F32),
        scratch_shapes=_tile_scratch(tm, D) + [pltpu.VMEM((nf, TAIL, tn), F32)] * 2,
        compiler_params=_params(("arbitrary", "arbitrary"), vmem_limit),
        name="ffn_final" if final_norm else "ffn",
    )(h, norm_w, *([_bf16(w_up)] * 2), conv_w, conv_w, _bf16(w_down), final_w)


def _sc_kernel(h_hbm, nw_ref, wb_ref, wc_ref, wh_ref, cw_ref, wd_ref,
               out_ref, xn_ref, h_buf, h_sem, tz_ref, *, tm, tiles_per_batch, row_parts):
    j = pl.program_id(1)
    _start_tile(h_hbm, h_buf, h_sem, nw_ref, xn_ref, out_ref, (tz_ref,), tm, tiles_per_batch)

    rp = tm // row_parts
    xns = [xn_ref[p * rp:(p + 1) * rp, :] for p in range(row_parts)]
    ups = [(_dot(xn, wc_ref[...]), _dot(xn, wh_ref[...]), _dot(xn, wb_ref[...])) for xn in xns]
    prev_z = tz_ref[j]
    ys = []
    for uc, uh, gb in ups:
        z = uc * uh
        ys.append((gb * _conv3(prev_z, z, cw_ref[...])).astype(BF16))
        prev_z = z[rp - TAIL:]
    tz_ref[j] = prev_z
    for p, y in enumerate(ys):
        out_ref[p * rp:(p + 1) * rp, :] += _dot(y, wd_ref[...])


def _short_conv(h, seq_len, norm_w, w_in, conv_w, w_out, *, tm=1024, tn=512, row_parts=ROW_PARTS,
                vmem_limit=60 * 1024 * 1024):
    N, D = h.shape
    nd = D // tn
    assert nd >= 2
    w_spec = lambda part: pl.BlockSpec((D, tn), lambda i, j: (0, part * nd + j))
    return pl.pallas_call(
        functools.partial(_sc_kernel, tm=tm, tiles_per_batch=seq_len // tm, row_parts=row_parts),
        grid=(N // tm, nd),
        in_specs=[
            pl.BlockSpec(memory_space=pl.ANY),
            pl.BlockSpec((1, D), lambda i, j: (0, 0)),
            w_spec(0), w_spec(1), w_spec(2),
            pl.BlockSpec((3, tn), lambda i, j: (0, j)),
            pl.BlockSpec((tn, D), lambda i, j: (j, 0)),
        ],
        out_specs=pl.BlockSpec((tm, D), lambda i, j: (i, 0)),
        out_shape=jax.ShapeDtypeStruct((N, D), F32),
        scratch_shapes=_tile_scratch(tm, D) + [pltpu.VMEM((nd, TAIL, tn), F32)],
        compiler_params=_params(("arbitrary", "arbitrary"), vmem_limit),
        name="short_conv",
    )(h, norm_w, *([_bf16(w_in)] * 3), conv_w, _bf16(w_out))


def kernel(x, norm_mix, norm_ffn, hgrn_w_in, hgrn_lb_table, hgrn_out_norm, hgrn_w_out,
           sc_w_in, sc_conv, sc_w_out, ffn_w_up, ffn_conv, ffn_w_down, final_norm):
    B, T, D = x.shape
    depth = norm_mix.shape[0]
    n_mixers = 2
    row = lambda w: w.reshape(1, -1).astype(F32)
    stacks = dict(hgrn_w_in=hgrn_w_in, hgrn_w_out=hgrn_w_out, sc_w_in=sc_w_in, sc_w_out=sc_w_out,
                  ffn_w_up=ffn_w_up, ffn_w_down=ffn_w_down)

    def layer_weights(i):
        mixer = ("hgrn_w_in", "hgrn_w_out") if i % n_mixers == 0 else ("sc_w_in", "sc_w_out")
        return [(name, i // n_mixers) for name in mixer] + [("ffn_w_up", i), ("ffn_w_down", i)]

    cast = {}

    def weight(name, idx):
        return cast.get((name, idx), stacks[name][idx])

    h = x.reshape(B * T, D)
    for i in range(depth):
        j = i // n_mixers
        if i % n_mixers == 0:
            proj_side = layer_weights(i)[1:] if i == 0 else []
            rec_side = [kw for l in range(1, depth) for kw in layer_weights(l)] if i == 0 else []
            side = lambda keys: [(stacks[name], idx) for name, idx in keys]
            xn = _norm_cast(h, row(norm_mix[i])).reshape(B, T, D)
            outs, done = _hgrn_proj(xn, hgrn_lb_table.astype(F32), weight("hgrn_w_in", j),
                                    side(proj_side), layer=i)
            cast.update(zip(proj_side, done))
            q, k, b2, bmin, v, gate = outs
            o, done = _hgrn_rec(q, k, b2, bmin, v, side(rec_side))
            cast.update(zip(rec_side, done))
            h = _hgrn_out(o.reshape(B * T, D), gate, h, row(hgrn_out_norm[j]),
                          weight("hgrn_w_out", j).astype(BF16))
        else:
            h = _short_conv(h, T, row(norm_mix[i]), weight("sc_w_in", j), sc_conv[j].astype(F32),
                            weight("sc_w_out", j))
        h = _ffn(h, T, row(norm_ffn[i]), weight("ffn_w_up", i), ffn_conv[i].astype(F32),
                 weight("ffn_w_down", i), row(final_norm), final_norm=(i == depth - 1))
    return h.reshape(B, T, D)
```

```python
import functools
import math

import jax
import jax.numpy as jnp
from jax import lax
from jax.experimental import pallas as pl
from jax.experimental.pallas import tpu as pltpu

F32 = jnp.float32
BF16 = jnp.bfloat16

EPS = 1e-6
HEAD_DIM = 128
CHUNK = 64
SUB = 16
N_SUB = CHUNK // SUB
N_PIECES = N_SUB * (N_SUB + 1) // 2
A_COLS = SUB * N_PIECES
EXP2_CLAMP = 112.0
INTRA_UNROLL = 16
STATE_UNROLL = 16
ROW_PARTS = 2
VMEM_LIMIT = 56 * 1024 * 1024


def _dot(a, b):
    return jnp.dot(a, b, preferred_element_type=F32)


def _dot_nt(a, b):
    return lax.dot_general(a, b, (((1,), (1,)), ((), ())), preferred_element_type=F32)


def _dot_tn(a, b):
    return lax.dot_general(a, b, (((0,), (0,)), ((), ())), preferred_element_type=F32)


def _sigmoid(x):
    return 1.0 / (1.0 + jnp.exp(-x))


def _rms_rows(x, w):
    ms = jnp.mean(x * x, axis=-1, keepdims=True)
    return x * lax.rsqrt(ms + EPS) * w


def _params(semantics, vmem_limit=VMEM_LIMIT):
    return pltpu.CompilerParams(dimension_semantics=semantics, vmem_limit_bytes=vmem_limit)


def _bf16(w):
    return w.astype(BF16)


def _side_cast_plan(weights, grid):
    n_steps = math.prod(grid)
    strides = [math.prod(grid[d + 1:]) for d in range(len(grid))]

    def step(ids):
        return sum(i * s for i, s in zip(ids, strides))

    ins, in_specs, out_specs, shapes = [], [], [], []
    for stack, idx in weights:
        n, rows, cols = stack.shape
        assert rows % n_steps == 0, (stack.shape, grid)
        block = (None, rows // n_steps, cols)
        ins.append(stack.reshape(n * n_steps, rows // n_steps, cols))
        in_specs.append(pl.BlockSpec(block, lambda *ids, base=idx * n_steps: (base + step(ids), 0, 0)))
        out_specs.append(pl.BlockSpec(block, lambda *ids: (step(ids), 0, 0)))
        shapes.append(jax.ShapeDtypeStruct((n_steps, rows // n_steps, cols), BF16))
    return ins, in_specs, out_specs, shapes


def _side_cast(src_refs, dst_refs):
    for src, dst in zip(src_refs, dst_refs):
        dst[...] = src[...].astype(BF16)


def _chunk_cumsum(x):
    n, d = x.shape
    pos = lax.broadcasted_iota(jnp.int32, (n, d), 0) & (CHUNK - 1)
    s = 1
    while s < CHUNK:
        pad = max(8, s)
        xp = jnp.concatenate([jnp.zeros((pad, d), x.dtype), x], axis=0)
        x = x + jnp.where(pos >= s, xp[pad - s:pad - s + n], 0.0)
        s *= 2
    return x


def _norm_cast_kernel(x_ref, w_ref, o_ref):
    o_ref[...] = _rms_rows(x_ref[...], w_ref[...]).astype(BF16)


def _norm_cast(x, norm_w, *, tm=1024):
    N, D = x.shape
    return pl.pallas_call(
        _norm_cast_kernel,
        grid=(N // tm,),
        in_specs=[pl.BlockSpec((tm, D), lambda i: (i, 0)), pl.BlockSpec((1, D), lambda i: (0, 0))],
        out_specs=pl.BlockSpec((tm, D), lambda i: (i, 0)),
        out_shape=jax.ShapeDtypeStruct((N, D), BF16),
        compiler_params=_params(("parallel",)),
        name="norm_cast",
    )(x, norm_w)


def _hgrn_proj_kernel(*refs, layer, heads_per_step, n_side):
    xn_ref, lbt_ref = refs[:2]
    w_refs = refs[2:6]
    q_ref, k_ref, b_ref, bmin_ref, v_ref, g_ref = refs[6 + n_side:12 + n_side]
    wb_ref = refs[-1]
    _side_cast(refs[6:6 + n_side], refs[12 + n_side:12 + 2 * n_side])

    @pl.when((pl.program_id(1) == 0) & (pl.program_id(2) == 0))
    def _():
        for p, w_ref in enumerate(w_refs):
            wb_ref[p] = w_ref[...].astype(BF16)

    rows = [lbt_ref[l:l + 1, :] for l in range(lbt_ref.shape[0])]
    m = functools.reduce(jnp.maximum, rows)
    es = [jnp.exp(r - m) for r in rows]
    lb = sum(es[:layer + 1]) / sum(es)

    rp = xn_ref.shape[0] // ROW_PARTS
    bmins = [None] * heads_per_step
    for p in range(ROW_PARTS):
        rs = slice(p * rp, (p + 1) * rp)
        xn = xn_ref[rs, :]
        q = _dot(xn, wb_ref[0])
        q = q * _sigmoid(q) * (HEAD_DIM ** -0.5)
        f = lb + (1.0 - lb) * _sigmoid(_dot(xn, wb_ref[1]))
        b2 = _chunk_cumsum(jnp.log2(f))
        k = 1.0 - f
        v = _dot(xn, wb_ref[2])
        g = _dot(xn, wb_ref[3])
        g_ref[rs, :] = (g * _sigmoid(g)).astype(BF16)
        for hh in range(heads_per_step):
            sl = slice(hh * HEAD_DIM, (hh + 1) * HEAD_DIM)
            q_ref[0, hh, rs, :] = q[:, sl].astype(BF16)
            k_ref[0, hh, rs, :] = k[:, sl].astype(BF16)
            b_ref[0, hh, rs, :] = b2[:, sl]
            v_ref[0, hh, rs, :] = v[:, sl].astype(BF16)
            bm = jnp.min(b2[:, sl].reshape(-1, 8, HEAD_DIM), axis=0)
            bmins[hh] = bm if bmins[hh] is None else jnp.minimum(bmins[hh], bm)
    for hh in range(heads_per_step):
        bmin_ref[0, hh, 0] = bmins[hh]


def _hgrn_proj(xn, lb_table, w_in, side_weights, *, layer, tm=1024, heads_per_step=2):
    B, T, D = xn.shape
    xn = xn.reshape(B * T, D)
    H = D // HEAD_DIM
    tn = heads_per_step * HEAD_DIM
    nq = D // tn
    grid = (H // heads_per_step, B, T // tm)
    head_shape = (B, H, T, HEAD_DIM)
    head_spec = pl.BlockSpec((1, heads_per_step, tm, HEAD_DIM), lambda j, b, i: (b, j, i, 0))
    w_spec = lambda part: pl.BlockSpec((D, tn), lambda j, b, i: (0, part * nq + j))
    side_in, side_in_specs, side_out_specs, side_shapes = _side_cast_plan(side_weights, grid)
    outs = pl.pallas_call(
        functools.partial(_hgrn_proj_kernel, layer=layer, heads_per_step=heads_per_step,
                          n_side=len(side_in)),
        grid=grid,
        in_specs=[
            pl.BlockSpec((tm, D), lambda j, b, i: (b * (T // tm) + i, 0)),
            pl.BlockSpec((lb_table.shape[0], tn), lambda j, b, i: (0, j)),
            w_spec(0), w_spec(1), w_spec(2), w_spec(3),
        ] + side_in_specs,
        out_specs=[
            head_spec, head_spec, head_spec,
            pl.BlockSpec((1, heads_per_step, 1, 8, HEAD_DIM), lambda j, b, i: (b, j, i, 0, 0)),
            head_spec,
            pl.BlockSpec((tm, tn), lambda j, b, i: (b * (T // tm) + i, j)),
        ] + side_out_specs,
        out_shape=[
            jax.ShapeDtypeStruct(head_shape, BF16),
            jax.ShapeDtypeStruct(head_shape, BF16),
            jax.ShapeDtypeStruct(head_shape, F32),
            jax.ShapeDtypeStruct((B, H, T // tm, 8, HEAD_DIM), F32),
            jax.ShapeDtypeStruct(head_shape, BF16),
            jax.ShapeDtypeStruct((B * T, D), BF16),
        ] + side_shapes,
        scratch_shapes=[pltpu.VMEM((4, D, tn), BF16)],
        compiler_params=_params(("arbitrary", "arbitrary", "arbitrary")),
        name="hgrn_proj",
    )(xn, lb_table, *([w_in] * 4), *side_in)
    return outs[:6], [o.reshape(w.shape[1:]) for o, (w, _) in zip(outs[6:], side_weights)]


def _piece_offset(i):
    return SUB * i * (i + 1) // 2


def _score_mask():
    rw = lax.broadcasted_iota(jnp.int32, (CHUNK, A_COLS), 0)
    cw = lax.broadcasted_iota(jnp.int32, (CHUNK, A_COLS), 1)
    keep = None
    for i in range(N_SUB):
        diag0 = _piece_offset(i) + SUB * i
        in_rows = (rw >= SUB * i) & (rw < SUB * (i + 1))
        left = (cw >= _piece_offset(i)) & (cw < diag0)
        diag = (cw >= diag0) & (cw < diag0 + SUB) & (cw - diag0 <= rw - SUB * i)
        m = in_rows & (left | diag)
        keep = m if keep is None else keep | m
    return keep


def _exact_diag(slab, qb, kb, bb, lane0):
    lr = lax.broadcasted_iota(jnp.int32, slab.shape, 0)
    lc = lax.broadcasted_iota(jnp.int32, slab.shape, 1)
    for s in range(SUB):
        e = jnp.exp2(jnp.minimum(bb - bb[s:s + 1, :], 0.0))
        col = jnp.sum(qb * kb[s:s + 1, :] * e, axis=-1, keepdims=True)
        slab = jnp.where((lc == lane0 + s) & (lr >= s), col, slab)
    return slab


def _block_rows(rows):
    return jnp.concatenate([jnp.broadcast_to(r, (SUB, r.shape[-1])) for r in rows], axis=0)


def _chunk_factors(q_ref, k_ref, b_ref, rows):
    q = q_ref[0, 0, rows, :].astype(F32)
    k = k_ref[0, 0, rows, :].astype(F32)
    b = b_ref[0, 0, rows, :]

    ends = [b[SUB * (j + 1) - 1:SUB * (j + 1), :] for j in range(N_SUB)]
    refs = [jnp.zeros_like(ends[0])] + ends[:-1]
    b_last = ends[-1]
    d = b - _block_rows(refs)
    q_blk = (q * jnp.exp2(d)).astype(BF16)
    k_diag = k * jnp.exp2(jnp.minimum(-d, EXP2_CLAMP))
    k_end = k * jnp.exp2(_block_rows(ends) - b)
    pieces = []
    for i in range(N_SUB):
        for j in range(i):
            kj = k_end[SUB * j:SUB * (j + 1)]
            if j < i - 1:
                kj = kj * jnp.exp2(refs[i] - ends[j])
            pieces.append(kj.astype(BF16))
        pieces.append(k_diag[SUB * i:SUB * (i + 1)].astype(BF16))
    k_all = jnp.concatenate(pieces, axis=0)
    q_dec = (q * jnp.exp2(b)).astype(BF16)
    k_dec = (k_end * _block_rows([jnp.exp2(b_last - e) for e in ends])).astype(BF16)
    return (q, k, b), q_blk, k_all, q_dec, k_dec, jnp.exp2(b_last)


def _fix_diag(a, q, k, b):
    blocks = []
    for i in range(N_SUB):
        rs = slice(SUB * i, SUB * (i + 1))
        diag0 = _piece_offset(i) + SUB * i
        tile = diag0 // 128
        slabs = [a[rs, 128 * t:min(128 * (t + 1), A_COLS)] for t in range(pl.cdiv(A_COLS, 128))]
        slabs[tile] = _exact_diag(slabs[tile], q[rs], k[rs], b[rs], diag0 % 128)
        blocks.append(jnp.concatenate(slabs, axis=1))
    return jnp.concatenate(blocks, axis=0)


def _stack_values(v):
    return jnp.concatenate([v[0:SUB * (i + 1)] for i in range(N_SUB)], axis=0)


def _hgrn_rec_kernel(*refs, tm, n_side):
    q_ref, k_ref, b_ref, bmin_ref, v_ref = refs[:5]
    o_ref = refs[5 + n_side]
    st_ref, qd_scr, oi_scr, kv_scr, dl_scr = refs[6 + 2 * n_side:]
    _side_cast(refs[5:5 + n_side], refs[6 + n_side:6 + 2 * n_side])

    @pl.when(pl.program_id(2) == 0)
    def _():
        st_ref[...] = jnp.zeros_like(st_ref)

    n_chunks = tm // CHUNK

    def chunk_rows(c):
        return pl.ds(pl.multiple_of(c * CHUNK, CHUNK), CHUNK)

    def state_rows(c):
        return pl.ds(pl.multiple_of(c * HEAD_DIM, HEAD_DIM), HEAD_DIM)

    def decay_rows(c, n):
        return pl.ds(pl.multiple_of(c * 8, 8), n)

    keep = _score_mask()

    def fast():
        def intra_body(c, carry):
            ccs = [c * INTRA_UNROLL + u for u in range(INTRA_UNROLL)]
            fac = [_chunk_factors(q_ref, k_ref, b_ref, chunk_rows(cc)) for cc in ccs]
            scores = [jnp.where(keep, _dot_nt(f[1], f[2]), 0.0).astype(BF16) for f in fac]
            for cc, f in zip(ccs, fac):
                kv_scr[state_rows(cc), :] = _dot_tn(v_ref[0, 0, chunk_rows(cc), :], f[4])
                qd_scr[chunk_rows(cc), :] = f[3]
                dl_scr[decay_rows(cc, 8), :] = jnp.broadcast_to(f[5], (8, HEAD_DIM))
            for cc, a in zip(ccs, scores):
                rows = chunk_rows(cc)
                oi_scr[rows, :] = _dot(a, _stack_values(v_ref[0, 0, rows, :]))
            return carry

        lax.fori_loop(0, n_chunks // INTRA_UNROLL, intra_body, 0)

        def state_body(c, st):
            for u in range(STATE_UNROLL):
                cc = c * STATE_UNROLL + u
                rows = chunk_rows(cc)
                o = oi_scr[rows, :] + _dot_nt(qd_scr[rows, :], st.astype(BF16))
                o_ref[0, rows, :] = o.astype(o_ref.dtype)
                st = st * dl_scr[decay_rows(cc, 1), :] + kv_scr[state_rows(cc), :]
            return st

        st_ref[...] = lax.fori_loop(0, n_chunks // STATE_UNROLL, state_body, st_ref[...])

    def exact():
        def body(c, st):
            rows = chunk_rows(c)
            (q, k, b), q_blk, k_all, q_dec, k_dec, dl = _chunk_factors(q_ref, k_ref, b_ref, rows)
            a = _fix_diag(jnp.where(keep, _dot_nt(q_blk, k_all), 0.0), q, k, b).astype(BF16)
            v = v_ref[0, 0, rows, :]
            o = _dot(a, _stack_values(v)) + _dot_nt(q_dec, st.astype(BF16))
            o_ref[0, rows, :] = o.astype(o_ref.dtype)
            return st * dl + _dot_tn(v, k_dec)

        st_ref[...] = lax.fori_loop(0, n_chunks, body, st_ref[...])

    clamp_may_bind = jnp.min(bmin_ref[0, 0]) < -(EXP2_CLAMP - 1.0)
    pl.when(jnp.logical_not(clamp_may_bind))(fast)
    pl.when(clamp_may_bind)(exact)


def _hgrn_rec(q, k, b2, bmin, v, side_weights, *, tm=4096):
    B, H, T, _ = q.shape
    tm = min(tm, T)
    assert T % tm == 0 and (tm // CHUNK) % INTRA_UNROLL == 0 and (tm // CHUNK) % STATE_UNROLL == 0
    n_min = bmin.shape[2] // (T // tm)
    spec = pl.BlockSpec((1, 1, tm, HEAD_DIM), lambda b, h, t: (b, h, t, 0))
    grid = (B, H, T // tm)
    side_in, side_in_specs, side_out_specs, side_shapes = _side_cast_plan(side_weights, grid)
    outs = pl.pallas_call(
        functools.partial(_hgrn_rec_kernel, tm=tm, n_side=len(side_in)),
        grid=grid,
        in_specs=[spec, spec, spec,
                  pl.BlockSpec((1, 1, n_min, 8, HEAD_DIM), lambda b, h, t: (b, h, t, 0, 0)),
                  spec] + side_in_specs,
        out_specs=[pl.BlockSpec((1, tm, HEAD_DIM), lambda b, h, t: (b, t, h))] + side_out_specs,
        out_shape=[jax.ShapeDtypeStruct((B, T, H * HEAD_DIM), BF16)] + side_shapes,
        scratch_shapes=[pltpu.VMEM((HEAD_DIM, HEAD_DIM), F32),
                        pltpu.VMEM((tm, HEAD_DIM), BF16),
                        pltpu.VMEM((tm, HEAD_DIM), F32),
                        pltpu.VMEM((tm // CHUNK * HEAD_DIM, HEAD_DIM), F32),
                        pltpu.VMEM((tm // CHUNK * 8, HEAD_DIM), F32)],
        compiler_params=_params(("parallel", "parallel", "arbitrary")),
        name="hgrn_rec",
    )(q, k, b2, bmin, v, *side_in)
    return outs[0], [o.reshape(w.shape[1:]) for o, (w, _) in zip(outs[1:], side_weights)]


def _hgrn_out_kernel(o_ref, g_ref, h_ref, gain_ref, w_ref, out_ref):
    y = _rms_rows(o_ref[...].astype(F32), gain_ref[...]) * g_ref[...].astype(F32)
    out_ref[...] = h_ref[...] + _dot(y.astype(BF16), w_ref[...])


def _hgrn_out(o, gate, h, gain, w_out, *, tm=512):
    N, D = h.shape
    row_spec = pl.BlockSpec((tm, D), lambda i: (i, 0))
    return pl.pallas_call(
        _hgrn_out_kernel,
        grid=(N // tm,),
        in_specs=[row_spec, row_spec, row_spec,
                  pl.BlockSpec((1, D), lambda i: (0, 0)),
                  pl.BlockSpec((D, D), lambda i: (0, 0))],
        out_specs=row_spec,
        out_shape=jax.ShapeDtypeStruct((N, D), F32),
        compiler_params=_params(("parallel",)),
        name="hgrn_out",
    )(o, gate, h, gain, w_out)


TAIL = 8


def _conv3(prev, u, cw):
    rows = u.shape[0]
    ue = jnp.concatenate([prev, u], axis=0)
    return (cw[0:1, :] * ue[TAIL - 2:TAIL - 2 + rows]
            + cw[1:2, :] * ue[TAIL - 1:TAIL - 1 + rows]
            + cw[2:3, :] * u)


def _tile_step(h_hbm, h_buf, h_sem, nw_ref, xn_ref, out_ref, tails, tm, tiles_per_batch, row_parts,
               body):
    i = pl.program_id(0)
    j = pl.program_id(1)
    rp = tm // row_parts

    def tile_copy(t):
        rows = pl.ds(pl.multiple_of(t * tm, tm), tm)
        return pltpu.make_async_copy(h_hbm.at[rows, :], h_buf, h_sem)

    @pl.when((i == 0) & (j == 0))
    def _():
        tile_copy(i).start()

    @pl.when((j == 1) & (i + 1 < pl.num_programs(0)))
    def _():
        tile_copy(i + 1).start()

    @pl.when(i % tiles_per_batch == 0)
    def _():
        for t in tails:
            t[j] = jnp.zeros(t.shape[1:], t.dtype)

    @pl.when(j == 0)
    def _():
        tile_copy(i).wait()
        xns = []
        for p in range(row_parts):
            rs = slice(p * rp, (p + 1) * rp)
            hp = h_buf[rs, :]
            xns.append(_rms_rows(hp, nw_ref[...]).astype(BF16))
            xn_ref[rs, :] = xns[-1]
            out_ref[rs, :] = hp
        body(xns)

    @pl.when(j > 0)
    def _():
        body([xn_ref[p * rp:(p + 1) * rp, :] for p in range(row_parts)])


def _tile_scratch(tm, d):
    return [pltpu.VMEM((tm, d), BF16), pltpu.VMEM((tm, d), F32), pltpu.SemaphoreType.DMA(())]


def _ffn_kernel(h_hbm, nw_ref, wg_ref, wv_ref, cwg_ref, cwv_ref, wd_ref, fw_ref,
                out_ref, xn_ref, h_buf, h_sem, tg_ref, tv_ref,
                *, tm, tiles_per_batch, final_norm, row_parts):
    j = pl.program_id(1)
    rp = tm // row_parts

    def body(xns):
        ups = [(_dot(xn, wg_ref[...]), _dot(xn, wv_ref[...])) for xn in xns]
        prev_g, prev_v = tg_ref[j], tv_ref[j]
        acts = []
        for ug, uv in ups:
            cg = _conv3(prev_g, ug, cwg_ref[...])
            cv = _conv3(prev_v, uv, cwv_ref[...])
            prev_g, prev_v = ug[rp - TAIL:], uv[rp - TAIL:]
            acts.append((cg * _sigmoid(cg) * cv).astype(BF16))
        tg_ref[j] = prev_g
        tv_ref[j] = prev_v
        for p, act in enumerate(acts):
            out_ref[p * rp:(p + 1) * rp, :] += _dot(act, wd_ref[...])

    _tile_step(h_hbm, h_buf, h_sem, nw_ref, xn_ref, out_ref, (tg_ref, tv_ref), tm, tiles_per_batch,
               row_parts, body)

    if final_norm:
        @pl.when(j == pl.num_programs(1) - 1)
        def _():
            out_ref[...] = _rms_rows(out_ref[...], fw_ref[...])


def _ffn(h, seq_len, norm_w, w_up, conv_w, w_down, final_w, *, final_norm, tm=1024, tn=512,
         row_parts=ROW_PARTS, vmem_limit=VMEM_LIMIT):
    N, D = h.shape
    F = w_down.shape[0]
    nf = F // tn
    assert nf >= 2
    return pl.pallas_call(
        functools.partial(_ffn_kernel, tm=tm, tiles_per_batch=seq_len // tm,
                          final_norm=final_norm, row_parts=row_parts),
        grid=(N // tm, nf),
        in_specs=[
            pl.BlockSpec(memory_space=pl.ANY),
            pl.BlockSpec((1, D), lambda i, j: (0, 0)),
            pl.BlockSpec((D, tn), lambda i, j: (0, j)),
            pl.BlockSpec((D, tn), lambda i, j: (0, nf + j)),
            pl.BlockSpec((3, tn), lambda i, j: (0, j)),
            pl.BlockSpec((3, tn), lambda i, j: (0, nf + j)),
            pl.BlockSpec((tn, D), lambda i, j: (j, 0)),
            pl.BlockSpec((1, D), lambda i, j: (0, 0)),
        ],
        out_specs=pl.BlockSpec((tm, D), lambda i, j: (i, 0)),
        out_shape=jax.ShapeDtypeStruct((N, D), F32),
        scratch_shapes=_tile_scratch(tm, D) + [pltpu.VMEM((nf, TAIL, tn), F32)] * 2,
        compiler_params=_params(("arbitrary", "arbitrary"), vmem_limit),
        name="ffn_final" if final_norm else "ffn",
    )(h, norm_w, *([_bf16(w_up)] * 2), conv_w, conv_w, _bf16(w_down), final_w)


def _sc_kernel(h_hbm, nw_ref, wb_ref, wc_ref, wh_ref, cw_ref, wd_ref,
               out_ref, xn_ref, h_buf, h_sem, tz_ref, *, tm, tiles_per_batch, row_parts):
    j = pl.program_id(1)
    rp = tm // row_parts

    def body(xns):
        ups = [(_dot(xn, wc_ref[...]), _dot(xn, wh_ref[...]), _dot(xn, wb_ref[...])) for xn in xns]
        prev_z = tz_ref[j]
        ys = []
        for uc, uh, gb in ups:
            z = uc * uh
            ys.append((gb * _conv3(prev_z, z, cw_ref[...])).astype(BF16))
            prev_z = z[rp - TAIL:]
        tz_ref[j] = prev_z
        for p, y in enumerate(ys):
            out_ref[p * rp:(p + 1) * rp, :] += _dot(y, wd_ref[...])

    _tile_step(h_hbm, h_buf, h_sem, nw_ref, xn_ref, out_ref, (tz_ref,), tm, tiles_per_batch,
               row_parts, body)


def _short_conv(h, seq_len, norm_w, w_in, conv_w, w_out, *, tm=1024, tn=512, row_parts=ROW_PARTS,
                vmem_limit=60 * 1024 * 1024):
    N, D = h.shape
    nd = D // tn
    assert nd >= 2
    w_spec = lambda part: pl.BlockSpec((D, tn), lambda i, j: (0, part * nd + j))
    return pl.pallas_call(
        functools.partial(_sc_kernel, tm=tm, tiles_per_batch=seq_len // tm, row_parts=row_parts),
        grid=(N // tm, nd),
        in_specs=[
            pl.BlockSpec(memory_space=pl.ANY),
            pl.BlockSpec((1, D), lambda i, j: (0, 0)),
            w_spec(0), w_spec(1), w_spec(2),
            pl.BlockSpec((3, tn), lambda i, j: (0, j)),
            pl.BlockSpec((tn, D), lambda i, j: (j, 0)),
        ],
        out_specs=pl.BlockSpec((tm, D), lambda i, j: (i, 0)),
        out_shape=jax.ShapeDtypeStruct((N, D), F32),
        scratch_shapes=_tile_scratch(tm, D) + [pltpu.VMEM((nd, TAIL, tn), F32)],
        compiler_params=_params(("arbitrary", "arbitrary"), vmem_limit),
        name="short_conv",
    )(h, norm_w, *([_bf16(w_in)] * 3), conv_w, _bf16(w_out))


def kernel(x, norm_mix, norm_ffn, hgrn_w_in, hgrn_lb_table, hgrn_out_norm, hgrn_w_out,
           sc_w_in, sc_conv, sc_w_out, ffn_w_up, ffn_conv, ffn_w_down, final_norm):
    B, T, D = x.shape
    depth = norm_mix.shape[0]
    n_mixers = 2
    row = lambda w: w.reshape(1, -1).astype(F32)
    stacks = dict(hgrn_w_in=hgrn_w_in, hgrn_w_out=hgrn_w_out, sc_w_in=sc_w_in, sc_w_out=sc_w_out,
                  ffn_w_up=ffn_w_up, ffn_w_down=ffn_w_down)

    def layer_weights(i):
        mixer = ("hgrn_w_in", "hgrn_w_out") if i % n_mixers == 0 else ("sc_w_in", "sc_w_out")
        return [(name, i // n_mixers) for name in mixer] + [("ffn_w_up", i), ("ffn_w_down", i)]

    cast = {}

    def weight(name, idx):
        return cast.get((name, idx), stacks[name][idx])

    h = x.reshape(B * T, D)
    for i in range(depth):
        j = i // n_mixers
        if i % n_mixers == 0:
            proj_side = layer_weights(i)[1:] if i == 0 else []
            rec_side = [kw for l in range(1, depth) for kw in layer_weights(l)] if i == 0 else []
            side = lambda keys: [(stacks[name], idx) for name, idx in keys]
            xn = _norm_cast(h, row(norm_mix[i])).reshape(B, T, D)
            outs, done = _hgrn_proj(xn, hgrn_lb_table.astype(F32), weight("hgrn_w_in", j),
                                    side(proj_side), layer=i)
            cast.update(zip(proj_side, done))
            q, k, b2, bmin, v, gate = outs
            o, done = _hgrn_rec(q, k, b2, bmin, v, side(rec_side))
            cast.update(zip(rec_side, done))
            h = _hgrn_out(o.reshape(B * T, D), gate, h, row(hgrn_out_norm[j]),
                          weight("hgrn_w_out", j).astype(BF16))
        else:
            h = _short_conv(h, T, row(norm_mix[i]), weight("sc_w_in", j), sc_conv[j].astype(F32),
                            weight("sc_w_out", j))
        h = _ffn(h, T, row(norm_ffn[i]), weight("ffn_w_up", i), ffn_conv[i].astype(F32),
                 weight("ffn_w_down", i), row(final_norm), final_norm=(i == depth - 1))
    return h.reshape(B, T, D)
```

```python
import functools
import math

import jax
import jax.numpy as jnp
from jax import lax
from jax.experimental import pallas as pl
from jax.experimental.pallas import tpu as pltpu

F32 = jnp.float32
BF16 = jnp.bfloat16

EPS = 1e-6
HEAD_DIM = 128
CHUNK = 64
SUB = 16
N_SUB = CHUNK // SUB
N_PIECES = N_SUB * (N_SUB + 1) // 2
A_COLS = SUB * N_PIECES
EXP2_CLAMP = 112.0
INTRA_UNROLL = 16
STATE_UNROLL = 16
ROW_PARTS = 2
VMEM_LIMIT = 56 * 1024 * 1024


def _dot(a, b):
    return jnp.dot(a, b, preferred_element_type=F32)


def _dot_nt(a, b):
    return lax.dot_general(a, b, (((1,), (1,)), ((), ())), preferred_element_type=F32)


def _dot_tn(a, b):
    return lax.dot_general(a, b, (((0,), (0,)), ((), ())), preferred_element_type=F32)


def _sigmoid(x):
    return 1.0 / (1.0 + jnp.exp(-x))


def _rms_rows(x, w):
    ms = jnp.mean(x * x, axis=-1, keepdims=True)
    return x * lax.rsqrt(ms + EPS) * w


def _params(semantics, vmem_limit=VMEM_LIMIT):
    return pltpu.CompilerParams(dimension_semantics=semantics, vmem_limit_bytes=vmem_limit)


def _bf16(w):
    return w.astype(BF16)


def _side_cast_plan(weights, grid):
    n_steps = math.prod(grid)
    strides = [math.prod(grid[d + 1:]) for d in range(len(grid))]

    def step(ids):
        return sum(i * s for i, s in zip(ids, strides))

    ins, in_specs, out_specs, shapes = [], [], [], []
    for stack, idx in weights:
        n, rows, cols = stack.shape
        f = n_steps // math.gcd(rows, n_steps)
        assert cols % (f * 128) == 0, (stack.shape, grid)
        rows, cols = rows * f, cols // f
        block = (None, rows // n_steps, cols)
        ins.append(stack.reshape(n * n_steps, rows // n_steps, cols))
        in_specs.append(pl.BlockSpec(block, lambda *ids, base=idx * n_steps: (base + step(ids), 0, 0)))
        out_specs.append(pl.BlockSpec(block, lambda *ids: (step(ids), 0, 0)))
        shapes.append(jax.ShapeDtypeStruct((n_steps, rows // n_steps, cols), BF16))
    return ins, in_specs, out_specs, shapes


def _side_cast(src_refs, dst_refs):
    for src, dst in zip(src_refs, dst_refs):
        dst[...] = src[...].astype(BF16)


def _chunk_cumsum(x):
    n, d = x.shape
    pos = lax.broadcasted_iota(jnp.int32, (n, d), 0) & (CHUNK - 1)
    s = 1
    while s < CHUNK:
        pad = max(8, s)
        xp = jnp.concatenate([jnp.zeros((pad, d), x.dtype), x], axis=0)
        x = x + jnp.where(pos >= s, xp[pad - s:pad - s + n], 0.0)
        s *= 2
    return x


def _norm_cast_kernel(x_ref, w_ref, o_ref):
    o_ref[...] = _rms_rows(x_ref[...], w_ref[...]).astype(BF16)


def _norm_cast(x, norm_w, *, tm=1024):
    N, D = x.shape
    return pl.pallas_call(
        _norm_cast_kernel,
        grid=(N // tm,),
        in_specs=[pl.BlockSpec((tm, D), lambda i: (i, 0)), pl.BlockSpec((1, D), lambda i: (0, 0))],
        out_specs=pl.BlockSpec((tm, D), lambda i: (i, 0)),
        out_shape=jax.ShapeDtypeStruct((N, D), BF16),
        compiler_params=_params(("parallel",)),
        name="norm_cast",
    )(x, norm_w)


def _hgrn_proj_kernel(*refs, layer, heads_per_step, n_side):
    xn_ref, lbt_ref = refs[:2]
    w_refs = refs[2:6]
    q_ref, k_ref, b_ref, bmin_ref, v_ref, g_ref = refs[6 + n_side:12 + n_side]
    wb_ref = refs[-1]
    _side_cast(refs[6:6 + n_side], refs[12 + n_side:12 + 2 * n_side])

    @pl.when((pl.program_id(1) == 0) & (pl.program_id(2) == 0))
    def _():
        for p, w_ref in enumerate(w_refs):
            wb_ref[p] = w_ref[...].astype(BF16)

    rows = [lbt_ref[l:l + 1, :] for l in range(lbt_ref.shape[0])]
    m = functools.reduce(jnp.maximum, rows)
    es = [jnp.exp(r - m) for r in rows]
    lb = sum(es[:layer + 1]) / sum(es)

    rp = xn_ref.shape[0] // ROW_PARTS
    bmins = [None] * heads_per_step
    for p in range(ROW_PARTS):
        rs = slice(p * rp, (p + 1) * rp)
        xn = xn_ref[rs, :]
        q = _dot(xn, wb_ref[0])
        q = q * _sigmoid(q) * (HEAD_DIM ** -0.5)
        f = lb + (1.0 - lb) * _sigmoid(_dot(xn, wb_ref[1]))
        b2 = _chunk_cumsum(jnp.log2(f))
        k = 1.0 - f
        v = _dot(xn, wb_ref[2])
        g = _dot(xn, wb_ref[3])
        g_ref[rs, :] = (g * _sigmoid(g)).astype(BF16)
        for hh in range(heads_per_step):
            sl = slice(hh * HEAD_DIM, (hh + 1) * HEAD_DIM)
            q_ref[0, hh, rs, :] = q[:, sl].astype(BF16)
            k_ref[0, hh, rs, :] = k[:, sl].astype(BF16)
            b_ref[0, hh, rs, :] = b2[:, sl]
            v_ref[0, hh, rs, :] = v[:, sl].astype(BF16)
            bm = jnp.min(b2[:, sl].reshape(-1, 8, HEAD_DIM), axis=0)
            bmins[hh] = bm if bmins[hh] is None else jnp.minimum(bmins[hh], bm)
    for hh in range(heads_per_step):
        bmin_ref[0, hh, 0] = bmins[hh]


def _hgrn_proj(xn, lb_table, w_in, side_weights, *, layer, tm=1024, heads_per_step=2):
    B, T, D = xn.shape
    xn = xn.reshape(B * T, D)
    H = D // HEAD_DIM
    tn = heads_per_step * HEAD_DIM
    nq = D // tn
    grid = (H // heads_per_step, B, T // tm)
    head_shape = (B, H, T, HEAD_DIM)
    head_spec = pl.BlockSpec((1, heads_per_step, tm, HEAD_DIM), lambda j, b, i: (b, j, i, 0))
    w_spec = lambda part: pl.BlockSpec((D, tn), lambda j, b, i: (0, part * nq + j))
    side_in, side_in_specs, side_out_specs, side_shapes = _side_cast_plan(side_weights, grid)
    outs = pl.pallas_call(
        functools.partial(_hgrn_proj_kernel, layer=layer, heads_per_step=heads_per_step,
                          n_side=len(side_in)),
        grid=grid,
        in_specs=[
            pl.BlockSpec((tm, D), lambda j, b, i: (b * (T // tm) + i, 0)),
            pl.BlockSpec((lb_table.shape[0], tn), lambda j, b, i: (0, j)),
            w_spec(0), w_spec(1), w_spec(2), w_spec(3),
        ] + side_in_specs,
        out_specs=[
            head_spec, head_spec, head_spec,
            pl.BlockSpec((1, heads_per_step, 1, 8, HEAD_DIM), lambda j, b, i: (b, j, i, 0, 0)),
            head_spec,
            pl.BlockSpec((tm, tn), lambda j, b, i: (b * (T // tm) + i, j)),
        ] + side_out_specs,
        out_shape=[
            jax.ShapeDtypeStruct(head_shape, BF16),
            jax.ShapeDtypeStruct(head_shape, BF16),
            jax.ShapeDtypeStruct(head_shape, F32),
            jax.ShapeDtypeStruct((B, H, T // tm, 8, HEAD_DIM), F32),
            jax.ShapeDtypeStruct(head_shape, BF16),
            jax.ShapeDtypeStruct((B * T, D), BF16),
        ] + side_shapes,
        scratch_shapes=[pltpu.VMEM((4, D, tn), BF16)],
        compiler_params=_params(("arbitrary", "arbitrary", "arbitrary")),
        name="hgrn_proj",
    )(xn, lb_table, *([w_in] * 4), *side_in)
    return outs[:6], [o.reshape(w.shape[1:]) for o, (w, _) in zip(outs[6:], side_weights)]


def _piece_offset(i):
    return SUB * i * (i + 1) // 2


def _score_mask():
    rw = lax.broadcasted_iota(jnp.int32, (CHUNK, A_COLS), 0)
    cw = lax.broadcasted_iota(jnp.int32, (CHUNK, A_COLS), 1)
    keep = None
    for i in range(N_SUB):
        diag0 = _piece_offset(i) + SUB * i
        in_rows = (rw >= SUB * i) & (rw < SUB * (i + 1))
        left = (cw >= _piece_offset(i)) & (cw < diag0)
        diag = (cw >= diag0) & (cw < diag0 + SUB) & (cw - diag0 <= rw - SUB * i)
        m = in_rows & (left | diag)
        keep = m if keep is None else keep | m
    return keep


def _exact_diag(slab, qb, kb, bb, lane0):
    lr = lax.broadcasted_iota(jnp.int32, slab.shape, 0)
    lc = lax.broadcasted_iota(jnp.int32, slab.shape, 1)
    for s in range(SUB):
        e = jnp.exp2(jnp.minimum(bb - bb[s:s + 1, :], 0.0))
        col = jnp.sum(qb * kb[s:s + 1, :] * e, axis=-1, keepdims=True)
        slab = jnp.where((lc == lane0 + s) & (lr >= s), col, slab)
    return slab


def _block_rows(rows):
    return jnp.concatenate([jnp.broadcast_to(r, (SUB, r.shape[-1])) for r in rows], axis=0)


def _chunk_factors(q_ref, k_ref, b_ref, rows):
    q = q_ref[0, 0, rows, :].astype(F32)
    k = k_ref[0, 0, rows, :].astype(F32)
    b = b_ref[0, 0, rows, :]

    ends = [b[SUB * (j + 1) - 1:SUB * (j + 1), :] for j in range(N_SUB)]
    refs = [jnp.zeros_like(ends[0])] + ends[:-1]
    b_last = ends[-1]
    d = b - _block_rows(refs)
    q_blk = (q * jnp.exp2(d)).astype(BF16)
    k_diag = k * jnp.exp2(jnp.minimum(-d, EXP2_CLAMP))
    k_end = k * jnp.exp2(_block_rows(ends) - b)
    pieces = []
    for i in range(N_SUB):
        for j in range(i):
            kj = k_end[SUB * j:SUB * (j + 1)]
            if j < i - 1:
                kj = kj * jnp.exp2(refs[i] - ends[j])
            pieces.append(kj.astype(BF16))
        pieces.append(k_diag[SUB * i:SUB * (i + 1)].astype(BF16))
    k_all = jnp.concatenate(pieces, axis=0)
    q_dec = (q * jnp.exp2(b)).astype(BF16)
    k_dec = (k_end * _block_rows([jnp.exp2(b_last - e) for e in ends])).astype(BF16)
    return (q, k, b), q_blk, k_all, q_dec, k_dec, jnp.exp2(b_last)


def _fix_diag(a, q, k, b):
    blocks = []
    for i in range(N_SUB):
        rs = slice(SUB * i, SUB * (i + 1))
        diag0 = _piece_offset(i) + SUB * i
        tile = diag0 // 128
        slabs = [a[rs, 128 * t:min(128 * (t + 1), A_COLS)] for t in range(pl.cdiv(A_COLS, 128))]
        slabs[tile] = _exact_diag(slabs[tile], q[rs], k[rs], b[rs], diag0 % 128)
        blocks.append(jnp.concatenate(slabs, axis=1))
    return jnp.concatenate(blocks, axis=0)


def _stack_values(v):
    return jnp.concatenate([v[0:SUB * (i + 1)] for i in range(N_SUB)], axis=0)


def _hgrn_rec_kernel(*refs, tm, n_side):
    q_ref, k_ref, b_ref, bmin_ref, v_ref = refs[:5]
    o_ref = refs[5 + n_side]
    st_ref, qd_scr, oi_scr, kv_scr, dl_scr = refs[6 + 2 * n_side:]
    _side_cast(refs[5:5 + n_side], refs[6 + n_side:6 + 2 * n_side])

    @pl.when(pl.program_id(2) == 0)
    def _():
        st_ref[...] = jnp.zeros_like(st_ref)

    n_chunks = tm // CHUNK

    def chunk_rows(c):
        return pl.ds(pl.multiple_of(c * CHUNK, CHUNK), CHUNK)

    def state_rows(c):
        return pl.ds(pl.multiple_of(c * HEAD_DIM, HEAD_DIM), HEAD_DIM)

    def decay_rows(c, n):
        return pl.ds(pl.multiple_of(c * 8, 8), n)

    keep = _score_mask()

    def fast():
        def intra_body(c, carry):
            ccs = [c * INTRA_UNROLL + u for u in range(INTRA_UNROLL)]
            fac = [_chunk_factors(q_ref, k_ref, b_ref, chunk_rows(cc)) for cc in ccs]
            scores = [jnp.where(keep, _dot_nt(f[1], f[2]), 0.0).astype(BF16) for f in fac]
            for cc, f in zip(ccs, fac):
                kv_scr[state_rows(cc), :] = _dot_tn(v_ref[0, 0, chunk_rows(cc), :], f[4])
                qd_scr[chunk_rows(cc), :] = f[3]
                dl_scr[decay_rows(cc, 8), :] = jnp.broadcast_to(f[5], (8, HEAD_DIM))
            for cc, a in zip(ccs, scores):
                rows = chunk_rows(cc)
                oi_scr[rows, :] = _dot(a, _stack_values(v_ref[0, 0, rows, :]))
            return carry

        lax.fori_loop(0, n_chunks // INTRA_UNROLL, intra_body, 0)

        def state_body(c, st):
            for u in range(STATE_UNROLL):
                cc = c * STATE_UNROLL + u
                rows = chunk_rows(cc)
                o = oi_scr[rows, :] + _dot_nt(qd_scr[rows, :], st.astype(BF16))
                o_ref[0, rows, :] = o.astype(o_ref.dtype)
                st = st * dl_scr[decay_rows(cc, 1), :] + kv_scr[state_rows(cc), :]
            return st

        st_ref[...] = lax.fori_loop(0, n_chunks // STATE_UNROLL, state_body, st_ref[...])

    def exact():
        def body(c, st):
            rows = chunk_rows(c)
            (q, k, b), q_blk, k_all, q_dec, k_dec, dl = _chunk_factors(q_ref, k_ref, b_ref, rows)
            a = _fix_diag(jnp.where(keep, _dot_nt(q_blk, k_all), 0.0), q, k, b).astype(BF16)
            v = v_ref[0, 0, rows, :]
            o = _dot(a, _stack_values(v)) + _dot_nt(q_dec, st.astype(BF16))
            o_ref[0, rows, :] = o.astype(o_ref.dtype)
            return st * dl + _dot_tn(v, k_dec)

        st_ref[...] = lax.fori_loop(0, n_chunks, body, st_ref[...])

    clamp_may_bind = jnp.min(bmin_ref[0, 0]) < -(EXP2_CLAMP - 1.0)
    pl.when(jnp.logical_not(clamp_may_bind))(fast)
    pl.when(clamp_may_bind)(exact)


def _hgrn_rec(q, k, b2, bmin, v, side_weights, *, tm=4096):
    B, H, T, _ = q.shape
    tm = min(tm, T)
    assert T % tm == 0 and (tm // CHUNK) % INTRA_UNROLL == 0 and (tm // CHUNK) % STATE_UNROLL == 0
    n_min = bmin.shape[2] // (T // tm)
    spec = pl.BlockSpec((1, 1, tm, HEAD_DIM), lambda b, h, t: (b, h, t, 0))
    grid = (B, H, T // tm)
    side_in, side_in_specs, side_out_specs, side_shapes = _side_cast_plan(side_weights, grid)
    outs = pl.pallas_call(
        functools.partial(_hgrn_rec_kernel, tm=tm, n_side=len(side_in)),
        grid=grid,
        in_specs=[spec, spec, spec,
                  pl.BlockSpec((1, 1, n_min, 8, HEAD_DIM), lambda b, h, t: (b, h, t, 0, 0)),
                  spec] + side_in_specs,
        out_specs=[pl.BlockSpec((1, tm, HEAD_DIM), lambda b, h, t: (b, t, h))] + side_out_specs,
        out_shape=[jax.ShapeDtypeStruct((B, T, H * HEAD_DIM), BF16)] + side_shapes,
        scratch_shapes=[pltpu.VMEM((HEAD_DIM, HEAD_DIM), F32),
                        pltpu.VMEM((tm, HEAD_DIM), BF16),
                        pltpu.VMEM((tm, HEAD_DIM), F32),
                        pltpu.VMEM((tm // CHUNK * HEAD_DIM, HEAD_DIM), F32),
                        pltpu.VMEM((tm // CHUNK * 8, HEAD_DIM), F32)],
        compiler_params=_params(("parallel", "parallel", "arbitrary")),
        name="hgrn_rec",
    )(q, k, b2, bmin, v, *side_in)
    return outs[0], [o.reshape(w.shape[1:]) for o, (w, _) in zip(outs[1:], side_weights)]


def _hgrn_out_kernel(o_ref, g_ref, h_ref, gain_ref, w_ref, out_ref):
    y = _rms_rows(o_ref[...].astype(F32), gain_ref[...]) * g_ref[...].astype(F32)
    out_ref[...] = h_ref[...] + _dot(y.astype(BF16), w_ref[...])


def _hgrn_out(o, gate, h, gain, w_out, *, tm=512):
    N, D = h.shape
    row_spec = pl.BlockSpec((tm, D), lambda i: (i, 0))
    return pl.pallas_call(
        _hgrn_out_kernel,
        grid=(N // tm,),
        in_specs=[row_spec, row_spec, row_spec,
                  pl.BlockSpec((1, D), lambda i: (0, 0)),
                  pl.BlockSpec((D, D), lambda i: (0, 0))],
        out_specs=row_spec,
        out_shape=jax.ShapeDtypeStruct((N, D), F32),
        compiler_params=_params(("parallel",)),
        name="hgrn_out",
    )(o, gate, h, gain, w_out)


TAIL = 8


def _conv3(prev, u, cw):
    rows = u.shape[0]
    ue = jnp.concatenate([prev, u], axis=0)
    return (cw[0:1, :] * ue[TAIL - 2:TAIL - 2 + rows]
            + cw[1:2, :] * ue[TAIL - 1:TAIL - 1 + rows]
            + cw[2:3, :] * u)


def _tile_step(h_hbm, h_buf, h_sem, nw_ref, xn_ref, out_ref, tails, tm, tiles_per_batch, row_parts,
               body):
    i = pl.program_id(0)
    j = pl.program_id(1)
    rp = tm // row_parts

    def tile_copy(t):
        rows = pl.ds(pl.multiple_of(t * tm, tm), tm)
        return pltpu.make_async_copy(h_hbm.at[rows, :], h_buf, h_sem)

    @pl.when((i == 0) & (j == 0))
    def _():
        tile_copy(i).start()

    @pl.when((j == 1) & (i + 1 < pl.num_programs(0)))
    def _():
        tile_copy(i + 1).start()

    @pl.when(i % tiles_per_batch == 0)
    def _():
        for t in tails:
            t[j] = jnp.zeros(t.shape[1:], t.dtype)

    @pl.when(j == 0)
    def _():
        tile_copy(i).wait()
        xns = []
        for p in range(row_parts):
            rs = slice(p * rp, (p + 1) * rp)
            hp = h_buf[rs, :]
            xns.append(_rms_rows(hp, nw_ref[...]).astype(BF16))
            xn_ref[rs, :] = xns[-1]
            out_ref[rs, :] = hp
        body(xns)

    @pl.when(j > 0)
    def _():
        body([xn_ref[p * rp:(p + 1) * rp, :] for p in range(row_parts)])


def _tile_scratch(tm, d):
    return [pltpu.VMEM((tm, d), BF16), pltpu.VMEM((tm, d), F32), pltpu.SemaphoreType.DMA(())]


def _ffn_kernel(*refs, tm, tiles_per_batch, final_norm, row_parts, n_side):
    h_hbm, nw_ref, wg_ref, wv_ref, cwg_ref, cwv_ref, wd_ref, fw_ref = refs[:8]
    out_ref = refs[8 + n_side]
    xn_ref, h_buf, h_sem, tg_ref, tv_ref = refs[9 + 2 * n_side:]
    _side_cast(refs[8:8 + n_side], refs[9 + n_side:9 + 2 * n_side])
    j = pl.program_id(1)
    rp = tm // row_parts

    def body(xns):
        ups = [(_dot(xn, wg_ref[...]), _dot(xn, wv_ref[...])) for xn in xns]
        prev_g, prev_v = tg_ref[j], tv_ref[j]
        acts = []
        for ug, uv in ups:
            cg = _conv3(prev_g, ug, cwg_ref[...])
            cv = _conv3(prev_v, uv, cwv_ref[...])
            prev_g, prev_v = ug[rp - TAIL:], uv[rp - TAIL:]
            acts.append((cg * _sigmoid(cg) * cv).astype(BF16))
        tg_ref[j] = prev_g
        tv_ref[j] = prev_v
        for p, act in enumerate(acts):
            out_ref[p * rp:(p + 1) * rp, :] += _dot(act, wd_ref[...])

    _tile_step(h_hbm, h_buf, h_sem, nw_ref, xn_ref, out_ref, (tg_ref, tv_ref), tm, tiles_per_batch,
               row_parts, body)

    if final_norm:
        @pl.when(j == pl.num_programs(1) - 1)
        def _():
            out_ref[...] = _rms_rows(out_ref[...], fw_ref[...])


def _ffn(h, seq_len, norm_w, w_up, conv_w, w_down, final_w, side_weights, *, final_norm, tm=1024,
         tn=512, row_parts=ROW_PARTS, vmem_limit=VMEM_LIMIT):
    N, D = h.shape
    F = w_down.shape[0]
    nf = F // tn
    assert nf >= 2
    grid = (N // tm, nf)
    side_in, side_in_specs, side_out_specs, side_shapes = _side_cast_plan(side_weights, grid)
    outs = pl.pallas_call(
        functools.partial(_ffn_kernel, tm=tm, tiles_per_batch=seq_len // tm,
                          final_norm=final_norm, row_parts=row_parts, n_side=len(side_in)),
        grid=grid,
        in_specs=[
            pl.BlockSpec(memory_space=pl.ANY),
            pl.BlockSpec((1, D), lambda i, j: (0, 0)),
            pl.BlockSpec((D, tn), lambda i, j: (0, j)),
            pl.BlockSpec((D, tn), lambda i, j: (0, nf + j)),
            pl.BlockSpec((3, tn), lambda i, j: (0, j)),
            pl.BlockSpec((3, tn), lambda i, j: (0, nf + j)),
            pl.BlockSpec((tn, D), lambda i, j: (j, 0)),
            pl.BlockSpec((1, D), lambda i, j: (0, 0)),
        ] + side_in_specs,
        out_specs=[pl.BlockSpec((tm, D), lambda i, j: (i, 0))] + side_out_specs,
        out_shape=[jax.ShapeDtypeStruct((N, D), F32)] + side_shapes,
        scratch_shapes=_tile_scratch(tm, D) + [pltpu.VMEM((nf, TAIL, tn), F32)] * 2,
        compiler_params=_params(("arbitrary", "arbitrary"), vmem_limit),
        name="ffn_final" if final_norm else "ffn",
    )(h, norm_w, *([_bf16(w_up)] * 2), conv_w, conv_w, _bf16(w_down), final_w, *side_in)
    return outs[0], [o.reshape(w.shape[1:]) for o, (w, _) in zip(outs[1:], side_weights)]


def _sc_kernel(h_hbm, nw_ref, wb_ref, wc_ref, wh_ref, cw_ref, wd_ref,
               out_ref, xn_ref, h_buf, h_sem, tz_ref, *, tm, tiles_per_batch, row_parts):
    j = pl.program_id(1)
    rp = tm // row_parts

    def body(xns):
        ups = [(_dot(xn, wc_ref[...]), _dot(xn, wh_ref[...]), _dot(xn, wb_ref[...])) for xn in xns]
        prev_z = tz_ref[j]
        ys = []
        for uc, uh, gb in ups:
            z = uc * uh
            ys.append((gb * _conv3(prev_z, z, cw_ref[...])).astype(BF16))
            prev_z = z[rp - TAIL:]
        tz_ref[j] = prev_z
        for p, y in enumerate(ys):
            out_ref[p * rp:(p + 1) * rp, :] += _dot(y, wd_ref[...])

    _tile_step(h_hbm, h_buf, h_sem, nw_ref, xn_ref, out_ref, (tz_ref,), tm, tiles_per_batch,
               row_parts, body)


def _short_conv(h, seq_len, norm_w, w_in, conv_w, w_out, *, tm=1024, tn=512, row_parts=ROW_PARTS,
                vmem_limit=60 * 1024 * 1024):
    N, D = h.shape
    nd = D // tn
    assert nd >= 2
    w_spec = lambda part: pl.BlockSpec((D, tn), lambda i, j: (0, part * nd + j))
    return pl.pallas_call(
        functools.partial(_sc_kernel, tm=tm, tiles_per_batch=seq_len // tm, row_parts=row_parts),
        grid=(N // tm, nd),
        in_specs=[
            pl.BlockSpec(memory_space=pl.ANY),
            pl.BlockSpec((1, D), lambda i, j: (0, 0)),
            w_spec(0), w_spec(1), w_spec(2),
            pl.BlockSpec((3, tn), lambda i, j: (0, j)),
            pl.BlockSpec((tn, D), lambda i, j: (j, 0)),
        ],
        out_specs=pl.BlockSpec((tm, D), lambda i, j: (i, 0)),
        out_shape=jax.ShapeDtypeStruct((N, D), F32),
        scratch_shapes=_tile_scratch(tm, D) + [pltpu.VMEM((nd, TAIL, tn), F32)],
        compiler_params=_params(("arbitrary", "arbitrary"), vmem_limit),
        name="short_conv",
    )(h, norm_w, *([_bf16(w_in)] * 3), conv_w, _bf16(w_out))


def kernel(x, norm_mix, norm_ffn, hgrn_w_in, hgrn_lb_table, hgrn_out_norm, hgrn_w_out,
           sc_w_in, sc_conv, sc_w_out, ffn_w_up, ffn_conv, ffn_w_down, final_norm):
    B, T, D = x.shape
    depth = norm_mix.shape[0]
    n_mixers = 2
    row = lambda w: w.reshape(1, -1).astype(F32)
    stacks = dict(hgrn_w_in=hgrn_w_in, hgrn_w_out=hgrn_w_out, sc_w_in=sc_w_in, sc_w_out=sc_w_out,
                  ffn_w_up=ffn_w_up, ffn_w_down=ffn_w_down)

    def layer_weights(i):
        mixer = ("hgrn_w_in", "hgrn_w_out") if i % n_mixers == 0 else ("sc_w_in", "sc_w_out")
        return [(name, i // n_mixers) for name in mixer] + [("ffn_w_up", i), ("ffn_w_down", i)]

    cast = {}

    def weight(name, idx):
        return cast.get((name, idx), stacks[name][idx])

    def side(keys):
        return [(stacks[name], idx) for name, idx in keys if (name, idx) not in cast]

    h = x.reshape(B * T, D)
    for i in range(depth):
        j = i // n_mixers
        if i % n_mixers == 0:
            rec_side = []
            if i == 0:
                rec_side = layer_weights(0)[1:] + [kw for l in range(1, depth) for kw in layer_weights(l)[:2]]
            xn = _norm_cast(h, row(norm_mix[i])).reshape(B, T, D)
            outs, _ = _hgrn_proj(xn, hgrn_lb_table.astype(F32), weight("hgrn_w_in", j), [], layer=i)
            q, k, b2, bmin, v, gate = outs
            o, done = _hgrn_rec(q, k, b2, bmin, v, side(rec_side))
            cast.update(zip(rec_side, done))
            h = _hgrn_out(o.reshape(B * T, D), gate, h, row(hgrn_out_norm[j]),
                          weight("hgrn_w_out", j).astype(BF16))
        else:
            h = _short_conv(h, T, row(norm_mix[i]), weight("sc_w_in", j), sc_conv[j].astype(F32),
                            weight("sc_w_out", j))
        ffn_side = [kw for kw in layer_weights(i + 1)[2:] if kw not in cast] if i + 1 < depth else []
        h, done = _ffn(h, T, row(norm_ffn[i]), weight("ffn_w_up", i), ffn_conv[i].astype(F32),
                       weight("ffn_w_down", i), row(final_norm), side(ffn_side),
                       final_norm=(i == depth - 1))
        cast.update(zip(ffn_side, done))
    return h.reshape(B, T, D)
```

```python
import functools
import math

import jax
import jax.numpy as jnp
from jax import lax
from jax.experimental import pallas as pl
from jax.experimental.pallas import tpu as pltpu

F32 = jnp.float32
BF16 = jnp.bfloat16

EPS = 1e-6
HEAD_DIM = 128
CHUNK = 64
SUB = 16
N_SUB = CHUNK // SUB
N_PIECES = N_SUB * (N_SUB + 1) // 2
A_COLS = SUB * N_PIECES
EXP2_CLAMP = 112.0
INTRA_UNROLL = 16
STATE_UNROLL = 16
ROW_PARTS = 2
VMEM_LIMIT = 56 * 1024 * 1024


def _dot(a, b):
    return jnp.dot(a, b, preferred_element_type=F32)


def _dot_nt(a, b):
    return lax.dot_general(a, b, (((1,), (1,)), ((), ())), preferred_element_type=F32)


def _dot_tn(a, b):
    return lax.dot_general(a, b, (((0,), (0,)), ((), ())), preferred_element_type=F32)


def _sigmoid(x):
    return 1.0 / (1.0 + jnp.exp(-x))


def _rms_rows(x, w):
    ms = jnp.mean(x * x, axis=-1, keepdims=True)
    return x * lax.rsqrt(ms + EPS) * w


def _params(semantics, vmem_limit=VMEM_LIMIT):
    return pltpu.CompilerParams(dimension_semantics=semantics, vmem_limit_bytes=vmem_limit)


def _bf16(w):
    return w.astype(BF16)


def _side_cast_plan(weights, grid):
    n_steps = math.prod(grid)
    strides = [math.prod(grid[d + 1:]) for d in range(len(grid))]

    def step(ids):
        return sum(i * s for i, s in zip(ids, strides))

    ins, in_specs, out_specs, shapes = [], [], [], []
    for stack, idx in weights:
        n, rows, cols = stack.shape
        assert rows % n_steps == 0, (stack.shape, grid)
        block = (None, rows // n_steps, cols)
        ins.append(stack.reshape(n * n_steps, rows // n_steps, cols))
        in_specs.append(pl.BlockSpec(block, lambda *ids, base=idx * n_steps: (base + step(ids), 0, 0)))
        out_specs.append(pl.BlockSpec(block, lambda *ids: (step(ids), 0, 0)))
        shapes.append(jax.ShapeDtypeStruct((n_steps, rows // n_steps, cols), BF16))
    return ins, in_specs, out_specs, shapes


def _side_cast(src_refs, dst_refs):
    for src, dst in zip(src_refs, dst_refs):
        dst[...] = src[...].astype(BF16)


def _chunk_cumsum(x):
    n, d = x.shape
    pos = lax.broadcasted_iota(jnp.int32, (n, d), 0) & (CHUNK - 1)
    s = 1
    while s < CHUNK:
        pad = max(8, s)
        xp = jnp.concatenate([jnp.zeros((pad, d), x.dtype), x], axis=0)
        x = x + jnp.where(pos >= s, xp[pad - s:pad - s + n], 0.0)
        s *= 2
    return x


def _norm_cast_kernel(x_ref, w_ref, o_ref):
    o_ref[...] = _rms_rows(x_ref[...], w_ref[...]).astype(BF16)


def _norm_cast(x, norm_w, *, tm=1024):
    N, D = x.shape
    return pl.pallas_call(
        _norm_cast_kernel,
        grid=(N // tm,),
        in_specs=[pl.BlockSpec((tm, D), lambda i: (i, 0)), pl.BlockSpec((1, D), lambda i: (0, 0))],
        out_specs=pl.BlockSpec((tm, D), lambda i: (i, 0)),
        out_shape=jax.ShapeDtypeStruct((N, D), BF16),
        compiler_params=_params(("parallel",)),
        name="norm_cast",
    )(x, norm_w)


def _hgrn_proj_kernel(*refs, layer, heads_per_step, n_side):
    xn_ref, lbt_ref = refs[:2]
    w_refs = refs[2:6]
    qkv_ref, b_ref, g_ref = refs[6 + n_side:9 + n_side]
    wb_ref = refs[-1]
    _side_cast(refs[6:6 + n_side], refs[9 + n_side:9 + 2 * n_side])

    @pl.when((pl.program_id(1) == 0) & (pl.program_id(2) == 0))
    def _():
        for p, w_ref in enumerate(w_refs):
            wb_ref[p] = w_ref[...].astype(BF16)

    rows = [lbt_ref[l:l + 1, :] for l in range(lbt_ref.shape[0])]
    m = functools.reduce(jnp.maximum, rows)
    es = [jnp.exp(r - m) for r in rows]
    lb = sum(es[:layer + 1]) / sum(es)

    rp = xn_ref.shape[0] // ROW_PARTS
    for p in range(ROW_PARTS):
        rs = slice(p * rp, (p + 1) * rp)
        xn = xn_ref[rs, :]
        q = _dot(xn, wb_ref[0])
        q = q * _sigmoid(q) * (HEAD_DIM ** -0.5)
        f = lb + (1.0 - lb) * _sigmoid(_dot(xn, wb_ref[1]))
        b2 = _chunk_cumsum(jnp.log2(f))
        k = 1.0 - f
        v = _dot(xn, wb_ref[2])
        g = _dot(xn, wb_ref[3])
        g_ref[rs, :] = (g * _sigmoid(g)).astype(BF16)
        for hh in range(heads_per_step):
            sl = slice(hh * HEAD_DIM, (hh + 1) * HEAD_DIM)
            qkv_ref[0, hh, 0, rs, :] = q[:, sl].astype(BF16)
            qkv_ref[0, hh, 1, rs, :] = k[:, sl].astype(BF16)
            qkv_ref[0, hh, 2, rs, :] = v[:, sl].astype(BF16)
            b_ref[0, hh, rs, :] = b2[:, sl]


def _hgrn_proj(xn, lb_table, w_in, side_weights, *, layer, tm=1024, heads_per_step=2):
    B, T, D = xn.shape
    xn = xn.reshape(B * T, D)
    H = D // HEAD_DIM
    tn = heads_per_step * HEAD_DIM
    nq = D // tn
    grid = (H // heads_per_step, B, T // tm)
    head_shape = (B, H, T, HEAD_DIM)
    head_spec = pl.BlockSpec((1, heads_per_step, tm, HEAD_DIM), lambda j, b, i: (b, j, i, 0))
    w_spec = lambda part: pl.BlockSpec((D, tn), lambda j, b, i: (0, part * nq + j))
    side_in, side_in_specs, side_out_specs, side_shapes = _side_cast_plan(side_weights, grid)
    outs = pl.pallas_call(
        functools.partial(_hgrn_proj_kernel, layer=layer, heads_per_step=heads_per_step,
                          n_side=len(side_in)),
        grid=grid,
        in_specs=[
            pl.BlockSpec((tm, D), lambda j, b, i: (b * (T // tm) + i, 0)),
            pl.BlockSpec((lb_table.shape[0], tn), lambda j, b, i: (0, j)),
            w_spec(0), w_spec(1), w_spec(2), w_spec(3),
        ] + side_in_specs,
        out_specs=[
            pl.BlockSpec((1, heads_per_step, 3, tm, HEAD_DIM), lambda j, b, i: (b, j, 0, i, 0)),
            head_spec,
            pl.BlockSpec((tm, tn), lambda j, b, i: (b * (T // tm) + i, j)),
        ] + side_out_specs,
        out_shape=[
            jax.ShapeDtypeStruct((B, H, 3, T, HEAD_DIM), BF16),
            jax.ShapeDtypeStruct(head_shape, F32),
            jax.ShapeDtypeStruct((B * T, D), BF16),
        ] + side_shapes,
        scratch_shapes=[pltpu.VMEM((4, D, tn), BF16)],
        compiler_params=_params(("arbitrary", "arbitrary", "arbitrary")),
        name="hgrn_proj",
    )(xn, lb_table, *([w_in] * 4), *side_in)
    return outs[:3], [o.reshape(w.shape[1:]) for o, (w, _) in zip(outs[3:], side_weights)]


def _piece_offset(i):
    return SUB * i * (i + 1) // 2


def _score_mask():
    rw = lax.broadcasted_iota(jnp.int32, (CHUNK, A_COLS), 0)
    cw = lax.broadcasted_iota(jnp.int32, (CHUNK, A_COLS), 1)
    keep = None
    for i in range(N_SUB):
        diag0 = _piece_offset(i) + SUB * i
        in_rows = (rw >= SUB * i) & (rw < SUB * (i + 1))
        left = (cw >= _piece_offset(i)) & (cw < diag0)
        diag = (cw >= diag0) & (cw < diag0 + SUB) & (cw - diag0 <= rw - SUB * i)
        m = in_rows & (left | diag)
        keep = m if keep is None else keep | m
    return keep


def _exact_diag(slab, qb, kb, bb, lane0):
    lr = lax.broadcasted_iota(jnp.int32, slab.shape, 0)
    lc = lax.broadcasted_iota(jnp.int32, slab.shape, 1)
    for s in range(SUB):
        e = jnp.exp2(jnp.minimum(bb - bb[s:s + 1, :], 0.0))
        col = jnp.sum(qb * kb[s:s + 1, :] * e, axis=-1, keepdims=True)
        slab = jnp.where((lc == lane0 + s) & (lr >= s), col, slab)
    return slab


def _block_rows(rows):
    return jnp.concatenate([jnp.broadcast_to(r, (SUB, r.shape[-1])) for r in rows], axis=0)


def _chunk_factors(q_ref, k_ref, b_ref, rows):
    q = q_ref[rows, :].astype(F32)
    k = k_ref[rows, :].astype(F32)
    b = b_ref[rows, :]

    ends = [b[SUB * (j + 1) - 1:SUB * (j + 1), :] for j in range(N_SUB)]
    refs = [jnp.zeros_like(ends[0])] + ends[:-1]
    b_last = ends[-1]
    d = b - _block_rows(refs)
    q_blk = (q * jnp.exp2(d)).astype(BF16)
    k_diag = k * jnp.exp2(jnp.minimum(-d, EXP2_CLAMP))
    k_end = k * jnp.exp2(_block_rows(ends) - b)
    pieces = []
    for i in range(N_SUB):
        for j in range(i):
            kj = k_end[SUB * j:SUB * (j + 1)]
            if j < i - 1:
                kj = kj * jnp.exp2(refs[i] - ends[j])
            pieces.append(kj.astype(BF16))
        pieces.append(k_diag[SUB * i:SUB * (i + 1)].astype(BF16))
    k_all = jnp.concatenate(pieces, axis=0)
    q_dec = (q * jnp.exp2(b)).astype(BF16)
    k_dec = (k_end * _block_rows([jnp.exp2(b_last - e) for e in ends])).astype(BF16)
    return (q, k, b), q_blk, k_all, q_dec, k_dec, jnp.exp2(b_last)


def _fix_diag(a, q, k, b):
    blocks = []
    for i in range(N_SUB):
        rs = slice(SUB * i, SUB * (i + 1))
        diag0 = _piece_offset(i) + SUB * i
        tile = diag0 // 128
        slabs = [a[rs, 128 * t:min(128 * (t + 1), A_COLS)] for t in range(pl.cdiv(A_COLS, 128))]
        slabs[tile] = _exact_diag(slabs[tile], q[rs], k[rs], b[rs], diag0 % 128)
        blocks.append(jnp.concatenate(slabs, axis=1))
    return jnp.concatenate(blocks, axis=0)


def _stack_values(v):
    return jnp.concatenate([v[0:SUB * (i + 1)] for i in range(N_SUB)], axis=0)


def _hgrn_rec_kernel(*refs, tm, n_side):
    qkv_ref, b_ref = refs[:2]
    q_ref, k_ref, v_ref = (qkv_ref.at[0, 0, p] for p in range(3))
    b_ref = b_ref.at[0, 0]
    o_ref = refs[2 + n_side]
    st_ref, qd_scr, oi_scr, kv_scr, dl_scr = refs[3 + 2 * n_side:]
    _side_cast(refs[2:2 + n_side], refs[3 + n_side:3 + 2 * n_side])

    @pl.when(pl.program_id(2) == 0)
    def _():
        st_ref[...] = jnp.zeros_like(st_ref)

    n_chunks = tm // CHUNK

    def chunk_rows(c):
        return pl.ds(pl.multiple_of(c * CHUNK, CHUNK), CHUNK)

    def state_rows(c):
        return pl.ds(pl.multiple_of(c * HEAD_DIM, HEAD_DIM), HEAD_DIM)

    def decay_rows(c, n):
        return pl.ds(pl.multiple_of(c * 8, 8), n)

    keep = _score_mask()

    def fast():
        def intra_body(c, carry):
            ccs = [c * INTRA_UNROLL + u for u in range(INTRA_UNROLL)]
            fac = [_chunk_factors(q_ref, k_ref, b_ref, chunk_rows(cc)) for cc in ccs]
            scores = [jnp.where(keep, _dot_nt(f[1], f[2]), 0.0).astype(BF16) for f in fac]
            for cc, f in zip(ccs, fac):
                kv_scr[state_rows(cc), :] = _dot_tn(v_ref[chunk_rows(cc), :], f[4])
                qd_scr[chunk_rows(cc), :] = f[3]
                dl_scr[decay_rows(cc, 8), :] = jnp.broadcast_to(f[5], (8, HEAD_DIM))
            for cc, a in zip(ccs, scores):
                rows = chunk_rows(cc)
                oi_scr[rows, :] = _dot(a, _stack_values(v_ref[rows, :]))
            return carry

        lax.fori_loop(0, n_chunks // INTRA_UNROLL, intra_body, 0)

        def state_body(c, st):
            for u in range(STATE_UNROLL):
                cc = c * STATE_UNROLL + u
                rows = chunk_rows(cc)
                o = oi_scr[rows, :] + _dot_nt(qd_scr[rows, :], st.astype(BF16))
                o_ref[0, rows, :] = o.astype(o_ref.dtype)
                st = st * dl_scr[decay_rows(cc, 1), :] + kv_scr[state_rows(cc), :]
            return st

        st_ref[...] = lax.fori_loop(0, n_chunks // STATE_UNROLL, state_body, st_ref[...])

    def exact():
        def body(c, st):
            rows = chunk_rows(c)
            (q, k, b), q_blk, k_all, q_dec, k_dec, dl = _chunk_factors(q_ref, k_ref, b_ref, rows)
            a = _fix_diag(jnp.where(keep, _dot_nt(q_blk, k_all), 0.0), q, k, b).astype(BF16)
            v = v_ref[rows, :]
            o = _dot(a, _stack_values(v)) + _dot_nt(q_dec, st.astype(BF16))
            o_ref[0, rows, :] = o.astype(o_ref.dtype)
            return st * dl + _dot_tn(v, k_dec)

        st_ref[...] = lax.fori_loop(0, n_chunks, body, st_ref[...])

    clamp_may_bind = jnp.min(b_ref[...]) < -(EXP2_CLAMP - 1.0)
    pl.when(jnp.logical_not(clamp_may_bind))(fast)
    pl.when(clamp_may_bind)(exact)


def _hgrn_rec(qkv, b2, side_weights, *, tm=4096):
    B, H, T, _ = b2.shape
    tm = min(tm, T)
    assert T % tm == 0 and (tm // CHUNK) % INTRA_UNROLL == 0 and (tm // CHUNK) % STATE_UNROLL == 0
    grid = (B, H, T // tm)
    side_in, side_in_specs, side_out_specs, side_shapes = _side_cast_plan(side_weights, grid)
    outs = pl.pallas_call(
        functools.partial(_hgrn_rec_kernel, tm=tm, n_side=len(side_in)),
        grid=grid,
        in_specs=[pl.BlockSpec((1, 1, 3, tm, HEAD_DIM), lambda b, h, t: (b, h, 0, t, 0)),
                  pl.BlockSpec((1, 1, tm, HEAD_DIM), lambda b, h, t: (b, h, t, 0))] + side_in_specs,
        out_specs=[pl.BlockSpec((1, tm, HEAD_DIM), lambda b, h, t: (b, t, h))] + side_out_specs,
        out_shape=[jax.ShapeDtypeStruct((B, T, H * HEAD_DIM), BF16)] + side_shapes,
        scratch_shapes=[pltpu.VMEM((HEAD_DIM, HEAD_DIM), F32),
                        pltpu.VMEM((tm, HEAD_DIM), BF16),
                        pltpu.VMEM((tm, HEAD_DIM), F32),
                        pltpu.VMEM((tm // CHUNK * HEAD_DIM, HEAD_DIM), F32),
                        pltpu.VMEM((tm // CHUNK * 8, HEAD_DIM), F32)],
        compiler_params=_params(("parallel", "parallel", "arbitrary")),
        name="hgrn_rec",
    )(qkv, b2, *side_in)
    return outs[0], [o.reshape(w.shape[1:]) for o, (w, _) in zip(outs[1:], side_weights)]


def _hgrn_out_kernel(o_ref, g_ref, h_ref, gain_ref, w_ref, out_ref):
    y = _rms_rows(o_ref[...].astype(F32), gain_ref[...]) * g_ref[...].astype(F32)
    out_ref[...] = h_ref[...] + _dot(y.astype(BF16), w_ref[...])


def _hgrn_out(o, gate, h, gain, w_out, *, tm=512):
    N, D = h.shape
    row_spec = pl.BlockSpec((tm, D), lambda i: (i, 0))
    return pl.pallas_call(
        _hgrn_out_kernel,
        grid=(N // tm,),
        in_specs=[row_spec, row_spec, row_spec,
                  pl.BlockSpec((1, D), lambda i: (0, 0)),
                  pl.BlockSpec((D, D), lambda i: (0, 0))],
        out_specs=row_spec,
        out_shape=jax.ShapeDtypeStruct((N, D), F32),
        compiler_params=_params(("parallel",)),
        name="hgrn_out",
    )(o, gate, h, gain, w_out)


TAIL = 8


def _conv3(prev, u, cw):
    rows = u.shape[0]
    ue = jnp.concatenate([prev, u], axis=0)
    return (cw[0:1, :] * ue[TAIL - 2:TAIL - 2 + rows]
            + cw[1:2, :] * ue[TAIL - 1:TAIL - 1 + rows]
            + cw[2:3, :] * u)


def _tile_step(h_hbm, h_buf, h_sem, nw_ref, xn_ref, out_ref, tails, tm, tiles_per_batch, row_parts,
               body):
    i = pl.program_id(0)
    j = pl.program_id(1)
    rp = tm // row_parts

    def tile_copy(t):
        rows = pl.ds(pl.multiple_of(t * tm, tm), tm)
        return pltpu.make_async_copy(h_hbm.at[rows, :], h_buf, h_sem)

    @pl.when((i == 0) & (j == 0))
    def _():
        tile_copy(i).start()

    @pl.when((j == 1) & (i + 1 < pl.num_programs(0)))
    def _():
        tile_copy(i + 1).start()

    @pl.when(i % tiles_per_batch == 0)
    def _():
        for t in tails:
            t[j] = jnp.zeros(t.shape[1:], t.dtype)

    @pl.when(j == 0)
    def _():
        tile_copy(i).wait()
        xns = []
        for p in range(row_parts):
            rs = slice(p * rp, (p + 1) * rp)
            hp = h_buf[rs, :]
            xns.append(_rms_rows(hp, nw_ref[...]).astype(BF16))
            xn_ref[rs, :] = xns[-1]
            out_ref[rs, :] = hp
        body(xns)

    @pl.when(j > 0)
    def _():
        body([xn_ref[p * rp:(p + 1) * rp, :] for p in range(row_parts)])


def _tile_scratch(tm, d):
    return [pltpu.VMEM((tm, d), BF16), pltpu.VMEM((tm, d), F32), pltpu.SemaphoreType.DMA(())]


def _ffn_kernel(*refs, tm, tiles_per_batch, final_norm, row_parts, n_side):
    h_hbm, nw_ref, wg_ref, wv_ref, cwg_ref, cwv_ref, wd_ref, fw_ref = refs[:8]
    out_ref = refs[8 + n_side]
    xn_ref, h_buf, h_sem, tg_ref, tv_ref = refs[9 + 2 * n_side:]
    _side_cast(refs[8:8 + n_side], refs[9 + n_side:9 + 2 * n_side])
    j = pl.program_id(1)
    rp = tm // row_parts

    def body(xns):
        ups = [(_dot(xn, wg_ref[...]), _dot(xn, wv_ref[...])) for xn in xns]
        prev_g, prev_v = tg_ref[j], tv_ref[j]
        acts = []
        for ug, uv in ups:
            cg = _conv3(prev_g, ug, cwg_ref[...])
            cv = _conv3(prev_v, uv, cwv_ref[...])
            prev_g, prev_v = ug[rp - TAIL:], uv[rp - TAIL:]
            acts.append((cg * _sigmoid(cg) * cv).astype(BF16))
        tg_ref[j] = prev_g
        tv_ref[j] = prev_v
        for p, act in enumerate(acts):
            out_ref[p * rp:(p + 1) * rp, :] += _dot(act, wd_ref[...])

    _tile_step(h_hbm, h_buf, h_sem, nw_ref, xn_ref, out_ref, (tg_ref, tv_ref), tm, tiles_per_batch,
               row_parts, body)

    if final_norm:
        @pl.when(j == pl.num_programs(1) - 1)
        def _():
            out_ref[...] = _rms_rows(out_ref[...], fw_ref[...])


def _ffn(h, seq_len, norm_w, w_up, conv_w, w_down, final_w, side_weights, *, final_norm, tm=1024,
         tn=512, row_parts=ROW_PARTS, vmem_limit=VMEM_LIMIT):
    N, D = h.shape
    F = w_down.shape[0]
    nf = F // tn
    assert nf >= 2
    grid = (N // tm, nf)
    side_in, side_in_specs, side_out_specs, side_shapes = _side_cast_plan(side_weights, grid)
    outs = pl.pallas_call(
        functools.partial(_ffn_kernel, tm=tm, tiles_per_batch=seq_len // tm,
                          final_norm=final_norm, row_parts=row_parts, n_side=len(side_in)),
        grid=grid,
        in_specs=[
            pl.BlockSpec(memory_space=pl.ANY),
            pl.BlockSpec((1, D), lambda i, j: (0, 0)),
            pl.BlockSpec((D, tn), lambda i, j: (0, j)),
            pl.BlockSpec((D, tn), lambda i, j: (0, nf + j)),
            pl.BlockSpec((3, tn), lambda i, j: (0, j)),
            pl.BlockSpec((3, tn), lambda i, j: (0, nf + j)),
            pl.BlockSpec((tn, D), lambda i, j: (j, 0)),
            pl.BlockSpec((1, D), lambda i, j: (0, 0)),
        ] + side_in_specs,
        out_specs=[pl.BlockSpec((tm, D), lambda i, j: (i, 0))] + side_out_specs,
        out_shape=[jax.ShapeDtypeStruct((N, D), F32)] + side_shapes,
        scratch_shapes=_tile_scratch(tm, D) + [pltpu.VMEM((nf, TAIL, tn), F32)] * 2,
        compiler_params=_params(("arbitrary", "arbitrary"), vmem_limit),
        name="ffn_final" if final_norm else "ffn",
    )(h, norm_w, *([_bf16(w_up)] * 2), conv_w, conv_w, _bf16(w_down), final_w, *side_in)
    return outs[0], [o.reshape(w.shape[1:]) for o, (w, _) in zip(outs[1:], side_weights)]


def _sc_kernel(h_hbm, nw_ref, wb_ref, wc_ref, wh_ref, cw_ref, wd_ref,
               out_ref, xn_ref, h_buf, h_sem, tz_ref, *, tm, tiles_per_batch, row_parts):
    j = pl.program_id(1)
    rp = tm // row_parts

    def body(xns):
        ups = [(_dot(xn, wc_ref[...]), _dot(xn, wh_ref[...]), _dot(xn, wb_ref[...])) for xn in xns]
        prev_z = tz_ref[j]
        ys = []
        for uc, uh, gb in ups:
            z = uc * uh
            ys.append((gb * _conv3(prev_z, z, cw_ref[...])).astype(BF16))
            prev_z = z[rp - TAIL:]
        tz_ref[j] = prev_z
        for p, y in enumerate(ys):
            out_ref[p * rp:(p + 1) * rp, :] += _dot(y, wd_ref[...])

    _tile_step(h_hbm, h_buf, h_sem, nw_ref, xn_ref, out_ref, (tz_ref,), tm, tiles_per_batch,
               row_parts, body)


def _short_conv(h, seq_len, norm_w, w_in, conv_w, w_out, *, tm=1024, tn=512, row_parts=ROW_PARTS,
                vmem_limit=60 * 1024 * 1024):
    N, D = h.shape
    nd = D // tn
    assert nd >= 2
    w_spec = lambda part: pl.BlockSpec((D, tn), lambda i, j: (0, part * nd + j))
    return pl.pallas_call(
        functools.partial(_sc_kernel, tm=tm, tiles_per_batch=seq_len // tm, row_parts=row_parts),
        grid=(N // tm, nd),
        in_specs=[
            pl.BlockSpec(memory_space=pl.ANY),
            pl.BlockSpec((1, D), lambda i, j: (0, 0)),
            w_spec(0), w_spec(1), w_spec(2),
            pl.BlockSpec((3, tn), lambda i, j: (0, j)),
            pl.BlockSpec((tn, D), lambda i, j: (j, 0)),
        ],
        out_specs=pl.BlockSpec((tm, D), lambda i, j: (i, 0)),
        out_shape=jax.ShapeDtypeStruct((N, D), F32),
        scratch_shapes=_tile_scratch(tm, D) + [pltpu.VMEM((nd, TAIL, tn), F32)],
        compiler_params=_params(("arbitrary", "arbitrary"), vmem_limit),
        name="short_conv",
    )(h, norm_w, *([_bf16(w_in)] * 3), conv_w, _bf16(w_out))


def kernel(x, norm_mix, norm_ffn, hgrn_w_in, hgrn_lb_table, hgrn_out_norm, hgrn_w_out,
           sc_w_in, sc_conv, sc_w_out, ffn_w_up, ffn_conv, ffn_w_down, final_norm):
    B, T, D = x.shape
    depth = norm_mix.shape[0]
    n_mixers = 2
    row = lambda w: w.reshape(1, -1).astype(F32)
    stacks = dict(hgrn_w_in=hgrn_w_in, hgrn_w_out=hgrn_w_out, sc_w_in=sc_w_in, sc_w_out=sc_w_out,
                  ffn_w_up=ffn_w_up, ffn_w_down=ffn_w_down)

    def layer_weights(i):
        mixer = ("hgrn_w_in", "hgrn_w_out") if i % n_mixers == 0 else ("sc_w_in", "sc_w_out")
        return [(name, i // n_mixers) for name in mixer] + [("ffn_w_up", i), ("ffn_w_down", i)]

    cast = {}

    def weight(name, idx):
        return cast.get((name, idx), stacks[name][idx])

    def side(keys):
        return [(stacks[name], idx) for name, idx in keys if (name, idx) not in cast]

    h = x.reshape(B * T, D)
    for i in range(depth):
        j = i // n_mixers
        if i % n_mixers == 0:
            rec_side = []
            if i == 0:
                rec_side = layer_weights(0)[1:] + [kw for l in range(1, depth) for kw in layer_weights(l)[:3]]
            xn = _norm_cast(h, row(norm_mix[i])).reshape(B, T, D)
            (qkv, b2, gate), _ = _hgrn_proj(xn, hgrn_lb_table.astype(F32), weight("hgrn_w_in", j), [],
                                            layer=i)
            o, done = _hgrn_rec(qkv, b2, side(rec_side))
            cast.update(zip(rec_side, done))
            h = _hgrn_out(o.reshape(B * T, D), gate, h, row(hgrn_out_norm[j]),
                          weight("hgrn_w_out", j).astype(BF16))
        else:
            h = _short_conv(h, T, row(norm_mix[i]), weight("sc_w_in", j), sc_conv[j].astype(F32),
                            weight("sc_w_out", j))
        ffn_side = [kw for kw in layer_weights(i + 1)[3:] if kw not in cast] if i + 1 < depth else []
        h, done = _ffn(h, T, row(norm_ffn[i]), weight("ffn_w_up", i), ffn_conv[i].astype(F32),
                       weight("ffn_w_down", i), row(final_norm), side(ffn_side),
                       final_norm=(i == depth - 1))
        cast.update(zip(ffn_side, done))
    return h.reshape(B, T, D)
```

```python
import functools
import math

import jax
import jax.numpy as jnp
from jax import lax
from jax.experimental import pallas as pl
from jax.experimental.pallas import tpu as pltpu

F32 = jnp.float32
BF16 = jnp.bfloat16

EPS = 1e-6
HEAD_DIM = 128
CHUNK = 64
SUB = 16
N_SUB = CHUNK // SUB
N_PIECES = N_SUB * (N_SUB + 1) // 2
A_COLS = SUB * N_PIECES
EXP2_CLAMP = 112.0
INTRA_UNROLL = 16
STATE_UNROLL = 16
ROW_PARTS = 2
VMEM_LIMIT = 56 * 1024 * 1024


def _dot(a, b):
    return jnp.dot(a, b, preferred_element_type=F32)


def _dot_nt(a, b):
    return lax.dot_general(a, b, (((1,), (1,)), ((), ())), preferred_element_type=F32)


def _dot_tn(a, b):
    return lax.dot_general(a, b, (((0,), (0,)), ((), ())), preferred_element_type=F32)


def _sigmoid(x):
    return 1.0 / (1.0 + jnp.exp(-x))


def _rms_rows(x, w):
    ms = jnp.mean(x * x, axis=-1, keepdims=True)
    return x * lax.rsqrt(ms + EPS) * w


def _params(semantics, vmem_limit=VMEM_LIMIT):
    return pltpu.CompilerParams(dimension_semantics=semantics, vmem_limit_bytes=vmem_limit)


def _bf16(w):
    return w.astype(BF16)


def _side_cast_plan(weights, grid):
    n_steps = math.prod(grid)
    strides = [math.prod(grid[d + 1:]) for d in range(len(grid))]

    def chunk(ids, n_chunks):
        return jnp.minimum(sum(i * s for i, s in zip(ids, strides)), n_chunks - 1)

    ins, in_specs, out_specs, shapes = [], [], [], []
    for stack, idx in weights:
        n, rows, cols = stack.shape
        nc = max(c for c in range(1, n_steps + 1) if rows % c == 0)
        block = (None, rows // nc, cols)
        ins.append(stack.reshape(n * nc, rows // nc, cols))
        in_specs.append(pl.BlockSpec(block, lambda *ids, base=idx * nc, nc=nc: (base + chunk(ids, nc), 0, 0)))
        out_specs.append(pl.BlockSpec(block, lambda *ids, nc=nc: (chunk(ids, nc), 0, 0)))
        shapes.append(jax.ShapeDtypeStruct((nc, rows // nc, cols), BF16))
    return ins, in_specs, out_specs, shapes


def _side_cast(src_refs, dst_refs):
    for src, dst in zip(src_refs, dst_refs):
        dst[...] = src[...].astype(BF16)


def _chunk_cumsum(x):
    n, d = x.shape
    pos = lax.broadcasted_iota(jnp.int32, (n, d), 0) & (CHUNK - 1)
    s = 1
    while s < CHUNK:
        pad = max(8, s)
        xp = jnp.concatenate([jnp.zeros((pad, d), x.dtype), x], axis=0)
        x = x + jnp.where(pos >= s, xp[pad - s:pad - s + n], 0.0)
        s *= 2
    return x


def _norm_cast_kernel(x_ref, w_ref, o_ref):
    o_ref[...] = _rms_rows(x_ref[...], w_ref[...]).astype(BF16)


def _norm_cast(x, norm_w, *, tm=1024):
    N, D = x.shape
    return pl.pallas_call(
        _norm_cast_kernel,
        grid=(N // tm,),
        in_specs=[pl.BlockSpec((tm, D), lambda i: (i, 0)), pl.BlockSpec((1, D), lambda i: (0, 0))],
        out_specs=pl.BlockSpec((tm, D), lambda i: (i, 0)),
        out_shape=jax.ShapeDtypeStruct((N, D), BF16),
        compiler_params=_params(("parallel",)),
        name="norm_cast",
    )(x, norm_w)


def _hgrn_proj_kernel(*refs, layer, heads_per_step, n_side):
    xn_ref, lbt_ref = refs[:2]
    w_refs = refs[2:6]
    q_ref, k_ref, b_ref, bmin_ref, v_ref, g_ref = refs[6 + n_side:12 + n_side]
    wb_ref = refs[-1]
    _side_cast(refs[6:6 + n_side], refs[12 + n_side:12 + 2 * n_side])

    @pl.when((pl.program_id(1) == 0) & (pl.program_id(2) == 0))
    def _():
        for p, w_ref in enumerate(w_refs):
            wb_ref[p] = w_ref[...].astype(BF16)

    rows = [lbt_ref[l:l + 1, :] for l in range(lbt_ref.shape[0])]
    m = functools.reduce(jnp.maximum, rows)
    es = [jnp.exp(r - m) for r in rows]
    lb = sum(es[:layer + 1]) / sum(es)

    rp = xn_ref.shape[0] // ROW_PARTS
    bmins = [None] * heads_per_step
    for p in range(ROW_PARTS):
        rs = slice(p * rp, (p + 1) * rp)
        xn = xn_ref[rs, :]
        q = _dot(xn, wb_ref[0])
        q = q * _sigmoid(q) * (HEAD_DIM ** -0.5)
        f = lb + (1.0 - lb) * _sigmoid(_dot(xn, wb_ref[1]))
        b2 = _chunk_cumsum(jnp.log2(f))
        k = 1.0 - f
        v = _dot(xn, wb_ref[2])
        g = _dot(xn, wb_ref[3])
        g_ref[rs, :] = (g * _sigmoid(g)).astype(BF16)
        for hh in range(heads_per_step):
            sl = slice(hh * HEAD_DIM, (hh + 1) * HEAD_DIM)
            q_ref[0, hh, rs, :] = q[:, sl].astype(BF16)
            k_ref[0, hh, rs, :] = k[:, sl].astype(BF16)
            b_ref[0, hh, rs, :] = b2[:, sl]
            v_ref[0, hh, rs, :] = v[:, sl].astype(BF16)
            bm = jnp.min(b2[:, sl].reshape(-1, 8, HEAD_DIM), axis=0)
            bmins[hh] = bm if bmins[hh] is None else jnp.minimum(bmins[hh], bm)
    for hh in range(heads_per_step):
        bmin_ref[0, hh, 0] = bmins[hh]


def _hgrn_proj(xn, lb_table, w_in, side_weights, *, layer, tm=1024, heads_per_step=2):
    B, T, D = xn.shape
    xn = xn.reshape(B * T, D)
    H = D // HEAD_DIM
    tn = heads_per_step * HEAD_DIM
    nq = D // tn
    grid = (H // heads_per_step, B, T // tm)
    head_shape = (B, H, T, HEAD_DIM)
    head_spec = pl.BlockSpec((1, heads_per_step, tm, HEAD_DIM), lambda j, b, i: (b, j, i, 0))
    w_spec = lambda part: pl.BlockSpec((D, tn), lambda j, b, i: (0, part * nq + j))
    side_in, side_in_specs, side_out_specs, side_shapes = _side_cast_plan(side_weights, grid)
    outs = pl.pallas_call(
        functools.partial(_hgrn_proj_kernel, layer=layer, heads_per_step=heads_per_step,
                          n_side=len(side_in)),
        grid=grid,
        in_specs=[
            pl.BlockSpec((tm, D), lambda j, b, i: (b * (T // tm) + i, 0)),
            pl.BlockSpec((lb_table.shape[0], tn), lambda j, b, i: (0, j)),
            w_spec(0), w_spec(1), w_spec(2), w_spec(3),
        ] + side_in_specs,
        out_specs=[
            head_spec, head_spec, head_spec,
            pl.BlockSpec((1, heads_per_step, 1, 8, HEAD_DIM), lambda j, b, i: (b, j, i, 0, 0)),
            head_spec,
            pl.BlockSpec((tm, tn), lambda j, b, i: (b * (T // tm) + i, j)),
        ] + side_out_specs,
        out_shape=[
            jax.ShapeDtypeStruct(head_shape, BF16),
            jax.ShapeDtypeStruct(head_shape, BF16),
            jax.ShapeDtypeStruct(head_shape, F32),
            jax.ShapeDtypeStruct((B, H, T // tm, 8, HEAD_DIM), F32),
            jax.ShapeDtypeStruct(head_shape, BF16),
            jax.ShapeDtypeStruct((B * T, D), BF16),
        ] + side_shapes,
        scratch_shapes=[pltpu.VMEM((4, D, tn), BF16)],
        compiler_params=_params(("arbitrary", "arbitrary", "arbitrary")),
        name="hgrn_proj",
    )(xn, lb_table, *([w_in] * 4), *side_in)
    return outs[:6], [o.reshape(w.shape[1:]) for o, (w, _) in zip(outs[6:], side_weights)]


def _piece_offset(i):
    return SUB * i * (i + 1) // 2


def _score_mask():
    rw = lax.broadcasted_iota(jnp.int32, (CHUNK, A_COLS), 0)
    cw = lax.broadcasted_iota(jnp.int32, (CHUNK, A_COLS), 1)
    keep = None
    for i in range(N_SUB):
        diag0 = _piece_offset(i) + SUB * i
        in_rows = (rw >= SUB * i) & (rw < SUB * (i + 1))
        left = (cw >= _piece_offset(i)) & (cw < diag0)
        diag = (cw >= diag0) & (cw < diag0 + SUB) & (cw - diag0 <= rw - SUB * i)
        m = in_rows & (left | diag)
        keep = m if keep is None else keep | m
    return keep


def _exact_diag(slab, qb, kb, bb, lane0):
    lr = lax.broadcasted_iota(jnp.int32, slab.shape, 0)
    lc = lax.broadcasted_iota(jnp.int32, slab.shape, 1)
    for s in range(SUB):
        e = jnp.exp2(jnp.minimum(bb - bb[s:s + 1, :], 0.0))
        col = jnp.sum(qb * kb[s:s + 1, :] * e, axis=-1, keepdims=True)
        slab = jnp.where((lc == lane0 + s) & (lr >= s), col, slab)
    return slab


def _block_rows(rows):
    return jnp.concatenate([jnp.broadcast_to(r, (SUB, r.shape[-1])) for r in rows], axis=0)


def _chunk_factors(q_ref, k_ref, b_ref, rows):
    q = q_ref[0, 0, rows, :].astype(F32)
    k = k_ref[0, 0, rows, :].astype(F32)
    b = b_ref[0, 0, rows, :]

    ends = [b[SUB * (j + 1) - 1:SUB * (j + 1), :] for j in range(N_SUB)]
    refs = [jnp.zeros_like(ends[0])] + ends[:-1]
    b_last = ends[-1]
    d = b - _block_rows(refs)
    q_blk = (q * jnp.exp2(d)).astype(BF16)
    k_diag = k * jnp.exp2(jnp.minimum(-d, EXP2_CLAMP))
    k_end = k * jnp.exp2(_block_rows(ends) - b)
    pieces = []
    for i in range(N_SUB):
        for j in range(i):
            kj = k_end[SUB * j:SUB * (j + 1)]
            if j < i - 1:
                kj = kj * jnp.exp2(refs[i] - ends[j])
            pieces.append(kj.astype(BF16))
        pieces.append(k_diag[SUB * i:SUB * (i + 1)].astype(BF16))
    k_all = jnp.concatenate(pieces, axis=0)
    q_dec = (q * jnp.exp2(b)).astype(BF16)
    k_dec = (k_end * _block_rows([jnp.exp2(b_last - e) for e in ends])).astype(BF16)
    return (q, k, b), q_blk, k_all, q_dec, k_dec, jnp.exp2(b_last)


def _fix_diag(a, q, k, b):
    blocks = []
    for i in range(N_SUB):
        rs = slice(SUB * i, SUB * (i + 1))
        diag0 = _piece_offset(i) + SUB * i
        tile = diag0 // 128
        slabs = [a[rs, 128 * t:min(128 * (t + 1), A_COLS)] for t in range(pl.cdiv(A_COLS, 128))]
        slabs[tile] = _exact_diag(slabs[tile], q[rs], k[rs], b[rs], diag0 % 128)
        blocks.append(jnp.concatenate(slabs, axis=1))
    return jnp.concatenate(blocks, axis=0)


def _stack_values(v):
    return jnp.concatenate([v[0:SUB * (i + 1)] for i in range(N_SUB)], axis=0)


def _hgrn_rec_kernel(*refs, tm, n_side):
    q_ref, k_ref, b_ref, bmin_ref, v_ref = refs[:5]
    o_ref = refs[5 + n_side]
    st_ref, qd_scr, oi_scr, kv_scr, dl_scr = refs[6 + 2 * n_side:]
    _side_cast(refs[5:5 + n_side], refs[6 + n_side:6 + 2 * n_side])

    @pl.when(pl.program_id(2) == 0)
    def _():
        st_ref[...] = jnp.zeros_like(st_ref)

    n_chunks = tm // CHUNK

    def chunk_rows(c):
        return pl.ds(pl.multiple_of(c * CHUNK, CHUNK), CHUNK)

    def state_rows(c):
        return pl.ds(pl.multiple_of(c * HEAD_DIM, HEAD_DIM), HEAD_DIM)

    def decay_rows(c, n):
        return pl.ds(pl.multiple_of(c * 8, 8), n)

    keep = _score_mask()

    def fast():
        def intra_body(c, carry):
            ccs = [c * INTRA_UNROLL + u for u in range(INTRA_UNROLL)]
            fac = [_chunk_factors(q_ref, k_ref, b_ref, chunk_rows(cc)) for cc in ccs]
            scores = [jnp.where(keep, _dot_nt(f[1], f[2]), 0.0).astype(BF16) for f in fac]
            for cc, f in zip(ccs, fac):
                kv_scr[state_rows(cc), :] = _dot_tn(v_ref[0, 0, chunk_rows(cc), :], f[4])
                qd_scr[chunk_rows(cc), :] = f[3]
                dl_scr[decay_rows(cc, 8), :] = jnp.broadcast_to(f[5], (8, HEAD_DIM))
            for cc, a in zip(ccs, scores):
                rows = chunk_rows(cc)
                oi_scr[rows, :] = _dot(a, _stack_values(v_ref[0, 0, rows, :]))
            return carry

        lax.fori_loop(0, n_chunks // INTRA_UNROLL, intra_body, 0)

        def state_body(c, st):
            for u in range(STATE_UNROLL):
                cc = c * STATE_UNROLL + u
                rows = chunk_rows(cc)
                o = oi_scr[rows, :] + _dot_nt(qd_scr[rows, :], st.astype(BF16))
                o_ref[0, rows, :] = o.astype(o_ref.dtype)
                st = st * dl_scr[decay_rows(cc, 1), :] + kv_scr[state_rows(cc), :]
            return st

        st_ref[...] = lax.fori_loop(0, n_chunks // STATE_UNROLL, state_body, st_ref[...])

    def exact():
        def body(c, st):
            rows = chunk_rows(c)
            (q, k, b), q_blk, k_all, q_dec, k_dec, dl = _chunk_factors(q_ref, k_ref, b_ref, rows)
            a = _fix_diag(jnp.where(keep, _dot_nt(q_blk, k_all), 0.0), q, k, b).astype(BF16)
            v = v_ref[0, 0, rows, :]
            o = _dot(a, _stack_values(v)) + _dot_nt(q_dec, st.astype(BF16))
            o_ref[0, rows, :] = o.astype(o_ref.dtype)
            return st * dl + _dot_tn(v, k_dec)

        st_ref[...] = lax.fori_loop(0, n_chunks, body, st_ref[...])

    clamp_may_bind = jnp.min(bmin_ref[0, 0]) < -(EXP2_CLAMP - 1.0)
    pl.when(jnp.logical_not(clamp_may_bind))(fast)
    pl.when(clamp_may_bind)(exact)


def _hgrn_rec(q, k, b2, bmin, v, side_weights, *, tm=4096):
    B, H, T, _ = q.shape
    tm = min(tm, T)
    assert T % tm == 0 and (tm // CHUNK) % INTRA_UNROLL == 0 and (tm // CHUNK) % STATE_UNROLL == 0
    n_min = bmin.shape[2] // (T // tm)
    spec = pl.BlockSpec((1, 1, tm, HEAD_DIM), lambda b, h, t: (b, h, t, 0))
    grid = (B, H, T // tm)
    side_in, side_in_specs, side_out_specs, side_shapes = _side_cast_plan(side_weights, grid)
    outs = pl.pallas_call(
        functools.partial(_hgrn_rec_kernel, tm=tm, n_side=len(side_in)),
        grid=grid,
        in_specs=[spec, spec, spec,
                  pl.BlockSpec((1, 1, n_min, 8, HEAD_DIM), lambda b, h, t: (b, h, t, 0, 0)),
                  spec] + side_in_specs,
        out_specs=[pl.BlockSpec((1, tm, HEAD_DIM), lambda b, h, t: (b, t, h))] + side_out_specs,
        out_shape=[jax.ShapeDtypeStruct((B, T, H * HEAD_DIM), BF16)] + side_shapes,
        scratch_shapes=[pltpu.VMEM((HEAD_DIM, HEAD_DIM), F32),
                        pltpu.VMEM((tm, HEAD_DIM), BF16),
                        pltpu.VMEM((tm, HEAD_DIM), F32),
                        pltpu.VMEM((tm // CHUNK * HEAD_DIM, HEAD_DIM), F32),
                        pltpu.VMEM((tm // CHUNK * 8, HEAD_DIM), F32)],
        compiler_params=_params(("parallel", "parallel", "arbitrary")),
        name="hgrn_rec",
    )(q, k, b2, bmin, v, *side_in)
    return outs[0], [o.reshape(w.shape[1:]) for o, (w, _) in zip(outs[1:], side_weights)]


def _hgrn_out_kernel(o_ref, g_ref, h_ref, gain_ref, w_ref, out_ref):
    y = _rms_rows(o_ref[...].astype(F32), gain_ref[...]) * g_ref[...].astype(F32)
    out_ref[...] = h_ref[...] + _dot(y.astype(BF16), w_ref[...])


def _hgrn_out(o, gate, h, gain, w_out, *, tm=512):
    N, D = h.shape
    row_spec = pl.BlockSpec((tm, D), lambda i: (i, 0))
    return pl.pallas_call(
        _hgrn_out_kernel,
        grid=(N // tm,),
        in_specs=[row_spec, row_spec, row_spec,
                  pl.BlockSpec((1, D), lambda i: (0, 0)),
                  pl.BlockSpec((D, D), lambda i: (0, 0))],
        out_specs=row_spec,
        out_shape=jax.ShapeDtypeStruct((N, D), F32),
        compiler_params=_params(("parallel",)),
        name="hgrn_out",
    )(o, gate, h, gain, w_out)


TAIL = 8


def _conv3(prev, u, cw):
    rows = u.shape[0]
    ue = jnp.concatenate([prev, u], axis=0)
    return (cw[0:1, :] * ue[TAIL - 2:TAIL - 2 + rows]
            + cw[1:2, :] * ue[TAIL - 1:TAIL - 1 + rows]
            + cw[2:3, :] * u)


def _tile_step(h_hbm, h_buf, h_sem, nw_ref, xn_ref, out_ref, tails, tm, tiles_per_batch, row_parts,
               body):
    i = pl.program_id(0)
    j = pl.program_id(1)
    rp = tm // row_parts

    def tile_copy(t):
        rows = pl.ds(pl.multiple_of(t * tm, tm), tm)
        return pltpu.make_async_copy(h_hbm.at[rows, :], h_buf, h_sem)

    @pl.when((i == 0) & (j == 0))
    def _():
        tile_copy(i).start()

    @pl.when((j == 1) & (i + 1 < pl.num_programs(0)))
    def _():
        tile_copy(i + 1).start()

    @pl.when(i % tiles_per_batch == 0)
    def _():
        for t in tails:
            t[j] = jnp.zeros(t.shape[1:], t.dtype)

    @pl.when(j == 0)
    def _():
        tile_copy(i).wait()
        xns = []
        for p in range(row_parts):
            rs = slice(p * rp, (p + 1) * rp)
            hp = h_buf[rs, :]
            xns.append(_rms_rows(hp, nw_ref[...]).astype(BF16))
            xn_ref[rs, :] = xns[-1]
            out_ref[rs, :] = hp
        body(xns)

    @pl.when(j > 0)
    def _():
        body([xn_ref[p * rp:(p + 1) * rp, :] for p in range(row_parts)])


def _tile_scratch(tm, d):
    return [pltpu.VMEM((tm, d), BF16), pltpu.VMEM((tm, d), F32), pltpu.SemaphoreType.DMA(())]


def _ffn_kernel(*refs, tm, tiles_per_batch, final_norm, row_parts, n_side):
    h_hbm, nw_ref, wg_ref, wv_ref, cwg_ref, cwv_ref, wd_ref, fw_ref = refs[:8]
    out_ref = refs[8 + n_side]
    xn_ref, h_buf, h_sem, tg_ref, tv_ref = refs[9 + 2 * n_side:]
    _side_cast(refs[8:8 + n_side], refs[9 + n_side:9 + 2 * n_side])
    j = pl.program_id(1)
    rp = tm // row_parts

    def body(xns):
        ups = [(_dot(xn, wg_ref[...]), _dot(xn, wv_ref[...])) for xn in xns]
        prev_g, prev_v = tg_ref[j], tv_ref[j]
        acts = []
        for ug, uv in ups:
            cg = _conv3(prev_g, ug, cwg_ref[...])
            cv = _conv3(prev_v, uv, cwv_ref[...])
            prev_g, prev_v = ug[rp - TAIL:], uv[rp - TAIL:]
            acts.append((cg * _sigmoid(cg) * cv).astype(BF16))
        tg_ref[j] = prev_g
        tv_ref[j] = prev_v
        for p, act in enumerate(acts):
            out_ref[p * rp:(p + 1) * rp, :] += _dot(act, wd_ref[...])

    _tile_step(h_hbm, h_buf, h_sem, nw_ref, xn_ref, out_ref, (tg_ref, tv_ref), tm, tiles_per_batch,
               row_parts, body)

    if final_norm:
        @pl.when(j == pl.num_programs(1) - 1)
        def _():
            out_ref[...] = _rms_rows(out_ref[...], fw_ref[...])


def _ffn(h, seq_len, norm_w, w_up, conv_w, w_down, final_w, side_weights, *, final_norm, tm=1024,
         tn=512, row_parts=ROW_PARTS, vmem_limit=VMEM_LIMIT):
    N, D = h.shape
    F = w_down.shape[0]
    nf = F // tn
    assert nf >= 2
    grid = (N // tm, nf)
    side_in, side_in_specs, side_out_specs, side_shapes = _side_cast_plan(side_weights, grid)
    outs = pl.pallas_call(
        functools.partial(_ffn_kernel, tm=tm, tiles_per_batch=seq_len // tm,
                          final_norm=final_norm, row_parts=row_parts, n_side=len(side_in)),
        grid=grid,
        in_specs=[
            pl.BlockSpec(memory_space=pl.ANY),
            pl.BlockSpec((1, D), lambda i, j: (0, 0)),
            pl.BlockSpec((D, tn), lambda i, j: (0, j)),
            pl.BlockSpec((D, tn), lambda i, j: (0, nf + j)),
            pl.BlockSpec((3, tn), lambda i, j: (0, j)),
            pl.BlockSpec((3, tn), lambda i, j: (0, nf + j)),
            pl.BlockSpec((tn, D), lambda i, j: (j, 0)),
            pl.BlockSpec((1, D), lambda i, j: (0, 0)),
        ] + side_in_specs,
        out_specs=[pl.BlockSpec((tm, D), lambda i, j: (i, 0))] + side_out_specs,
        out_shape=[jax.ShapeDtypeStruct((N, D), F32)] + side_shapes,
        scratch_shapes=_tile_scratch(tm, D) + [pltpu.VMEM((nf, TAIL, tn), F32)] * 2,
        compiler_params=_params(("arbitrary", "arbitrary"), vmem_limit),
        name="ffn_final" if final_norm else "ffn",
    )(h, norm_w, *([_bf16(w_up)] * 2), conv_w, conv_w, _bf16(w_down), final_w, *side_in)
    return outs[0], [o.reshape(w.shape[1:]) for o, (w, _) in zip(outs[1:], side_weights)]


def _sc_kernel(h_hbm, nw_ref, wb_ref, wc_ref, wh_ref, cw_ref, wd_ref,
               out_ref, xn_ref, h_buf, h_sem, tz_ref, *, tm, tiles_per_batch, row_parts):
    j = pl.program_id(1)
    rp = tm // row_parts

    def body(xns):
        ups = [(_dot(xn, wc_ref[...]), _dot(xn, wh_ref[...]), _dot(xn, wb_ref[...])) for xn in xns]
        prev_z = tz_ref[j]
        ys = []
        for uc, uh, gb in ups:
            z = uc * uh
            ys.append((gb * _conv3(prev_z, z, cw_ref[...])).astype(BF16))
            prev_z = z[rp - TAIL:]
        tz_ref[j] = prev_z
        for p, y in enumerate(ys):
            out_ref[p * rp:(p + 1) * rp, :] += _dot(y, wd_ref[...])

    _tile_step(h_hbm, h_buf, h_sem, nw_ref, xn_ref, out_ref, (tz_ref,), tm, tiles_per_batch,
               row_parts, body)


def _short_conv(h, seq_len, norm_w, w_in, conv_w, w_out, *, tm=1024, tn=512, row_parts=ROW_PARTS,
                vmem_limit=60 * 1024 * 1024):
    N, D = h.shape
    nd = D // tn
    assert nd >= 2
    w_spec = lambda part: pl.BlockSpec((D, tn), lambda i, j: (0, part * nd + j))
    return pl.pallas_call(
        functools.partial(_sc_kernel, tm=tm, tiles_per_batch=seq_len // tm, row_parts=row_parts),
        grid=(N // tm, nd),
        in_specs=[
            pl.BlockSpec(memory_space=pl.ANY),
            pl.BlockSpec((1, D), lambda i, j: (0, 0)),
            w_spec(0), w_spec(1), w_spec(2),
            pl.BlockSpec((3, tn), lambda i, j: (0, j)),
            pl.BlockSpec((tn, D), lambda i, j: (j, 0)),
        ],
        out_specs=pl.BlockSpec((tm, D), lambda i, j: (i, 0)),
        out_shape=jax.ShapeDtypeStruct((N, D), F32),
        scratch_shapes=_tile_scratch(tm, D) + [pltpu.VMEM((nd, TAIL, tn), F32)],
        compiler_params=_params(("arbitrary", "arbitrary"), vmem_limit),
        name="short_conv",
    )(h, norm_w, *([_bf16(w_in)] * 3), conv_w, _bf16(w_out))


def kernel(x, norm_mix, norm_ffn, hgrn_w_in, hgrn_lb_table, hgrn_out_norm, hgrn_w_out,
           sc_w_in, sc_conv, sc_w_out, ffn_w_up, ffn_conv, ffn_w_down, final_norm):
    B, T, D = x.shape
    depth = norm_mix.shape[0]
    n_mixers = 2
    row = lambda w: w.reshape(1, -1).astype(F32)
    stacks = dict(hgrn_w_in=hgrn_w_in, hgrn_w_out=hgrn_w_out, sc_w_in=sc_w_in, sc_w_out=sc_w_out,
                  ffn_w_up=ffn_w_up, ffn_w_down=ffn_w_down)

    def layer_weights(i):
        mixer = ("hgrn_w_in", "hgrn_w_out") if i % n_mixers == 0 else ("sc_w_in", "sc_w_out")
        return [(name, i // n_mixers) for name in mixer] + [("ffn_w_up", i), ("ffn_w_down", i)]

    cast = {}

    def weight(name, idx):
        return cast.get((name, idx), stacks[name][idx])

    def side(keys):
        return [(stacks[name], idx) for name, idx in keys]

    h = x.reshape(B * T, D)
    for i in range(depth):
        j = i // n_mixers
        if i % n_mixers == 0:
            proj_side = layer_weights(i)[1:] if i == 0 else []
            rec_side = [kw for l in range(1, depth) for kw in layer_weights(l)[:2]] if i == 0 else []
            xn = _norm_cast(h, row(norm_mix[i])).reshape(B, T, D)
            outs, done = _hgrn_proj(xn, hgrn_lb_table.astype(F32), weight("hgrn_w_in", j),
                                    side(proj_side), layer=i)
            cast.update(zip(proj_side, done))
            q, k, b2, bmin, v, gate = outs
            o, done = _hgrn_rec(q, k, b2, bmin, v, side(rec_side))
            cast.update(zip(rec_side, done))
            h = _hgrn_out(o.reshape(B * T, D), gate, h, row(hgrn_out_norm[j]),
                          weight("hgrn_w_out", j).astype(BF16))
        else:
            h = _short_conv(h, T, row(norm_mix[i]), weight("sc_w_in", j), sc_conv[j].astype(F32),
                            weight("sc_w_out", j))
        ffn_side = [kw for kw in layer_weights(i + 1)[2:] if kw not in cast] if i + 1 < depth else []
        h, done = _ffn(h, T, row(norm_ffn[i]), weight("ffn_w_up", i), ffn_conv[i].astype(F32),
                       weight("ffn_w_down", i), row(final_norm), side(ffn_side),
                       final_norm=(i == depth - 1))
        cast.update(zip(ffn_side, done))
    return h.reshape(B, T, D)
```

```python
import functools
import math

import jax
import jax.numpy as jnp
from jax import lax
from jax.experimental import pallas as pl
from jax.experimental.pallas import tpu as pltpu

F32 = jnp.float32
BF16 = jnp.bfloat16

EPS = 1e-6
HEAD_DIM = 128
CHUNK = 64
SUB = 16
N_SUB = CHUNK // SUB
N_PIECES = N_SUB * (N_SUB + 1) // 2
A_COLS = SUB * N_PIECES
EXP2_CLAMP = 112.0
REC_UNROLL = 64
ROW_PARTS = 2
VMEM_LIMIT = 56 * 1024 * 1024


def _dot(a, b):
    return jnp.dot(a, b, preferred_element_type=F32)


def _dot_nt(a, b):
    return lax.dot_general(a, b, (((1,), (1,)), ((), ())), preferred_element_type=F32)


def _dot_tn(a, b):
    return lax.dot_general(a, b, (((0,), (0,)), ((), ())), preferred_element_type=F32)


def _sigmoid(x):
    return 1.0 / (1.0 + jnp.exp(-x))


def _rms_rows(x, w):
    ms = jnp.mean(x * x, axis=-1, keepdims=True)
    return x * lax.rsqrt(ms + EPS) * w


def _params(semantics, vmem_limit=VMEM_LIMIT):
    return pltpu.CompilerParams(dimension_semantics=semantics, vmem_limit_bytes=vmem_limit)


def _bf16(w):
    return w.astype(BF16)


def _side_cast_plan(weights, grid):
    n_steps = math.prod(grid)
    strides = [math.prod(grid[d + 1:]) for d in range(len(grid))]

    def chunk(ids, n_chunks):
        return jnp.minimum(sum(i * s for i, s in zip(ids, strides)), n_chunks - 1)

    ins, in_specs, out_specs, shapes = [], [], [], []
    for stack, idx in weights:
        n, rows, cols = stack.shape
        nc = max(c for c in range(1, n_steps + 1) if rows % c == 0)
        block = (None, rows // nc, cols)
        ins.append(stack.reshape(n * nc, rows // nc, cols))
        in_specs.append(pl.BlockSpec(block, lambda *ids, base=idx * nc, nc=nc: (base + chunk(ids, nc), 0, 0)))
        out_specs.append(pl.BlockSpec(block, lambda *ids, nc=nc: (chunk(ids, nc), 0, 0)))
        shapes.append(jax.ShapeDtypeStruct((nc, rows // nc, cols), BF16))
    return ins, in_specs, out_specs, shapes


def _side_cast(src_refs, dst_refs):
    for src, dst in zip(src_refs, dst_refs):
        dst[...] = src[...].astype(BF16)


def _chunk_cumsum(x):
    n, d = x.shape
    pos = lax.broadcasted_iota(jnp.int32, (n, d), 0) & (CHUNK - 1)
    s = 1
    while s < CHUNK:
        pad = max(8, s)
        xp = jnp.concatenate([jnp.zeros((pad, d), x.dtype), x], axis=0)
        x = x + jnp.where(pos >= s, xp[pad - s:pad - s + n], 0.0)
        s *= 2
    return x


def _norm_cast_kernel(x_ref, w_ref, o_ref):
    o_ref[...] = _rms_rows(x_ref[...], w_ref[...]).astype(BF16)


def _norm_cast(x, norm_w, *, tm=1024):
    N, D = x.shape
    return pl.pallas_call(
        _norm_cast_kernel,
        grid=(N // tm,),
        in_specs=[pl.BlockSpec((tm, D), lambda i: (i, 0)), pl.BlockSpec((1, D), lambda i: (0, 0))],
        out_specs=pl.BlockSpec((tm, D), lambda i: (i, 0)),
        out_shape=jax.ShapeDtypeStruct((N, D), BF16),
        compiler_params=_params(("parallel",)),
        name="norm_cast",
    )(x, norm_w)


def _hgrn_proj_kernel(*refs, layer, heads_per_step, n_side):
    xn_ref, lbt_ref = refs[:2]
    w_refs = refs[2:6]
    q_ref, k_ref, b_ref, bmin_ref, v_ref, g_ref = refs[6 + n_side:12 + n_side]
    wb_ref = refs[-1]
    _side_cast(refs[6:6 + n_side], refs[12 + n_side:12 + 2 * n_side])

    @pl.when((pl.program_id(1) == 0) & (pl.program_id(2) == 0))
    def _():
        for p, w_ref in enumerate(w_refs):
            wb_ref[p] = w_ref[...].astype(BF16)

    rows = [lbt_ref[l:l + 1, :] for l in range(lbt_ref.shape[0])]
    m = functools.reduce(jnp.maximum, rows)
    es = [jnp.exp(r - m) for r in rows]
    lb = sum(es[:layer + 1]) / sum(es)

    rp = xn_ref.shape[0] // ROW_PARTS
    bmins = [None] * heads_per_step
    for p in range(ROW_PARTS):
        rs = slice(p * rp, (p + 1) * rp)
        xn = xn_ref[rs, :]
        q = _dot(xn, wb_ref[0])
        q = q * _sigmoid(q) * (HEAD_DIM ** -0.5)
        f = lb + (1.0 - lb) * _sigmoid(_dot(xn, wb_ref[1]))
        b2 = _chunk_cumsum(jnp.log2(f))
        k = 1.0 - f
        v = _dot(xn, wb_ref[2])
        g = _dot(xn, wb_ref[3])
        g_ref[rs, :] = (g * _sigmoid(g)).astype(BF16)
        for hh in range(heads_per_step):
            sl = slice(hh * HEAD_DIM, (hh + 1) * HEAD_DIM)
            q_ref[0, hh, rs, :] = q[:, sl].astype(BF16)
            k_ref[0, hh, rs, :] = k[:, sl].astype(BF16)
            b_ref[0, hh, rs, :] = b2[:, sl]
            v_ref[0, hh, rs, :] = v[:, sl].astype(BF16)
            bm = jnp.min(b2[:, sl].reshape(-1, 8, HEAD_DIM), axis=0)
            bmins[hh] = bm if bmins[hh] is None else jnp.minimum(bmins[hh], bm)
    for hh in range(heads_per_step):
        bmin_ref[0, hh, 0] = bmins[hh]


def _hgrn_proj(xn, lb_table, w_in, side_weights, *, layer, tm=1024, heads_per_step=2):
    B, T, D = xn.shape
    xn = xn.reshape(B * T, D)
    H = D // HEAD_DIM
    tn = heads_per_step * HEAD_DIM
    nq = D // tn
    grid = (H // heads_per_step, B, T // tm)
    head_shape = (B, H, T, HEAD_DIM)
    head_spec = pl.BlockSpec((1, heads_per_step, tm, HEAD_DIM), lambda j, b, i: (b, j, i, 0))
    w_spec = lambda part: pl.BlockSpec((D, tn), lambda j, b, i: (0, part * nq + j))
    side_in, side_in_specs, side_out_specs, side_shapes = _side_cast_plan(side_weights, grid)
    outs = pl.pallas_call(
        functools.partial(_hgrn_proj_kernel, layer=layer, heads_per_step=heads_per_step,
                          n_side=len(side_in)),
        grid=grid,
        in_specs=[
            pl.BlockSpec((tm, D), lambda j, b, i: (b * (T // tm) + i, 0)),
            pl.BlockSpec((lb_table.shape[0], tn), lambda j, b, i: (0, j)),
            w_spec(0), w_spec(1), w_spec(2), w_spec(3),
        ] + side_in_specs,
        out_specs=[
            head_spec, head_spec, head_spec,
            pl.BlockSpec((1, heads_per_step, 1, 8, HEAD_DIM), lambda j, b, i: (b, j, i, 0, 0)),
            head_spec,
            pl.BlockSpec((tm, tn), lambda j, b, i: (b * (T // tm) + i, j)),
        ] + side_out_specs,
        out_shape=[
            jax.ShapeDtypeStruct(head_shape, BF16),
            jax.ShapeDtypeStruct(head_shape, BF16),
            jax.ShapeDtypeStruct(head_shape, F32),
            jax.ShapeDtypeStruct((B, H, T // tm, 8, HEAD_DIM), F32),
            jax.ShapeDtypeStruct(head_shape, BF16),
            jax.ShapeDtypeStruct((B * T, D), BF16),
        ] + side_shapes,
        scratch_shapes=[pltpu.VMEM((4, D, tn), BF16)],
        compiler_params=_params(("arbitrary", "arbitrary", "arbitrary")),
        name="hgrn_proj",
    )(xn, lb_table, *([w_in] * 4), *side_in)
    return outs[:6], [o.reshape(w.shape[1:]) for o, (w, _) in zip(outs[6:], side_weights)]


def _piece_offset(i):
    return SUB * i * (i + 1) // 2


def _score_mask():
    rw = lax.broadcasted_iota(jnp.int32, (CHUNK, A_COLS), 0)
    cw = lax.broadcasted_iota(jnp.int32, (CHUNK, A_COLS), 1)
    keep = None
    for i in range(N_SUB):
        diag0 = _piece_offset(i) + SUB * i
        in_rows = (rw >= SUB * i) & (rw < SUB * (i + 1))
        left = (cw >= _piece_offset(i)) & (cw < diag0)
        diag = (cw >= diag0) & (cw < diag0 + SUB) & (cw - diag0 <= rw - SUB * i)
        m = in_rows & (left | diag)
        keep = m if keep is None else keep | m
    return keep


def _exact_diag(slab, qb, kb, bb, lane0):
    lr = lax.broadcasted_iota(jnp.int32, slab.shape, 0)
    lc = lax.broadcasted_iota(jnp.int32, slab.shape, 1)
    for s in range(SUB):
        e = jnp.exp2(jnp.minimum(bb - bb[s:s + 1, :], 0.0))
        col = jnp.sum(qb * kb[s:s + 1, :] * e, axis=-1, keepdims=True)
        slab = jnp.where((lc == lane0 + s) & (lr >= s), col, slab)
    return slab


def _block_rows(rows):
    return jnp.concatenate([jnp.broadcast_to(r, (SUB, r.shape[-1])) for r in rows], axis=0)


def _chunk_factors(q_ref, k_ref, b_ref, rows):
    q = q_ref[0, 0, rows, :].astype(F32)
    k = k_ref[0, 0, rows, :].astype(F32)
    b = b_ref[0, 0, rows, :]

    ends = [b[SUB * (j + 1) - 1:SUB * (j + 1), :] for j in range(N_SUB)]
    refs = [jnp.zeros_like(ends[0])] + ends[:-1]
    b_last = ends[-1]
    d = b - _block_rows(refs)
    q_blk = (q * jnp.exp2(d)).astype(BF16)
    k_diag = k * jnp.exp2(jnp.minimum(-d, EXP2_CLAMP))
    k_end = k * jnp.exp2(_block_rows(ends) - b)
    pieces = []
    for i in range(N_SUB):
        for j in range(i):
            kj = k_end[SUB * j:SUB * (j + 1)]
            if j < i - 1:
                kj = kj * jnp.exp2(refs[i] - ends[j])
            pieces.append(kj.astype(BF16))
        pieces.append(k_diag[SUB * i:SUB * (i + 1)].astype(BF16))
    k_all = jnp.concatenate(pieces, axis=0)
    q_dec = (q * jnp.exp2(b)).astype(BF16)
    k_dec = (k_end * _block_rows([jnp.exp2(b_last - e) for e in ends])).astype(BF16)
    return (q, k, b), q_blk, k_all, q_dec, k_dec, jnp.exp2(b_last)


def _fix_diag(a, q, k, b):
    blocks = []
    for i in range(N_SUB):
        rs = slice(SUB * i, SUB * (i + 1))
        diag0 = _piece_offset(i) + SUB * i
        tile = diag0 // 128
        slabs = [a[rs, 128 * t:min(128 * (t + 1), A_COLS)] for t in range(pl.cdiv(A_COLS, 128))]
        slabs[tile] = _exact_diag(slabs[tile], q[rs], k[rs], b[rs], diag0 % 128)
        blocks.append(jnp.concatenate(slabs, axis=1))
    return jnp.concatenate(blocks, axis=0)


def _stack_values(v):
    return jnp.concatenate([v[0:SUB * (i + 1)] for i in range(N_SUB)], axis=0)


def _hgrn_rec_kernel(*refs, tm, n_side):
    q_ref, k_ref, b_ref, bmin_ref, v_ref = refs[:5]
    o_ref = refs[5 + n_side]
    st_ref, qd_scr, oi_scr, kv_scr, dl_scr = refs[6 + 2 * n_side:]
    _side_cast(refs[5:5 + n_side], refs[6 + n_side:6 + 2 * n_side])

    @pl.when(pl.program_id(2) == 0)
    def _():
        st_ref[...] = jnp.zeros_like(st_ref)

    n_chunks = tm // CHUNK
    unroll = math.gcd(n_chunks, REC_UNROLL)

    def chunk_rows(c):
        return pl.ds(pl.multiple_of(c * CHUNK, CHUNK), CHUNK)

    def state_rows(c):
        return pl.ds(pl.multiple_of(c * HEAD_DIM, HEAD_DIM), HEAD_DIM)

    def decay_rows(c, n):
        return pl.ds(pl.multiple_of(c * 8, 8), n)

    keep = _score_mask()

    def fast():
        def intra_body(c, carry):
            ccs = [c * unroll + u for u in range(unroll)]
            fac = [_chunk_factors(q_ref, k_ref, b_ref, chunk_rows(cc)) for cc in ccs]
            scores = [jnp.where(keep, _dot_nt(f[1], f[2]), 0.0).astype(BF16) for f in fac]
            for cc, f in zip(ccs, fac):
                kv_scr[state_rows(cc), :] = _dot_tn(v_ref[0, 0, chunk_rows(cc), :], f[4])
                qd_scr[chunk_rows(cc), :] = f[3]
                dl_scr[decay_rows(cc, 8), :] = jnp.broadcast_to(f[5], (8, HEAD_DIM))
            for cc, a in zip(ccs, scores):
                rows = chunk_rows(cc)
                oi_scr[rows, :] = _dot(a, _stack_values(v_ref[0, 0, rows, :]))
            return carry

        lax.fori_loop(0, n_chunks // unroll, intra_body, 0)

        def state_body(c, st):
            for u in range(unroll):
                cc = c * unroll + u
                rows = chunk_rows(cc)
                o = oi_scr[rows, :] + _dot_nt(qd_scr[rows, :], st.astype(BF16))
                o_ref[0, rows, :] = o.astype(o_ref.dtype)
                st = st * dl_scr[decay_rows(cc, 1), :] + kv_scr[state_rows(cc), :]
            return st

        st_ref[...] = lax.fori_loop(0, n_chunks // unroll, state_body, st_ref[...])

    def exact():
        def body(c, st):
            rows = chunk_rows(c)
            (q, k, b), q_blk, k_all, q_dec, k_dec, dl = _chunk_factors(q_ref, k_ref, b_ref, rows)
            a = _fix_diag(jnp.where(keep, _dot_nt(q_blk, k_all), 0.0), q, k, b).astype(BF16)
            v = v_ref[0, 0, rows, :]
            o = _dot(a, _stack_values(v)) + _dot_nt(q_dec, st.astype(BF16))
            o_ref[0, rows, :] = o.astype(o_ref.dtype)
            return st * dl + _dot_tn(v, k_dec)

        st_ref[...] = lax.fori_loop(0, n_chunks, body, st_ref[...])

    clamp_may_bind = jnp.min(bmin_ref[0, 0]) < -(EXP2_CLAMP - 1.0)
    pl.when(jnp.logical_not(clamp_may_bind))(fast)
    pl.when(clamp_may_bind)(exact)


def _hgrn_rec(q, k, b2, bmin, v, side_weights, *, tm=4096):
    B, H, T, _ = q.shape
    tm = min(tm, T)
    assert T % tm == 0 and tm % CHUNK == 0
    n_min = bmin.shape[2] // (T // tm)
    spec = pl.BlockSpec((1, 1, tm, HEAD_DIM), lambda b, h, t: (b, h, t, 0))
    grid = (B, H, T // tm)
    side_in, side_in_specs, side_out_specs, side_shapes = _side_cast_plan(side_weights, grid)
    outs = pl.pallas_call(
        functools.partial(_hgrn_rec_kernel, tm=tm, n_side=len(side_in)),
        grid=grid,
        in_specs=[spec, spec, spec,
                  pl.BlockSpec((1, 1, n_min, 8, HEAD_DIM), lambda b, h, t: (b, h, t, 0, 0)),
                  spec] + side_in_specs,
        out_specs=[pl.BlockSpec((1, tm, HEAD_DIM), lambda b, h, t: (b, t, h))] + side_out_specs,
        out_shape=[jax.ShapeDtypeStruct((B, T, H * HEAD_DIM), BF16)] + side_shapes,
        scratch_shapes=[pltpu.VMEM((HEAD_DIM, HEAD_DIM), F32),
                        pltpu.VMEM((tm, HEAD_DIM), BF16),
                        pltpu.VMEM((tm, HEAD_DIM), F32),
                        pltpu.VMEM((tm // CHUNK * HEAD_DIM, HEAD_DIM), F32),
                        pltpu.VMEM((tm // CHUNK * 8, HEAD_DIM), F32)],
        compiler_params=_params(("parallel", "parallel", "arbitrary")),
        name="hgrn_rec",
    )(q, k, b2, bmin, v, *side_in)
    return outs[0], [o.reshape(w.shape[1:]) for o, (w, _) in zip(outs[1:], side_weights)]


def _hgrn_out_kernel(o_ref, g_ref, h_ref, gain_ref, w_ref, out_ref):
    y = _rms_rows(o_ref[...].astype(F32), gain_ref[...]) * g_ref[...].astype(F32)
    out_ref[...] = h_ref[...] + _dot(y.astype(BF16), w_ref[...])


def _hgrn_out(o, gate, h, gain, w_out, *, tm=512):
    N, D = h.shape
    row_spec = pl.BlockSpec((tm, D), lambda i: (i, 0))
    return pl.pallas_call(
        _hgrn_out_kernel,
        grid=(N // tm,),
        in_specs=[row_spec, row_spec, row_spec,
                  pl.BlockSpec((1, D), lambda i: (0, 0)),
                  pl.BlockSpec((D, D), lambda i: (0, 0))],
        out_specs=row_spec,
        out_shape=jax.ShapeDtypeStruct((N, D), F32),
        compiler_params=_params(("parallel",)),
        name="hgrn_out",
    )(o, gate, h, gain, w_out)


TAIL = 8


def _conv3(prev, u, cw):
    rows = u.shape[0]
    ue = jnp.concatenate([prev, u], axis=0)
    return (cw[0:1, :] * ue[TAIL - 2:TAIL - 2 + rows]
            + cw[1:2, :] * ue[TAIL - 1:TAIL - 1 + rows]
            + cw[2:3, :] * u)


def _tile_step(h_hbm, h_buf, h_sem, nw_ref, xn_ref, out_ref, tails, tm, tiles_per_batch, row_parts,
               body):
    i = pl.program_id(0)
    j = pl.program_id(1)
    rp = tm // row_parts

    def tile_copy(t):
        rows = pl.ds(pl.multiple_of(t * tm, tm), tm)
        return pltpu.make_async_copy(h_hbm.at[rows, :], h_buf, h_sem)

    @pl.when((i == 0) & (j == 0))
    def _():
        tile_copy(i).start()

    @pl.when((j == 1) & (i + 1 < pl.num_programs(0)))
    def _():
        tile_copy(i + 1).start()

    @pl.when(i % tiles_per_batch == 0)
    def _():
        for t in tails:
            t[j] = jnp.zeros(t.shape[1:], t.dtype)

    @pl.when(j == 0)
    def _():
        tile_copy(i).wait()
        xns = []
        for p in range(row_parts):
            rs = slice(p * rp, (p + 1) * rp)
            hp = h_buf[rs, :]
            xns.append(_rms_rows(hp, nw_ref[...]).astype(BF16))
            xn_ref[rs, :] = xns[-1]
            out_ref[rs, :] = hp
        body(xns)

    @pl.when(j > 0)
    def _():
        body([xn_ref[p * rp:(p + 1) * rp, :] for p in range(row_parts)])


def _tile_scratch(tm, d):
    return [pltpu.VMEM((tm, d), BF16), pltpu.VMEM((tm, d), F32), pltpu.SemaphoreType.DMA(())]


def _ffn_kernel(*refs, tm, tiles_per_batch, final_norm, row_parts, n_side):
    h_hbm, nw_ref, wg_ref, wv_ref, cwg_ref, cwv_ref, wd_ref, fw_ref = refs[:8]
    out_ref = refs[8 + n_side]
    xn_ref, h_buf, h_sem, tg_ref, tv_ref = refs[9 + 2 * n_side:]
    _side_cast(refs[8:8 + n_side], refs[9 + n_side:9 + 2 * n_side])
    j = pl.program_id(1)
    rp = tm // row_parts

    def body(xns):
        ups = [(_dot(xn, wg_ref[...]), _dot(xn, wv_ref[...])) for xn in xns]
        prev_g, prev_v = tg_ref[j], tv_ref[j]
        acts = []
        for ug, uv in ups:
            cg = _conv3(prev_g, ug, cwg_ref[...])
            cv = _conv3(prev_v, uv, cwv_ref[...])
            prev_g, prev_v = ug[rp - TAIL:], uv[rp - TAIL:]
            acts.append((cg * _sigmoid(cg) * cv).astype(BF16))
        tg_ref[j] = prev_g
        tv_ref[j] = prev_v
        for p, act in enumerate(acts):
            out_ref[p * rp:(p + 1) * rp, :] += _dot(act, wd_ref[...])

    _tile_step(h_hbm, h_buf, h_sem, nw_ref, xn_ref, out_ref, (tg_ref, tv_ref), tm, tiles_per_batch,
               row_parts, body)

    if final_norm:
        @pl.when(j == pl.num_programs(1) - 1)
        def _():
            out_ref[...] = _rms_rows(out_ref[...], fw_ref[...])


def _ffn(h, seq_len, norm_w, w_up, conv_w, w_down, final_w, side_weights, *, final_norm, tm=1024,
         tn=512, row_parts=ROW_PARTS, vmem_limit=VMEM_LIMIT):
    N, D = h.shape
    F = w_down.shape[0]
    nf = F // tn
    assert nf >= 2
    grid = (N // tm, nf)
    side_in, side_in_specs, side_out_specs, side_shapes = _side_cast_plan(side_weights, grid)
    outs = pl.pallas_call(
        functools.partial(_ffn_kernel, tm=tm, tiles_per_batch=seq_len // tm,
                          final_norm=final_norm, row_parts=row_parts, n_side=len(side_in)),
        grid=grid,
        in_specs=[
            pl.BlockSpec(memory_space=pl.ANY),
            pl.BlockSpec((1, D), lambda i, j: (0, 0)),
            pl.BlockSpec((D, tn), lambda i, j: (0, j)),
            pl.BlockSpec((D, tn), lambda i, j: (0, nf + j)),
            pl.BlockSpec((3, tn), lambda i, j: (0, j)),
            pl.BlockSpec((3, tn), lambda i, j: (0, nf + j)),
            pl.BlockSpec((tn, D), lambda i, j: (j, 0)),
            pl.BlockSpec((1, D), lambda i, j: (0, 0)),
        ] + side_in_specs,
        out_specs=[pl.BlockSpec((tm, D), lambda i, j: (i, 0))] + side_out_specs,
        out_shape=[jax.ShapeDtypeStruct((N, D), F32)] + side_shapes,
        scratch_shapes=_tile_scratch(tm, D) + [pltpu.VMEM((nf, TAIL, tn), F32)] * 2,
        compiler_params=_params(("arbitrary", "arbitrary"), vmem_limit),
        name="ffn_final" if final_norm else "ffn",
    )(h, norm_w, *([_bf16(w_up)] * 2), conv_w, conv_w, _bf16(w_down), final_w, *side_in)
    return outs[0], [o.reshape(w.shape[1:]) for o, (w, _) in zip(outs[1:], side_weights)]


def _sc_kernel(h_hbm, nw_ref, wb_ref, wc_ref, wh_ref, cw_ref, wd_ref,
               out_ref, xn_ref, h_buf, h_sem, tz_ref, *, tm, tiles_per_batch, row_parts):
    j = pl.program_id(1)
    rp = tm // row_parts

    def body(xns):
        ups = [(_dot(xn, wc_ref[...]), _dot(xn, wh_ref[...]), _dot(xn, wb_ref[...])) for xn in xns]
        prev_z = tz_ref[j]
        ys = []
        for uc, uh, gb in ups:
            z = uc * uh
            ys.append((gb * _conv3(prev_z, z, cw_ref[...])).astype(BF16))
            prev_z = z[rp - TAIL:]
        tz_ref[j] = prev_z
        for p, y in enumerate(ys):
            out_ref[p * rp:(p + 1) * rp, :] += _dot(y, wd_ref[...])

    _tile_step(h_hbm, h_buf, h_sem, nw_ref, xn_ref, out_ref, (tz_ref,), tm, tiles_per_batch,
               row_parts, body)


def _short_conv(h, seq_len, norm_w, w_in, conv_w, w_out, *, tm=1024, tn=512, row_parts=ROW_PARTS,
                vmem_limit=60 * 1024 * 1024):
    N, D = h.shape
    nd = D // tn
    assert nd >= 2
    w_spec = lambda part: pl.BlockSpec((D, tn), lambda i, j: (0, part * nd + j))
    return pl.pallas_call(
        functools.partial(_sc_kernel, tm=tm, tiles_per_batch=seq_len // tm, row_parts=row_parts),
        grid=(N // tm, nd),
        in_specs=[
            pl.BlockSpec(memory_space=pl.ANY),
            pl.BlockSpec((1, D), lambda i, j: (0, 0)),
            w_spec(0), w_spec(1), w_spec(2),
            pl.BlockSpec((3, tn), lambda i, j: (0, j)),
            pl.BlockSpec((tn, D), lambda i, j: (j, 0)),
        ],
        out_specs=pl.BlockSpec((tm, D), lambda i, j: (i, 0)),
        out_shape=jax.ShapeDtypeStruct((N, D), F32),
        scratch_shapes=_tile_scratch(tm, D) + [pltpu.VMEM((nd, TAIL, tn), F32)],
        compiler_params=_params(("arbitrary", "arbitrary"), vmem_limit),
        name="short_conv",
    )(h, norm_w, *([_bf16(w_in)] * 3), conv_w, _bf16(w_out))


def kernel(x, norm_mix, norm_ffn, hgrn_w_in, hgrn_lb_table, hgrn_out_norm, hgrn_w_out,
           sc_w_in, sc_conv, sc_w_out, ffn_w_up, ffn_conv, ffn_w_down, final_norm):
    B, T, D = x.shape
    depth = norm_mix.shape[0]
    n_mixers = 2
    row = lambda w: w.reshape(1, -1).astype(F32)
    stacks = dict(hgrn_w_in=hgrn_w_in, hgrn_w_out=hgrn_w_out, sc_w_in=sc_w_in, sc_w_out=sc_w_out,
                  ffn_w_up=ffn_w_up, ffn_w_down=ffn_w_down)

    def layer_weights(i):
        mixer = ("hgrn_w_in", "hgrn_w_out") if i % n_mixers == 0 else ("sc_w_in", "sc_w_out")
        return [(name, i // n_mixers) for name in mixer] + [("ffn_w_up", i), ("ffn_w_down", i)]

    cast = {}

    def weight(name, idx):
        return cast.get((name, idx), stacks[name][idx])

    def side(keys):
        return [(stacks[name], idx) for name, idx in keys]

    h = x.reshape(B * T, D)
    for i in range(depth):
        j = i // n_mixers
        if i % n_mixers == 0:
            proj_side = layer_weights(i)[1:] if i == 0 else []
            rec_side = [kw for l in range(1, depth) for kw in layer_weights(l)[:2]] if i == 0 else []
            xn = _norm_cast(h, row(norm_mix[i])).reshape(B, T, D)
            outs, done = _hgrn_proj(xn, hgrn_lb_table.astype(F32), weight("hgrn_w_in", j),
                                    side(proj_side), layer=i)
            cast.update(zip(proj_side, done))
            q, k, b2, bmin, v, gate = outs
            o, done = _hgrn_rec(q, k, b2, bmin, v, side(rec_side))
            cast.update(zip(rec_side, done))
            h = _hgrn_out(o.reshape(B * T, D), gate, h, row(hgrn_out_norm[j]),
                          weight("hgrn_w_out", j).astype(BF16))
        else:
            h = _short_conv(h, T, row(norm_mix[i]), weight("sc_w_in", j), sc_conv[j].astype(F32),
                            weight("sc_w_out", j))
        ffn_side = [kw for kw in layer_weights(i + 1)[2:] if kw not in cast] if i + 1 < depth else []
        h, done = _ffn(h, T, row(norm_ffn[i]), weight("ffn_w_up", i), ffn_conv[i].astype(F32),
                       weight("ffn_w_down", i), row(final_norm), side(ffn_side),
                       final_norm=(i == depth - 1))
        cast.update(zip(ffn_side, done))
    return h.reshape(B, T, D)
```

```python
import functools
import math

import jax
import jax.numpy as jnp
from jax import lax
from jax.experimental import pallas as pl
from jax.experimental.pallas import tpu as pltpu

F32 = jnp.float32
BF16 = jnp.bfloat16

EPS = 1e-6
HEAD_DIM = 128
CHUNK = 64
SUB = 16
N_SUB = CHUNK // SUB
N_PIECES = N_SUB * (N_SUB + 1) // 2
A_COLS = SUB * N_PIECES
EXP2_CLAMP = 112.0
REC_UNROLL = 64
ROW_PARTS = 2
VMEM_LIMIT = 56 * 1024 * 1024


def _dot(a, b):
    return jnp.dot(a, b, preferred_element_type=F32)


def _dot_nt(a, b):
    return lax.dot_general(a, b, (((1,), (1,)), ((), ())), preferred_element_type=F32)


def _dot_tn(a, b):
    return lax.dot_general(a, b, (((0,), (0,)), ((), ())), preferred_element_type=F32)


def _sigmoid(x):
    return 1.0 / (1.0 + jnp.exp(-x))


def _rms_rows(x, w):
    ms = jnp.mean(x * x, axis=-1, keepdims=True)
    return x * lax.rsqrt(ms + EPS) * w


def _params(semantics, vmem_limit=VMEM_LIMIT):
    return pltpu.CompilerParams(dimension_semantics=semantics, vmem_limit_bytes=vmem_limit)


def _bf16(w):
    return w.astype(BF16)


def _side_cast_plan(weights, grid):
    n_steps = math.prod(grid)
    strides = [math.prod(grid[d + 1:]) for d in range(len(grid))]

    def chunk(ids, n_chunks):
        return jnp.minimum(sum(i * s for i, s in zip(ids, strides)), n_chunks - 1)

    ins, in_specs, out_specs, shapes = [], [], [], []
    for stack, idx in weights:
        n, rows, cols = stack.shape
        nc = max(c for c in range(1, n_steps + 1) if rows % c == 0)
        block = (None, rows // nc, cols)
        ins.append(stack.reshape(n * nc, rows // nc, cols))
        in_specs.append(pl.BlockSpec(block, lambda *ids, base=idx * nc, nc=nc: (base + chunk(ids, nc), 0, 0)))
        out_specs.append(pl.BlockSpec(block, lambda *ids, nc=nc: (chunk(ids, nc), 0, 0)))
        shapes.append(jax.ShapeDtypeStruct((nc, rows // nc, cols), BF16))
    return ins, in_specs, out_specs, shapes


def _side_cast(src_refs, dst_refs):
    for src, dst in zip(src_refs, dst_refs):
        dst[...] = src[...].astype(BF16)


def _chunk_cumsum(x):
    n, d = x.shape
    pos = lax.broadcasted_iota(jnp.int32, (n, d), 0) & (CHUNK - 1)
    s = 1
    while s < CHUNK:
        pad = max(8, s)
        xp = jnp.concatenate([jnp.zeros((pad, d), x.dtype), x], axis=0)
        x = x + jnp.where(pos >= s, xp[pad - s:pad - s + n], 0.0)
        s *= 2
    return x


def _norm_cast_kernel(x_ref, w_ref, o_ref):
    o_ref[...] = _rms_rows(x_ref[...], w_ref[...]).astype(BF16)


def _norm_cast(x, norm_w, *, tm=1024):
    N, D = x.shape
    return pl.pallas_call(
        _norm_cast_kernel,
        grid=(N // tm,),
        in_specs=[pl.BlockSpec((tm, D), lambda i: (i, 0)), pl.BlockSpec((1, D), lambda i: (0, 0))],
        out_specs=pl.BlockSpec((tm, D), lambda i: (i, 0)),
        out_shape=jax.ShapeDtypeStruct((N, D), BF16),
        compiler_params=_params(("parallel",)),
        name="norm_cast",
    )(x, norm_w)


def _hgrn_proj_kernel(*refs, layer, heads_per_step, n_side):
    xn_ref, lbt_ref = refs[:2]
    w_refs = refs[2:6]
    q_ref, k_ref, b_ref, bmin_ref, v_ref, g_ref = refs[6 + n_side:12 + n_side]
    wb_ref = refs[-1]
    _side_cast(refs[6:6 + n_side], refs[12 + n_side:12 + 2 * n_side])

    @pl.when((pl.program_id(1) == 0) & (pl.program_id(2) == 0))
    def _():
        for p, w_ref in enumerate(w_refs):
            wb_ref[p] = w_ref[...].astype(BF16)

    rows = [lbt_ref[l:l + 1, :] for l in range(lbt_ref.shape[0])]
    m = functools.reduce(jnp.maximum, rows)
    es = [jnp.exp(r - m) for r in rows]
    lb = sum(es[:layer + 1]) / sum(es)

    rp = xn_ref.shape[0] // ROW_PARTS
    bmins = [None] * heads_per_step
    for p in range(ROW_PARTS):
        rs = slice(p * rp, (p + 1) * rp)
        xn = xn_ref[rs, :]
        q = _dot(xn, wb_ref[0])
        q = q * _sigmoid(q) * (HEAD_DIM ** -0.5)
        f = lb + (1.0 - lb) * _sigmoid(_dot(xn, wb_ref[1]))
        b2 = _chunk_cumsum(jnp.log2(f))
        k = 1.0 - f
        v = _dot(xn, wb_ref[2])
        g = _dot(xn, wb_ref[3])
        g_ref[rs, :] = (g * _sigmoid(g)).astype(BF16)
        for hh in range(heads_per_step):
            sl = slice(hh * HEAD_DIM, (hh + 1) * HEAD_DIM)
            q_ref[0, hh, rs, :] = q[:, sl].astype(BF16)
            k_ref[0, hh, rs, :] = k[:, sl].astype(BF16)
            b_ref[0, hh, rs, :] = b2[:, sl]
            v_ref[0, hh, rs, :] = v[:, sl].astype(BF16)
            bm = jnp.min(b2[:, sl].reshape(-1, 8, HEAD_DIM), axis=0)
            bmins[hh] = bm if bmins[hh] is None else jnp.minimum(bmins[hh], bm)
    for hh in range(heads_per_step):
        bmin_ref[0, hh, 0] = bmins[hh]


def _hgrn_proj(xn, lb_table, w_in, side_weights, *, layer, tm=1024, heads_per_step=2):
    B, T, D = xn.shape
    xn = xn.reshape(B * T, D)
    H = D // HEAD_DIM
    tn = heads_per_step * HEAD_DIM
    nq = D // tn
    grid = (H // heads_per_step, B, T // tm)
    head_shape = (B, H, T, HEAD_DIM)
    head_spec = pl.BlockSpec((1, heads_per_step, tm, HEAD_DIM), lambda j, b, i: (b, j, i, 0))
    w_spec = lambda part: pl.BlockSpec((D, tn), lambda j, b, i: (0, part * nq + j))
    side_in, side_in_specs, side_out_specs, side_shapes = _side_cast_plan(side_weights, grid)
    outs = pl.pallas_call(
        functools.partial(_hgrn_proj_kernel, layer=layer, heads_per_step=heads_per_step,
                          n_side=len(side_in)),
        grid=grid,
        in_specs=[
            pl.BlockSpec((tm, D), lambda j, b, i: (b * (T // tm) + i, 0)),
            pl.BlockSpec((lb_table.shape[0], tn), lambda j, b, i: (0, j)),
            w_spec(0), w_spec(1), w_spec(2), w_spec(3),
        ] + side_in_specs,
        out_specs=[
            head_spec, head_spec, head_spec,
            pl.BlockSpec((1, heads_per_step, 1, 8, HEAD_DIM), lambda j, b, i: (b, j, i, 0, 0)),
            head_spec,
            pl.BlockSpec((tm, tn), lambda j, b, i: (b * (T // tm) + i, j)),
        ] + side_out_specs,
        out_shape=[
            jax.ShapeDtypeStruct(head_shape, BF16),
            jax.ShapeDtypeStruct(head_shape, BF16),
            jax.ShapeDtypeStruct(head_shape, F32),
            jax.ShapeDtypeStruct((B, H, T // tm, 8, HEAD_DIM), F32),
            jax.ShapeDtypeStruct(head_shape, BF16),
            jax.ShapeDtypeStruct((B * T, D), BF16),
        ] + side_shapes,
        scratch_shapes=[pltpu.VMEM((4, D, tn), BF16)],
        compiler_params=_params(("arbitrary", "arbitrary", "arbitrary")),
        name="hgrn_proj",
    )(xn, lb_table, *([w_in] * 4), *side_in)
    return outs[:6], [o.reshape(w.shape[1:]) for o, (w, _) in zip(outs[6:], side_weights)]


def _piece_offset(i):
    return SUB * i * (i + 1) // 2


def _score_mask():
    rw = lax.broadcasted_iota(jnp.int32, (CHUNK, A_COLS), 0)
    cw = lax.broadcasted_iota(jnp.int32, (CHUNK, A_COLS), 1)
    keep = None
    for i in range(N_SUB):
        diag0 = _piece_offset(i) + SUB * i
        in_rows = (rw >= SUB * i) & (rw < SUB * (i + 1))
        left = (cw >= _piece_offset(i)) & (cw < diag0)
        diag = (cw >= diag0) & (cw < diag0 + SUB) & (cw - diag0 <= rw - SUB * i)
        m = in_rows & (left | diag)
        keep = m if keep is None else keep | m
    return keep


def _exact_diag(slab, qb, kb, bb, lane0):
    lr = lax.broadcasted_iota(jnp.int32, slab.shape, 0)
    lc = lax.broadcasted_iota(jnp.int32, slab.shape, 1)
    for s in range(SUB):
        e = jnp.exp2(jnp.minimum(bb - bb[s:s + 1, :], 0.0))
        col = jnp.sum(qb * kb[s:s + 1, :] * e, axis=-1, keepdims=True)
        slab = jnp.where((lc == lane0 + s) & (lr >= s), col, slab)
    return slab


def _block_rows(rows):
    return jnp.concatenate([jnp.broadcast_to(r, (SUB, r.shape[-1])) for r in rows], axis=0)


def _chunk_factors(q_ref, k_ref, b_ref, rows):
    q = q_ref[0, 0, rows, :].astype(F32)
    k = k_ref[0, 0, rows, :].astype(F32)
    b = b_ref[0, 0, rows, :]

    ends = [b[SUB * (j + 1) - 1:SUB * (j + 1), :] for j in range(N_SUB)]
    refs = [jnp.zeros_like(ends[0])] + ends[:-1]
    b_last = ends[-1]
    d = b - _block_rows(refs)
    q_blk = (q * jnp.exp2(d)).astype(BF16)
    k_diag = k * jnp.exp2(jnp.minimum(-d, EXP2_CLAMP))
    k_end = k * jnp.exp2(_block_rows(ends) - b)
    pieces = []
    for i in range(N_SUB):
        for j in range(i):
            kj = k_end[SUB * j:SUB * (j + 1)]
            if j < i - 1:
                kj = kj * jnp.exp2(refs[i] - ends[j])
            pieces.append(kj.astype(BF16))
        pieces.append(k_diag[SUB * i:SUB * (i + 1)].astype(BF16))
    k_all = jnp.concatenate(pieces, axis=0)
    q_dec = (q * jnp.exp2(b)).astype(BF16)
    k_dec = (k_end * _block_rows([jnp.exp2(b_last - e) for e in ends])).astype(BF16)
    return (q, k, b), q_blk, k_all, q_dec, k_dec, jnp.exp2(b_last)


def _fix_diag(a, q, k, b):
    blocks = []
    for i in range(N_SUB):
        rs = slice(SUB * i, SUB * (i + 1))
        diag0 = _piece_offset(i) + SUB * i
        tile = diag0 // 128
        slabs = [a[rs, 128 * t:min(128 * (t + 1), A_COLS)] for t in range(pl.cdiv(A_COLS, 128))]
        slabs[tile] = _exact_diag(slabs[tile], q[rs], k[rs], b[rs], diag0 % 128)
        blocks.append(jnp.concatenate(slabs, axis=1))
    return jnp.concatenate(blocks, axis=0)


def _stack_values(v):
    return jnp.concatenate([v[0:SUB * (i + 1)] for i in range(N_SUB)], axis=0)


def _hgrn_rec_kernel(*refs, tm, n_side):
    q_ref, k_ref, b_ref, bmin_ref, v_ref = refs[:5]
    o_ref = refs[5 + n_side]
    st_ref, qd_scr, oi_scr, kv_scr, dl_scr = refs[6 + 2 * n_side:]
    side_cast = functools.partial(_side_cast, refs[5:5 + n_side], refs[6 + n_side:6 + 2 * n_side])

    @pl.when(pl.program_id(2) == 0)
    def _():
        st_ref[...] = jnp.zeros_like(st_ref)

    n_chunks = tm // CHUNK
    unroll = math.gcd(n_chunks, REC_UNROLL)

    def chunk_rows(c):
        return pl.ds(pl.multiple_of(c * CHUNK, CHUNK), CHUNK)

    def state_rows(c):
        return pl.ds(pl.multiple_of(c * HEAD_DIM, HEAD_DIM), HEAD_DIM)

    def decay_rows(c, n):
        return pl.ds(pl.multiple_of(c * 8, 8), n)

    keep = _score_mask()

    def fast():
        side_cast()

        def intra_body(c, carry):
            ccs = [c * unroll + u for u in range(unroll)]
            fac = [_chunk_factors(q_ref, k_ref, b_ref, chunk_rows(cc)) for cc in ccs]
            scores = [jnp.where(keep, _dot_nt(f[1], f[2]), 0.0).astype(BF16) for f in fac]
            for cc, f in zip(ccs, fac):
                kv_scr[state_rows(cc), :] = _dot_tn(v_ref[0, 0, chunk_rows(cc), :], f[4])
                qd_scr[chunk_rows(cc), :] = f[3]
                dl_scr[decay_rows(cc, 8), :] = jnp.broadcast_to(f[5], (8, HEAD_DIM))
            for cc, a in zip(ccs, scores):
                rows = chunk_rows(cc)
                oi_scr[rows, :] = _dot(a, _stack_values(v_ref[0, 0, rows, :]))
            return carry

        lax.fori_loop(0, n_chunks // unroll, intra_body, 0)

        def state_body(c, st):
            for u in range(unroll):
                cc = c * unroll + u
                rows = chunk_rows(cc)
                o = oi_scr[rows, :] + _dot_nt(qd_scr[rows, :], st.astype(BF16))
                o_ref[0, rows, :] = o.astype(o_ref.dtype)
                st = st * dl_scr[decay_rows(cc, 1), :] + kv_scr[state_rows(cc), :]
            return st

        st_ref[...] = lax.fori_loop(0, n_chunks // unroll, state_body, st_ref[...])

    def exact():
        side_cast()

        def body(c, st):
            rows = chunk_rows(c)
            (q, k, b), q_blk, k_all, q_dec, k_dec, dl = _chunk_factors(q_ref, k_ref, b_ref, rows)
            a = _fix_diag(jnp.where(keep, _dot_nt(q_blk, k_all), 0.0), q, k, b).astype(BF16)
            v = v_ref[0, 0, rows, :]
            o = _dot(a, _stack_values(v)) + _dot_nt(q_dec, st.astype(BF16))
            o_ref[0, rows, :] = o.astype(o_ref.dtype)
            return st * dl + _dot_tn(v, k_dec)

        st_ref[...] = lax.fori_loop(0, n_chunks, body, st_ref[...])

    clamp_may_bind = jnp.min(bmin_ref[0, 0]) < -(EXP2_CLAMP - 1.0)
    pl.when(jnp.logical_not(clamp_may_bind))(fast)
    pl.when(clamp_may_bind)(exact)


def _hgrn_rec(q, k, b2, bmin, v, side_weights, *, tm=4096):
    B, H, T, _ = q.shape
    tm = min(tm, T)
    assert T % tm == 0 and tm % CHUNK == 0
    n_min = bmin.shape[2] // (T // tm)
    spec = pl.BlockSpec((1, 1, tm, HEAD_DIM), lambda b, h, t: (b, h, t, 0))
    grid = (B, H, T // tm)
    side_in, side_in_specs, side_out_specs, side_shapes = _side_cast_plan(side_weights, grid)
    outs = pl.pallas_call(
        functools.partial(_hgrn_rec_kernel, tm=tm, n_side=len(side_in)),
        grid=grid,
        in_specs=[spec, spec, spec,
                  pl.BlockSpec((1, 1, n_min, 8, HEAD_DIM), lambda b, h, t: (b, h, t, 0, 0)),
                  spec] + side_in_specs,
        out_specs=[pl.BlockSpec((1, tm, HEAD_DIM), lambda b, h, t: (b, t, h))] + side_out_specs,
        out_shape=[jax.ShapeDtypeStruct((B, T, H * HEAD_DIM), BF16)] + side_shapes,
        scratch_shapes=[pltpu.VMEM((HEAD_DIM, HEAD_DIM), F32),
                        pltpu.VMEM((tm, HEAD_DIM), BF16),
                        pltpu.VMEM((tm, HEAD_DIM), F32),
                        pltpu.VMEM((tm // CHUNK * HEAD_DIM, HEAD_DIM), F32),
                        pltpu.VMEM((tm // CHUNK * 8, HEAD_DIM), F32)],
        compiler_params=_params(("parallel", "parallel", "arbitrary")),
        name="hgrn_rec",
    )(q, k, b2, bmin, v, *side_in)
    return outs[0], [o.reshape(w.shape[1:]) for o, (w, _) in zip(outs[1:], side_weights)]


def _hgrn_out_kernel(o_ref, g_ref, h_ref, gain_ref, w_ref, out_ref):
    y = _rms_rows(o_ref[...].astype(F32), gain_ref[...]) * g_ref[...].astype(F32)
    out_ref[...] = h_ref[...] + _dot(y.astype(BF16), w_ref[...])


def _hgrn_out(o, gate, h, gain, w_out, *, tm=512):
    N, D = h.shape
    row_spec = pl.BlockSpec((tm, D), lambda i: (i, 0))
    return pl.pallas_call(
        _hgrn_out_kernel,
        grid=(N // tm,),
        in_specs=[row_spec, row_spec, row_spec,
                  pl.BlockSpec((1, D), lambda i: (0, 0)),
                  pl.BlockSpec((D, D), lambda i: (0, 0))],
        out_specs=row_spec,
        out_shape=jax.ShapeDtypeStruct((N, D), F32),
        compiler_params=_params(("parallel",)),
        name="hgrn_out",
    )(o, gate, h, gain, w_out)


TAIL = 8


def _conv3(prev, u, cw):
    rows = u.shape[0]
    ue = jnp.concatenate([prev, u], axis=0)
    return (cw[0:1, :] * ue[TAIL - 2:TAIL - 2 + rows]
            + cw[1:2, :] * ue[TAIL - 1:TAIL - 1 + rows]
            + cw[2:3, :] * u)


def _tile_step(h_hbm, h_buf, h_sem, nw_ref, xn_ref, out_ref, tails, tm, tiles_per_batch, row_parts,
               body):
    i = pl.program_id(0)
    j = pl.program_id(1)
    rp = tm // row_parts

    def tile_copy(t):
        rows = pl.ds(pl.multiple_of(t * tm, tm), tm)
        return pltpu.make_async_copy(h_hbm.at[rows, :], h_buf, h_sem)

    @pl.when((i == 0) & (j == 0))
    def _():
        tile_copy(i).start()

    @pl.when((j == 1) & (i + 1 < pl.num_programs(0)))
    def _():
        tile_copy(i + 1).start()

    @pl.when(i % tiles_per_batch == 0)
    def _():
        for t in tails:
            t[j] = jnp.zeros(t.shape[1:], t.dtype)

    @pl.when(j == 0)
    def _():
        tile_copy(i).wait()
        xns = []
        for p in range(row_parts):
            rs = slice(p * rp, (p + 1) * rp)
            hp = h_buf[rs, :]
            xns.append(_rms_rows(hp, nw_ref[...]).astype(BF16))
            xn_ref[rs, :] = xns[-1]
            out_ref[rs, :] = hp
        body(xns)

    @pl.when(j > 0)
    def _():
        body([xn_ref[p * rp:(p + 1) * rp, :] for p in range(row_parts)])


def _tile_scratch(tm, d):
    return [pltpu.VMEM((tm, d), BF16), pltpu.VMEM((tm, d), F32), pltpu.SemaphoreType.DMA(())]


def _ffn_kernel(*refs, tm, tiles_per_batch, final_norm, row_parts, n_side):
    h_hbm, nw_ref, wg_ref, wv_ref, cwg_ref, cwv_ref, wd_ref, fw_ref = refs[:8]
    out_ref = refs[8 + n_side]
    xn_ref, h_buf, h_sem, tg_ref, tv_ref = refs[9 + 2 * n_side:]
    j = pl.program_id(1)
    rp = tm // row_parts

    def body(xns):
        _side_cast(refs[8:8 + n_side], refs[9 + n_side:9 + 2 * n_side])
        ups = [(_dot(xn, wg_ref[...]), _dot(xn, wv_ref[...])) for xn in xns]
        prev_g, prev_v = tg_ref[j], tv_ref[j]
        acts = []
        for ug, uv in ups:
            cg = _conv3(prev_g, ug, cwg_ref[...])
            cv = _conv3(prev_v, uv, cwv_ref[...])
            prev_g, prev_v = ug[rp - TAIL:], uv[rp - TAIL:]
            acts.append((cg * _sigmoid(cg) * cv).astype(BF16))
        tg_ref[j] = prev_g
        tv_ref[j] = prev_v
        for p, act in enumerate(acts):
            out_ref[p * rp:(p + 1) * rp, :] += _dot(act, wd_ref[...])

    _tile_step(h_hbm, h_buf, h_sem, nw_ref, xn_ref, out_ref, (tg_ref, tv_ref), tm, tiles_per_batch,
               row_parts, body)

    if final_norm:
        @pl.when(j == pl.num_programs(1) - 1)
        def _():
            out_ref[...] = _rms_rows(out_ref[...], fw_ref[...])


def _ffn(h, seq_len, norm_w, w_up, conv_w, w_down, final_w, side_weights, *, final_norm, tm=1024,
         tn=512, row_parts=ROW_PARTS, vmem_limit=VMEM_LIMIT):
    N, D = h.shape
    F = w_down.shape[0]
    nf = F // tn
    assert nf >= 2
    grid = (N // tm, nf)
    side_in, side_in_specs, side_out_specs, side_shapes = _side_cast_plan(side_weights, grid)
    outs = pl.pallas_call(
        functools.partial(_ffn_kernel, tm=tm, tiles_per_batch=seq_len // tm,
                          final_norm=final_norm, row_parts=row_parts, n_side=len(side_in)),
        grid=grid,
        in_specs=[
            pl.BlockSpec(memory_space=pl.ANY),
            pl.BlockSpec((1, D), lambda i, j: (0, 0)),
            pl.BlockSpec((D, tn), lambda i, j: (0, j)),
            pl.BlockSpec((D, tn), lambda i, j: (0, nf + j)),
            pl.BlockSpec((3, tn), lambda i, j: (0, j)),
            pl.BlockSpec((3, tn), lambda i, j: (0, nf + j)),
            pl.BlockSpec((tn, D), lambda i, j: (j, 0)),
            pl.BlockSpec((1, D), lambda i, j: (0, 0)),
        ] + side_in_specs,
        out_specs=[pl.BlockSpec((tm, D), lambda i, j: (i, 0))] + side_out_specs,
        out_shape=[jax.ShapeDtypeStruct((N, D), F32)] + side_shapes,
        scratch_shapes=_tile_scratch(tm, D) + [pltpu.VMEM((nf, TAIL, tn), F32)] * 2,
        compiler_params=_params(("arbitrary", "arbitrary"), vmem_limit),
        name="ffn_final" if final_norm else "ffn",
    )(h, norm_w, *([_bf16(w_up)] * 2), conv_w, conv_w, _bf16(w_down), final_w, *side_in)
    return outs[0], [o.reshape(w.shape[1:]) for o, (w, _) in zip(outs[1:], side_weights)]


def _sc_kernel(h_hbm, nw_ref, wb_ref, wc_ref, wh_ref, cw_ref, wd_ref,
               out_ref, xn_ref, h_buf, h_sem, tz_ref, *, tm, tiles_per_batch, row_parts):
    j = pl.program_id(1)
    rp = tm // row_parts

    def body(xns):
        ups = [(_dot(xn, wc_ref[...]), _dot(xn, wh_ref[...]), _dot(xn, wb_ref[...])) for xn in xns]
        prev_z = tz_ref[j]
        ys = []
        for uc, uh, gb in ups:
            z = uc * uh
            ys.append((gb * _conv3(prev_z, z, cw_ref[...])).astype(BF16))
            prev_z = z[rp - TAIL:]
        tz_ref[j] = prev_z
        for p, y in enumerate(ys):
            out_ref[p * rp:(p + 1) * rp, :] += _dot(y, wd_ref[...])

    _tile_step(h_hbm, h_buf, h_sem, nw_ref, xn_ref, out_ref, (tz_ref,), tm, tiles_per_batch,
               row_parts, body)


def _short_conv(h, seq_len, norm_w, w_in, conv_w, w_out, *, tm=1024, tn=512, row_parts=ROW_PARTS,
                vmem_limit=60 * 1024 * 1024):
    N, D = h.shape
    nd = D // tn
    assert nd >= 2
    w_spec = lambda part: pl.BlockSpec((D, tn), lambda i, j: (0, part * nd + j))
    return pl.pallas_call(
        functools.partial(_sc_kernel, tm=tm, tiles_per_batch=seq_len // tm, row_parts=row_parts),
        grid=(N // tm, nd),
        in_specs=[
            pl.BlockSpec(memory_space=pl.ANY),
            pl.BlockSpec((1, D), lambda i, j: (0, 0)),
            w_spec(0), w_spec(1), w_spec(2),
            pl.BlockSpec((3, tn), lambda i, j: (0, j)),
            pl.BlockSpec((tn, D), lambda i, j: (j, 0)),
        ],
        out_specs=pl.BlockSpec((tm, D), lambda i, j: (i, 0)),
        out_shape=jax.ShapeDtypeStruct((N, D), F32),
        scratch_shapes=_tile_scratch(tm, D) + [pltpu.VMEM((nd, TAIL, tn), F32)],
        compiler_params=_params(("arbitrary", "arbitrary"), vmem_limit),
        name="short_conv",
    )(h, norm_w, *([_bf16(w_in)] * 3), conv_w, _bf16(w_out))


def kernel(x, norm_mix, norm_ffn, hgrn_w_in, hgrn_lb_table, hgrn_out_norm, hgrn_w_out,
           sc_w_in, sc_conv, sc_w_out, ffn_w_up, ffn_conv, ffn_w_down, final_norm):
    B, T, D = x.shape
    depth = norm_mix.shape[0]
    n_mixers = 2
    row = lambda w: w.reshape(1, -1).astype(F32)
    stacks = dict(hgrn_w_in=hgrn_w_in, hgrn_w_out=hgrn_w_out, sc_w_in=sc_w_in, sc_w_out=sc_w_out,
                  ffn_w_up=ffn_w_up, ffn_w_down=ffn_w_down)

    def layer_weights(i):
        mixer = ("hgrn_w_in", "hgrn_w_out") if i % n_mixers == 0 else ("sc_w_in", "sc_w_out")
        return [(name, i // n_mixers) for name in mixer] + [("ffn_w_up", i), ("ffn_w_down", i)]

    cast = {}

    def weight(name, idx):
        return cast.get((name, idx), stacks[name][idx])

    def side(keys):
        return [(stacks[name], idx) for name, idx in keys]

    h = x.reshape(B * T, D)
    for i in range(depth):
        j = i // n_mixers
        if i % n_mixers == 0:
            proj_side = layer_weights(i)[1:] if i == 0 else []
            rec_side = [kw for l in range(1, depth) for kw in layer_weights(l)[:2]] if i == 0 else []
            xn = _norm_cast(h, row(norm_mix[i])).reshape(B, T, D)
            outs, done = _hgrn_proj(xn, hgrn_lb_table.astype(F32), weight("hgrn_w_in", j),
                                    side(proj_side), layer=i)
            cast.update(zip(proj_side, done))
            q, k, b2, bmin, v, gate = outs
            o, done = _hgrn_rec(q, k, b2, bmin, v, side(rec_side))
            cast.update(zip(rec_side, done))
            h = _hgrn_out(o.reshape(B * T, D), gate, h, row(hgrn_out_norm[j]),
                          weight("hgrn_w_out", j).astype(BF16))
        else:
            h = _short_conv(h, T, row(norm_mix[i]), weight("sc_w_in", j), sc_conv[j].astype(F32),
                            weight("sc_w_out", j))
        ffn_side = [kw for kw in layer_weights(i + 1)[2:] if kw not in cast] if i + 1 < depth else []
        h, done = _ffn(h, T, row(norm_ffn[i]), weight("ffn_w_up", i), ffn_conv[i].astype(F32),
                       weight("ffn_w_down", i), row(final_norm), side(ffn_side),
                       final_norm=(i == depth - 1))
        cast.update(zip(ffn_side, done))
    return h.reshape(B, T, D)
```

```python
import functools
import math

import jax
import jax.numpy as jnp
from jax import lax
from jax.experimental import pallas as pl
from jax.experimental.pallas import tpu as pltpu

F32 = jnp.float32
BF16 = jnp.bfloat16

EPS = 1e-6
HEAD_DIM = 128
CHUNK = 64
SUB = 16
N_SUB = CHUNK // SUB
N_PIECES = N_SUB * (N_SUB + 1) // 2
A_COLS = SUB * N_PIECES
EXP2_CLAMP = 112.0
REC_UNROLL = 64
ROW_PARTS = 2
VMEM_LIMIT = 56 * 1024 * 1024


def _dot(a, b):
    return jnp.dot(a, b, preferred_element_type=F32)


def _dot_nt(a, b):
    return lax.dot_general(a, b, (((1,), (1,)), ((), ())), preferred_element_type=F32)


def _dot_tn(a, b):
    return lax.dot_general(a, b, (((0,), (0,)), ((), ())), preferred_element_type=F32)


def _sigmoid(x):
    return 1.0 / (1.0 + jnp.exp(-x))


def _rms_rows(x, w):
    ms = jnp.mean(x * x, axis=-1, keepdims=True)
    return x * lax.rsqrt(ms + EPS) * w


def _params(semantics, vmem_limit=VMEM_LIMIT):
    return pltpu.CompilerParams(dimension_semantics=semantics, vmem_limit_bytes=vmem_limit)


def _bf16(w):
    return w.astype(BF16)


def _side_cast_plan(weights, grid):
    n_steps = math.prod(grid)
    strides = [math.prod(grid[d + 1:]) for d in range(len(grid))]

    def chunk(ids, n_chunks):
        return jnp.minimum(sum(i * s for i, s in zip(ids, strides)), n_chunks - 1)

    ins, in_specs, out_specs, shapes = [], [], [], []
    for stack, idx in weights:
        n, rows, cols = stack.shape
        nc = max(c for c in range(1, n_steps + 1) if rows % c == 0)
        block = (None, rows // nc, cols)
        ins.append(stack.reshape(n * nc, rows // nc, cols))
        in_specs.append(pl.BlockSpec(block, lambda *ids, base=idx * nc, nc=nc: (base + chunk(ids, nc), 0, 0)))
        out_specs.append(pl.BlockSpec(block, lambda *ids, nc=nc: (chunk(ids, nc), 0, 0)))
        shapes.append(jax.ShapeDtypeStruct((nc, rows // nc, cols), BF16))
    return ins, in_specs, out_specs, shapes


def _side_cast(src_refs, dst_refs):
    for src, dst in zip(src_refs, dst_refs):
        dst[...] = src[...].astype(BF16)


def _chunk_cumsum(x):
    n, d = x.shape
    pos = lax.broadcasted_iota(jnp.int32, (n, d), 0) & (CHUNK - 1)
    s = 1
    while s < CHUNK:
        pad = max(8, s)
        xp = jnp.concatenate([jnp.zeros((pad, d), x.dtype), x], axis=0)
        x = x + jnp.where(pos >= s, xp[pad - s:pad - s + n], 0.0)
        s *= 2
    return x


def _hgrn_proj_kernel(*refs, layer, heads_per_step, n_side, fuse_norm):
    refs = list(refs)
    wb_ref = refs.pop()
    src_ref = refs.pop(0)
    nw_ref = refs.pop(0) if fuse_norm else None
    lbt_ref = refs.pop(0)
    w_refs = [refs.pop(0) for _ in range(4)]
    side_in = [refs.pop(0) for _ in range(n_side)]
    side_out = [refs.pop() for _ in range(n_side)][::-1]
    xn_out_ref = refs.pop() if fuse_norm else None
    q_ref, k_ref, b_ref, bmin_ref, v_ref, g_ref = refs[-6:]
    _side_cast(side_in, side_out)

    @pl.when((pl.program_id(1) == 0) & (pl.program_id(2) == 0))
    def _():
        for p, w_ref in enumerate(w_refs):
            wb_ref[p] = w_ref[...].astype(BF16)

    rows = [lbt_ref[l:l + 1, :] for l in range(lbt_ref.shape[0])]
    m = functools.reduce(jnp.maximum, rows)
    es = [jnp.exp(r - m) for r in rows]
    lb = sum(es[:layer + 1]) / sum(es)

    rp = src_ref.shape[0] // ROW_PARTS
    bmins = [None] * heads_per_step
    for p in range(ROW_PARTS):
        rs = slice(p * rp, (p + 1) * rp)
        if fuse_norm:
            xn = _rms_rows(src_ref[rs, :], nw_ref[...]).astype(BF16)
            xn_out_ref[rs, :] = xn
        else:
            xn = src_ref[rs, :]
        q = _dot(xn, wb_ref[0])
        q = q * _sigmoid(q) * (HEAD_DIM ** -0.5)
        f = lb + (1.0 - lb) * _sigmoid(_dot(xn, wb_ref[1]))
        b2 = _chunk_cumsum(jnp.log2(f))
        k = 1.0 - f
        v = _dot(xn, wb_ref[2])
        g = _dot(xn, wb_ref[3])
        g_ref[rs, :] = (g * _sigmoid(g)).astype(BF16)
        for hh in range(heads_per_step):
            sl = slice(hh * HEAD_DIM, (hh + 1) * HEAD_DIM)
            q_ref[0, hh, rs, :] = q[:, sl].astype(BF16)
            k_ref[0, hh, rs, :] = k[:, sl].astype(BF16)
            b_ref[0, hh, rs, :] = b2[:, sl]
            v_ref[0, hh, rs, :] = v[:, sl].astype(BF16)
            bm = jnp.min(b2[:, sl].reshape(-1, 8, HEAD_DIM), axis=0)
            bmins[hh] = bm if bmins[hh] is None else jnp.minimum(bmins[hh], bm)
    for hh in range(heads_per_step):
        bmin_ref[0, hh, 0] = bmins[hh]


def _hgrn_proj(src, batch, lb_table, w_in, side_weights, *, layer, groups, norm_w=None, prev=None,
               tm=1024, heads_per_step=2):
    N, D = src.shape
    B, T = batch, N // batch
    H = D // HEAD_DIM
    tn = heads_per_step * HEAD_DIM
    nq = D // tn
    first, count = groups
    fuse_norm = norm_w is not None
    grid = (count, B, T // tm)
    head_shape = (B, H, T, HEAD_DIM)
    head_spec = pl.BlockSpec((1, heads_per_step, tm, HEAD_DIM), lambda j, b, i: (b, first + j, i, 0))
    tile_spec = pl.BlockSpec((tm, D), lambda j, b, i: (b * (T // tm) + i, 0))
    w_spec = lambda part: pl.BlockSpec((D, tn), lambda j, b, i: (0, part * nq + first + j))
    side_in, side_in_specs, side_out_specs, side_shapes = _side_cast_plan(side_weights, grid)
    prev = list(prev or [])
    n_in = 6 + fuse_norm + len(side_in)
    outs = pl.pallas_call(
        functools.partial(_hgrn_proj_kernel, layer=layer, heads_per_step=heads_per_step,
                          n_side=len(side_in), fuse_norm=fuse_norm),
        grid=grid,
        in_specs=[tile_spec]
        + ([pl.BlockSpec((1, D), lambda j, b, i: (0, 0))] if fuse_norm else [])
        + [pl.BlockSpec((lb_table.shape[0], tn), lambda j, b, i: (0, first + j)),
           w_spec(0), w_spec(1), w_spec(2), w_spec(3)]
        + side_in_specs + [pl.BlockSpec(memory_space=pl.ANY)] * len(prev),
        out_specs=[
            head_spec, head_spec, head_spec,
            pl.BlockSpec((1, heads_per_step, 1, 8, HEAD_DIM), lambda j, b, i: (b, first + j, i, 0, 0)),
            head_spec,
            pl.BlockSpec((tm, tn), lambda j, b, i: (b * (T // tm) + i, first + j)),
        ] + ([tile_spec] if fuse_norm else []) + side_out_specs,
        out_shape=[
            jax.ShapeDtypeStruct(head_shape, BF16),
            jax.ShapeDtypeStruct(head_shape, BF16),
            jax.ShapeDtypeStruct(head_shape, F32),
            jax.ShapeDtypeStruct((B, H, T // tm, 8, HEAD_DIM), F32),
            jax.ShapeDtypeStruct(head_shape, BF16),
            jax.ShapeDtypeStruct((N, D), BF16),
        ] + ([jax.ShapeDtypeStruct((N, D), BF16)] if fuse_norm else []) + side_shapes,
        input_output_aliases={n_in + k: k for k in range(len(prev))},
        scratch_shapes=[pltpu.VMEM((4, D, tn), BF16)],
        compiler_params=_params(("arbitrary", "arbitrary", "arbitrary")),
        name="hgrn_proj_first" if fuse_norm else "hgrn_proj",
    )(src, *([norm_w] if fuse_norm else []), lb_table, *([w_in] * 4), *side_in, *prev)
    n_main = 6 + fuse_norm
    casts = [o.reshape(w.shape[1:]) for o, (w, _) in zip(outs[n_main:], side_weights)]
    return outs[:6], (outs[6] if fuse_norm else None), casts


def _piece_offset(i):
    return SUB * i * (i + 1) // 2


def _score_mask():
    rw = lax.broadcasted_iota(jnp.int32, (CHUNK, A_COLS), 0)
    cw = lax.broadcasted_iota(jnp.int32, (CHUNK, A_COLS), 1)
    keep = None
    for i in range(N_SUB):
        diag0 = _piece_offset(i) + SUB * i
        in_rows = (rw >= SUB * i) & (rw < SUB * (i + 1))
        left = (cw >= _piece_offset(i)) & (cw < diag0)
        diag = (cw >= diag0) & (cw < diag0 + SUB) & (cw - diag0 <= rw - SUB * i)
        m = in_rows & (left | diag)
        keep = m if keep is None else keep | m
    return keep


def _exact_diag(slab, qb, kb, bb, lane0):
    lr = lax.broadcasted_iota(jnp.int32, slab.shape, 0)
    lc = lax.broadcasted_iota(jnp.int32, slab.shape, 1)
    for s in range(SUB):
        e = jnp.exp2(jnp.minimum(bb - bb[s:s + 1, :], 0.0))
        col = jnp.sum(qb * kb[s:s + 1, :] * e, axis=-1, keepdims=True)
        slab = jnp.where((lc == lane0 + s) & (lr >= s), col, slab)
    return slab


def _block_rows(rows):
    return jnp.concatenate([jnp.broadcast_to(r, (SUB, r.shape[-1])) for r in rows], axis=0)


def _chunk_factors(q_ref, k_ref, b_ref, rows):
    q = q_ref[0, 0, rows, :].astype(F32)
    k = k_ref[0, 0, rows, :].astype(F32)
    b = b_ref[0, 0, rows, :]

    ends = [b[SUB * (j + 1) - 1:SUB * (j + 1), :] for j in range(N_SUB)]
    refs = [jnp.zeros_like(ends[0])] + ends[:-1]
    b_last = ends[-1]
    d = b - _block_rows(refs)
    q_blk = (q * jnp.exp2(d)).astype(BF16)
    k_diag = k * jnp.exp2(jnp.minimum(-d, EXP2_CLAMP))
    k_end = k * jnp.exp2(_block_rows(ends) - b)
    pieces = []
    for i in range(N_SUB):
        for j in range(i):
            kj = k_end[SUB * j:SUB * (j + 1)]
            if j < i - 1:
                kj = kj * jnp.exp2(refs[i] - ends[j])
            pieces.append(kj.astype(BF16))
        pieces.append(k_diag[SUB * i:SUB * (i + 1)].astype(BF16))
    k_all = jnp.concatenate(pieces, axis=0)
    q_dec = (q * jnp.exp2(b)).astype(BF16)
    k_dec = (k_end * _block_rows([jnp.exp2(b_last - e) for e in ends])).astype(BF16)
    return (q, k, b), q_blk, k_all, q_dec, k_dec, jnp.exp2(b_last)


def _fix_diag(a, q, k, b):
    blocks = []
    for i in range(N_SUB):
        rs = slice(SUB * i, SUB * (i + 1))
        diag0 = _piece_offset(i) + SUB * i
        tile = diag0 // 128
        slabs = [a[rs, 128 * t:min(128 * (t + 1), A_COLS)] for t in range(pl.cdiv(A_COLS, 128))]
        slabs[tile] = _exact_diag(slabs[tile], q[rs], k[rs], b[rs], diag0 % 128)
        blocks.append(jnp.concatenate(slabs, axis=1))
    return jnp.concatenate(blocks, axis=0)


def _stack_values(v):
    return jnp.concatenate([v[0:SUB * (i + 1)] for i in range(N_SUB)], axis=0)


def _hgrn_rec_kernel(*refs, tm, n_side):
    q_ref, k_ref, b_ref, bmin_ref, v_ref = refs[:5]
    o_ref = refs[5 + n_side]
    st_ref, qd_scr, oi_scr, kv_scr, dl_scr = refs[6 + 2 * n_side:]
    _side_cast(refs[5:5 + n_side], refs[6 + n_side:6 + 2 * n_side])

    @pl.when(pl.program_id(2) == 0)
    def _():
        st_ref[...] = jnp.zeros_like(st_ref)

    n_chunks = tm // CHUNK
    unroll = math.gcd(n_chunks, REC_UNROLL)

    def chunk_rows(c):
        return pl.ds(pl.multiple_of(c * CHUNK, CHUNK), CHUNK)

    def state_rows(c):
        return pl.ds(pl.multiple_of(c * HEAD_DIM, HEAD_DIM), HEAD_DIM)

    def decay_rows(c, n):
        return pl.ds(pl.multiple_of(c * 8, 8), n)

    keep = _score_mask()

    def fast():
        def intra_body(c, carry):
            ccs = [c * unroll + u for u in range(unroll)]
            fac = [_chunk_factors(q_ref, k_ref, b_ref, chunk_rows(cc)) for cc in ccs]
            scores = [jnp.where(keep, _dot_nt(f[1], f[2]), 0.0).astype(BF16) for f in fac]
            for cc, f in zip(ccs, fac):
                kv_scr[state_rows(cc), :] = _dot_tn(v_ref[0, 0, chunk_rows(cc), :], f[4])
                qd_scr[chunk_rows(cc), :] = f[3]
                dl_scr[decay_rows(cc, 8), :] = jnp.broadcast_to(f[5], (8, HEAD_DIM))
            for cc, a in zip(ccs, scores):
                rows = chunk_rows(cc)
                oi_scr[rows, :] = _dot(a, _stack_values(v_ref[0, 0, rows, :]))
            return carry

        lax.fori_loop(0, n_chunks // unroll, intra_body, 0)

        def state_body(c, st):
            for u in range(unroll):
                cc = c * unroll + u
                rows = chunk_rows(cc)
                o = oi_scr[rows, :] + _dot_nt(qd_scr[rows, :], st.astype(BF16))
                o_ref[0, rows, :] = o.astype(o_ref.dtype)
                st = st * dl_scr[decay_rows(cc, 1), :] + kv_scr[state_rows(cc), :]
            return st

        st_ref[...] = lax.fori_loop(0, n_chunks // unroll, state_body, st_ref[...])

    def exact():
        def body(c, st):
            rows = chunk_rows(c)
            (q, k, b), q_blk, k_all, q_dec, k_dec, dl = _chunk_factors(q_ref, k_ref, b_ref, rows)
            a = _fix_diag(jnp.where(keep, _dot_nt(q_blk, k_all), 0.0), q, k, b).astype(BF16)
            v = v_ref[0, 0, rows, :]
            o = _dot(a, _stack_values(v)) + _dot_nt(q_dec, st.astype(BF16))
            o_ref[0, rows, :] = o.astype(o_ref.dtype)
            return st * dl + _dot_tn(v, k_dec)

        st_ref[...] = lax.fori_loop(0, n_chunks, body, st_ref[...])

    clamp_may_bind = jnp.min(bmin_ref[0, 0]) < -(EXP2_CLAMP - 1.0)
    pl.when(jnp.logical_not(clamp_may_bind))(fast)
    pl.when(clamp_may_bind)(exact)


def _hgrn_rec(q, k, b2, bmin, v, side_weights, *, tm=4096):
    B, H, T, _ = q.shape
    tm = min(tm, T)
    assert T % tm == 0 and tm % CHUNK == 0
    n_min = bmin.shape[2] // (T // tm)
    spec = pl.BlockSpec((1, 1, tm, HEAD_DIM), lambda b, h, t: (b, h, t, 0))
    grid = (B, H, T // tm)
    side_in, side_in_specs, side_out_specs, side_shapes = _side_cast_plan(side_weights, grid)
    outs = pl.pallas_call(
        functools.partial(_hgrn_rec_kernel, tm=tm, n_side=len(side_in)),
        grid=grid,
        in_specs=[spec, spec, spec,
                  pl.BlockSpec((1, 1, n_min, 8, HEAD_DIM), lambda b, h, t: (b, h, t, 0, 0)),
                  spec] + side_in_specs,
        out_specs=[pl.BlockSpec((1, tm, HEAD_DIM), lambda b, h, t: (b, t, h))] + side_out_specs,
        out_shape=[jax.ShapeDtypeStruct((B, T, H * HEAD_DIM), BF16)] + side_shapes,
        scratch_shapes=[pltpu.VMEM((HEAD_DIM, HEAD_DIM), F32),
                        pltpu.VMEM((tm, HEAD_DIM), BF16),
                        pltpu.VMEM((tm, HEAD_DIM), F32),
                        pltpu.VMEM((tm // CHUNK * HEAD_DIM, HEAD_DIM), F32),
                        pltpu.VMEM((tm // CHUNK * 8, HEAD_DIM), F32)],
        compiler_params=_params(("parallel", "parallel", "arbitrary")),
        name="hgrn_rec",
    )(q, k, b2, bmin, v, *side_in)
    return outs[0], [o.reshape(w.shape[1:]) for o, (w, _) in zip(outs[1:], side_weights)]


def _hgrn_out_kernel(o_ref, g_ref, h_ref, gain_ref, w_ref, out_ref):
    y = _rms_rows(o_ref[...].astype(F32), gain_ref[...]) * g_ref[...].astype(F32)
    out_ref[...] = h_ref[...] + _dot(y.astype(BF16), w_ref[...])


def _hgrn_out(o, gate, h, gain, w_out, *, tm=512):
    N, D = h.shape
    row_spec = pl.BlockSpec((tm, D), lambda i: (i, 0))
    return pl.pallas_call(
        _hgrn_out_kernel,
        grid=(N // tm,),
        in_specs=[row_spec, row_spec, row_spec,
                  pl.BlockSpec((1, D), lambda i: (0, 0)),
                  pl.BlockSpec((D, D), lambda i: (0, 0))],
        out_specs=row_spec,
        out_shape=jax.ShapeDtypeStruct((N, D), F32),
        compiler_params=_params(("parallel",)),
        name="hgrn_out",
    )(o, gate, h, gain, w_out)


TAIL = 8


def _conv3(prev, u, cw):
    rows = u.shape[0]
    ue = jnp.concatenate([prev, u], axis=0)
    return (cw[0:1, :] * ue[TAIL - 2:TAIL - 2 + rows]
            + cw[1:2, :] * ue[TAIL - 1:TAIL - 1 + rows]
            + cw[2:3, :] * u)


def _tile_step(h_hbm, h_buf, h_sem, nw_ref, xn_ref, out_ref, tails, tm, tiles_per_batch, row_parts,
               body):
    i = pl.program_id(0)
    j = pl.program_id(1)
    rp = tm // row_parts

    def tile_copy(t):
        rows = pl.ds(pl.multiple_of(t * tm, tm), tm)
        return pltpu.make_async_copy(h_hbm.at[rows, :], h_buf, h_sem)

    @pl.when((i == 0) & (j == 0))
    def _():
        tile_copy(i).start()

    @pl.when((j == 1) & (i + 1 < pl.num_programs(0)))
    def _():
        tile_copy(i + 1).start()

    @pl.when(i % tiles_per_batch == 0)
    def _():
        for t in tails:
            t[j] = jnp.zeros(t.shape[1:], t.dtype)

    @pl.when(j == 0)
    def _():
        tile_copy(i).wait()
        xns = []
        for p in range(row_parts):
            rs = slice(p * rp, (p + 1) * rp)
            hp = h_buf[rs, :]
            xns.append(_rms_rows(hp, nw_ref[...]).astype(BF16))
            xn_ref[rs, :] = xns[-1]
            out_ref[rs, :] = hp
        body(xns)

    @pl.when(j > 0)
    def _():
        body([xn_ref[p * rp:(p + 1) * rp, :] for p in range(row_parts)])


def _tile_scratch(tm, d):
    return [pltpu.VMEM((tm, d), BF16), pltpu.VMEM((tm, d), F32), pltpu.SemaphoreType.DMA(())]


def _ffn_kernel(*refs, tm, tiles_per_batch, final_norm, row_parts, n_side):
    h_hbm, nw_ref, wg_ref, wv_ref, cwg_ref, cwv_ref, wd_ref, fw_ref = refs[:8]
    out_ref = refs[8 + n_side]
    xn_ref, h_buf, h_sem, tg_ref, tv_ref = refs[9 + 2 * n_side:]
    _side_cast(refs[8:8 + n_side], refs[9 + n_side:9 + 2 * n_side])
    j = pl.program_id(1)
    rp = tm // row_parts

    def body(xns):
        ups = [(_dot(xn, wg_ref[...]), _dot(xn, wv_ref[...])) for xn in xns]
        prev_g, prev_v = tg_ref[j], tv_ref[j]
        acts = []
        for ug, uv in ups:
            cg = _conv3(prev_g, ug, cwg_ref[...])
            cv = _conv3(prev_v, uv, cwv_ref[...])
            prev_g, prev_v = ug[rp - TAIL:], uv[rp - TAIL:]
            acts.append((cg * _sigmoid(cg) * cv).astype(BF16))
        tg_ref[j] = prev_g
        tv_ref[j] = prev_v
        for p, act in enumerate(acts):
            out_ref[p * rp:(p + 1) * rp, :] += _dot(act, wd_ref[...])

    _tile_step(h_hbm, h_buf, h_sem, nw_ref, xn_ref, out_ref, (tg_ref, tv_ref), tm, tiles_per_batch,
               row_parts, body)

    if final_norm:
        @pl.when(j == pl.num_programs(1) - 1)
        def _():
            out_ref[...] = _rms_rows(out_ref[...], fw_ref[...])


def _ffn(h, seq_len, norm_w, w_up, conv_w, w_down, final_w, side_weights, *, final_norm, tm=1024,
         tn=512, row_parts=ROW_PARTS, vmem_limit=VMEM_LIMIT):
    N, D = h.shape
    F = w_down.shape[0]
    nf = F // tn
    assert nf >= 2
    grid = (N // tm, nf)
    side_in, side_in_specs, side_out_specs, side_shapes = _side_cast_plan(side_weights, grid)
    outs = pl.pallas_call(
        functools.partial(_ffn_kernel, tm=tm, tiles_per_batch=seq_len // tm,
                          final_norm=final_norm, row_parts=row_parts, n_side=len(side_in)),
        grid=grid,
        in_specs=[
            pl.BlockSpec(memory_space=pl.ANY),
            pl.BlockSpec((1, D), lambda i, j: (0, 0)),
            pl.BlockSpec((D, tn), lambda i, j: (0, j)),
            pl.BlockSpec((D, tn), lambda i, j: (0, nf + j)),
            pl.BlockSpec((3, tn), lambda i, j: (0, j)),
            pl.BlockSpec((3, tn), lambda i, j: (0, nf + j)),
            pl.BlockSpec((tn, D), lambda i, j: (j, 0)),
            pl.BlockSpec((1, D), lambda i, j: (0, 0)),
        ] + side_in_specs,
        out_specs=[pl.BlockSpec((tm, D), lambda i, j: (i, 0))] + side_out_specs,
        out_shape=[jax.ShapeDtypeStruct((N, D), F32)] + side_shapes,
        scratch_shapes=_tile_scratch(tm, D) + [pltpu.VMEM((nf, TAIL, tn), F32)] * 2,
        compiler_params=_params(("arbitrary", "arbitrary"), vmem_limit),
        name="ffn_final" if final_norm else "ffn",
    )(h, norm_w, *([_bf16(w_up)] * 2), conv_w, conv_w, _bf16(w_down), final_w, *side_in)
    return outs[0], [o.reshape(w.shape[1:]) for o, (w, _) in zip(outs[1:], side_weights)]


def _sc_kernel(h_hbm, nw_ref, wb_ref, wc_ref, wh_ref, cw_ref, wd_ref,
               out_ref, xn_ref, h_buf, h_sem, tz_ref, *, tm, tiles_per_batch, row_parts):
    j = pl.program_id(1)
    rp = tm // row_parts

    def body(xns):
        ups = [(_dot(xn, wc_ref[...]), _dot(xn, wh_ref[...]), _dot(xn, wb_ref[...])) for xn in xns]
        prev_z = tz_ref[j]
        ys = []
        for uc, uh, gb in ups:
            z = uc * uh
            ys.append((gb * _conv3(prev_z, z, cw_ref[...])).astype(BF16))
            prev_z = z[rp - TAIL:]
        tz_ref[j] = prev_z
        for p, y in enumerate(ys):
            out_ref[p * rp:(p + 1) * rp, :] += _dot(y, wd_ref[...])

    _tile_step(h_hbm, h_buf, h_sem, nw_ref, xn_ref, out_ref, (tz_ref,), tm, tiles_per_batch,
               row_parts, body)


def _short_conv(h, seq_len, norm_w, w_in, conv_w, w_out, *, tm=1024, tn=512, row_parts=ROW_PARTS,
                vmem_limit=60 * 1024 * 1024):
    N, D = h.shape
    nd = D // tn
    assert nd >= 2
    w_spec = lambda part: pl.BlockSpec((D, tn), lambda i, j: (0, part * nd + j))
    return pl.pallas_call(
        functools.partial(_sc_kernel, tm=tm, tiles_per_batch=seq_len // tm, row_parts=row_parts),
        grid=(N // tm, nd),
        in_specs=[
            pl.BlockSpec(memory_space=pl.ANY),
            pl.BlockSpec((1, D), lambda i, j: (0, 0)),
            w_spec(0), w_spec(1), w_spec(2),
            pl.BlockSpec((3, tn), lambda i, j: (0, j)),
            pl.BlockSpec((tn, D), lambda i, j: (j, 0)),
        ],
        out_specs=pl.BlockSpec((tm, D), lambda i, j: (i, 0)),
        out_shape=jax.ShapeDtypeStruct((N, D), F32),
        scratch_shapes=_tile_scratch(tm, D) + [pltpu.VMEM((nd, TAIL, tn), F32)],
        compiler_params=_params(("arbitrary", "arbitrary"), vmem_limit),
        name="short_conv",
    )(h, norm_w, *([_bf16(w_in)] * 3), conv_w, _bf16(w_out))


def kernel(x, norm_mix, norm_ffn, hgrn_w_in, hgrn_lb_table, hgrn_out_norm, hgrn_w_out,
           sc_w_in, sc_conv, sc_w_out, ffn_w_up, ffn_conv, ffn_w_down, final_norm):
    B, T, D = x.shape
    depth = norm_mix.shape[0]
    n_mixers = 2
    row = lambda w: w.reshape(1, -1).astype(F32)
    stacks = dict(hgrn_w_in=hgrn_w_in, hgrn_w_out=hgrn_w_out, sc_w_in=sc_w_in, sc_w_out=sc_w_out,
                  ffn_w_up=ffn_w_up, ffn_w_down=ffn_w_down)

    def layer_weights(i):
        mixer = ("hgrn_w_in", "hgrn_w_out") if i % n_mixers == 0 else ("sc_w_in", "sc_w_out")
        return [(name, i // n_mixers) for name in mixer] + [("ffn_w_up", i), ("ffn_w_down", i)]

    cast = {}

    def weight(name, idx):
        return cast.get((name, idx), stacks[name][idx])

    def side(keys):
        return [(stacks[name], idx) for name, idx in keys]

    h = x.reshape(B * T, D)
    for i in range(depth):
        j = i // n_mixers
        if i % n_mixers == 0:
            proj_side = layer_weights(i)[1:] if i == 0 else []
            rec_side = [kw for l in range(1, depth) for kw in layer_weights(l)[:2]] if i == 0 else []
            n_groups = D // (2 * HEAD_DIM)
            lbt, w_in = hgrn_lb_table.astype(F32), weight("hgrn_w_in", j)
            outs, xn, _ = _hgrn_proj(h, B, lbt, w_in, [], layer=i, groups=(0, 1),
                                     norm_w=row(norm_mix[i]))
            outs, _, done = _hgrn_proj(xn, B, lbt, w_in, side(proj_side), layer=i,
                                       groups=(1, n_groups - 1), prev=outs)
            cast.update(zip(proj_side, done))
            q, k, b2, bmin, v, gate = outs
            o, done = _hgrn_rec(q, k, b2, bmin, v, side(rec_side))
            cast.update(zip(rec_side, done))
            h = _hgrn_out(o.reshape(B * T, D), gate, h, row(hgrn_out_norm[j]),
                          weight("hgrn_w_out", j).astype(BF16))
        else:
            h = _short_conv(h, T, row(norm_mix[i]), weight("sc_w_in", j), sc_conv[j].astype(F32),
                            weight("sc_w_out", j))
        ffn_side = [kw for kw in layer_weights(i + 1)[2:] if kw not in cast] if i + 1 < depth else []
        h, done = _ffn(h, T, row(norm_ffn[i]), weight("ffn_w_up", i), ffn_conv[i].astype(F32),
                       weight("ffn_w_down", i), row(final_norm), side(ffn_side),
                       final_norm=(i == depth - 1))
        cast.update(zip(ffn_side, done))
    return h.reshape(B, T, D)
```

```python
import functools
import math

import jax
import jax.numpy as jnp
from jax import lax
from jax.experimental import pallas as pl
from jax.experimental.pallas import tpu as pltpu

F32 = jnp.float32
BF16 = jnp.bfloat16

EPS = 1e-6
HEAD_DIM = 128
CHUNK = 64
SUB = 16
N_SUB = CHUNK // SUB
N_PIECES = N_SUB * (N_SUB + 1) // 2
A_COLS = SUB * N_PIECES
EXP2_CLAMP = 112.0
REC_UNROLL = 64
ROW_PARTS = 2
VMEM_LIMIT = 56 * 1024 * 1024


def _dot(a, b):
    return jnp.dot(a, b, preferred_element_type=F32)


def _dot_nt(a, b):
    return lax.dot_general(a, b, (((1,), (1,)), ((), ())), preferred_element_type=F32)


def _dot_tn(a, b):
    return lax.dot_general(a, b, (((0,), (0,)), ((), ())), preferred_element_type=F32)


def _sigmoid(x):
    return 1.0 / (1.0 + jnp.exp(-x))


def _rms_rows(x, w):
    ms = jnp.mean(x * x, axis=-1, keepdims=True)
    return x * lax.rsqrt(ms + EPS) * w


def _params(semantics, vmem_limit=VMEM_LIMIT):
    return pltpu.CompilerParams(dimension_semantics=semantics, vmem_limit_bytes=vmem_limit)


def _bf16(w):
    return w.astype(BF16)


def _side_cast_plan(weights, grid):
    n_steps = math.prod(grid)
    strides = [math.prod(grid[d + 1:]) for d in range(len(grid))]

    def chunk(ids, n_chunks):
        return jnp.minimum(sum(i * s for i, s in zip(ids, strides)), n_chunks - 1)

    ins, in_specs, out_specs, shapes = [], [], [], []
    for stack, idx in weights:
        n, rows, cols = stack.shape
        nc = max(c for c in range(1, n_steps + 1) if rows % c == 0)
        block = (None, rows // nc, cols)
        ins.append(stack.reshape(n * nc, rows // nc, cols))
        in_specs.append(pl.BlockSpec(block, lambda *ids, base=idx * nc, nc=nc: (base + chunk(ids, nc), 0, 0)))
        out_specs.append(pl.BlockSpec(block, lambda *ids, nc=nc: (chunk(ids, nc), 0, 0)))
        shapes.append(jax.ShapeDtypeStruct((nc, rows // nc, cols), BF16))
    return ins, in_specs, out_specs, shapes


def _side_cast(src_refs, dst_refs):
    for src, dst in zip(src_refs, dst_refs):
        dst[...] = src[...].astype(BF16)


def _chunk_cumsum(x):
    n, d = x.shape
    pos = lax.broadcasted_iota(jnp.int32, (n, d), 0) & (CHUNK - 1)
    s = 1
    while s < CHUNK:
        pad = max(8, s)
        xp = jnp.concatenate([jnp.zeros((pad, d), x.dtype), x], axis=0)
        x = x + jnp.where(pos >= s, xp[pad - s:pad - s + n], 0.0)
        s *= 2
    return x


def _hgrn_proj_kernel(*refs, layer, heads_per_step, n_side):
    h_hbm, nw_ref, lbt_ref = refs[:3]
    w_refs = refs[3:7]
    q_ref, k_ref, b_ref, bmin_ref, v_ref, g_ref, xn_hbm = refs[7 + n_side:14 + n_side]
    wb_ref, h_buf, xn_buf, h_sem, wr_sem, rd_sem = refs[-6:]
    _side_cast(refs[7:7 + n_side], refs[14 + n_side:14 + 2 * n_side])

    j = pl.program_id(0)
    tiles_per_batch = pl.num_programs(2)
    n_tiles = pl.num_programs(1) * tiles_per_batch
    n_steps = pl.num_programs(0) * n_tiles
    tile = pl.program_id(1) * tiles_per_batch + pl.program_id(2)
    step = j * n_tiles + tile
    slot = tile % 2
    tm = h_buf.shape[0]
    tile_rows = lambda t: pl.ds(pl.multiple_of(t * tm, tm), tm)
    h_copy = lambda t: pltpu.make_async_copy(h_hbm.at[tile_rows(t)], h_buf, h_sem)
    xn_write = lambda t: pltpu.make_async_copy(xn_buf.at[t % 2], xn_hbm.at[tile_rows(t)], wr_sem)
    xn_read = lambda t: pltpu.make_async_copy(xn_hbm.at[tile_rows(t)], xn_buf.at[t % 2],
                                              rd_sem.at[t % 2])

    @pl.when(step == 0)
    def _():
        h_copy(0).start()

    @pl.when((step + 1 >= n_tiles) & (step + 1 < n_steps))
    def _():
        xn_read((tile + 1) % n_tiles).start()

    @pl.when(j == 0)
    def _():
        h_copy(tile).wait()
        rp = tm // ROW_PARTS
        for p in range(ROW_PARTS):
            rs = slice(p * rp, (p + 1) * rp)
            xn_buf[slot, rs, :] = _rms_rows(h_buf[rs, :], nw_ref[...]).astype(BF16)

        @pl.when(tile + 1 < n_tiles)
        def _():
            h_copy(tile + 1).start()

        xn_write(tile).start()

    @pl.when(j > 0)
    def _():
        xn_read(tile).wait()

    @pl.when(tile == 0)
    def _():
        for p, w_ref in enumerate(w_refs):
            wb_ref[p] = w_ref[...].astype(BF16)

    rows = [lbt_ref[l:l + 1, :] for l in range(lbt_ref.shape[0])]
    m = functools.reduce(jnp.maximum, rows)
    es = [jnp.exp(r - m) for r in rows]
    lb = sum(es[:layer + 1]) / sum(es)

    rp = tm // ROW_PARTS
    bmins = [None] * heads_per_step
    for p in range(ROW_PARTS):
        rs = slice(p * rp, (p + 1) * rp)
        xn = xn_buf[slot, rs, :]
        q = _dot(xn, wb_ref[0])
        q = q * _sigmoid(q) * (HEAD_DIM ** -0.5)
        f = lb + (1.0 - lb) * _sigmoid(_dot(xn, wb_ref[1]))
        b2 = _chunk_cumsum(jnp.log2(f))
        k = 1.0 - f
        v = _dot(xn, wb_ref[2])
        g = _dot(xn, wb_ref[3])
        g_ref[rs, :] = (g * _sigmoid(g)).astype(BF16)
        for hh in range(heads_per_step):
            sl = slice(hh * HEAD_DIM, (hh + 1) * HEAD_DIM)
            q_ref[0, hh, rs, :] = q[:, sl].astype(BF16)
            k_ref[0, hh, rs, :] = k[:, sl].astype(BF16)
            b_ref[0, hh, rs, :] = b2[:, sl]
            v_ref[0, hh, rs, :] = v[:, sl].astype(BF16)
            bm = jnp.min(b2[:, sl].reshape(-1, 8, HEAD_DIM), axis=0)
            bmins[hh] = bm if bmins[hh] is None else jnp.minimum(bmins[hh], bm)
    for hh in range(heads_per_step):
        bmin_ref[0, hh, 0] = bmins[hh]

    @pl.when(j == 0)
    def _():
        xn_write(tile).wait()


def _hgrn_proj(h, batch, norm_w, lb_table, w_in, side_weights, *, layer, tm=1024, heads_per_step=2):
    N, D = h.shape
    B, T = batch, N // batch
    assert (N // tm) % 2 == 0
    H = D // HEAD_DIM
    tn = heads_per_step * HEAD_DIM
    nq = D // tn
    grid = (H // heads_per_step, B, T // tm)
    head_shape = (B, H, T, HEAD_DIM)
    head_spec = pl.BlockSpec((1, heads_per_step, tm, HEAD_DIM), lambda j, b, i: (b, j, i, 0))
    w_spec = lambda part: pl.BlockSpec((D, tn), lambda j, b, i: (0, part * nq + j))
    side_in, side_in_specs, side_out_specs, side_shapes = _side_cast_plan(side_weights, grid)
    outs = pl.pallas_call(
        functools.partial(_hgrn_proj_kernel, layer=layer, heads_per_step=heads_per_step,
                          n_side=len(side_in)),
        grid=grid,
        in_specs=[
            pl.BlockSpec(memory_space=pl.ANY),
            pl.BlockSpec((1, D), lambda j, b, i: (0, 0)),
            pl.BlockSpec((lb_table.shape[0], tn), lambda j, b, i: (0, j)),
            w_spec(0), w_spec(1), w_spec(2), w_spec(3),
        ] + side_in_specs,
        out_specs=[
            head_spec, head_spec, head_spec,
            pl.BlockSpec((1, heads_per_step, 1, 8, HEAD_DIM), lambda j, b, i: (b, j, i, 0, 0)),
            head_spec,
            pl.BlockSpec((tm, tn), lambda j, b, i: (b * (T // tm) + i, j)),
            pl.BlockSpec(memory_space=pl.ANY),
        ] + side_out_specs,
        out_shape=[
            jax.ShapeDtypeStruct(head_shape, BF16),
            jax.ShapeDtypeStruct(head_shape, BF16),
            jax.ShapeDtypeStruct(head_shape, F32),
            jax.ShapeDtypeStruct((B, H, T // tm, 8, HEAD_DIM), F32),
            jax.ShapeDtypeStruct(head_shape, BF16),
            jax.ShapeDtypeStruct((N, D), BF16),
            jax.ShapeDtypeStruct((N, D), BF16),
        ] + side_shapes,
        scratch_shapes=[pltpu.VMEM((4, D, tn), BF16), pltpu.VMEM((tm, D), F32),
                        pltpu.VMEM((2, tm, D), BF16), pltpu.SemaphoreType.DMA(()),
                        pltpu.SemaphoreType.DMA(()), pltpu.SemaphoreType.DMA((2,))],
        compiler_params=_params(("arbitrary", "arbitrary", "arbitrary")),
        name="hgrn_proj",
    )(h, norm_w, lb_table, *([w_in] * 4), *side_in)
    return outs[:6], [o.reshape(w.shape[1:]) for o, (w, _) in zip(outs[7:], side_weights)]


def _piece_offset(i):
    return SUB * i * (i + 1) // 2


def _score_mask():
    rw = lax.broadcasted_iota(jnp.int32, (CHUNK, A_COLS), 0)
    cw = lax.broadcasted_iota(jnp.int32, (CHUNK, A_COLS), 1)
    keep = None
    for i in range(N_SUB):
        diag0 = _piece_offset(i) + SUB * i
        in_rows = (rw >= SUB * i) & (rw < SUB * (i + 1))
        left = (cw >= _piece_offset(i)) & (cw < diag0)
        diag = (cw >= diag0) & (cw < diag0 + SUB) & (cw - diag0 <= rw - SUB * i)
        m = in_rows & (left | diag)
        keep = m if keep is None else keep | m
    return keep


def _exact_diag(slab, qb, kb, bb, lane0):
    lr = lax.broadcasted_iota(jnp.int32, slab.shape, 0)
    lc = lax.broadcasted_iota(jnp.int32, slab.shape, 1)
    for s in range(SUB):
        e = jnp.exp2(jnp.minimum(bb - bb[s:s + 1, :], 0.0))
        col = jnp.sum(qb * kb[s:s + 1, :] * e, axis=-1, keepdims=True)
        slab = jnp.where((lc == lane0 + s) & (lr >= s), col, slab)
    return slab


def _block_rows(rows):
    return jnp.concatenate([jnp.broadcast_to(r, (SUB, r.shape[-1])) for r in rows], axis=0)


def _chunk_factors(q_ref, k_ref, b_ref, rows):
    q = q_ref[0, 0, rows, :].astype(F32)
    k = k_ref[0, 0, rows, :].astype(F32)
    b = b_ref[0, 0, rows, :]

    ends = [b[SUB * (j + 1) - 1:SUB * (j + 1), :] for j in range(N_SUB)]
    refs = [jnp.zeros_like(ends[0])] + ends[:-1]
    b_last = ends[-1]
    d = b - _block_rows(refs)
    q_blk = (q * jnp.exp2(d)).astype(BF16)
    k_diag = k * jnp.exp2(jnp.minimum(-d, EXP2_CLAMP))
    k_end = k * jnp.exp2(_block_rows(ends) - b)
    pieces = []
    for i in range(N_SUB):
        for j in range(i):
            kj = k_end[SUB * j:SUB * (j + 1)]
            if j < i - 1:
                kj = kj * jnp.exp2(refs[i] - ends[j])
            pieces.append(kj.astype(BF16))
        pieces.append(k_diag[SUB * i:SUB * (i + 1)].astype(BF16))
    k_all = jnp.concatenate(pieces, axis=0)
    q_dec = (q * jnp.exp2(b)).astype(BF16)
    k_dec = (k_end * _block_rows([jnp.exp2(b_last - e) for e in ends])).astype(BF16)
    return (q, k, b), q_blk, k_all, q_dec, k_dec, jnp.exp2(b_last)


def _fix_diag(a, q, k, b):
    blocks = []
    for i in range(N_SUB):
        rs = slice(SUB * i, SUB * (i + 1))
        diag0 = _piece_offset(i) + SUB * i
        tile = diag0 // 128
        slabs = [a[rs, 128 * t:min(128 * (t + 1), A_COLS)] for t in range(pl.cdiv(A_COLS, 128))]
        slabs[tile] = _exact_diag(slabs[tile], q[rs], k[rs], b[rs], diag0 % 128)
        blocks.append(jnp.concatenate(slabs, axis=1))
    return jnp.concatenate(blocks, axis=0)


def _stack_values(v):
    return jnp.concatenate([v[0:SUB * (i + 1)] for i in range(N_SUB)], axis=0)


def _hgrn_rec_kernel(*refs, tm, n_side):
    q_ref, k_ref, b_ref, bmin_ref, v_ref = refs[:5]
    o_ref = refs[5 + n_side]
    st_ref, qd_scr, oi_scr, kv_scr, dl_scr = refs[6 + 2 * n_side:]
    _side_cast(refs[5:5 + n_side], refs[6 + n_side:6 + 2 * n_side])

    @pl.when(pl.program_id(2) == 0)
    def _():
        st_ref[...] = jnp.zeros_like(st_ref)

    n_chunks = tm // CHUNK
    unroll = math.gcd(n_chunks, REC_UNROLL)

    def chunk_rows(c):
        return pl.ds(pl.multiple_of(c * CHUNK, CHUNK), CHUNK)

    def state_rows(c):
        return pl.ds(pl.multiple_of(c * HEAD_DIM, HEAD_DIM), HEAD_DIM)

    def decay_rows(c, n):
        return pl.ds(pl.multiple_of(c * 8, 8), n)

    keep = _score_mask()

    def fast():
        def intra_body(c, carry):
            ccs = [c * unroll + u for u in range(unroll)]
            fac = [_chunk_factors(q_ref, k_ref, b_ref, chunk_rows(cc)) for cc in ccs]
            scores = [jnp.where(keep, _dot_nt(f[1], f[2]), 0.0).astype(BF16) for f in fac]
            for cc, f in zip(ccs, fac):
                kv_scr[state_rows(cc), :] = _dot_tn(v_ref[0, 0, chunk_rows(cc), :], f[4])
                qd_scr[chunk_rows(cc), :] = f[3]
                dl_scr[decay_rows(cc, 8), :] = jnp.broadcast_to(f[5], (8, HEAD_DIM))
            for cc, a in zip(ccs, scores):
                rows = chunk_rows(cc)
                oi_scr[rows, :] = _dot(a, _stack_values(v_ref[0, 0, rows, :]))
            return carry

        lax.fori_loop(0, n_chunks // unroll, intra_body, 0)

        def state_body(c, st):
            for u in range(unroll):
                cc = c * unroll + u
                rows = chunk_rows(cc)
                o = oi_scr[rows, :] + _dot_nt(qd_scr[rows, :], st.astype(BF16))
                o_ref[0, rows, :] = o.astype(o_ref.dtype)
                st = st * dl_scr[decay_rows(cc, 1), :] + kv_scr[state_rows(cc), :]
            return st

        st_ref[...] = lax.fori_loop(0, n_chunks // unroll, state_body, st_ref[...])

    def exact():
        def body(c, st):
            rows = chunk_rows(c)
            (q, k, b), q_blk, k_all, q_dec, k_dec, dl = _chunk_factors(q_ref, k_ref, b_ref, rows)
            a = _fix_diag(jnp.where(keep, _dot_nt(q_blk, k_all), 0.0), q, k, b).astype(BF16)
            v = v_ref[0, 0, rows, :]
            o = _dot(a, _stack_values(v)) + _dot_nt(q_dec, st.astype(BF16))
            o_ref[0, rows, :] = o.astype(o_ref.dtype)
            return st * dl + _dot_tn(v, k_dec)

        st_ref[...] = lax.fori_loop(0, n_chunks, body, st_ref[...])

    clamp_may_bind = jnp.min(bmin_ref[0, 0]) < -(EXP2_CLAMP - 1.0)
    pl.when(jnp.logical_not(clamp_may_bind))(fast)
    pl.when(clamp_may_bind)(exact)


def _hgrn_rec(q, k, b2, bmin, v, side_weights, *, tm=4096):
    B, H, T, _ = q.shape
    tm = min(tm, T)
    assert T % tm == 0 and tm % CHUNK == 0
    n_min = bmin.shape[2] // (T // tm)
    spec = pl.BlockSpec((1, 1, tm, HEAD_DIM), lambda b, h, t: (b, h, t, 0))
    grid = (B, H, T // tm)
    side_in, side_in_specs, side_out_specs, side_shapes = _side_cast_plan(side_weights, grid)
    outs = pl.pallas_call(
        functools.partial(_hgrn_rec_kernel, tm=tm, n_side=len(side_in)),
        grid=grid,
        in_specs=[spec, spec, spec,
                  pl.BlockSpec((1, 1, n_min, 8, HEAD_DIM), lambda b, h, t: (b, h, t, 0, 0)),
                  spec] + side_in_specs,
        out_specs=[pl.BlockSpec((1, tm, HEAD_DIM), lambda b, h, t: (b, t, h))] + side_out_specs,
        out_shape=[jax.ShapeDtypeStruct((B, T, H * HEAD_DIM), BF16)] + side_shapes,
        scratch_shapes=[pltpu.VMEM((HEAD_DIM, HEAD_DIM), F32),
                        pltpu.VMEM((tm, HEAD_DIM), BF16),
                        pltpu.VMEM((tm, HEAD_DIM), F32),
                        pltpu.VMEM((tm // CHUNK * HEAD_DIM, HEAD_DIM), F32),
                        pltpu.VMEM((tm // CHUNK * 8, HEAD_DIM), F32)],
        compiler_params=_params(("parallel", "parallel", "arbitrary")),
        name="hgrn_rec",
    )(q, k, b2, bmin, v, *side_in)
    return outs[0], [o.reshape(w.shape[1:]) for o, (w, _) in zip(outs[1:], side_weights)]


def _hgrn_out_kernel(o_ref, g_ref, h_ref, gain_ref, w_ref, out_ref):
    y = _rms_rows(o_ref[...].astype(F32), gain_ref[...]) * g_ref[...].astype(F32)
    out_ref[...] = h_ref[...] + _dot(y.astype(BF16), w_ref[...])


def _hgrn_out(o, gate, h, gain, w_out, *, tm=512):
    N, D = h.shape
    row_spec = pl.BlockSpec((tm, D), lambda i: (i, 0))
    return pl.pallas_call(
        _hgrn_out_kernel,
        grid=(N // tm,),
        in_specs=[row_spec, row_spec, row_spec,
                  pl.BlockSpec((1, D), lambda i: (0, 0)),
                  pl.BlockSpec((D, D), lambda i: (0, 0))],
        out_specs=row_spec,
        out_shape=jax.ShapeDtypeStruct((N, D), F32),
        compiler_params=_params(("parallel",)),
        name="hgrn_out",
    )(o, gate, h, gain, w_out)


TAIL = 8


def _conv3(prev, u, cw):
    rows = u.shape[0]
    ue = jnp.concatenate([prev, u], axis=0)
    return (cw[0:1, :] * ue[TAIL - 2:TAIL - 2 + rows]
            + cw[1:2, :] * ue[TAIL - 1:TAIL - 1 + rows]
            + cw[2:3, :] * u)


def _tile_step(h_hbm, h_buf, h_sem, nw_ref, xn_ref, out_ref, tails, tm, tiles_per_batch, row_parts,
               body):
    i = pl.program_id(0)
    j = pl.program_id(1)
    rp = tm // row_parts

    def tile_copy(t):
        rows = pl.ds(pl.multiple_of(t * tm, tm), tm)
        return pltpu.make_async_copy(h_hbm.at[rows, :], h_buf, h_sem)

    @pl.when((i == 0) & (j == 0))
    def _():
        tile_copy(i).start()

    @pl.when((j == 1) & (i + 1 < pl.num_programs(0)))
    def _():
        tile_copy(i + 1).start()

    @pl.when(i % tiles_per_batch == 0)
    def _():
        for t in tails:
            t[j] = jnp.zeros(t.shape[1:], t.dtype)

    @pl.when(j == 0)
    def _():
        tile_copy(i).wait()
        xns = []
        for p in range(row_parts):
            rs = slice(p * rp, (p + 1) * rp)
            hp = h_buf[rs, :]
            xns.append(_rms_rows(hp, nw_ref[...]).astype(BF16))
            xn_ref[rs, :] = xns[-1]
            out_ref[rs, :] = hp
        body(xns)

    @pl.when(j > 0)
    def _():
        body([xn_ref[p * rp:(p + 1) * rp, :] for p in range(row_parts)])


def _tile_scratch(tm, d):
    return [pltpu.VMEM((tm, d), BF16), pltpu.VMEM((tm, d), F32), pltpu.SemaphoreType.DMA(())]


def _ffn_kernel(*refs, tm, tiles_per_batch, final_norm, row_parts, n_side):
    h_hbm, nw_ref, wg_ref, wv_ref, cwg_ref, cwv_ref, wd_ref, fw_ref = refs[:8]
    out_ref = refs[8 + n_side]
    xn_ref, h_buf, h_sem, tg_ref, tv_ref = refs[9 + 2 * n_side:]
    _side_cast(refs[8:8 + n_side], refs[9 + n_side:9 + 2 * n_side])
    j = pl.program_id(1)
    rp = tm // row_parts

    def body(xns):
        ups = [(_dot(xn, wg_ref[...]), _dot(xn, wv_ref[...])) for xn in xns]
        prev_g, prev_v = tg_ref[j], tv_ref[j]
        acts = []
        for ug, uv in ups:
            cg = _conv3(prev_g, ug, cwg_ref[...])
            cv = _conv3(prev_v, uv, cwv_ref[...])
            prev_g, prev_v = ug[rp - TAIL:], uv[rp - TAIL:]
            acts.append((cg * _sigmoid(cg) * cv).astype(BF16))
        tg_ref[j] = prev_g
        tv_ref[j] = prev_v
        for p, act in enumerate(acts):
            out_ref[p * rp:(p + 1) * rp, :] += _dot(act, wd_ref[...])

    _tile_step(h_hbm, h_buf, h_sem, nw_ref, xn_ref, out_ref, (tg_ref, tv_ref), tm, tiles_per_batch,
               row_parts, body)

    if final_norm:
        @pl.when(j == pl.num_programs(1) - 1)
        def _():
            out_ref[...] = _rms_rows(out_ref[...], fw_ref[...])


def _ffn(h, seq_len, norm_w, w_up, conv_w, w_down, final_w, side_weights, *, final_norm, tm=1024,
         tn=512, row_parts=ROW_PARTS, vmem_limit=VMEM_LIMIT):
    N, D = h.shape
    F = w_down.shape[0]
    nf = F // tn
    assert nf >= 2
    grid = (N // tm, nf)
    side_in, side_in_specs, side_out_specs, side_shapes = _side_cast_plan(side_weights, grid)
    outs = pl.pallas_call(
        functools.partial(_ffn_kernel, tm=tm, tiles_per_batch=seq_len // tm,
                          final_norm=final_norm, row_parts=row_parts, n_side=len(side_in)),
        grid=grid,
        in_specs=[
            pl.BlockSpec(memory_space=pl.ANY),
            pl.BlockSpec((1, D), lambda i, j: (0, 0)),
            pl.BlockSpec((D, tn), lambda i, j: (0, j)),
            pl.BlockSpec((D, tn), lambda i, j: (0, nf + j)),
            pl.BlockSpec((3, tn), lambda i, j: (0, j)),
            pl.BlockSpec((3, tn), lambda i, j: (0, nf + j)),
            pl.BlockSpec((tn, D), lambda i, j: (j, 0)),
            pl.BlockSpec((1, D), lambda i, j: (0, 0)),
        ] + side_in_specs,
        out_specs=[pl.BlockSpec((tm, D), lambda i, j: (i, 0))] + side_out_specs,
        out_shape=[jax.ShapeDtypeStruct((N, D), F32)] + side_shapes,
        scratch_shapes=_tile_scratch(tm, D) + [pltpu.VMEM((nf, TAIL, tn), F32)] * 2,
        compiler_params=_params(("arbitrary", "arbitrary"), vmem_limit),
        name="ffn_final" if final_norm else "ffn",
    )(h, norm_w, *([_bf16(w_up)] * 2), conv_w, conv_w, _bf16(w_down), final_w, *side_in)
    return outs[0], [o.reshape(w.shape[1:]) for o, (w, _) in zip(outs[1:], side_weights)]


def _sc_kernel(h_hbm, nw_ref, wb_ref, wc_ref, wh_ref, cw_ref, wd_ref,
               out_ref, xn_ref, h_buf, h_sem, tz_ref, *, tm, tiles_per_batch, row_parts):
    j = pl.program_id(1)
    rp = tm // row_parts

    def body(xns):
        ups = [(_dot(xn, wc_ref[...]), _dot(xn, wh_ref[...]), _dot(xn, wb_ref[...])) for xn in xns]
        prev_z = tz_ref[j]
        ys = []
        for uc, uh, gb in ups:
            z = uc * uh
            ys.append((gb * _conv3(prev_z, z, cw_ref[...])).astype(BF16))
            prev_z = z[rp - TAIL:]
        tz_ref[j] = prev_z
        for p, y in enumerate(ys):
            out_ref[p * rp:(p + 1) * rp, :] += _dot(y, wd_ref[...])

    _tile_step(h_hbm, h_buf, h_sem, nw_ref, xn_ref, out_ref, (tz_ref,), tm, tiles_per_batch,
               row_parts, body)


def _short_conv(h, seq_len, norm_w, w_in, conv_w, w_out, *, tm=1024, tn=512, row_parts=ROW_PARTS,
                vmem_limit=60 * 1024 * 1024):
    N, D = h.shape
    nd = D // tn
    assert nd >= 2
    w_spec = lambda part: pl.BlockSpec((D, tn), lambda i, j: (0, part * nd + j))
    return pl.pallas_call(
        functools.partial(_sc_kernel, tm=tm, tiles_per_batch=seq_len // tm, row_parts=row_parts),
        grid=(N // tm, nd),
        in_specs=[
            pl.BlockSpec(memory_space=pl.ANY),
            pl.BlockSpec((1, D), lambda i, j: (0, 0)),
            w_spec(0), w_spec(1), w_spec(2),
            pl.BlockSpec((3, tn), lambda i, j: (0, j)),
            pl.BlockSpec((tn, D), lambda i, j: (j, 0)),
        ],
        out_specs=pl.BlockSpec((tm, D), lambda i, j: (i, 0)),
        out_shape=jax.ShapeDtypeStruct((N, D), F32),
        scratch_shapes=_tile_scratch(tm, D) + [pltpu.VMEM((nd, TAIL, tn), F32)],
        compiler_params=_params(("arbitrary", "arbitrary"), vmem_limit),
        name="short_conv",
    )(h, norm_w, *([_bf16(w_in)] * 3), conv_w, _bf16(w_out))


def kernel(x, norm_mix, norm_ffn, hgrn_w_in, hgrn_lb_table, hgrn_out_norm, hgrn_w_out,
           sc_w_in, sc_conv, sc_w_out, ffn_w_up, ffn_conv, ffn_w_down, final_norm):
    B, T, D = x.shape
    depth = norm_mix.shape[0]
    n_mixers = 2
    row = lambda w: w.reshape(1, -1).astype(F32)
    stacks = dict(hgrn_w_in=hgrn_w_in, hgrn_w_out=hgrn_w_out, sc_w_in=sc_w_in, sc_w_out=sc_w_out,
                  ffn_w_up=ffn_w_up, ffn_w_down=ffn_w_down)

    def layer_weights(i):
        mixer = ("hgrn_w_in", "hgrn_w_out") if i % n_mixers == 0 else ("sc_w_in", "sc_w_out")
        return [(name, i // n_mixers) for name in mixer] + [("ffn_w_up", i), ("ffn_w_down", i)]

    cast = {}

    def weight(name, idx):
        return cast.get((name, idx), stacks[name][idx])

    def side(keys):
        return [(stacks[name], idx) for name, idx in keys]

    h = x.reshape(B * T, D)
    for i in range(depth):
        j = i // n_mixers
        if i % n_mixers == 0:
            proj_side = layer_weights(i)[1:] if i == 0 else []
            rec_side = [kw for l in range(1, depth) for kw in layer_weights(l)[:2]] if i == 0 else []
            outs, done = _hgrn_proj(h, B, row(norm_mix[i]), hgrn_lb_table.astype(F32),
                                    weight("hgrn_w_in", j), side(proj_side), layer=i)
            cast.update(zip(proj_side, done))
            q, k, b2, bmin, v, gate = outs
            o, done = _hgrn_rec(q, k, b2, bmin, v, side(rec_side))
            cast.update(zip(rec_side, done))
            h = _hgrn_out(o.reshape(B * T, D), gate, h, row(hgrn_out_norm[j]),
                          weight("hgrn_w_out", j).astype(BF16))
        else:
            h = _short_conv(h, T, row(norm_mix[i]), weight("sc_w_in", j), sc_conv[j].astype(F32),
                            weight("sc_w_out", j))
        ffn_side = [kw for kw in layer_weights(i + 1)[2:] if kw not in cast] if i + 1 < depth else []
        h, done = _ffn(h, T, row(norm_ffn[i]), weight("ffn_w_up", i), ffn_conv[i].astype(F32),
                       weight("ffn_w_down", i), row(final_norm), side(ffn_side),
                       final_norm=(i == depth - 1))
        cast.update(zip(ffn_side, done))
    return h.reshape(B, T, D)
```

```python
import functools
import math

import jax
import jax.numpy as jnp
from jax import lax
from jax.experimental import pallas as pl
from jax.experimental.pallas import tpu as pltpu

F32 = jnp.float32
BF16 = jnp.bfloat16

EPS = 1e-6
HEAD_DIM = 128
CHUNK = 64
SUB = 16
N_SUB = CHUNK // SUB
N_PIECES = N_SUB * (N_SUB + 1) // 2
A_COLS = SUB * N_PIECES
EXP2_CLAMP = 112.0
REC_UNROLL = 64
ROW_PARTS = 2
VMEM_LIMIT = 56 * 1024 * 1024


def _dot(a, b):
    return jnp.dot(a, b, preferred_element_type=F32)


def _dot_nt(a, b):
    return lax.dot_general(a, b, (((1,), (1,)), ((), ())), preferred_element_type=F32)


def _dot_tn(a, b):
    return lax.dot_general(a, b, (((0,), (0,)), ((), ())), preferred_element_type=F32)


def _sigmoid(x):
    return 1.0 / (1.0 + jnp.exp(-x))


def _rms_rows(x, w):
    ms = jnp.mean(x * x, axis=-1, keepdims=True)
    return x * lax.rsqrt(ms + EPS) * w


def _params(semantics, vmem_limit=VMEM_LIMIT):
    return pltpu.CompilerParams(dimension_semantics=semantics, vmem_limit_bytes=vmem_limit)


def _bf16(w):
    return w.astype(BF16)


def _side_cast_plan(weights, grid):
    n_steps = math.prod(grid)
    strides = [math.prod(grid[d + 1:]) for d in range(len(grid))]

    def chunk(ids, n_chunks):
        return jnp.minimum(sum(i * s for i, s in zip(ids, strides)), n_chunks - 1)

    ins, in_specs, out_specs, shapes = [], [], [], []
    for stack, idx in weights:
        n, rows, cols = stack.shape
        nc = max(c for c in range(1, n_steps + 1) if rows % c == 0)
        block = (None, rows // nc, cols)
        ins.append(stack.reshape(n * nc, rows // nc, cols))
        in_specs.append(pl.BlockSpec(block, lambda *ids, base=idx * nc, nc=nc: (base + chunk(ids, nc), 0, 0)))
        out_specs.append(pl.BlockSpec(block, lambda *ids, nc=nc: (chunk(ids, nc), 0, 0)))
        shapes.append(jax.ShapeDtypeStruct((nc, rows // nc, cols), BF16))
    return ins, in_specs, out_specs, shapes


def _side_cast(src_refs, dst_refs):
    for src, dst in zip(src_refs, dst_refs):
        dst[...] = src[...].astype(BF16)


def _chunk_cumsum(x):
    n, d = x.shape
    pos = lax.broadcasted_iota(jnp.int32, (n, d), 0) & (CHUNK - 1)
    s = 1
    while s < CHUNK:
        pad = max(8, s)
        xp = jnp.concatenate([jnp.zeros((pad, d), x.dtype), x], axis=0)
        x = x + jnp.where(pos >= s, xp[pad - s:pad - s + n], 0.0)
        s *= 2
    return x


def _hgrn_proj_kernel(*refs, layer, heads_per_step, row_parts, n_side):
    h_hbm, nw_ref, lbt_ref = refs[:3]
    w_refs = refs[3:7]
    q_ref, k_ref, b_ref, bmin_ref, v_ref, g_ref, xn_hbm = refs[7 + n_side:14 + n_side]
    wb_ref, h_buf, xn_buf, h_sem, wr_sem, rd_sem = refs[-6:]
    _side_cast(refs[7:7 + n_side], refs[14 + n_side:14 + 2 * n_side])

    j = pl.program_id(0)
    tiles_per_batch = pl.num_programs(2)
    n_tiles = pl.num_programs(1) * tiles_per_batch
    n_steps = pl.num_programs(0) * n_tiles
    tile = pl.program_id(1) * tiles_per_batch + pl.program_id(2)
    step = j * n_tiles + tile
    slot = tile % 2
    tm = h_buf.shape[0]
    tile_rows = lambda t: pl.ds(pl.multiple_of(t * tm, tm), tm)
    h_copy = lambda t: pltpu.make_async_copy(h_hbm.at[tile_rows(t)], h_buf, h_sem)
    xn_write = lambda t: pltpu.make_async_copy(xn_buf.at[t % 2], xn_hbm.at[tile_rows(t)], wr_sem)
    xn_read = lambda t: pltpu.make_async_copy(xn_hbm.at[tile_rows(t)], xn_buf.at[t % 2],
                                              rd_sem.at[t % 2])

    @pl.when(step == 0)
    def _():
        h_copy(0).start()

    @pl.when((step + 1 >= n_tiles) & (step + 1 < n_steps))
    def _():
        xn_read((tile + 1) % n_tiles).start()

    @pl.when(j == 0)
    def _():
        h_copy(tile).wait()
        rp = tm // ROW_PARTS
        for p in range(ROW_PARTS):
            rs = slice(p * rp, (p + 1) * rp)
            xn_buf[slot, rs, :] = _rms_rows(h_buf[rs, :], nw_ref[...]).astype(BF16)

        @pl.when(tile + 1 < n_tiles)
        def _():
            h_copy(tile + 1).start()

        xn_write(tile).start()

    @pl.when(j > 0)
    def _():
        xn_read(tile).wait()

    @pl.when(tile == 0)
    def _():
        for p, w_ref in enumerate(w_refs):
            wb_ref[p] = w_ref[...].astype(BF16)

    rows = [lbt_ref[l:l + 1, :] for l in range(lbt_ref.shape[0])]
    m = functools.reduce(jnp.maximum, rows)
    es = [jnp.exp(r - m) for r in rows]
    lb = sum(es[:layer + 1]) / sum(es)

    rp = tm // row_parts
    bmins = [None] * heads_per_step
    for p in range(row_parts):
        rs = slice(p * rp, (p + 1) * rp)
        xn = xn_buf[slot, rs, :]
        q = _dot(xn, wb_ref[0])
        q = q * _sigmoid(q) * (HEAD_DIM ** -0.5)
        f = lb + (1.0 - lb) * _sigmoid(_dot(xn, wb_ref[1]))
        b2 = _chunk_cumsum(jnp.log2(f))
        k = 1.0 - f
        v = _dot(xn, wb_ref[2])
        g = _dot(xn, wb_ref[3])
        g_ref[rs, :] = (g * _sigmoid(g)).astype(BF16)
        for hh in range(heads_per_step):
            sl = slice(hh * HEAD_DIM, (hh + 1) * HEAD_DIM)
            q_ref[0, hh, rs, :] = q[:, sl].astype(BF16)
            k_ref[0, hh, rs, :] = k[:, sl].astype(BF16)
            b_ref[0, hh, rs, :] = b2[:, sl]
            v_ref[0, hh, rs, :] = v[:, sl].astype(BF16)
            bm = jnp.min(b2[:, sl].reshape(-1, 8, HEAD_DIM), axis=0)
            bmins[hh] = bm if bmins[hh] is None else jnp.minimum(bmins[hh], bm)
    for hh in range(heads_per_step):
        bmin_ref[0, hh, 0] = bmins[hh]

    @pl.when(j == 0)
    def _():
        xn_write(tile).wait()


def _hgrn_proj(h, batch, norm_w, lb_table, w_in, side_weights, *, layer, tm=1024, heads_per_step=2,
               row_parts=4):
    N, D = h.shape
    B, T = batch, N // batch
    assert (N // tm) % 2 == 0
    H = D // HEAD_DIM
    tn = heads_per_step * HEAD_DIM
    nq = D // tn
    grid = (H // heads_per_step, B, T // tm)
    head_shape = (B, H, T, HEAD_DIM)
    head_spec = pl.BlockSpec((1, heads_per_step, tm, HEAD_DIM), lambda j, b, i: (b, j, i, 0))
    w_spec = lambda part: pl.BlockSpec((D, tn), lambda j, b, i: (0, part * nq + j))
    side_in, side_in_specs, side_out_specs, side_shapes = _side_cast_plan(side_weights, grid)
    outs = pl.pallas_call(
        functools.partial(_hgrn_proj_kernel, layer=layer, heads_per_step=heads_per_step,
                          row_parts=row_parts, n_side=len(side_in)),
        grid=grid,
        in_specs=[
            pl.BlockSpec(memory_space=pl.ANY),
            pl.BlockSpec((1, D), lambda j, b, i: (0, 0)),
            pl.BlockSpec((lb_table.shape[0], tn), lambda j, b, i: (0, j)),
            w_spec(0), w_spec(1), w_spec(2), w_spec(3),
        ] + side_in_specs,
        out_specs=[
            head_spec, head_spec, head_spec,
            pl.BlockSpec((1, heads_per_step, 1, 8, HEAD_DIM), lambda j, b, i: (b, j, i, 0, 0)),
            head_spec,
            pl.BlockSpec((tm, tn), lambda j, b, i: (b * (T // tm) + i, j)),
            pl.BlockSpec(memory_space=pl.ANY),
        ] + side_out_specs,
        out_shape=[
            jax.ShapeDtypeStruct(head_shape, BF16),
            jax.ShapeDtypeStruct(head_shape, BF16),
            jax.ShapeDtypeStruct(head_shape, F32),
            jax.ShapeDtypeStruct((B, H, T // tm, 8, HEAD_DIM), F32),
            jax.ShapeDtypeStruct(head_shape, BF16),
            jax.ShapeDtypeStruct((N, D), BF16),
            jax.ShapeDtypeStruct((N, D), BF16),
        ] + side_shapes,
        scratch_shapes=[pltpu.VMEM((4, D, tn), BF16), pltpu.VMEM((tm, D), F32),
                        pltpu.VMEM((2, tm, D), BF16), pltpu.SemaphoreType.DMA(()),
                        pltpu.SemaphoreType.DMA(()), pltpu.SemaphoreType.DMA((2,))],
        compiler_params=_params(("arbitrary", "arbitrary", "arbitrary")),
        name="hgrn_proj",
    )(h, norm_w, lb_table, *([w_in] * 4), *side_in)
    return outs[:6], [o.reshape(w.shape[1:]) for o, (w, _) in zip(outs[7:], side_weights)]


def _piece_offset(i):
    return SUB * i * (i + 1) // 2


def _score_mask():
    rw = lax.broadcasted_iota(jnp.int32, (CHUNK, A_COLS), 0)
    cw = lax.broadcasted_iota(jnp.int32, (CHUNK, A_COLS), 1)
    keep = None
    for i in range(N_SUB):
        diag0 = _piece_offset(i) + SUB * i
        in_rows = (rw >= SUB * i) & (rw < SUB * (i + 1))
        left = (cw >= _piece_offset(i)) & (cw < diag0)
        diag = (cw >= diag0) & (cw < diag0 + SUB) & (cw - diag0 <= rw - SUB * i)
        m = in_rows & (left | diag)
        keep = m if keep is None else keep | m
    return keep


def _exact_diag(slab, qb, kb, bb, lane0):
    lr = lax.broadcasted_iota(jnp.int32, slab.shape, 0)
    lc = lax.broadcasted_iota(jnp.int32, slab.shape, 1)
    for s in range(SUB):
        e = jnp.exp2(jnp.minimum(bb - bb[s:s + 1, :], 0.0))
        col = jnp.sum(qb * kb[s:s + 1, :] * e, axis=-1, keepdims=True)
        slab = jnp.where((lc == lane0 + s) & (lr >= s), col, slab)
    return slab


def _block_rows(rows):
    return jnp.concatenate([jnp.broadcast_to(r, (SUB, r.shape[-1])) for r in rows], axis=0)


def _chunk_factors(q_ref, k_ref, b_ref, rows):
    q = q_ref[0, 0, rows, :].astype(F32)
    k = k_ref[0, 0, rows, :].astype(F32)
    b = b_ref[0, 0, rows, :]

    ends = [b[SUB * (j + 1) - 1:SUB * (j + 1), :] for j in range(N_SUB)]
    refs = [jnp.zeros_like(ends[0])] + ends[:-1]
    b_last = ends[-1]
    d = b - _block_rows(refs)
    q_blk = (q * jnp.exp2(d)).astype(BF16)
    k_diag = k * jnp.exp2(jnp.minimum(-d, EXP2_CLAMP))
    k_end = k * jnp.exp2(_block_rows(ends) - b)
    pieces = []
    for i in range(N_SUB):
        for j in range(i):
            kj = k_end[SUB * j:SUB * (j + 1)]
            if j < i - 1:
                kj = kj * jnp.exp2(refs[i] - ends[j])
            pieces.append(kj.astype(BF16))
        pieces.append(k_diag[SUB * i:SUB * (i + 1)].astype(BF16))
    k_all = jnp.concatenate(pieces, axis=0)
    q_dec = (q * jnp.exp2(b)).astype(BF16)
    k_dec = (k_end * _block_rows([jnp.exp2(b_last - e) for e in ends])).astype(BF16)
    return (q, k, b), q_blk, k_all, q_dec, k_dec, jnp.exp2(b_last)


def _fix_diag(a, q, k, b):
    blocks = []
    for i in range(N_SUB):
        rs = slice(SUB * i, SUB * (i + 1))
        diag0 = _piece_offset(i) + SUB * i
        tile = diag0 // 128
        slabs = [a[rs, 128 * t:min(128 * (t + 1), A_COLS)] for t in range(pl.cdiv(A_COLS, 128))]
        slabs[tile] = _exact_diag(slabs[tile], q[rs], k[rs], b[rs], diag0 % 128)
        blocks.append(jnp.concatenate(slabs, axis=1))
    return jnp.concatenate(blocks, axis=0)


def _stack_values(v):
    return jnp.concatenate([v[0:SUB * (i + 1)] for i in range(N_SUB)], axis=0)


def _hgrn_rec_kernel(*refs, tm, n_side):
    q_ref, k_ref, b_ref, bmin_ref, v_ref = refs[:5]
    o_ref = refs[5 + n_side]
    st_ref, qd_scr, oi_scr, kv_scr, dl_scr = refs[6 + 2 * n_side:]
    _side_cast(refs[5:5 + n_side], refs[6 + n_side:6 + 2 * n_side])

    @pl.when(pl.program_id(2) == 0)
    def _():
        st_ref[...] = jnp.zeros_like(st_ref)

    n_chunks = tm // CHUNK
    unroll = math.gcd(n_chunks, REC_UNROLL)

    def chunk_rows(c):
        return pl.ds(pl.multiple_of(c * CHUNK, CHUNK), CHUNK)

    def state_rows(c):
        return pl.ds(pl.multiple_of(c * HEAD_DIM, HEAD_DIM), HEAD_DIM)

    def decay_rows(c, n):
        return pl.ds(pl.multiple_of(c * 8, 8), n)

    keep = _score_mask()

    def fast():
        def intra_body(c, carry):
            ccs = [c * unroll + u for u in range(unroll)]
            fac = [_chunk_factors(q_ref, k_ref, b_ref, chunk_rows(cc)) for cc in ccs]
            scores = [jnp.where(keep, _dot_nt(f[1], f[2]), 0.0).astype(BF16) for f in fac]
            for cc, f in zip(ccs, fac):
                kv_scr[state_rows(cc), :] = _dot_tn(v_ref[0, 0, chunk_rows(cc), :], f[4])
                qd_scr[chunk_rows(cc), :] = f[3]
                dl_scr[decay_rows(cc, 8), :] = jnp.broadcast_to(f[5], (8, HEAD_DIM))
            for cc, a in zip(ccs, scores):
                rows = chunk_rows(cc)
                oi_scr[rows, :] = _dot(a, _stack_values(v_ref[0, 0, rows, :]))
            return carry

        lax.fori_loop(0, n_chunks // unroll, intra_body, 0)

        def state_body(c, st):
            for u in range(unroll):
                cc = c * unroll + u
                rows = chunk_rows(cc)
                o = oi_scr[rows, :] + _dot_nt(qd_scr[rows, :], st.astype(BF16))
                o_ref[0, rows, :] = o.astype(o_ref.dtype)
                st = st * dl_scr[decay_rows(cc, 1), :] + kv_scr[state_rows(cc), :]
            return st

        st_ref[...] = lax.fori_loop(0, n_chunks // unroll, state_body, st_ref[...])

    def exact():
        def body(c, st):
            rows = chunk_rows(c)
            (q, k, b), q_blk, k_all, q_dec, k_dec, dl = _chunk_factors(q_ref, k_ref, b_ref, rows)
            a = _fix_diag(jnp.where(keep, _dot_nt(q_blk, k_all), 0.0), q, k, b).astype(BF16)
            v = v_ref[0, 0, rows, :]
            o = _dot(a, _stack_values(v)) + _dot_nt(q_dec, st.astype(BF16))
            o_ref[0, rows, :] = o.astype(o_ref.dtype)
            return st * dl + _dot_tn(v, k_dec)

        st_ref[...] = lax.fori_loop(0, n_chunks, body, st_ref[...])

    clamp_may_bind = jnp.min(bmin_ref[0, 0]) < -(EXP2_CLAMP - 1.0)
    pl.when(jnp.logical_not(clamp_may_bind))(fast)
    pl.when(clamp_may_bind)(exact)


def _hgrn_rec(q, k, b2, bmin, v, side_weights, *, tm=4096):
    B, H, T, _ = q.shape
    tm = min(tm, T)
    assert T % tm == 0 and tm % CHUNK == 0
    n_min = bmin.shape[2] // (T // tm)
    spec = pl.BlockSpec((1, 1, tm, HEAD_DIM), lambda b, h, t: (b, h, t, 0))
    grid = (B, H, T // tm)
    side_in, side_in_specs, side_out_specs, side_shapes = _side_cast_plan(side_weights, grid)
    outs = pl.pallas_call(
        functools.partial(_hgrn_rec_kernel, tm=tm, n_side=len(side_in)),
        grid=grid,
        in_specs=[spec, spec, spec,
                  pl.BlockSpec((1, 1, n_min, 8, HEAD_DIM), lambda b, h, t: (b, h, t, 0, 0)),
                  spec] + side_in_specs,
        out_specs=[pl.BlockSpec((1, tm, HEAD_DIM), lambda b, h, t: (b, t, h))] + side_out_specs,
        out_shape=[jax.ShapeDtypeStruct((B, T, H * HEAD_DIM), BF16)] + side_shapes,
        scratch_shapes=[pltpu.VMEM((HEAD_DIM, HEAD_DIM), F32),
                        pltpu.VMEM((tm, HEAD_DIM), BF16),
                        pltpu.VMEM((tm, HEAD_DIM), F32),
                        pltpu.VMEM((tm // CHUNK * HEAD_DIM, HEAD_DIM), F32),
                        pltpu.VMEM((tm // CHUNK * 8, HEAD_DIM), F32)],
        compiler_params=_params(("parallel", "parallel", "arbitrary")),
        name="hgrn_rec",
    )(q, k, b2, bmin, v, *side_in)
    return outs[0], [o.reshape(w.shape[1:]) for o, (w, _) in zip(outs[1:], side_weights)]


def _hgrn_out_kernel(o_ref, g_ref, h_ref, gain_ref, w_ref, out_ref):
    y = _rms_rows(o_ref[...].astype(F32), gain_ref[...]) * g_ref[...].astype(F32)
    out_ref[...] = h_ref[...] + _dot(y.astype(BF16), w_ref[...])


def _hgrn_out(o, gate, h, gain, w_out, *, tm=512):
    N, D = h.shape
    row_spec = pl.BlockSpec((tm, D), lambda i: (i, 0))
    return pl.pallas_call(
        _hgrn_out_kernel,
        grid=(N // tm,),
        in_specs=[row_spec, row_spec, row_spec,
                  pl.BlockSpec((1, D), lambda i: (0, 0)),
                  pl.BlockSpec((D, D), lambda i: (0, 0))],
        out_specs=row_spec,
        out_shape=jax.ShapeDtypeStruct((N, D), F32),
        compiler_params=_params(("parallel",)),
        name="hgrn_out",
    )(o, gate, h, gain, w_out)


TAIL = 8


def _conv3(prev, u, cw):
    rows = u.shape[0]
    ue = jnp.concatenate([prev, u], axis=0)
    return (cw[0:1, :] * ue[TAIL - 2:TAIL - 2 + rows]
            + cw[1:2, :] * ue[TAIL - 1:TAIL - 1 + rows]
            + cw[2:3, :] * u)


def _tile_step(h_hbm, h_buf, h_sem, nw_ref, xn_ref, out_ref, tails, tm, tiles_per_batch, row_parts,
               body):
    i = pl.program_id(0)
    j = pl.program_id(1)
    rp = tm // row_parts

    def tile_copy(t):
        rows = pl.ds(pl.multiple_of(t * tm, tm), tm)
        return pltpu.make_async_copy(h_hbm.at[rows, :], h_buf, h_sem)

    @pl.when((i == 0) & (j == 0))
    def _():
        tile_copy(i).start()

    @pl.when((j == 1) & (i + 1 < pl.num_programs(0)))
    def _():
        tile_copy(i + 1).start()

    @pl.when(i % tiles_per_batch == 0)
    def _():
        for t in tails:
            t[j] = jnp.zeros(t.shape[1:], t.dtype)

    @pl.when(j == 0)
    def _():
        tile_copy(i).wait()
        xns = []
        for p in range(row_parts):
            rs = slice(p * rp, (p + 1) * rp)
            hp = h_buf[rs, :]
            xns.append(_rms_rows(hp, nw_ref[...]).astype(BF16))
            xn_ref[rs, :] = xns[-1]
            out_ref[rs, :] = hp
        body(xns)

    @pl.when(j > 0)
    def _():
        body([xn_ref[p * rp:(p + 1) * rp, :] for p in range(row_parts)])


def _tile_scratch(tm, d):
    return [pltpu.VMEM((tm, d), BF16), pltpu.VMEM((tm, d), F32), pltpu.SemaphoreType.DMA(())]


def _ffn_kernel(*refs, tm, tiles_per_batch, final_norm, row_parts, n_side):
    h_hbm, nw_ref, wg_ref, wv_ref, cwg_ref, cwv_ref, wd_ref, fw_ref = refs[:8]
    out_ref = refs[8 + n_side]
    xn_ref, h_buf, h_sem, tg_ref, tv_ref = refs[9 + 2 * n_side:]
    _side_cast(refs[8:8 + n_side], refs[9 + n_side:9 + 2 * n_side])
    j = pl.program_id(1)
    rp = tm // row_parts

    def body(xns):
        ups = [(_dot(xn, wg_ref[...]), _dot(xn, wv_ref[...])) for xn in xns]
        prev_g, prev_v = tg_ref[j], tv_ref[j]
        acts = []
        for ug, uv in ups:
            cg = _conv3(prev_g, ug, cwg_ref[...])
            cv = _conv3(prev_v, uv, cwv_ref[...])
            prev_g, prev_v = ug[rp - TAIL:], uv[rp - TAIL:]
            acts.append((cg * _sigmoid(cg) * cv).astype(BF16))
        tg_ref[j] = prev_g
        tv_ref[j] = prev_v
        for p, act in enumerate(acts):
            out_ref[p * rp:(p + 1) * rp, :] += _dot(act, wd_ref[...])

    _tile_step(h_hbm, h_buf, h_sem, nw_ref, xn_ref, out_ref, (tg_ref, tv_ref), tm, tiles_per_batch,
               row_parts, body)

    if final_norm:
        @pl.when(j == pl.num_programs(1) - 1)
        def _():
            out_ref[...] = _rms_rows(out_ref[...], fw_ref[...])


def _ffn(h, seq_len, norm_w, w_up, conv_w, w_down, final_w, side_weights, *, final_norm, tm=1024,
         tn=512, row_parts=ROW_PARTS, vmem_limit=VMEM_LIMIT):
    N, D = h.shape
    F = w_down.shape[0]
    nf = F // tn
    assert nf >= 2
    grid = (N // tm, nf)
    side_in, side_in_specs, side_out_specs, side_shapes = _side_cast_plan(side_weights, grid)
    outs = pl.pallas_call(
        functools.partial(_ffn_kernel, tm=tm, tiles_per_batch=seq_len // tm,
                          final_norm=final_norm, row_parts=row_parts, n_side=len(side_in)),
        grid=grid,
        in_specs=[
            pl.BlockSpec(memory_space=pl.ANY),
            pl.BlockSpec((1, D), lambda i, j: (0, 0)),
            pl.BlockSpec((D, tn), lambda i, j: (0, j)),
            pl.BlockSpec((D, tn), lambda i, j: (0, nf + j)),
            pl.BlockSpec((3, tn), lambda i, j: (0, j)),
            pl.BlockSpec((3, tn), lambda i, j: (0, nf + j)),
            pl.BlockSpec((tn, D), lambda i, j: (j, 0)),
            pl.BlockSpec((1, D), lambda i, j: (0, 0)),
        ] + side_in_specs,
        out_specs=[pl.BlockSpec((tm, D), lambda i, j: (i, 0))] + side_out_specs,
        out_shape=[jax.ShapeDtypeStruct((N, D), F32)] + side_shapes,
        scratch_shapes=_tile_scratch(tm, D) + [pltpu.VMEM((nf, TAIL, tn), F32)] * 2,
        compiler_params=_params(("arbitrary", "arbitrary"), vmem_limit),
        name="ffn_final" if final_norm else "ffn",
    )(h, norm_w, *([_bf16(w_up)] * 2), conv_w, conv_w, _bf16(w_down), final_w, *side_in)
    return outs[0], [o.reshape(w.shape[1:]) for o, (w, _) in zip(outs[1:], side_weights)]


def _sc_kernel(h_hbm, nw_ref, wb_ref, wc_ref, wh_ref, cw_ref, wd_ref,
               out_ref, xn_ref, h_buf, h_sem, tz_ref, *, tm, tiles_per_batch, row_parts):
    j = pl.program_id(1)
    rp = tm // row_parts

    def body(xns):
        ups = [(_dot(xn, wc_ref[...]), _dot(xn, wh_ref[...]), _dot(xn, wb_ref[...])) for xn in xns]
        prev_z = tz_ref[j]
        ys = []
        for uc, uh, gb in ups:
            z = uc * uh
            ys.append((gb * _conv3(prev_z, z, cw_ref[...])).astype(BF16))
            prev_z = z[rp - TAIL:]
        tz_ref[j] = prev_z
        for p, y in enumerate(ys):
            out_ref[p * rp:(p + 1) * rp, :] += _dot(y, wd_ref[...])

    _tile_step(h_hbm, h_buf, h_sem, nw_ref, xn_ref, out_ref, (tz_ref,), tm, tiles_per_batch,
               row_parts, body)


def _short_conv(h, seq_len, norm_w, w_in, conv_w, w_out, *, tm=1024, tn=512, row_parts=ROW_PARTS,
                vmem_limit=60 * 1024 * 1024):
    N, D = h.shape
    nd = D // tn
    assert nd >= 2
    w_spec = lambda part: pl.BlockSpec((D, tn), lambda i, j: (0, part * nd + j))
    return pl.pallas_call(
        functools.partial(_sc_kernel, tm=tm, tiles_per_batch=seq_len // tm, row_parts=row_parts),
        grid=(N // tm, nd),
        in_specs=[
            pl.BlockSpec(memory_space=pl.ANY),
            pl.BlockSpec((1, D), lambda i, j: (0, 0)),
            w_spec(0), w_spec(1), w_spec(2),
            pl.BlockSpec((3, tn), lambda i, j: (0, j)),
            pl.BlockSpec((tn, D), lambda i, j: (j, 0)),
        ],
        out_specs=pl.BlockSpec((tm, D), lambda i, j: (i, 0)),
        out_shape=jax.ShapeDtypeStruct((N, D), F32),
        scratch_shapes=_tile_scratch(tm, D) + [pltpu.VMEM((nd, TAIL, tn), F32)],
        compiler_params=_params(("arbitrary", "arbitrary"), vmem_limit),
        name="short_conv",
    )(h, norm_w, *([_bf16(w_in)] * 3), conv_w, _bf16(w_out))


def kernel(x, norm_mix, norm_ffn, hgrn_w_in, hgrn_lb_table, hgrn_out_norm, hgrn_w_out,
           sc_w_in, sc_conv, sc_w_out, ffn_w_up, ffn_conv, ffn_w_down, final_norm):
    B, T, D = x.shape
    depth = norm_mix.shape[0]
    n_mixers = 2
    row = lambda w: w.reshape(1, -1).astype(F32)
    stacks = dict(hgrn_w_in=hgrn_w_in, hgrn_w_out=hgrn_w_out, sc_w_in=sc_w_in, sc_w_out=sc_w_out,
                  ffn_w_up=ffn_w_up, ffn_w_down=ffn_w_down)

    def layer_weights(i):
        mixer = ("hgrn_w_in", "hgrn_w_out") if i % n_mixers == 0 else ("sc_w_in", "sc_w_out")
        return [(name, i // n_mixers) for name in mixer] + [("ffn_w_up", i), ("ffn_w_down", i)]

    cast = {}

    def weight(name, idx):
        return cast.get((name, idx), stacks[name][idx])

    def side(keys):
        return [(stacks[name], idx) for name, idx in keys]

    h = x.reshape(B * T, D)
    for i in range(depth):
        j = i // n_mixers
        if i % n_mixers == 0:
            proj_side = layer_weights(i)[1:] if i == 0 else []
            rec_side = [kw for l in range(1, depth) for kw in layer_weights(l)[:2]] if i == 0 else []
            outs, done = _hgrn_proj(h, B, row(norm_mix[i]), hgrn_lb_table.astype(F32),
                                    weight("hgrn_w_in", j), side(proj_side), layer=i)
            cast.update(zip(proj_side, done))
            q, k, b2, bmin, v, gate = outs
            o, done = _hgrn_rec(q, k, b2, bmin, v, side(rec_side))
            cast.update(zip(rec_side, done))
            h = _hgrn_out(o.reshape(B * T, D), gate, h, row(hgrn_out_norm[j]),
                          weight("hgrn_w_out", j).astype(BF16))
        else:
            h = _short_conv(h, T, row(norm_mix[i]), weight("sc_w_in", j), sc_conv[j].astype(F32),
                            weight("sc_w_out", j))
        ffn_side = [kw for kw in layer_weights(i + 1)[2:] if kw not in cast] if i + 1 < depth else []
        h, done = _ffn(h, T, row(norm_ffn[i]), weight("ffn_w_up", i), ffn_conv[i].astype(F32),
                       weight("ffn_w_down", i), row(final_norm), side(ffn_side),
                       final_norm=(i == depth - 1))
        cast.update(zip(ffn_side, done))
    return h.reshape(B, T, D)
```

```python
import functools
import math

import jax
import jax.numpy as jnp
from jax import lax
from jax.experimental import pallas as pl
from jax.experimental.pallas import tpu as pltpu

F32 = jnp.float32
BF16 = jnp.bfloat16

EPS = 1e-6
HEAD_DIM = 128
CHUNK = 64
SUB = 16
N_SUB = CHUNK // SUB
N_PIECES = N_SUB * (N_SUB + 1) // 2
A_COLS = SUB * N_PIECES
EXP2_CLAMP = 112.0
REC_UNROLL = 64
ROW_PARTS = 2
VMEM_LIMIT = 56 * 1024 * 1024


def _dot(a, b):
    return jnp.dot(a, b, preferred_element_type=F32)


def _dot_nt(a, b):
    return lax.dot_general(a, b, (((1,), (1,)), ((), ())), preferred_element_type=F32)


def _dot_tn(a, b):
    return lax.dot_general(a, b, (((0,), (0,)), ((), ())), preferred_element_type=F32)


def _sigmoid(x):
    return 1.0 / (1.0 + jnp.exp(-x))


def _rms_rows(x, w):
    ms = jnp.mean(x * x, axis=-1, keepdims=True)
    return x * lax.rsqrt(ms + EPS) * w


def _params(semantics, vmem_limit=VMEM_LIMIT):
    return pltpu.CompilerParams(dimension_semantics=semantics, vmem_limit_bytes=vmem_limit)


def _bf16(w):
    return w.astype(BF16)


def _side_cast_plan(weights, grid):
    n_steps = math.prod(grid)
    strides = [math.prod(grid[d + 1:]) for d in range(len(grid))]

    def chunk(ids, n_chunks):
        return jnp.minimum(sum(i * s for i, s in zip(ids, strides)), n_chunks - 1)

    ins, in_specs, out_specs, shapes = [], [], [], []
    for stack, idx in weights:
        n, rows, cols = stack.shape
        nc = max(c for c in range(1, n_steps + 1) if rows % c == 0)
        block = (None, rows // nc, cols)
        ins.append(stack.reshape(n * nc, rows // nc, cols))
        in_specs.append(pl.BlockSpec(block, lambda *ids, base=idx * nc, nc=nc: (base + chunk(ids, nc), 0, 0)))
        out_specs.append(pl.BlockSpec(block, lambda *ids, nc=nc: (chunk(ids, nc), 0, 0)))
        shapes.append(jax.ShapeDtypeStruct((nc, rows // nc, cols), BF16))
    return ins, in_specs, out_specs, shapes


def _side_cast(src_refs, dst_refs):
    for src, dst in zip(src_refs, dst_refs):
        dst[...] = src[...].astype(BF16)


def _chunk_cumsum(x):
    n, d = x.shape
    pos = lax.broadcasted_iota(jnp.int32, (n, d), 0) & (CHUNK - 1)
    s = 1
    while s < CHUNK:
        pad = max(8, s)
        xp = jnp.concatenate([jnp.zeros((pad, d), x.dtype), x], axis=0)
        x = x + jnp.where(pos >= s, xp[pad - s:pad - s + n], 0.0)
        s *= 2
    return x


def _hgrn_proj_kernel(*refs, layer, heads_per_step, n_side):
    h_hbm, nw_ref, lbt_ref = refs[:3]
    w_refs = refs[3:7]
    q_ref, k_ref, b_ref, bmin_ref, v_ref, g_ref, xn_hbm = refs[7 + n_side:14 + n_side]
    wb_ref, h_buf, xn_buf, h_sem, wr_sem, rd_sem = refs[-6:]
    _side_cast(refs[7:7 + n_side], refs[14 + n_side:14 + 2 * n_side])

    j = pl.program_id(0)
    tiles_per_batch = pl.num_programs(2)
    n_tiles = pl.num_programs(1) * tiles_per_batch
    n_steps = pl.num_programs(0) * n_tiles
    tile = pl.program_id(1) * tiles_per_batch + pl.program_id(2)
    step = j * n_tiles + tile
    slot = tile % 2
    tm = h_buf.shape[0]
    tile_rows = lambda t: pl.ds(pl.multiple_of(t * tm, tm), tm)
    h_copy = lambda t: pltpu.make_async_copy(h_hbm.at[tile_rows(t)], h_buf, h_sem)
    xn_write = lambda t: pltpu.make_async_copy(xn_buf.at[t % 2], xn_hbm.at[tile_rows(t)], wr_sem)
    xn_read = lambda t: pltpu.make_async_copy(xn_hbm.at[tile_rows(t)], xn_buf.at[t % 2],
                                              rd_sem.at[t % 2])

    @pl.when(step == 0)
    def _():
        h_copy(0).start()

    @pl.when((step + 1 >= n_tiles) & (step + 1 < n_steps))
    def _():
        xn_read((tile + 1) % n_tiles).start(priority=1)

    @pl.when(j == 0)
    def _():
        h_copy(tile).wait()
        rp = tm // ROW_PARTS
        for p in range(ROW_PARTS):
            rs = slice(p * rp, (p + 1) * rp)
            xn_buf[slot, rs, :] = _rms_rows(h_buf[rs, :], nw_ref[...]).astype(BF16)

        @pl.when(tile + 1 < n_tiles)
        def _():
            h_copy(tile + 1).start(priority=1)

        xn_write(tile).start()

    @pl.when(j > 0)
    def _():
        xn_read(tile).wait()

    @pl.when(tile == 0)
    def _():
        for p, w_ref in enumerate(w_refs):
            wb_ref[p] = w_ref[...].astype(BF16)

    rows = [lbt_ref[l:l + 1, :] for l in range(lbt_ref.shape[0])]
    m = functools.reduce(jnp.maximum, rows)
    es = [jnp.exp(r - m) for r in rows]
    lb = sum(es[:layer + 1]) / sum(es)

    rp = tm // ROW_PARTS
    bmins = [None] * heads_per_step
    for p in range(ROW_PARTS):
        rs = slice(p * rp, (p + 1) * rp)
        xn = xn_buf[slot, rs, :]
        q = _dot(xn, wb_ref[0])
        q = q * _sigmoid(q) * (HEAD_DIM ** -0.5)
        f = lb + (1.0 - lb) * _sigmoid(_dot(xn, wb_ref[1]))
        b2 = _chunk_cumsum(jnp.log2(f))
        k = 1.0 - f
        v = _dot(xn, wb_ref[2])
        g = _dot(xn, wb_ref[3])
        g_ref[rs, :] = (g * _sigmoid(g)).astype(BF16)
        for hh in range(heads_per_step):
            sl = slice(hh * HEAD_DIM, (hh + 1) * HEAD_DIM)
            q_ref[0, hh, rs, :] = q[:, sl].astype(BF16)
            k_ref[0, hh, rs, :] = k[:, sl].astype(BF16)
            b_ref[0, hh, rs, :] = b2[:, sl]
            v_ref[0, hh, rs, :] = v[:, sl].astype(BF16)
            bm = jnp.min(b2[:, sl].reshape(-1, 8, HEAD_DIM), axis=0)
            bmins[hh] = bm if bmins[hh] is None else jnp.minimum(bmins[hh], bm)
    for hh in range(heads_per_step):
        bmin_ref[0, hh, 0] = bmins[hh]

    @pl.when(j == 0)
    def _():
        xn_write(tile).wait()


def _hgrn_proj(h, batch, norm_w, lb_table, w_in, side_weights, *, layer, tm=1024, heads_per_step=2):
    N, D = h.shape
    B, T = batch, N // batch
    assert (N // tm) % 2 == 0
    H = D // HEAD_DIM
    tn = heads_per_step * HEAD_DIM
    nq = D // tn
    grid = (H // heads_per_step, B, T // tm)
    head_shape = (B, H, T, HEAD_DIM)
    head_spec = pl.BlockSpec((1, heads_per_step, tm, HEAD_DIM), lambda j, b, i: (b, j, i, 0))
    w_spec = lambda part: pl.BlockSpec((D, tn), lambda j, b, i: (0, part * nq + j))
    side_in, side_in_specs, side_out_specs, side_shapes = _side_cast_plan(side_weights, grid)
    outs = pl.pallas_call(
        functools.partial(_hgrn_proj_kernel, layer=layer, heads_per_step=heads_per_step,
                          n_side=len(side_in)),
        grid=grid,
        in_specs=[
            pl.BlockSpec(memory_space=pl.ANY),
            pl.BlockSpec((1, D), lambda j, b, i: (0, 0)),
            pl.BlockSpec((lb_table.shape[0], tn), lambda j, b, i: (0, j)),
            w_spec(0), w_spec(1), w_spec(2), w_spec(3),
        ] + side_in_specs,
        out_specs=[
            head_spec, head_spec, head_spec,
            pl.BlockSpec((1, heads_per_step, 1, 8, HEAD_DIM), lambda j, b, i: (b, j, i, 0, 0)),
            head_spec,
            pl.BlockSpec((tm, tn), lambda j, b, i: (b * (T // tm) + i, j)),
            pl.BlockSpec(memory_space=pl.ANY),
        ] + side_out_specs,
        out_shape=[
            jax.ShapeDtypeStruct(head_shape, BF16),
            jax.ShapeDtypeStruct(head_shape, BF16),
            jax.ShapeDtypeStruct(head_shape, F32),
            jax.ShapeDtypeStruct((B, H, T // tm, 8, HEAD_DIM), F32),
            jax.ShapeDtypeStruct(head_shape, BF16),
            jax.ShapeDtypeStruct((N, D), BF16),
            jax.ShapeDtypeStruct((N, D), BF16),
        ] + side_shapes,
        scratch_shapes=[pltpu.VMEM((4, D, tn), BF16), pltpu.VMEM((tm, D), F32),
                        pltpu.VMEM((2, tm, D), BF16), pltpu.SemaphoreType.DMA(()),
                        pltpu.SemaphoreType.DMA(()), pltpu.SemaphoreType.DMA((2,))],
        compiler_params=_params(("arbitrary", "arbitrary", "arbitrary")),
        name="hgrn_proj",
    )(h, norm_w, lb_table, *([w_in] * 4), *side_in)
    return outs[:6], [o.reshape(w.shape[1:]) for o, (w, _) in zip(outs[7:], side_weights)]


def _piece_offset(i):
    return SUB * i * (i + 1) // 2


def _score_mask():
    rw = lax.broadcasted_iota(jnp.int32, (CHUNK, A_COLS), 0)
    cw = lax.broadcasted_iota(jnp.int32, (CHUNK, A_COLS), 1)
    keep = None
    for i in range(N_SUB):
        diag0 = _piece_offset(i) + SUB * i
        in_rows = (rw >= SUB * i) & (rw < SUB * (i + 1))
        left = (cw >= _piece_offset(i)) & (cw < diag0)
        diag = (cw >= diag0) & (cw < diag0 + SUB) & (cw - diag0 <= rw - SUB * i)
        m = in_rows & (left | diag)
        keep = m if keep is None else keep | m
    return keep


def _exact_diag(slab, qb, kb, bb, lane0):
    lr = lax.broadcasted_iota(jnp.int32, slab.shape, 0)
    lc = lax.broadcasted_iota(jnp.int32, slab.shape, 1)
    for s in range(SUB):
        e = jnp.exp2(jnp.minimum(bb - bb[s:s + 1, :], 0.0))
        col = jnp.sum(qb * kb[s:s + 1, :] * e, axis=-1, keepdims=True)
        slab = jnp.where((lc == lane0 + s) & (lr >= s), col, slab)
    return slab


def _block_rows(rows):
    return jnp.concatenate([jnp.broadcast_to(r, (SUB, r.shape[-1])) for r in rows], axis=0)


def _chunk_factors(q_ref, k_ref, b_ref, rows):
    q = q_ref[0, 0, rows, :].astype(F32)
    k = k_ref[0, 0, rows, :].astype(F32)
    b = b_ref[0, 0, rows, :]

    ends = [b[SUB * (j + 1) - 1:SUB * (j + 1), :] for j in range(N_SUB)]
    refs = [jnp.zeros_like(ends[0])] + ends[:-1]
    b_last = ends[-1]
    d = b - _block_rows(refs)
    q_blk = (q * jnp.exp2(d)).astype(BF16)
    k_diag = k * jnp.exp2(jnp.minimum(-d, EXP2_CLAMP))
    k_end = k * jnp.exp2(_block_rows(ends) - b)
    pieces = []
    for i in range(N_SUB):
        for j in range(i):
            kj = k_end[SUB * j:SUB * (j + 1)]
            if j < i - 1:
                kj = kj * jnp.exp2(refs[i] - ends[j])
            pieces.append(kj.astype(BF16))
        pieces.append(k_diag[SUB * i:SUB * (i + 1)].astype(BF16))
    k_all = jnp.concatenate(pieces, axis=0)
    q_dec = (q * jnp.exp2(b)).astype(BF16)
    k_dec = (k_end * _block_rows([jnp.exp2(b_last - e) for e in ends])).astype(BF16)
    return (q, k, b), q_blk, k_all, q_dec, k_dec, jnp.exp2(b_last)


def _fix_diag(a, q, k, b):
    blocks = []
    for i in range(N_SUB):
        rs = slice(SUB * i, SUB * (i + 1))
        diag0 = _piece_offset(i) + SUB * i
        tile = diag0 // 128
        slabs = [a[rs, 128 * t:min(128 * (t + 1), A_COLS)] for t in range(pl.cdiv(A_COLS, 128))]
        slabs[tile] = _exact_diag(slabs[tile], q[rs], k[rs], b[rs], diag0 % 128)
        blocks.append(jnp.concatenate(slabs, axis=1))
    return jnp.concatenate(blocks, axis=0)


def _stack_values(v):
    return jnp.concatenate([v[0:SUB * (i + 1)] for i in range(N_SUB)], axis=0)


def _hgrn_rec_kernel(*refs, tm, n_side):
    q_ref, k_ref, b_ref, bmin_ref, v_ref = refs[:5]
    o_ref = refs[5 + n_side]
    st_ref, qd_scr, oi_scr, kv_scr, dl_scr = refs[6 + 2 * n_side:]
    _side_cast(refs[5:5 + n_side], refs[6 + n_side:6 + 2 * n_side])

    @pl.when(pl.program_id(2) == 0)
    def _():
        st_ref[...] = jnp.zeros_like(st_ref)

    n_chunks = tm // CHUNK
    unroll = math.gcd(n_chunks, REC_UNROLL)

    def chunk_rows(c):
        return pl.ds(pl.multiple_of(c * CHUNK, CHUNK), CHUNK)

    def state_rows(c):
        return pl.ds(pl.multiple_of(c * HEAD_DIM, HEAD_DIM), HEAD_DIM)

    def decay_rows(c, n):
        return pl.ds(pl.multiple_of(c * 8, 8), n)

    keep = _score_mask()

    def fast():
        def intra_body(c, carry):
            ccs = [c * unroll + u for u in range(unroll)]
            fac = [_chunk_factors(q_ref, k_ref, b_ref, chunk_rows(cc)) for cc in ccs]
            scores = [jnp.where(keep, _dot_nt(f[1], f[2]), 0.0).astype(BF16) for f in fac]
            for cc, f in zip(ccs, fac):
                kv_scr[state_rows(cc), :] = _dot_tn(v_ref[0, 0, chunk_rows(cc), :], f[4])
                qd_scr[chunk_rows(cc), :] = f[3]
                dl_scr[decay_rows(cc, 8), :] = jnp.broadcast_to(f[5], (8, HEAD_DIM))
            for cc, a in zip(ccs, scores):
                rows = chunk_rows(cc)
                oi_scr[rows, :] = _dot(a, _stack_values(v_ref[0, 0, rows, :]))
            return carry

        lax.fori_loop(0, n_chunks // unroll, intra_body, 0)

        def state_body(c, st):
            for u in range(unroll):
                cc = c * unroll + u
                rows = chunk_rows(cc)
                o = oi_scr[rows, :] + _dot_nt(qd_scr[rows, :], st.astype(BF16))
                o_ref[0, rows, :] = o.astype(o_ref.dtype)
                st = st * dl_scr[decay_rows(cc, 1), :] + kv_scr[state_rows(cc), :]
            return st

        st_ref[...] = lax.fori_loop(0, n_chunks // unroll, state_body, st_ref[...])

    def exact():
        def body(c, st):
            rows = chunk_rows(c)
            (q, k, b), q_blk, k_all, q_dec, k_dec, dl = _chunk_factors(q_ref, k_ref, b_ref, rows)
            a = _fix_diag(jnp.where(keep, _dot_nt(q_blk, k_all), 0.0), q, k, b).astype(BF16)
            v = v_ref[0, 0, rows, :]
            o = _dot(a, _stack_values(v)) + _dot_nt(q_dec, st.astype(BF16))
            o_ref[0, rows, :] = o.astype(o_ref.dtype)
            return st * dl + _dot_tn(v, k_dec)

        st_ref[...] = lax.fori_loop(0, n_chunks, body, st_ref[...])

    clamp_may_bind = jnp.min(bmin_ref[0, 0]) < -(EXP2_CLAMP - 1.0)
    pl.when(jnp.logical_not(clamp_may_bind))(fast)
    pl.when(clamp_may_bind)(exact)


def _hgrn_rec(q, k, b2, bmin, v, side_weights, *, tm=4096):
    B, H, T, _ = q.shape
    tm = min(tm, T)
    assert T % tm == 0 and tm % CHUNK == 0
    n_min = bmin.shape[2] // (T // tm)
    spec = pl.BlockSpec((1, 1, tm, HEAD_DIM), lambda b, h, t: (b, h, t, 0))
    grid = (B, H, T // tm)
    side_in, side_in_specs, side_out_specs, side_shapes = _side_cast_plan(side_weights, grid)
    outs = pl.pallas_call(
        functools.partial(_hgrn_rec_kernel, tm=tm, n_side=len(side_in)),
        grid=grid,
        in_specs=[spec, spec, spec,
                  pl.BlockSpec((1, 1, n_min, 8, HEAD_DIM), lambda b, h, t: (b, h, t, 0, 0)),
                  spec] + side_in_specs,
        out_specs=[pl.BlockSpec((1, tm, HEAD_DIM), lambda b, h, t: (b, t, h))] + side_out_specs,
        out_shape=[jax.ShapeDtypeStruct((B, T, H * HEAD_DIM), BF16)] + side_shapes,
        scratch_shapes=[pltpu.VMEM((HEAD_DIM, HEAD_DIM), F32),
                        pltpu.VMEM((tm, HEAD_DIM), BF16),
                        pltpu.VMEM((tm, HEAD_DIM), F32),
                        pltpu.VMEM((tm // CHUNK * HEAD_DIM, HEAD_DIM), F32),
                        pltpu.VMEM((tm // CHUNK * 8, HEAD_DIM), F32)],
        compiler_params=_params(("parallel", "parallel", "arbitrary")),
        name="hgrn_rec",
    )(q, k, b2, bmin, v, *side_in)
    return outs[0], [o.reshape(w.shape[1:]) for o, (w, _) in zip(outs[1:], side_weights)]


def _hgrn_out_kernel(o_ref, g_ref, h_ref, gain_ref, w_ref, out_ref):
    y = _rms_rows(o_ref[...].astype(F32), gain_ref[...]) * g_ref[...].astype(F32)
    out_ref[...] = h_ref[...] + _dot(y.astype(BF16), w_ref[...])


def _hgrn_out(o, gate, h, gain, w_out, *, tm=512):
    N, D = h.shape
    row_spec = pl.BlockSpec((tm, D), lambda i: (i, 0))
    return pl.pallas_call(
        _hgrn_out_kernel,
        grid=(N // tm,),
        in_specs=[row_spec, row_spec, row_spec,
                  pl.BlockSpec((1, D), lambda i: (0, 0)),
                  pl.BlockSpec((D, D), lambda i: (0, 0))],
        out_specs=row_spec,
        out_shape=jax.ShapeDtypeStruct((N, D), F32),
        compiler_params=_params(("parallel",)),
        name="hgrn_out",
    )(o, gate, h, gain, w_out)


TAIL = 8


def _conv3(prev, u, cw):
    rows = u.shape[0]
    ue = jnp.concatenate([prev, u], axis=0)
    return (cw[0:1, :] * ue[TAIL - 2:TAIL - 2 + rows]
            + cw[1:2, :] * ue[TAIL - 1:TAIL - 1 + rows]
            + cw[2:3, :] * u)


def _tile_step(h_hbm, h_buf, h_sem, nw_ref, xn_ref, out_ref, tails, tm, tiles_per_batch, row_parts,
               body):
    i = pl.program_id(0)
    j = pl.program_id(1)
    rp = tm // row_parts

    def tile_copy(t):
        rows = pl.ds(pl.multiple_of(t * tm, tm), tm)
        return pltpu.make_async_copy(h_hbm.at[rows, :], h_buf, h_sem)

    @pl.when((i == 0) & (j == 0))
    def _():
        tile_copy(i).start()

    @pl.when((j == 1) & (i + 1 < pl.num_programs(0)))
    def _():
        tile_copy(i + 1).start()

    @pl.when(i % tiles_per_batch == 0)
    def _():
        for t in tails:
            t[j] = jnp.zeros(t.shape[1:], t.dtype)

    @pl.when(j == 0)
    def _():
        tile_copy(i).wait()
        xns = []
        for p in range(row_parts):
            rs = slice(p * rp, (p + 1) * rp)
            hp = h_buf[rs, :]
            xns.append(_rms_rows(hp, nw_ref[...]).astype(BF16))
            xn_ref[rs, :] = xns[-1]
            out_ref[rs, :] = hp
        body(xns)

    @pl.when(j > 0)
    def _():
        body([xn_ref[p * rp:(p + 1) * rp, :] for p in range(row_parts)])


def _tile_scratch(tm, d):
    return [pltpu.VMEM((tm, d), BF16), pltpu.VMEM((tm, d), F32), pltpu.SemaphoreType.DMA(())]


def _ffn_kernel(*refs, tm, tiles_per_batch, final_norm, row_parts, n_side):
    h_hbm, nw_ref, wg_ref, wv_ref, cwg_ref, cwv_ref, wd_ref, fw_ref = refs[:8]
    out_ref = refs[8 + n_side]
    xn_ref, h_buf, h_sem, tg_ref, tv_ref = refs[9 + 2 * n_side:]
    _side_cast(refs[8:8 + n_side], refs[9 + n_side:9 + 2 * n_side])
    j = pl.program_id(1)
    rp = tm // row_parts

    def body(xns):
        ups = [(_dot(xn, wg_ref[...]), _dot(xn, wv_ref[...])) for xn in xns]
        prev_g, prev_v = tg_ref[j], tv_ref[j]
        acts = []
        for ug, uv in ups:
            cg = _conv3(prev_g, ug, cwg_ref[...])
            cv = _conv3(prev_v, uv, cwv_ref[...])
            prev_g, prev_v = ug[rp - TAIL:], uv[rp - TAIL:]
            acts.append((cg * _sigmoid(cg) * cv).astype(BF16))
        tg_ref[j] = prev_g
        tv_ref[j] = prev_v
        for p, act in enumerate(acts):
            out_ref[p * rp:(p + 1) * rp, :] += _dot(act, wd_ref[...])

    _tile_step(h_hbm, h_buf, h_sem, nw_ref, xn_ref, out_ref, (tg_ref, tv_ref), tm, tiles_per_batch,
               row_parts, body)

    if final_norm:
        @pl.when(j == pl.num_programs(1) - 1)
        def _():
            out_ref[...] = _rms_rows(out_ref[...], fw_ref[...])


def _ffn(h, seq_len, norm_w, w_up, conv_w, w_down, final_w, side_weights, *, final_norm, tm=1024,
         tn=512, row_parts=ROW_PARTS, vmem_limit=VMEM_LIMIT):
    N, D = h.shape
    F = w_down.shape[0]
    nf = F // tn
    assert nf >= 2
    grid = (N // tm, nf)
    side_in, side_in_specs, side_out_specs, side_shapes = _side_cast_plan(side_weights, grid)
    outs = pl.pallas_call(
        functools.partial(_ffn_kernel, tm=tm, tiles_per_batch=seq_len // tm,
                          final_norm=final_norm, row_parts=row_parts, n_side=len(side_in)),
        grid=grid,
        in_specs=[
            pl.BlockSpec(memory_space=pl.ANY),
            pl.BlockSpec((1, D), lambda i, j: (0, 0)),
            pl.BlockSpec((D, tn), lambda i, j: (0, j)),
            pl.BlockSpec((D, tn), lambda i, j: (0, nf + j)),
            pl.BlockSpec((3, tn), lambda i, j: (0, j)),
            pl.BlockSpec((3, tn), lambda i, j: (0, nf + j)),
            pl.BlockSpec((tn, D), lambda i, j: (j, 0)),
            pl.BlockSpec((1, D), lambda i, j: (0, 0)),
        ] + side_in_specs,
        out_specs=[pl.BlockSpec((tm, D), lambda i, j: (i, 0))] + side_out_specs,
        out_shape=[jax.ShapeDtypeStruct((N, D), F32)] + side_shapes,
        scratch_shapes=_tile_scratch(tm, D) + [pltpu.VMEM((nf, TAIL, tn), F32)] * 2,
        compiler_params=_params(("arbitrary", "arbitrary"), vmem_limit),
        name="ffn_final" if final_norm else "ffn",
    )(h, norm_w, *([_bf16(w_up)] * 2), conv_w, conv_w, _bf16(w_down), final_w, *side_in)
    return outs[0], [o.reshape(w.shape[1:]) for o, (w, _) in zip(outs[1:], side_weights)]


def _sc_kernel(h_hbm, nw_ref, wb_ref, wc_ref, wh_ref, cw_ref, wd_ref,
               out_ref, xn_ref, h_buf, h_sem, tz_ref, *, tm, tiles_per_batch, row_parts):
    j = pl.program_id(1)
    rp = tm // row_parts

    def body(xns):
        ups = [(_dot(xn, wc_ref[...]), _dot(xn, wh_ref[...]), _dot(xn, wb_ref[...])) for xn in xns]
        prev_z = tz_ref[j]
        ys = []
        for uc, uh, gb in ups:
            z = uc * uh
            ys.append((gb * _conv3(prev_z, z, cw_ref[...])).astype(BF16))
            prev_z = z[rp - TAIL:]
        tz_ref[j] = prev_z
        for p, y in enumerate(ys):
            out_ref[p * rp:(p + 1) * rp, :] += _dot(y, wd_ref[...])

    _tile_step(h_hbm, h_buf, h_sem, nw_ref, xn_ref, out_ref, (tz_ref,), tm, tiles_per_batch,
               row_parts, body)


def _short_conv(h, seq_len, norm_w, w_in, conv_w, w_out, *, tm=1024, tn=512, row_parts=ROW_PARTS,
                vmem_limit=60 * 1024 * 1024):
    N, D = h.shape
    nd = D // tn
    assert nd >= 2
    w_spec = lambda part: pl.BlockSpec((D, tn), lambda i, j: (0, part * nd + j))
    return pl.pallas_call(
        functools.partial(_sc_kernel, tm=tm, tiles_per_batch=seq_len // tm, row_parts=row_parts),
        grid=(N // tm, nd),
        in_specs=[
            pl.BlockSpec(memory_space=pl.ANY),
            pl.BlockSpec((1, D), lambda i, j: (0, 0)),
            w_spec(0), w_spec(1), w_spec(2),
            pl.BlockSpec((3, tn), lambda i, j: (0, j)),
            pl.BlockSpec((tn, D), lambda i, j: (j, 0)),
        ],
        out_specs=pl.BlockSpec((tm, D), lambda i, j: (i, 0)),
        out_shape=jax.ShapeDtypeStruct((N, D), F32),
        scratch_shapes=_tile_scratch(tm, D) + [pltpu.VMEM((nd, TAIL, tn), F32)],
        compiler_params=_params(("arbitrary", "arbitrary"), vmem_limit),
        name="short_conv",
    )(h, norm_w, *([_bf16(w_in)] * 3), conv_w, _bf16(w_out))


def kernel(x, norm_mix, norm_ffn, hgrn_w_in, hgrn_lb_table, hgrn_out_norm, hgrn_w_out,
           sc_w_in, sc_conv, sc_w_out, ffn_w_up, ffn_conv, ffn_w_down, final_norm):
    B, T, D = x.shape
    depth = norm_mix.shape[0]
    n_mixers = 2
    row = lambda w: w.reshape(1, -1).astype(F32)
    stacks = dict(hgrn_w_in=hgrn_w_in, hgrn_w_out=hgrn_w_out, sc_w_in=sc_w_in, sc_w_out=sc_w_out,
                  ffn_w_up=ffn_w_up, ffn_w_down=ffn_w_down)

    def layer_weights(i):
        mixer = ("hgrn_w_in", "hgrn_w_out") if i % n_mixers == 0 else ("sc_w_in", "sc_w_out")
        return [(name, i // n_mixers) for name in mixer] + [("ffn_w_up", i), ("ffn_w_down", i)]

    cast = {}

    def weight(name, idx):
        return cast.get((name, idx), stacks[name][idx])

    def side(keys):
        return [(stacks[name], idx) for name, idx in keys]

    h = x.reshape(B * T, D)
    for i in range(depth):
        j = i // n_mixers
        if i % n_mixers == 0:
            proj_side = layer_weights(i)[1:] if i == 0 else []
            rec_side = [kw for l in range(1, depth) for kw in layer_weights(l)[:2]] if i == 0 else []
            outs, done = _hgrn_proj(h, B, row(norm_mix[i]), hgrn_lb_table.astype(F32),
                                    weight("hgrn_w_in", j), side(proj_side), layer=i)
            cast.update(zip(proj_side, done))
            q, k, b2, bmin, v, gate = outs
            o, done = _hgrn_rec(q, k, b2, bmin, v, side(rec_side))
            cast.update(zip(rec_side, done))
            h = _hgrn_out(o.reshape(B * T, D), gate, h, row(hgrn_out_norm[j]),
                          weight("hgrn_w_out", j).astype(BF16))
        else:
            h = _short_conv(h, T, row(norm_mix[i]), weight("sc_w_in", j), sc_conv[j].astype(F32),
                            weight("sc_w_out", j))
        ffn_side = [kw for kw in layer_weights(i + 1)[2:] if kw not in cast] if i + 1 < depth else []
        h, done = _ffn(h, T, row(norm_ffn[i]), weight("ffn_w_up", i), ffn_conv[i].astype(F32),
                       weight("ffn_w_down", i), row(final_norm), side(ffn_side),
                       final_norm=(i == depth - 1))
        cast.update(zip(ffn_side, done))
    return h.reshape(B, T, D)
```
